```python
import jax, jax.numpy as jnp
from jax import lax
import numpy as np

D_MODEL = 1024
BATCH = 8
SEQ = 2048
DEPTH = 4
DEC_BATCH = 32
DEC_SEQ = 8
PAST_LEN = 8192
PAGE_SIZE = 128

N_MIXERS = 2
N_GDN_LAYERS = (DEPTH + 1) // 2
N_NSA_LAYERS = DEPTH // 2
RMS_EPS = 1e-6

GDN_QK_HEADS = 8
GDN_V_HEADS = 16
GDN_DK = 128
GDN_DV = 128
GDN_QK_DIM = GDN_QK_HEADS * GDN_DK
GDN_V_DIM = GDN_V_HEADS * GDN_DV
GDN_CONV_CH = 2 * GDN_QK_DIM + GDN_V_DIM
GDN_CONV_W = 4
GDN_CHUNK = 64
GDN_PROJ = GDN_CONV_CH + GDN_V_DIM + 2 * GDN_V_HEADS

NSA_HEADS = 16
NSA_KV_GROUPS = 4
NSA_GROUP_SIZE = NSA_HEADS // NSA_KV_GROUPS
NSA_DH = 64
NSA_Q_DIM = NSA_HEADS * NSA_DH
NSA_KV_DIM = NSA_KV_GROUPS * NSA_DH
NSA_N_KV = 6
NSA_N_GATES = 3
NSA_PROJ = NSA_Q_DIM + NSA_N_KV * NSA_KV_DIM + NSA_N_GATES * NSA_HEADS
CMP_STRIDE = 16
CMP_LEN = 2 * CMP_STRIDE
CMP_HIDDEN = 256
SEL_BLOCK = 64
SEL_TOPN = 8
WINDOW = 512
NSA_Q_BLOCK = 64
NEG_INF = -1e30
FORCE_SCORE = 1e9

D_FF = 2816
FFN_CONV_W = 3

kernel_name = "hybrid_gdn_nsa_convffn_step"


def rms_norm(x, w):
    xf = x.astype(jnp.float32)
    xf = xf * lax.rsqrt(jnp.mean(xf * xf, axis=-1, keepdims=True) + RMS_EPS)
    return xf.astype(x.dtype) * w


def l2_normalize(x):
    xf = x.astype(jnp.float32)
    return (xf * lax.rsqrt(jnp.sum(xf * xf, axis=-1, keepdims=True) + 1e-6)).astype(x.dtype)


def causal_dwconv(x, buf, w):
    width = w.shape[0]
    t = x.shape[1]
    xp = jnp.concatenate([buf.astype(x.dtype), x], axis=1)
    y = sum(xp[:, i:i + t] * w[i] for i in range(width))
    return y, xp[:, t:]


def masked_softmax(s, mask):
    p = jax.nn.softmax(jnp.where(mask, s, NEG_INF), axis=-1)
    return p * mask


def alibi_slopes():
    return 2.0 ** (-8.0 * jnp.arange(1, NSA_HEADS + 1, dtype=jnp.float32) / NSA_HEADS)


def chunk_gated_delta(q, k, v, g, beta, s0):
    b, t, h, dk = q.shape
    dv = v.shape[-1]
    c = min(GDN_CHUNK, t)
    n = -(-t // c)
    pad = n * c - t

    def to_chunks(a):
        a = jnp.pad(a.astype(jnp.float32), [(0, 0), (0, pad)] + [(0, 0)] * (a.ndim - 2))
        a = a.reshape((b, n, c) + a.shape[2:])
        return jnp.moveaxis(a, (1, 3), (0, 2))

    qc, kc, vc, gc, bc = map(to_chunks, (q, k, v, g, beta))
    gcum = jnp.cumsum(gc, axis=-1)
    tri = jnp.tril(jnp.ones((c, c), dtype=bool))
    stri = jnp.tril(jnp.ones((c, c), dtype=bool), -1)
    diff = gcum[..., :, None] - gcum[..., None, :]
    decay = jnp.where(tri, jnp.exp(jnp.where(tri, diff, 0.0)), 0.0)
    kb = kc * bc[..., None]
    a_mat = jnp.where(stri, jnp.einsum('nbhid,nbhjd->nbhij', kb, kc) * decay, 0.0) + jnp.eye(c, dtype=jnp.float32)
    u = lax.linalg.triangular_solve(a_mat, vc * bc[..., None], left_side=True, lower=True, unit_diagonal=True)
    w = lax.linalg.triangular_solve(a_mat, kb * jnp.exp(gcum)[..., None], left_side=True, lower=True, unit_diagonal=True)
    qk = jnp.where(tri, jnp.einsum('nbhid,nbhjd->nbhij', qc, kc) * decay, 0.0)
    q_dec = qc * jnp.exp(gcum)[..., None]
    k_dec = kc * jnp.exp(gcum[..., -1:] - gcum)[..., None]
    g_tot = jnp.exp(gcum[..., -1])

    def step(s, xs):
        qk_i, qd_i, kd_i, u_i, w_i, gt_i = xs
        v_new = u_i - jnp.einsum('bhcd,bhde->bhce', w_i, s)
        o_i = jnp.einsum('bhcd,bhde->bhce', qd_i, s) + jnp.einsum('bhij,bhje->bhie', qk_i, v_new)
        s = s * gt_i[..., None, None] + jnp.einsum('bhcd,bhce->bhde', kd_i, v_new)
        return s, o_i

    s_fin, o = lax.scan(step, s0.astype(jnp.float32), (qk, q_dec, k_dec, u, w, g_tot))
    o = jnp.moveaxis(o, (0, 2), (1, 3)).reshape(b, n * c, h, dv)[:, :t]
    return o.astype(v.dtype), s_fin.astype(s0.dtype)


def gdn_mixer(x, s0, conv_buf, norm_w, w_in, conv_w, a_log, dt_bias, out_norm_w, w_out):
    b, t, _ = x.shape
    proj = rms_norm(x, norm_w) @ w_in
    qkv, z, a, beta_logit = jnp.split(
        proj, [GDN_CONV_CH, GDN_CONV_CH + GDN_V_DIM, GDN_CONV_CH + GDN_V_DIM + GDN_V_HEADS], axis=-1)
    qkv, new_buf = causal_dwconv(qkv, conv_buf, conv_w)
    qkv = jax.nn.silu(qkv)
    q, k, v = jnp.split(qkv, [GDN_QK_DIM, 2 * GDN_QK_DIM], axis=-1)
    rep = GDN_V_HEADS // GDN_QK_HEADS
    q = jnp.repeat(l2_normalize(q.reshape(b, t, GDN_QK_HEADS, GDN_DK)), rep, axis=2) * (GDN_DK ** -0.5)
    k = jnp.repeat(l2_normalize(k.reshape(b, t, GDN_QK_HEADS, GDN_DK)), rep, axis=2)
    v = v.reshape(b, t, GDN_V_HEADS, GDN_DV)
    beta = jax.nn.sigmoid(beta_logit.astype(jnp.float32))
    g = -jnp.exp(a_log.astype(jnp.float32)) * jax.nn.softplus(a.astype(jnp.float32) + dt_bias.astype(jnp.float32))
    o, s_new = chunk_gated_delta(q, k, v, g, beta, s0)
    o = rms_norm(o, out_norm_w) * jax.nn.silu(z.reshape(b, t, GDN_V_HEADS, GDN_DV))
    return o.reshape(b, t, GDN_V_DIM) @ w_out, s_new, new_buf


def compress_blocks(kv, pe, w1, w2):
    b, tk, g, dh = kv.shape
    n_sub = tk // CMP_STRIDE
    sub = kv[:, :n_sub * CMP_STRIDE].reshape(b, n_sub, CMP_STRIDE, g, dh)
    pe = pe.reshape(2, CMP_STRIDE, 1, dh)
    w1 = w1.reshape(2, CMP_STRIDE, dh, CMP_HIDDEN)
    first = jnp.einsum('bnsgd,sdh->bngh', sub + pe[0], w1[0])
    second = jnp.einsum('bnsgd,sdh->bngh', sub + pe[1], w1[1])
    hid = jax.nn.silu(first[:, :-1] + second[:, 1:])
    return hid @ w2


def nsa_mixer(x, past_kv, win_buf, norm_w, w_in, cmp_pe, cmp_w1, cmp_w2, w_out):
    b, t, _ = x.shape
    p_len = past_kv.shape[1]
    w_len = win_buf.shape[1]
    g, r, dh = NSA_KV_GROUPS, NSA_GROUP_SIZE, NSA_DH
    proj = rms_norm(x, norm_w) @ w_in
    q = proj[..., :NSA_Q_DIM].reshape(b, t, g, r, dh)
    kv = proj[..., NSA_Q_DIM:NSA_Q_DIM + NSA_N_KV * NSA_KV_DIM].reshape(b, t, NSA_N_KV, g, dh)
    gates = jax.nn.sigmoid(proj[..., NSA_Q_DIM + NSA_N_KV * NSA_KV_DIM:].astype(jnp.float32))
    gates = gates.reshape(b, t, g, r, NSA_N_GATES).astype(x.dtype)
    new_rows = kv[:, :, :4]
    win_all = jnp.concatenate([win_buf.astype(x.dtype), kv[:, :, 4:]], axis=1)
    full = jnp.concatenate([past_kv.astype(x.dtype), new_rows], axis=1)
    tk = p_len + t

    cmp_k = compress_blocks(full[:, :, 0], cmp_pe[0], cmp_w1[0], cmp_w2[0])
    cmp_v = compress_blocks(full[:, :, 1], cmp_pe[1], cmp_w1[1], cmp_w2[1])
    n_cmp = cmp_k.shape[1]
    cmp_end = jnp.arange(n_cmp) * CMP_STRIDE + (CMP_LEN - 1)

    n_blk = -(-tk // SEL_BLOCK)
    sel = jnp.pad(full[:, :, 2:4], [(0, 0), (0, n_blk * SEL_BLOCK - tk), (0, 0), (0, 0), (0, 0)])
    sel = sel.reshape(b, n_blk, SEL_BLOCK, 2, g, dh).transpose(0, 4, 1, 2, 3, 5)
    start = jnp.arange(n_cmp)[:, None] * CMP_STRIDE
    blk_start = jnp.arange(n_blk)[None, :] * SEL_BLOCK
    overlap = ((start < blk_start + SEL_BLOCK) & (start + CMP_LEN > blk_start)).astype(jnp.float32)
    top_n = min(SEL_TOPN, n_blk)

    win_pad = jnp.pad(win_all, [(0, 0), (WINDOW, 0), (0, 0), (0, 0), (0, 0)])
    slopes = alibi_slopes().reshape(g, r)[:, :, None, None]
    qb = NSA_Q_BLOCK if t % NSA_Q_BLOCK == 0 else t
    scale = dh ** -0.5
    b_idx = jnp.arange(b)[:, None, None, None]
    g_idx = jnp.arange(g)[None, :, None, None]

    def query_block(i):
        o0 = i * qb
        q_b = lax.dynamic_slice_in_dim(q, o0, qb, axis=1)
        gate_b = lax.dynamic_slice_in_dim(gates, o0, qb, axis=1)
        t_pos = p_len + o0 + jnp.arange(qb)
        dist = t_pos[:, None] - cmp_end[None, :]
        s = jnp.einsum('bqgrd,bngd->bgrqn', q_b, cmp_k).astype(jnp.float32) * scale - slopes * dist.astype(jnp.float32)
        p_cmp = masked_softmax(s, dist >= 0)
        o_cmp = jnp.einsum('bgrqn,bngd->bqgrd', p_cmp.astype(x.dtype), cmp_v)
        imp = jnp.einsum('bgrqn,nj->bgqj', p_cmp, overlap)
        j = jnp.arange(n_blk)
        cur = (t_pos // SEL_BLOCK)[:, None]
        forced = (j == 0) | (j == cur) | (j == cur - 1)
        imp = jnp.where(forced, FORCE_SCORE, imp)
        imp = jnp.where(j > cur, -FORCE_SCORE, imp)
        _, idx = lax.top_k(imp, top_n)
        gathered = sel[b_idx, g_idx, idx].reshape(b, g, qb, top_n * SEL_BLOCK, 2, dh)
        key_pos = (idx[..., None] * SEL_BLOCK + jnp.arange(SEL_BLOCK)).reshape(b, g, qb, top_n * SEL_BLOCK)
        dist = (t_pos[:, None] - key_pos)[:, :, None]
        s = jnp.einsum('bqgrd,bgqkd->bgrqk', q_b, gathered[..., 0, :]).astype(jnp.float32) * scale - slopes * dist.astype(jnp.float32)
        p_sel = masked_softmax(s, dist >= 0)
        o_sel = jnp.einsum('bgrqk,bgqkd->bqgrd', p_sel.astype(x.dtype), gathered[..., 1, :])
        win_b = lax.dynamic_slice_in_dim(win_pad, o0 + w_len, WINDOW + qb, axis=1)
        w_pos = p_len + o0 - WINDOW + jnp.arange(WINDOW + qb)
        dist = t_pos[:, None] - w_pos[None, :]
        mask = (dist >= 0) & (dist < WINDOW) & (w_pos[None, :] >= 0)
        s = jnp.einsum('bqgrd,bkgd->bgrqk', q_b, win_b[:, :, 0]).astype(jnp.float32) * scale - slopes * dist.astype(jnp.float32)
        p_win = masked_softmax(s, mask)
        o_win = jnp.einsum('bgrqk,bkgd->bqgrd', p_win.astype(x.dtype), win_b[:, :, 1])
        o = gate_b[..., 0:1] * o_cmp + gate_b[..., 1:2] * o_sel + gate_b[..., 2:3] * o_win
        return o.reshape(b, qb, NSA_Q_DIM).astype(x.dtype)

    out = lax.map(query_block, jnp.arange(t // qb))
    out = jnp.moveaxis(out, 0, 1).reshape(b, t, NSA_Q_DIM)
    new_win = win_all[:, win_all.shape[1] - min(WINDOW, tk):]
    return out @ w_out, new_rows, new_win


def conv_ffn(x, buf, norm_w, w_up, conv_w, w_down):
    h = rms_norm(x, norm_w) @ w_up
    a, gate = jnp.split(h, [D_FF], axis=-1)
    a, new_buf = causal_dwconv(a, buf, conv_w)
    return (jax.nn.silu(a) * gate) @ w_down, new_buf


def trunk(x, past_kv_fn, nsa_win, gdn_state, gdn_conv, ffn_conv,
          gdn_norm, gdn_w_in, gdn_conv_w, gdn_a_log, gdn_dt_bias, gdn_out_norm, gdn_w_out,
          nsa_norm, nsa_w_in, nsa_cmp_pe, nsa_cmp_w1, nsa_cmp_w2, nsa_w_out,
          ffn_norm, ffn_w_up, ffn_conv_w, ffn_w_down, final_norm):
    rows_l, win_l, s_l, gconv_l, fconv_l = [], [], [], [], []
    for i in range(DEPTH):
        j = i // N_MIXERS
        if i % N_MIXERS == 0:
            y, s_new, cb = gdn_mixer(x, gdn_state[j], gdn_conv[j], gdn_norm[j], gdn_w_in[j], gdn_conv_w[j],
                                     gdn_a_log[j], gdn_dt_bias[j], gdn_out_norm[j], gdn_w_out[j])
            s_l.append(s_new)
            gconv_l.append(cb)
        else:
            y, rows, wb = nsa_mixer(x, past_kv_fn(j), nsa_win[j], nsa_norm[j], nsa_w_in[j],
                                    nsa_cmp_pe[j], nsa_cmp_w1[j], nsa_cmp_w2[j], nsa_w_out[j])
            rows_l.append(rows)
            win_l.append(wb)
        x = x + y
        y, fb = conv_ffn(x, ffn_conv[i], ffn_norm[i], ffn_w_up[i], ffn_conv_w[i], ffn_w_down[i])
        x = x + y
        fconv_l.append(fb)
    return (rms_norm(x, final_norm), jnp.stack(rows_l), jnp.stack(win_l), jnp.stack(s_l),
            jnp.stack(gconv_l), jnp.stack(fconv_l))


def setup_inputs(seed: int = 0) -> dict:
    key = jax.random.key(seed)
    ks = jax.random.split(key, 32)
    f32 = jnp.float32
    n_pages = PAST_LEN // PAGE_SIZE
    n_used = DEC_BATCH * n_pages
    n_pool = n_used + n_used // 4
    win_len = min(WINDOW, PAST_LEN)

    def normal(k, shape, scale):
        return jax.random.normal(k, shape, f32) * scale

    def gain(k, shape):
        return 1.0 + 0.02 * jax.random.normal(k, shape, f32)

    page_table = jax.random.permutation(ks[0], n_pool)[:n_used].reshape(DEC_BATCH, n_pages).astype(jnp.int32)
    return {
        "x_prompt": normal(ks[1], (BATCH, SEQ, D_MODEL), 1.0),
        "x_sample": normal(ks[2], (DEC_BATCH, DEC_SEQ, D_MODEL), 1.0),
        "cache_nsa_kv": normal(ks[3], (N_NSA_LAYERS, n_pool, PAGE_SIZE, 4, NSA_KV_GROUPS, NSA_DH), 1.0),
        "cache_nsa_win": normal(ks[4], (N_NSA_LAYERS, DEC_BATCH, win_len, 2, NSA_KV_GROUPS, NSA_DH), 1.0),
        "state_gdn_s": normal(ks[5], (N_GDN_LAYERS, DEC_BATCH, GDN_V_HEADS, GDN_DK, GDN_DV), GDN_DK ** -0.5),
        "state_gdn_conv": normal(ks[6], (N_GDN_LAYERS, DEC_BATCH, GDN_CONV_W - 1, GDN_CONV_CH), 1.0),
        "state_ffn_conv": normal(ks[7], (DEPTH, DEC_BATCH, FFN_CONV_W - 1, D_FF), 1.0),
        "page_table": page_table,
        "gdn_norm": gain(ks[8], (N_GDN_LAYERS, D_MODEL)),
        "gdn_w_in": normal(ks[9], (N_GDN_LAYERS, D_MODEL, GDN_PROJ), D_MODEL ** -0.5),
        "gdn_conv_w": normal(ks[10], (N_GDN_LAYERS, GDN_CONV_W, GDN_CONV_CH), GDN_CONV_W ** -0.5),
        "gdn_a_log": jnp.log(jax.random.uniform(ks[11], (N_GDN_LAYERS, GDN_V_HEADS), f32, 1.0, 16.0)),
        "gdn_dt_bias": normal(ks[12], (N_GDN_LAYERS, GDN_V_HEADS), 0.1),
        "gdn_out_norm": gain(ks[13], (N_GDN_LAYERS, GDN_DV)),
        "gdn_w_out": normal(ks[14], (N_GDN_LAYERS, GDN_V_DIM, D_MODEL), GDN_V_DIM ** -0.5),
        "nsa_norm": gain(ks[15], (N_NSA_LAYERS, D_MODEL)),
        "nsa_w_in": normal(ks[16], (N_NSA_LAYERS, D_MODEL, NSA_PROJ), D_MODEL ** -0.5),
        "nsa_cmp_pe": normal(ks[17], (N_NSA_LAYERS, 2, CMP_LEN, NSA_DH), 0.1),
        "nsa_cmp_w1": normal(ks[18], (N_NSA_LAYERS, 2, CMP_LEN, NSA_DH, CMP_HIDDEN), (CMP_LEN * NSA_DH) ** -0.5),
        "nsa_cmp_w2": normal(ks[19], (N_NSA_LAYERS, 2, CMP_HIDDEN, NSA_DH), CMP_HIDDEN ** -0.5),
        "nsa_w_out": normal(ks[20], (N_NSA_LAYERS, NSA_Q_DIM, D_MODEL), NSA_Q_DIM ** -0.5),
        "ffn_norm": gain(ks[21], (DEPTH, D_MODEL)),
        "ffn_w_up": normal(ks[22], (DEPTH, D_MODEL, 2 * D_FF), D_MODEL ** -0.5),
        "ffn_conv_w": normal(ks[23], (DEPTH, FFN_CONV_W, D_FF), FFN_CONV_W ** -0.5),
        "ffn_w_down": normal(ks[24], (DEPTH, D_FF, D_MODEL), D_FF ** -0.5),
        "final_norm": gain(ks[25], (D_MODEL,)),
    }


def reference(x_prompt, x_sample, cache_nsa_kv, cache_nsa_win, state_gdn_s, state_gdn_conv, state_ffn_conv,
              page_table, gdn_norm, gdn_w_in, gdn_conv_w, gdn_a_log, gdn_dt_bias, gdn_out_norm, gdn_w_out,
              nsa_norm, nsa_w_in, nsa_cmp_pe, nsa_cmp_w1, nsa_cmp_w2, nsa_w_out,
              ffn_norm, ffn_w_up, ffn_conv_w, ffn_w_down, final_norm):
    weights = (gdn_norm, gdn_w_in, gdn_conv_w, gdn_a_log, gdn_dt_bias, gdn_out_norm, gdn_w_out,
               nsa_norm, nsa_w_in, nsa_cmp_pe, nsa_cmp_w1, nsa_cmp_w2, nsa_w_out,
               ffn_norm, ffn_w_up, ffn_conv_w, ffn_w_down, final_norm)
    dt = x_prompt.dtype
    bp = x_prompt.shape[0]
    n_gdn = state_gdn_s.shape[0]
    n_nsa = cache_nsa_kv.shape[0]

    (y_prompt, kv_rows_prompt, win_prompt, gdn_s_prompt, gdn_conv_prompt, ffn_conv_prompt) = trunk(
        x_prompt,
        lambda j: jnp.zeros((bp, 0, 4, NSA_KV_GROUPS, NSA_DH), dt),
        jnp.zeros((n_nsa, bp, 0, 2, NSA_KV_GROUPS, NSA_DH), dt),
        jnp.zeros((n_gdn, bp, GDN_V_HEADS, GDN_DK, GDN_DV), dt),
        jnp.zeros((n_gdn, bp, GDN_CONV_W - 1, GDN_CONV_CH), dt),
        jnp.zeros((DEPTH, bp, FFN_CONV_W - 1, D_FF), dt),
        *weights)

    n_seq = page_table.shape[0]

    def past_sample(j):
        return cache_nsa_kv[j][page_table].reshape(n_seq, -1, 4, NSA_KV_GROUPS, NSA_DH)

    (y_sample, kv_rows_sample, win_sample, gdn_s_sample, gdn_conv_sample, ffn_conv_sample) = trunk(
        x_sample, past_sample, cache_nsa_win, state_gdn_s, state_gdn_conv, state_ffn_conv, *weights)

    return (y_prompt, y_sample, kv_rows_prompt, kv_rows_sample, win_prompt, win_sample,
            gdn_s_prompt, gdn_s_sample, gdn_conv_prompt, gdn_conv_sample, ffn_conv_prompt, ffn_conv_sample)
```

```python
import functools

import jax
import jax.numpy as jnp
from jax import lax
from jax.experimental import pallas as pl
from jax.experimental.pallas import tpu as pltpu

F32 = jnp.float32
BF16 = jnp.bfloat16

RMS_EPS = 1e-6
L2_EPS = 1e-6
NEG_INF = -1e30
FORCE_SCORE = 1e9
NEVER = -3e38

GDN_QK_HEADS = 8
GDN_V_HEADS = 16
GDN_DK = 128
GDN_DV = 128
GDN_QK_DIM = GDN_QK_HEADS * GDN_DK
GDN_V_DIM = GDN_V_HEADS * GDN_DV
GDN_CONV_CH = 2 * GDN_QK_DIM + GDN_V_DIM
GDN_CONV_W = 4
GDN_CHUNK = 64
GDN_HEADS_PER_STEP = 4
NSA_HEADS = 16
NSA_G = 4
NSA_R = 4
NSA_DH = 64
NSA_Q_DIM = NSA_HEADS * NSA_DH
NSA_KV_DIM = NSA_G * NSA_DH
CMP_STRIDE = 16
CMP_LEN = 32
CMP_HIDDEN = 256
SEL_BLOCK = 64
SEL_TOPN = 8
WINDOW = 512
NSA_Q_BLOCK = 64
KEY_CHUNK = 128
D_FF = 2816
FFN_CONV_W = 3
FFN_TC = 256

VMEM_LIMIT = 52 * 1024 * 1024


def _params(*sem):
    return pltpu.CompilerParams(dimension_semantics=sem, vmem_limit_bytes=VMEM_LIMIT)


def _mm(a, b):
    return jnp.dot(a.astype(BF16), b.astype(BF16), preferred_element_type=F32)


def _mm_nt(a, b):
    return lax.dot_general(a.astype(BF16), b.astype(BF16), (((1,), (1,)), ((), ())),
                           preferred_element_type=F32)


def _mm_tn(a, b):
    return lax.dot_general(a.astype(BF16), b.astype(BF16), (((0,), (0,)), ((), ())),
                           preferred_element_type=F32)


def _sigmoid(x):
    return 1.0 / (1.0 + jnp.exp(-x))


def _silu(x):
    return x * _sigmoid(x)


def _rms_matmul_body(x_ref, nw_ref, w_ref, o_ref, xn_ref):
    @pl.when(pl.program_id(1) == 0)
    def _():
        x = x_ref[...]
        inv = lax.rsqrt(jnp.mean(x * x, axis=-1, keepdims=True) + RMS_EPS)
        xn_ref[...] = ((x * inv) * nw_ref[...]).astype(BF16)

    o_ref[...] = jnp.dot(xn_ref[...], w_ref[...], preferred_element_type=F32)


def _rms_matmul(x, nw, w, tn):
    m, k = x.shape
    n = w.shape[1]
    tm = min(m, 1024)
    return pl.pallas_call(
        _rms_matmul_body,
        grid=(m // tm, n // tn),
        in_specs=[pl.BlockSpec((tm, k), lambda i, j: (i, 0)),
                  pl.BlockSpec((1, k), lambda i, j: (0, 0)),
                  pl.BlockSpec((k, tn), lambda i, j: (0, j))],
        out_specs=pl.BlockSpec((tm, tn), lambda i, j: (i, j)),
        out_shape=jax.ShapeDtypeStruct((m, n), F32),
        scratch_shapes=[pltpu.VMEM((tm, k), BF16)],
        compiler_params=_params("parallel", "arbitrary"),
        name="rms_matmul",
    )(x, nw.reshape(1, k), w)


def _matmul_res_body(a_ref, w_ref, r_ref, o_ref):
    o_ref[...] = r_ref[...] + jnp.dot(a_ref[...].astype(BF16), w_ref[...], preferred_element_type=F32)


def _matmul_res(a, w, res):
    m, k = a.shape
    n = w.shape[1]
    tm = min(m, 512)
    return pl.pallas_call(
        _matmul_res_body,
        grid=(m // tm,),
        in_specs=[pl.BlockSpec((tm, k), lambda i: (i, 0)),
                  pl.BlockSpec((k, n), lambda i: (0, 0)),
                  pl.BlockSpec((tm, n), lambda i: (i, 0))],
        out_specs=pl.BlockSpec((tm, n), lambda i: (i, 0)),
        out_shape=jax.ShapeDtypeStruct((m, n), F32),
        compiler_params=_params("parallel"),
        name="matmul_res",
    )(a, w, res)


def _rms_body(x_ref, nw_ref, o_ref):
    x = x_ref[...]
    inv = lax.rsqrt(jnp.mean(x * x, axis=-1, keepdims=True) + RMS_EPS)
    o_ref[...] = (x * inv) * nw_ref[...]


def _rms(x, nw):
    m, k = x.shape
    tm = min(m, 1024)
    return pl.pallas_call(
        _rms_body,
        grid=(m // tm,),
        in_specs=[pl.BlockSpec((tm, k), lambda i: (i, 0)), pl.BlockSpec((1, k), lambda i: (0, 0))],
        out_specs=pl.BlockSpec((tm, k), lambda i: (i, 0)),
        out_shape=jax.ShapeDtypeStruct((m, k), F32),
        compiler_params=_params("parallel"),
        name="final_rms",
    )(x, nw.reshape(1, k))


def _shifted(x, prev_rows, shift, row):
    nb = prev_rows.shape[0]
    y = pltpu.roll(x, shift, 0)
    for r in range(shift):
        y = jnp.where(row == r, prev_rows[nb - shift + r:nb - shift + r + 1], y)
    return y


def _ffn_act_body(a_ref, g_ref, buf_ref, cw_ref, act_ref, nb_ref):
    a = a_ref[0]
    t = a.shape[0]
    buf = buf_ref[0]
    w = cw_ref[...]
    row = lax.broadcasted_iota(jnp.int32, a.shape, 0)
    y = _shifted(a, buf, 2, row) * w[0:1] + _shifted(a, buf, 1, row) * w[1:2] + a * w[2:3]
    act_ref[0] = (_silu(y) * g_ref[0]).astype(act_ref.dtype)
    nb_ref[0] = a_ref[0, t - (FFN_CONV_W - 1):t, :]


def _ffn_act(h, buf, cw):
    b, t, _ = h.shape
    nj = D_FF // FFN_TC
    return pl.pallas_call(
        _ffn_act_body,
        grid=(b, nj),
        in_specs=[pl.BlockSpec((1, t, FFN_TC), lambda i, j: (i, 0, j)),
                  pl.BlockSpec((1, t, FFN_TC), lambda i, j: (i, 0, j + nj)),
                  pl.BlockSpec((1, FFN_CONV_W - 1, FFN_TC), lambda i, j: (i, 0, j)),
                  pl.BlockSpec((FFN_CONV_W, FFN_TC), lambda i, j: (0, j))],
        out_specs=[pl.BlockSpec((1, t, FFN_TC), lambda i, j: (i, 0, j)),
                   pl.BlockSpec((1, FFN_CONV_W - 1, FFN_TC), lambda i, j: (i, 0, j))],
        out_shape=[jax.ShapeDtypeStruct((b, t, D_FF), BF16),
                   jax.ShapeDtypeStruct((b, FFN_CONV_W - 1, D_FF), F32)],
        compiler_params=_params("parallel", "parallel"),
        name="ffn_act",
    )(h, h, buf, cw)


GDN_PRE_TC = 512


def _gdn_pre_body(x_ref, buf_ref, cw_ref, o_ref, nb_ref):
    j = pl.program_id(1)
    x = x_ref[0]
    t = x.shape[0]
    buf = buf_ref[0]
    w = cw_ref[...]
    row = lax.broadcasted_iota(jnp.int32, x.shape, 0)
    y = (_shifted(x, buf, 3, row) * w[0:1] + _shifted(x, buf, 2, row) * w[1:2]
         + _shifted(x, buf, 1, row) * w[2:3] + x * w[3:4])
    y = _silu(y)
    is_q = j < GDN_QK_DIM // GDN_PRE_TC
    is_v = j >= 2 * GDN_QK_DIM // GDN_PRE_TC
    qscale = jnp.where(is_q, GDN_DK ** -0.5, 1.0).astype(F32)
    for h in range(GDN_PRE_TC // GDN_DK):
        yh = y[:, h * GDN_DK:(h + 1) * GDN_DK]
        inv = lax.rsqrt(jnp.sum(yh * yh, axis=-1, keepdims=True) + L2_EPS)
        o_ref[0, :, h * GDN_DK:(h + 1) * GDN_DK] = jnp.where(is_v, yh, (yh * inv) * qscale)
    nb_ref[0] = x_ref[0, t - (GDN_CONV_W - 1):t, :]


def _gdn_pre(proj, buf, cw):
    b, t, _ = proj.shape
    nj = GDN_CONV_CH // GDN_PRE_TC
    return pl.pallas_call(
        _gdn_pre_body,
        grid=(b, nj),
        in_specs=[pl.BlockSpec((1, t, GDN_PRE_TC), lambda i, j: (i, 0, j)),
                  pl.BlockSpec((1, GDN_CONV_W - 1, GDN_PRE_TC), lambda i, j: (i, 0, j)),
                  pl.BlockSpec((GDN_CONV_W, GDN_PRE_TC), lambda i, j: (0, j))],
        out_specs=[pl.BlockSpec((1, t, GDN_PRE_TC), lambda i, j: (i, 0, j)),
                   pl.BlockSpec((1, GDN_CONV_W - 1, GDN_PRE_TC), lambda i, j: (i, 0, j))],
        out_shape=[jax.ShapeDtypeStruct((b, t, GDN_CONV_CH), F32),
                   jax.ShapeDtypeStruct((b, GDN_CONV_W - 1, GDN_CONV_CH), F32)],
        compiler_params=_params("parallel", "parallel"),
        name="gdn_pre",
    )(proj, buf, cw)


def _gdn_gate_body(x_ref, alog_ref, dtb_ref, g_ref, gcum_ref, beta_ref, *, chunk):
    x = x_ref[0]
    z = x + dtb_ref[...]
    softplus = jnp.maximum(z, 0.0) + jnp.log(1.0 + jnp.exp(-jnp.abs(z)))
    g = -jnp.exp(alog_ref[...]) * softplus
    g_ref[0] = g
    row = lax.broadcasted_iota(jnp.int32, x.shape, 0) % chunk
    acc = g
    s = 1
    while s < chunk:
        acc = acc + jnp.where(row >= s, pltpu.roll(acc, s, 0), 0.0)
        s *= 2
    gcum_ref[0] = acc
    beta_ref[0] = _sigmoid(x)


def _gdn_gate(proj, alog_pad, dtb_pad, lane_block):
    b, t, _ = proj.shape
    chunk = min(GDN_CHUNK, t)
    spec = pl.BlockSpec((1, t, 128), lambda i: (i, 0, 0))
    return pl.pallas_call(
        functools.partial(_gdn_gate_body, chunk=chunk),
        grid=(b,),
        in_specs=[pl.BlockSpec((1, t, 128), lambda i: (i, 0, lane_block)),
                  pl.BlockSpec((1, 128), lambda i: (0, 0)),
                  pl.BlockSpec((1, 128), lambda i: (0, 0))],
        out_specs=[spec, spec, spec],
        out_shape=[jax.ShapeDtypeStruct((b, t, 128), F32)] * 3,
        compiler_params=_params("parallel"),
        name="gdn_gate",
    )(proj, alog_pad, dtb_pad)


def _unit_lower_inverse(low, n):
    eye = (lax.broadcasted_iota(jnp.int32, (n, n), 0) == lax.broadcasted_iota(jnp.int32, (n, n), 1)).astype(F32)
    p = eye - low
    m = _mm(low, low)
    k = 2
    while True:
        p = p + _mm(p, m)
        k *= 2
        if k >= n:
            break
        m = _mm(m, m)
    return p


def _gdn_scan_body(q_ref, k_ref, v_ref, z_ref, gc_ref, bc_ref, gr_ref, s0_ref, onw_ref, o_ref, s_ref):
    c = GDN_CHUNK
    hb = GDN_HEADS_PER_STEP
    n_chunks = q_ref.shape[1] // c
    s_ref[...] = s0_ref[...]
    ri = lax.broadcasted_iota(jnp.int32, (c, c), 0)
    ci = lax.broadcasted_iota(jnp.int32, (c, c), 1)
    tri = ri >= ci
    stri = ri > ci
    onw = onw_ref[...]

    def body(n, carry):
        r0 = pl.multiple_of(n * c, c)
        gcol = gc_ref[0, 0, pl.ds(r0, c), :]
        bcol = bc_ref[0, 0, pl.ds(r0, c), :]
        grow = gr_ref[0, 0, n]
        for hh in range(hb):
            qh = hh // 2
            q = q_ref[0, pl.ds(r0, c), qh * GDN_DK:(qh + 1) * GDN_DK]
            k = k_ref[0, pl.ds(r0, c), qh * GDN_DK:(qh + 1) * GDN_DK]
            v = v_ref[0, pl.ds(r0, c), hh * GDN_DV:(hh + 1) * GDN_DV]
            z = z_ref[0, pl.ds(r0, c), hh * GDN_DV:(hh + 1) * GDN_DV]
            gc = gcol[:, hh:hh + 1]
            gr = grow[hh:hh + 1, :]
            bc = bcol[:, hh:hh + 1]
            s = s_ref[0, hh]
            decay = jnp.where(tri, jnp.exp(jnp.where(tri, gc - gr, 0.0)), 0.0)
            kb = k * bc
            low = jnp.where(stri, _mm_nt(kb, k) * decay, 0.0)
            tinv = _unit_lower_inverse(low, c)
            eg = jnp.exp(gc)
            u = _mm(tinv, v * bc)
            w = _mm(tinv, kb * eg)
            qk = jnp.where(tri, _mm_nt(q, k) * decay, 0.0)
            g_last = gc[c - 1:c, :]
            k_dec = k * jnp.exp(g_last - gc)
            v_new = u - _mm(w, s)
            o = _mm(q * eg, s) + _mm(qk, v_new)
            s_ref[0, hh] = s * jnp.exp(g_last) + _mm_tn(k_dec, v_new)
            inv = lax.rsqrt(jnp.mean(o * o, axis=-1, keepdims=True) + RMS_EPS)
            o_ref[0, pl.ds(r0, c), hh * GDN_DV:(hh + 1) * GDN_DV] = (
                ((o * inv) * onw) * _silu(z)).astype(o_ref.dtype)
        return carry

    lax.fori_loop(0, n_chunks, body, 0)


def _gdn_scan(qkv, proj, gcol, bcol, grow, s0, onw):
    b, t, _ = qkv.shape
    hb = GDN_HEADS_PER_STEP
    hg = GDN_V_HEADS // hb
    qw = hb // 2 * GDN_DK
    vw = hb * GDN_DV
    n = t // GDN_CHUNK
    return pl.pallas_call(
        _gdn_scan_body,
        grid=(b, hg),
        in_specs=[pl.BlockSpec((1, t, qw), lambda i, j: (i, 0, j)),
                  pl.BlockSpec((1, t, qw), lambda i, j: (i, 0, GDN_QK_DIM // qw + j)),
                  pl.BlockSpec((1, t, vw), lambda i, j: (i, 0, 2 * GDN_QK_DIM // vw + j)),
                  pl.BlockSpec((1, t, vw), lambda i, j: (i, 0, GDN_CONV_CH // vw + j)),
                  pl.BlockSpec((1, 1, t, hb), lambda i, j: (i, j, 0, 0)),
                  pl.BlockSpec((1, 1, t, hb), lambda i, j: (i, j, 0, 0)),
                  pl.BlockSpec((1, 1, n, hb, GDN_CHUNK), lambda i, j: (i, j, 0, 0, 0)),
                  pl.BlockSpec((1, hb, GDN_DK, GDN_DV), lambda i, j: (i, j, 0, 0)),
                  pl.BlockSpec((1, GDN_DV), lambda i, j: (0, 0))],
        out_specs=[pl.BlockSpec((1, t, vw), lambda i, j: (i, 0, j)),
                   pl.BlockSpec((1, hb, GDN_DK, GDN_DV), lambda i, j: (i, j, 0, 0))],
        out_shape=[jax.ShapeDtypeStruct((b, t, GDN_V_DIM), BF16),
                   jax.ShapeDtypeStruct((b, GDN_V_HEADS, GDN_DK, GDN_DV), F32)],
        compiler_params=_params("parallel", "parallel"),
        name="gdn_scan",
    )(qkv, qkv, qkv, proj, gcol, bcol, grow, s0, onw.reshape(1, GDN_DV))


def _gdn_layer(x, s0, conv_buf, norm_w, w_in, conv_w, alog_pad, dtb_pad, out_norm_w, w_out):
    b, t, d = x.shape
    xf = x.reshape(b * t, d)
    proj = _rms_matmul(xf, norm_w, w_in, 896).reshape(b, t, -1)
    qkv, new_buf = _gdn_pre(proj, conv_buf, conv_w)
    tail_block = (GDN_CONV_CH + GDN_V_DIM) // 128
    _, gcum, beta = _gdn_gate(proj, alog_pad, dtb_pad, tail_block)
    gcum = gcum[:, :, :GDN_V_HEADS]
    beta = beta[:, :, GDN_V_HEADS:2 * GDN_V_HEADS]
    tp = -(-t // GDN_CHUNK) * GDN_CHUNK
    if tp != t:
        pad = [(0, 0), (0, tp - t), (0, 0)]
        qkv = jnp.pad(qkv, pad)
        proj_z = jnp.pad(proj, pad)
        gcum = jnp.pad(gcum, pad, mode="edge")
        beta = jnp.pad(beta, pad)
    else:
        proj_z = proj
    hb = GDN_HEADS_PER_STEP
    hg = GDN_V_HEADS // hb
    n = tp // GDN_CHUNK
    gcol = gcum.reshape(b, tp, hg, hb).transpose(0, 2, 1, 3)
    bcol = beta.reshape(b, tp, hg, hb).transpose(0, 2, 1, 3)
    grow = gcum.reshape(b, n, GDN_CHUNK, hg, hb).transpose(0, 3, 1, 4, 2)
    o, s_new = _gdn_scan(qkv, proj_z, gcol, bcol, grow, s0, out_norm_w)
    o = o[:, :t].reshape(b * t, GDN_V_DIM)
    y = _matmul_res(o, w_out, xf).reshape(b, t, d)
    return y, s_new, new_buf


def _page_gather_body(pt_ref, page_ref, new_ref, o_ref, *, n_pages):
    p = pl.program_id(1)
    n_cg = o_ref.shape[1]

    @pl.when(p < n_pages)
    def _():
        for cg in range(n_cg):
            o_ref[0, cg] = page_ref[0, 0, :, cg * NSA_DH:(cg + 1) * NSA_DH].astype(o_ref.dtype)

    @pl.when(p == n_pages)
    def _():
        for cg in range(n_cg):
            o_ref[0, cg] = new_ref[0, :, cg * NSA_DH:(cg + 1) * NSA_DH].astype(o_ref.dtype)


def _page_gather(cache4, layer, page_table, new_rows_pad):
    b, n_pages = page_table.shape
    page = cache4.shape[2]
    n_cg = cache4.shape[3] // NSA_DH
    grid_spec = pltpu.PrefetchScalarGridSpec(
        num_scalar_prefetch=1,
        grid=(b, n_pages + 1),
        in_specs=[pl.BlockSpec((1, 1, page, n_cg * NSA_DH),
                               lambda i, p, pt: (layer, pt[i, jnp.minimum(p, n_pages - 1)], 0, 0)),
                  pl.BlockSpec((1, page, n_cg * NSA_DH), lambda i, p, pt: (i, 0, 0))],
        out_specs=pl.BlockSpec((1, n_cg, page, NSA_DH), lambda i, p, pt: (i, 0, p, 0)),
    )
    return pl.pallas_call(
        functools.partial(_page_gather_body, n_pages=n_pages),
        grid_spec=grid_spec,
        out_shape=jax.ShapeDtypeStruct((b, n_cg, (n_pages + 1) * page, NSA_DH), BF16),
        compiler_params=_params("parallel", "arbitrary"),
        name="page_gather",
    )(page_table, cache4, new_rows_pad)


def _compress_body(x_ref, w1_ref, pe_ref, w2_ref, o_ref, *, n_sub):
    hd = CMP_HIDDEN
    acc = jnp.dot(x_ref[0, 0], w1_ref[0], preferred_element_type=F32)
    pe = pe_ref[0]
    pe0 = jnp.broadcast_to(pe[0:1], (8, pe.shape[1]))
    pe1 = jnp.broadcast_to(pe[1:2], (8, pe.shape[1]))
    b0 = jnp.dot(pe0.astype(BF16), w1_ref[0, :, :hd], preferred_element_type=F32)[0:1]
    b1 = jnp.dot(pe1.astype(BF16), w1_ref[0, :, hd:], preferred_element_type=F32)[0:1]
    first = acc[:, :hd] + b0
    second = acc[:, hd:] + b1
    hid = _silu(first + pltpu.roll(second, n_sub - 1, 0))
    o_ref[0, 0] = jnp.dot(hid.astype(BF16), w2_ref[0], preferred_element_type=F32)


def _compress(kv_hm, n_sub, w1cat, pe_flat, w2):
    b, n_cg, rows, dh = kv_hm.shape
    flat = CMP_STRIDE * dh
    x = kv_hm.reshape(b, n_cg, rows // CMP_STRIDE, flat)
    return pl.pallas_call(
        functools.partial(_compress_body, n_sub=n_sub),
        grid=(b, 2 * NSA_G),
        in_specs=[pl.BlockSpec((1, 1, n_sub, flat), lambda i, j: (i, j, 0, 0)),
                  pl.BlockSpec((1, flat, 2 * CMP_HIDDEN), lambda i, j: (j // NSA_G, 0, 0)),
                  pl.BlockSpec((1, 2, flat), lambda i, j: (j // NSA_G, 0, 0)),
                  pl.BlockSpec((1, CMP_HIDDEN, dh), lambda i, j: (j // NSA_G, 0, 0))],
        out_specs=pl.BlockSpec((1, 1, n_sub, dh), lambda i, j: (i, j, 0, 0)),
        out_shape=jax.ShapeDtypeStruct((b, 2 * NSA_G, n_sub, dh), F32),
        compiler_params=_params("parallel", "parallel"),
        name="nsa_compress",
    )(x, w1cat, pe_flat, w2)


def _finish(acc, l):
    inv = jnp.where(l > 0.0, 1.0 / jnp.where(l > 0.0, l, 1.0), 0.0)
    return acc * inv


def _nsa_attn_body(q_ref, gate_ref, slope_ref, ck_ref, cv_ref, ks_ref, vs_ref, kw_ref, vw_ref, o_ref,
                   *, qb, p_len, n_cmp, n_blk, w0):
    i = pl.program_id(2)
    r = NSA_R
    rows = r * qb
    scale = NSA_DH ** -0.5
    kc = KEY_CHUNK
    q = q_ref[0, 0].reshape(rows, NSA_DH).astype(BF16)
    slopes = slope_ref[0]
    q0 = p_len + i * qb
    t_pos = q0 + lax.broadcasted_iota(jnp.int32, (1, qb, 1), 1)

    nc = ck_ref.shape[2]
    n_idx = lax.broadcasted_iota(jnp.int32, (1, 1, nc), 2)
    dist = t_pos - (n_idx * CMP_STRIDE + (CMP_LEN - 1))
    valid = (dist >= 0) & (n_idx < n_cmp)
    s = (_mm_nt(q, ck_ref[0, 0]) * scale).reshape(r, qb, nc) - slopes * dist.astype(F32)
    s = jnp.where(valid, s, NEG_INF)
    m = jnp.max(s, axis=-1, keepdims=True)
    p = jnp.where(valid, jnp.exp(s - m), 0.0)
    l = jnp.sum(p, axis=-1, keepdims=True)
    p = _finish(p, l)
    o_cmp = _mm(p.reshape(rows, nc), cv_ref[0, 0])

    nbp = -(-n_blk // 128) * 128
    nn = lax.broadcasted_iota(jnp.int32, (nc, nbp), 0)
    jj = lax.broadcasted_iota(jnp.int32, (nc, nbp), 1)
    overlap = ((nn * CMP_STRIDE < (jj + 1) * SEL_BLOCK) & (nn * CMP_STRIDE + CMP_LEN > jj * SEL_BLOCK)
               & (nn < n_cmp) & (jj < n_blk)).astype(F32)
    imp = jnp.dot(jnp.sum(p, axis=0), overlap, preferred_element_type=F32, precision=lax.Precision.HIGHEST)
    j = lax.broadcasted_iota(jnp.int32, (qb, nbp), 1)
    cur = (q0 + lax.broadcasted_iota(jnp.int32, (qb, 1), 0)) // SEL_BLOCK
    forced = (j == 0) | (j == cur) | (j == cur - 1)
    imp = jnp.where(forced, FORCE_SCORE, imp)
    imp = jnp.where(j > cur, -FORCE_SCORE, imp)
    imp = jnp.where(j >= n_blk, NEVER, imp)
    picks = []
    for _ in range(SEL_TOPN):
        best = jnp.max(imp, axis=-1, keepdims=True)
        idx = jnp.min(jnp.where(imp == best, j, nbp), axis=-1, keepdims=True)
        picks.append(idx.reshape(1, qb, 1))
        imp = jnp.where(j == idx, NEVER, imp)

    def attend(k_ref, v_ref, lo, hi, key0, mask_fn):
        def body(c, carry):
            m_i, l_i, acc = carry
            r0 = pl.multiple_of(c * kc, kc)
            k = k_ref[0, 0, pl.ds(r0, kc), :]
            v = v_ref[0, 0, pl.ds(r0, kc), :]
            kp = key0 + c * kc + lax.broadcasted_iota(jnp.int32, (1, 1, kc), 2)
            dist = t_pos - kp
            ok = mask_fn(kp, dist)
            s = (_mm_nt(q, k) * scale).reshape(r, qb, kc) - slopes * dist.astype(F32)
            s = jnp.where(ok, s, NEG_INF)
            m_new = jnp.maximum(m_i, jnp.max(s, axis=-1, keepdims=True))
            alpha = jnp.exp(m_i - m_new)
            p = jnp.where(ok, jnp.exp(s - m_new), 0.0)
            l_new = alpha * l_i + jnp.sum(p, axis=-1, keepdims=True)
            acc = acc * alpha.reshape(rows, 1) + _mm(p.reshape(rows, kc), v)
            return m_new, l_new, acc

        init = (jnp.full((r, qb, 1), NEG_INF, F32), jnp.zeros((r, qb, 1), F32), jnp.zeros((rows, NSA_DH), F32))
        _, l_f, acc = lax.fori_loop(lo, hi, body, init)
        return _finish(acc, l_f.reshape(rows, 1))

    def sel_mask(kp, dist):
        blk = kp // SEL_BLOCK
        hit = picks[0] == blk
        for idx in picks[1:]:
            hit = hit | (idx == blk)
        return hit & (dist >= 0)

    n_sel = ks_ref.shape[2] // kc
    hi_sel = jnp.minimum((q0 + qb - 1) // kc + 1, n_sel)
    o_sel = attend(ks_ref, vs_ref, 0, hi_sel, 0, sel_mask)

    def win_mask(kp, dist):
        return (dist >= 0) & (dist < WINDOW) & (kp >= 0)

    n_win = kw_ref.shape[2] // kc
    lo_win = jnp.maximum(q0 - (WINDOW - 1) - w0, 0) // kc
    hi_win = jnp.minimum((q0 + qb - 1 - w0) // kc + 1, n_win)
    o_win = attend(kw_ref, vw_ref, lo_win, hi_win, w0, win_mask)

    gates = _sigmoid(gate_ref[0, 0]).reshape(rows, 3)
    o = gates[:, 0:1] * o_cmp + gates[:, 1:2] * o_sel + gates[:, 2:3] * o_win
    o_ref[0, 0] = o.reshape(r, qb, NSA_DH)


def _nsa_attn(q_t, gates_t, slopes, cmp_kv, sel_arr, sel_k0, sel_v0, win_arr, win_k0, win_v0,
              *, p_len, n_cmp, n_blk, w0):
    b, g, r, tq, dh = q_t.shape
    qb = NSA_Q_BLOCK if tq % NSA_Q_BLOCK == 0 else tq
    nc = cmp_kv.shape[2]
    tk = sel_arr.shape[2]
    tw = win_arr.shape[2]
    body = functools.partial(_nsa_attn_body, qb=qb, p_len=p_len, n_cmp=n_cmp, n_blk=n_blk, w0=w0)
    return pl.pallas_call(
        body,
        grid=(b, g, tq // qb),
        in_specs=[pl.BlockSpec((1, 1, r, qb, dh), lambda bi, gi, i: (bi, gi, 0, i, 0)),
                  pl.BlockSpec((1, 1, r, qb, 3), lambda bi, gi, i: (bi, gi, 0, i, 0)),
                  pl.BlockSpec((1, r, 1, 1), lambda bi, gi, i: (gi, 0, 0, 0)),
                  pl.BlockSpec((1, 1, nc, dh), lambda bi, gi, i: (bi, gi, 0, 0)),
                  pl.BlockSpec((1, 1, nc, dh), lambda bi, gi, i: (bi, NSA_G + gi, 0, 0)),
                  pl.BlockSpec((1, 1, tk, dh), lambda bi, gi, i: (bi, sel_k0 + gi, 0, 0)),
                  pl.BlockSpec((1, 1, tk, dh), lambda bi, gi, i: (bi, sel_v0 + gi, 0, 0)),
                  pl.BlockSpec((1, 1, tw, dh), lambda bi, gi, i: (bi, win_k0 + gi, 0, 0)),
                  pl.BlockSpec((1, 1, tw, dh), lambda bi, gi, i: (bi, win_v0 + gi, 0, 0))],
        out_specs=pl.BlockSpec((1, 1, r, qb, dh), lambda bi, gi, i: (bi, gi, 0, i, 0)),
        out_shape=jax.ShapeDtypeStruct((b, g, r, tq, dh), F32),
        compiler_params=_params("parallel", "parallel", "arbitrary"),
        name="nsa_attn",
    )(q_t, gates_t, slopes, cmp_kv, cmp_kv, sel_arr, sel_arr, win_arr, win_arr)


def _nsa_layer(x, cache4, layer, page_table, win_buf, norm_w, w_in, w1cat, pe_flat, w2, w_out, slopes):
    b, t, d = x.shape
    g, r, dh = NSA_G, NSA_R, NSA_DH
    xf = x.reshape(b * t, d)
    proj = _rms_matmul(xf, norm_w, w_in, 896).reshape(b, t, -1)
    q_t = proj[..., :NSA_Q_DIM].reshape(b, t, g, r, dh).transpose(0, 2, 3, 1, 4)
    kv = proj[..., NSA_Q_DIM:NSA_Q_DIM + 6 * NSA_KV_DIM]
    gates_t = proj[..., NSA_Q_DIM + 6 * NSA_KV_DIM:NSA_Q_DIM + 6 * NSA_KV_DIM + 3 * NSA_HEADS]
    gates_t = gates_t.reshape(b, t, g, r, 3).transpose(0, 2, 3, 1, 4)
    kv6 = kv.reshape(b, t, 6, g, dh)
    new_rows = kv6[:, :, :4]
    if cache4 is None:
        p_len = 0
        tk = t
        kv_hm = kv6.astype(BF16).transpose(0, 2, 3, 1, 4).reshape(b, 6 * g, t, dh)
        sel_arr, sel_k0, sel_v0 = kv_hm, 2 * g, 3 * g
        win_arr, win_k0, win_v0 = kv_hm, 4 * g, 5 * g
        w0 = 0
        new_win = kv6[:, t - min(WINDOW, t):, 4:]
    else:
        n_pages = page_table.shape[1]
        page = cache4.shape[2]
        p_len = n_pages * page
        tk = p_len + t
        new_pad = jnp.pad(kv[..., :4 * NSA_KV_DIM], [(0, 0), (0, page - t), (0, 0)])
        kv_hm = _page_gather(cache4, layer, page_table, new_pad)
        sel_arr, sel_k0, sel_v0 = kv_hm, 2 * g, 3 * g
        win_all = jnp.concatenate([win_buf, kv6[:, :, 4:]], axis=1)
        wl = win_all.shape[1]
        wlp = -(-wl // KEY_CHUNK) * KEY_CHUNK
        win_arr = jnp.pad(win_all.astype(BF16), [(0, 0), (0, wlp - wl), (0, 0), (0, 0), (0, 0)])
        win_arr = win_arr.transpose(0, 2, 3, 1, 4).reshape(b, 2 * g, wlp, dh)
        win_k0, win_v0 = 0, g
        w0 = p_len - win_buf.shape[1]
        new_win = win_all[:, wl - min(WINDOW, tk):]
    n_sub = tk // CMP_STRIDE
    n_cmp = n_sub - 1
    n_blk = -(-tk // SEL_BLOCK)
    cmp_kv = _compress(kv_hm, n_sub, w1cat, pe_flat, w2)
    o_t = _nsa_attn(q_t, gates_t, slopes, cmp_kv, sel_arr, sel_k0, sel_v0, win_arr, win_k0, win_v0,
                    p_len=p_len, n_cmp=n_cmp, n_blk=n_blk, w0=w0)
    o = o_t.transpose(0, 3, 1, 2, 4).reshape(b * t, NSA_Q_DIM)
    y = _matmul_res(o, w_out, xf).reshape(b, t, d)
    return y, new_rows, new_win


def _ffn_layer(x, buf, norm_w, w_up, conv_w, w_down):
    b, t, d = x.shape
    xf = x.reshape(b * t, d)
    h = _rms_matmul(xf, norm_w, w_up, 512).reshape(b, t, -1)
    act, new_buf = _ffn_act(h, buf, conv_w)
    y = _matmul_res(act.reshape(b * t, D_FF), w_down, xf).reshape(b, t, d)
    return y, new_buf


def _pad_cols(w, n):
    return jnp.pad(w, [(0, 0)] * (w.ndim - 1) + [(0, n - w.shape[-1])])


def _trunk(x, cache4, page_table, nsa_win, gdn_state, gdn_conv, ffn_conv, wts):
    depth = wts["ffn_w_up"].shape[0]
    rows_l, win_l, s_l, gconv_l, fconv_l = [], [], [], [], []
    for i in range(depth):
        j = i // 2
        if i % 2 == 0:
            x, s_new, cb = _gdn_layer(x, gdn_state[j], gdn_conv[j], wts["gdn_norm"][j], wts["gdn_w_in"][j],
                                      wts["gdn_conv_w"][j], wts["gdn_a_log"][j], wts["gdn_dt_bias"][j],
                                      wts["gdn_out_norm"][j], wts["gdn_w_out"][j])
            s_l.append(s_new)
            gconv_l.append(cb)
        else:
            x, rows, wb = _nsa_layer(x, cache4, j, page_table, None if nsa_win is None else nsa_win[j],
                                     wts["nsa_norm"][j], wts["nsa_w_in"][j], wts["nsa_w1cat"][j],
                                     wts["nsa_pe_flat"][j], wts["nsa_w2"][j], wts["nsa_w_out"][j],
                                     wts["slopes"])
            rows_l.append(rows)
            win_l.append(wb)
        x, fb = _ffn_layer(x, ffn_conv[i], wts["ffn_norm"][i], wts["ffn_w_up"][i], wts["ffn_conv_w"][i],
                           wts["ffn_w_down"][i])
        fconv_l.append(fb)
    b, t, d = x.shape
    y = _rms(x.reshape(b * t, d), wts["final_norm"]).reshape(b, t, d)
    return (y, jnp.stack(rows_l), jnp.stack(win_l), jnp.stack(s_l), jnp.stack(gconv_l), jnp.stack(fconv_l))


def kernel(x_prompt, x_sample, cache_nsa_kv, cache_nsa_win, state_gdn_s, state_gdn_conv, state_ffn_conv,
           page_table, gdn_norm, gdn_w_in, gdn_conv_w, gdn_a_log, gdn_dt_bias, gdn_out_norm, gdn_w_out,
           nsa_norm, nsa_w_in, nsa_cmp_pe, nsa_cmp_w1, nsa_cmp_w2, nsa_w_out,
           ffn_norm, ffn_w_up, ffn_conv_w, ffn_w_down, final_norm):
    n_gdn = gdn_w_in.shape[0]
    n_nsa = nsa_w_in.shape[0]
    depth = ffn_w_up.shape[0]
    bp = x_prompt.shape[0]
    gdn_cols = -(-gdn_w_in.shape[2] // 896) * 896
    nsa_cols = -(-nsa_w_in.shape[2] // 896) * 896
    w1 = nsa_cmp_w1.reshape(n_nsa, 2, 2, CMP_STRIDE * NSA_DH, CMP_HIDDEN)
    w1cat = jnp.concatenate([w1[:, :, 0], w1[:, :, 1]], axis=-1).astype(BF16)
    slopes = 2.0 ** (-8.0 * jnp.arange(1, NSA_HEADS + 1, dtype=F32) / NSA_HEADS)
    wts = {
        "gdn_norm": gdn_norm,
        "gdn_w_in": _pad_cols(gdn_w_in, gdn_cols).astype(BF16),
        "gdn_conv_w": gdn_conv_w,
        "gdn_a_log": _pad_cols(gdn_a_log, 128).reshape(n_gdn, 1, 128),
        "gdn_dt_bias": _pad_cols(gdn_dt_bias, 128).reshape(n_gdn, 1, 128),
        "gdn_out_norm": gdn_out_norm,
        "gdn_w_out": gdn_w_out.astype(BF16),
        "nsa_norm": nsa_norm,
        "nsa_w_in": _pad_cols(nsa_w_in, nsa_cols).astype(BF16),
        "nsa_w1cat": w1cat,
        "nsa_pe_flat": nsa_cmp_pe.reshape(n_nsa, 2, 2, CMP_STRIDE * NSA_DH),
        "nsa_w2": nsa_cmp_w2.astype(BF16),
        "nsa_w_out": nsa_w_out.astype(BF16),
        "slopes": slopes.reshape(NSA_G, NSA_R, 1, 1),
        "ffn_norm": ffn_norm,
        "ffn_w_up": ffn_w_up.astype(BF16),
        "ffn_conv_w": ffn_conv_w,
        "ffn_w_down": ffn_w_down.astype(BF16),
        "final_norm": final_norm,
    }
    cache4 = cache_nsa_kv.reshape(cache_nsa_kv.shape[0], cache_nsa_kv.shape[1], cache_nsa_kv.shape[2], -1)

    zeros = functools.partial(jnp.zeros, dtype=F32)
    prompt = _trunk(x_prompt, None, None, None,
                    zeros((n_gdn, bp, GDN_V_HEADS, GDN_DK, GDN_DV)),
                    zeros((n_gdn, bp, GDN_CONV_W - 1, GDN_CONV_CH)),
                    zeros((depth, bp, FFN_CONV_W - 1, D_FF)), wts)
    sample = _trunk(x_sample, cache4, page_table, cache_nsa_win, state_gdn_s, state_gdn_conv, state_ffn_conv, wts)
    out = []
    for p, s in zip(prompt, sample):
        out.extend([p, s])
    return tuple(out)
```

```python
import functools

import jax
import jax.numpy as jnp
from jax import lax
from jax.experimental import pallas as pl
from jax.experimental.pallas import tpu as pltpu

F32 = jnp.float32
BF16 = jnp.bfloat16

RMS_EPS = 1e-6
L2_EPS = 1e-6
NEG_INF = -1e30
FORCE_SCORE = 1e9
NEVER = -3e38

GDN_QK_HEADS = 8
GDN_V_HEADS = 16
GDN_DK = 128
GDN_DV = 128
GDN_QK_DIM = GDN_QK_HEADS * GDN_DK
GDN_V_DIM = GDN_V_HEADS * GDN_DV
GDN_CONV_CH = 2 * GDN_QK_DIM + GDN_V_DIM
GDN_CONV_W = 4
GDN_CHUNK = 64
GDN_HEADS_PER_STEP = 4
NSA_HEADS = 16
NSA_G = 4
NSA_R = 4
NSA_DH = 64
NSA_Q_DIM = NSA_HEADS * NSA_DH
NSA_KV_DIM = NSA_G * NSA_DH
CMP_STRIDE = 16
CMP_LEN = 32
CMP_HIDDEN = 256
SEL_BLOCK = 64
SEL_TOPN = 8
WINDOW = 512
NSA_Q_BLOCK = 64
KEY_CHUNK = 128
NSA_SEL_CLASS = 512
D_FF = 2816
FFN_CONV_W = 3
FFN_TC = 256

VMEM_LIMIT = 52 * 1024 * 1024


def _params(*sem):
    return pltpu.CompilerParams(dimension_semantics=sem, vmem_limit_bytes=VMEM_LIMIT)


def _mm(a, b):
    return jnp.dot(a.astype(BF16), b.astype(BF16), preferred_element_type=F32)


def _mm_nt(a, b):
    return lax.dot_general(a.astype(BF16), b.astype(BF16), (((1,), (1,)), ((), ())),
                           preferred_element_type=F32)


def _mm_tn(a, b):
    return lax.dot_general(a.astype(BF16), b.astype(BF16), (((0,), (0,)), ((), ())),
                           preferred_element_type=F32)


def _sigmoid(x):
    return 1.0 / (1.0 + jnp.exp(-x))


def _silu(x):
    return x * _sigmoid(x)


def _rms_matmul_body(x_ref, nw_ref, w_ref, o_ref, xn_ref):
    @pl.when(pl.program_id(1) == 0)
    def _():
        x = x_ref[...]
        inv = lax.rsqrt(jnp.mean(x * x, axis=-1, keepdims=True) + RMS_EPS)
        xn_ref[...] = ((x * inv) * nw_ref[...]).astype(BF16)

    o_ref[...] = jnp.dot(xn_ref[...], w_ref[...], preferred_element_type=F32)


def _rms_matmul(x, nw, w, tn):
    m, k = x.shape
    n = w.shape[1]
    tm = min(m, 1024)
    return pl.pallas_call(
        _rms_matmul_body,
        grid=(m // tm, n // tn),
        in_specs=[pl.BlockSpec((tm, k), lambda i, j: (i, 0)),
                  pl.BlockSpec((1, k), lambda i, j: (0, 0)),
                  pl.BlockSpec((k, tn), lambda i, j: (0, j))],
        out_specs=pl.BlockSpec((tm, tn), lambda i, j: (i, j)),
        out_shape=jax.ShapeDtypeStruct((m, n), F32),
        scratch_shapes=[pltpu.VMEM((tm, k), BF16)],
        compiler_params=_params("parallel", "arbitrary"),
        name="rms_matmul",
    )(x, nw.reshape(1, k), w)


def _matmul_res_body(a_ref, w_ref, r_ref, o_ref):
    o_ref[...] = r_ref[...] + jnp.dot(a_ref[...].astype(BF16), w_ref[...], preferred_element_type=F32)


def _matmul_res(a, w, res):
    m, k = a.shape
    n = w.shape[1]
    tm = min(m, 512)
    return pl.pallas_call(
        _matmul_res_body,
        grid=(m // tm,),
        in_specs=[pl.BlockSpec((tm, k), lambda i: (i, 0)),
                  pl.BlockSpec((k, n), lambda i: (0, 0)),
                  pl.BlockSpec((tm, n), lambda i: (i, 0))],
        out_specs=pl.BlockSpec((tm, n), lambda i: (i, 0)),
        out_shape=jax.ShapeDtypeStruct((m, n), F32),
        compiler_params=_params("parallel"),
        name="matmul_res",
    )(a, w, res)


def _rms_body(x_ref, nw_ref, o_ref):
    x = x_ref[...]
    inv = lax.rsqrt(jnp.mean(x * x, axis=-1, keepdims=True) + RMS_EPS)
    o_ref[...] = (x * inv) * nw_ref[...]


def _rms(x, nw):
    m, k = x.shape
    tm = min(m, 1024)
    return pl.pallas_call(
        _rms_body,
        grid=(m // tm,),
        in_specs=[pl.BlockSpec((tm, k), lambda i: (i, 0)), pl.BlockSpec((1, k), lambda i: (0, 0))],
        out_specs=pl.BlockSpec((tm, k), lambda i: (i, 0)),
        out_shape=jax.ShapeDtypeStruct((m, k), F32),
        compiler_params=_params("parallel"),
        name="final_rms",
    )(x, nw.reshape(1, k))


def _shifted(x, prev_rows, shift, row):
    nb = prev_rows.shape[0]
    y = pltpu.roll(x, shift, 0)
    for r in range(shift):
        y = jnp.where(row == r, prev_rows[nb - shift + r:nb - shift + r + 1], y)
    return y


def _ffn_act_body(a_ref, g_ref, buf_ref, cw_ref, act_ref, nb_ref):
    a = a_ref[0]
    t = a.shape[0]
    buf = buf_ref[0]
    w = cw_ref[...]
    row = lax.broadcasted_iota(jnp.int32, a.shape, 0)
    y = _shifted(a, buf, 2, row) * w[0:1] + _shifted(a, buf, 1, row) * w[1:2] + a * w[2:3]
    act_ref[0] = (_silu(y) * g_ref[0]).astype(act_ref.dtype)
    nb_ref[0] = a_ref[0, t - (FFN_CONV_W - 1):t, :]


def _ffn_act(h, buf, cw):
    b, t, _ = h.shape
    nj = D_FF // FFN_TC
    return pl.pallas_call(
        _ffn_act_body,
        grid=(b, nj),
        in_specs=[pl.BlockSpec((1, t, FFN_TC), lambda i, j: (i, 0, j)),
                  pl.BlockSpec((1, t, FFN_TC), lambda i, j: (i, 0, j + nj)),
                  pl.BlockSpec((1, FFN_CONV_W - 1, FFN_TC), lambda i, j: (i, 0, j)),
                  pl.BlockSpec((FFN_CONV_W, FFN_TC), lambda i, j: (0, j))],
        out_specs=[pl.BlockSpec((1, t, FFN_TC), lambda i, j: (i, 0, j)),
                   pl.BlockSpec((1, FFN_CONV_W - 1, FFN_TC), lambda i, j: (i, 0, j))],
        out_shape=[jax.ShapeDtypeStruct((b, t, D_FF), BF16),
                   jax.ShapeDtypeStruct((b, FFN_CONV_W - 1, D_FF), F32)],
        compiler_params=_params("parallel", "parallel"),
        name="ffn_act",
    )(h, h, buf, cw)


GDN_PRE_TC = 512


def _gdn_pre_body(x_ref, buf_ref, cw_ref, o_ref, nb_ref):
    j = pl.program_id(1)
    x = x_ref[0]
    t = x.shape[0]
    buf = buf_ref[0]
    w = cw_ref[...]
    row = lax.broadcasted_iota(jnp.int32, x.shape, 0)
    y = (_shifted(x, buf, 3, row) * w[0:1] + _shifted(x, buf, 2, row) * w[1:2]
         + _shifted(x, buf, 1, row) * w[2:3] + x * w[3:4])
    y = _silu(y)
    is_q = j < GDN_QK_DIM // GDN_PRE_TC
    is_v = j >= 2 * GDN_QK_DIM // GDN_PRE_TC
    qscale = jnp.where(is_q, GDN_DK ** -0.5, 1.0).astype(F32)
    for h in range(GDN_PRE_TC // GDN_DK):
        yh = y[:, h * GDN_DK:(h + 1) * GDN_DK]
        inv = lax.rsqrt(jnp.sum(yh * yh, axis=-1, keepdims=True) + L2_EPS)
        o_ref[0, :, h * GDN_DK:(h + 1) * GDN_DK] = jnp.where(is_v, yh, (yh * inv) * qscale)
    nb_ref[0] = x_ref[0, t - (GDN_CONV_W - 1):t, :]


def _gdn_pre(proj, buf, cw):
    b, t, _ = proj.shape
    nj = GDN_CONV_CH // GDN_PRE_TC
    return pl.pallas_call(
        _gdn_pre_body,
        grid=(b, nj),
        in_specs=[pl.BlockSpec((1, t, GDN_PRE_TC), lambda i, j: (i, 0, j)),
                  pl.BlockSpec((1, GDN_CONV_W - 1, GDN_PRE_TC), lambda i, j: (i, 0, j)),
                  pl.BlockSpec((GDN_CONV_W, GDN_PRE_TC), lambda i, j: (0, j))],
        out_specs=[pl.BlockSpec((1, t, GDN_PRE_TC), lambda i, j: (i, 0, j)),
                   pl.BlockSpec((1, GDN_CONV_W - 1, GDN_PRE_TC), lambda i, j: (i, 0, j))],
        out_shape=[jax.ShapeDtypeStruct((b, t, GDN_CONV_CH), F32),
                   jax.ShapeDtypeStruct((b, GDN_CONV_W - 1, GDN_CONV_CH), F32)],
        compiler_params=_params("parallel", "parallel"),
        name="gdn_pre",
    )(proj, buf, cw)


def _gdn_gate_body(x_ref, alog_ref, dtb_ref, g_ref, gcum_ref, beta_ref, *, chunk):
    x = x_ref[0]
    z = x + dtb_ref[...]
    softplus = jnp.maximum(z, 0.0) + jnp.log(1.0 + jnp.exp(-jnp.abs(z)))
    g = -jnp.exp(alog_ref[...]) * softplus
    g_ref[0] = g
    row = lax.broadcasted_iota(jnp.int32, x.shape, 0) % chunk
    acc = g
    s = 1
    while s < chunk:
        acc = acc + jnp.where(row >= s, pltpu.roll(acc, s, 0), 0.0)
        s *= 2
    gcum_ref[0] = acc
    beta_ref[0] = _sigmoid(x)


def _gdn_gate(proj, alog_pad, dtb_pad, lane_block):
    b, t, _ = proj.shape
    chunk = min(GDN_CHUNK, t)
    spec = pl.BlockSpec((1, t, 128), lambda i: (i, 0, 0))
    return pl.pallas_call(
        functools.partial(_gdn_gate_body, chunk=chunk),
        grid=(b,),
        in_specs=[pl.BlockSpec((1, t, 128), lambda i: (i, 0, lane_block)),
                  pl.BlockSpec((1, 128), lambda i: (0, 0)),
                  pl.BlockSpec((1, 128), lambda i: (0, 0))],
        out_specs=[spec, spec, spec],
        out_shape=[jax.ShapeDtypeStruct((b, t, 128), F32)] * 3,
        compiler_params=_params("parallel"),
        name="gdn_gate",
    )(proj, alog_pad, dtb_pad)


def _unit_lower_inverses(lows, n):
    eye = (lax.broadcasted_iota(jnp.int32, (n, n), 0) == lax.broadcasted_iota(jnp.int32, (n, n), 1)).astype(F32)
    ps = [eye - low for low in lows]
    ms = [_mm(low, low) for low in lows]
    k = 2
    while True:
        ps = [p + _mm(p, m) for p, m in zip(ps, ms)]
        k *= 2
        if k >= n:
            break
        ms = [_mm(m, m) for m in ms]
    return ps


def _gdn_scan_body(q_ref, k_ref, v_ref, z_ref, gc_ref, bc_ref, gr_ref, br_ref, s0_ref, onw_ref, o_ref, s_ref,
                   u_ref, w_ref, qk_ref, qd_ref, kd_ref):
    c = GDN_CHUNK
    hb = GDN_HEADS_PER_STEP
    n_chunks = q_ref.shape[1] // c
    s_ref[...] = s0_ref[...]
    ri = lax.broadcasted_iota(jnp.int32, (c, c), 0)
    ci = lax.broadcasted_iota(jnp.int32, (c, c), 1)
    tri = ri >= ci
    stri = ri > ci
    onw = onw_ref[...]

    def qk_slice(ref, r0, qh):
        return ref[0, pl.ds(r0, c), qh * GDN_DK:(qh + 1) * GDN_DK]

    def v_slice(ref, r0, hh):
        return ref[0, pl.ds(r0, c), hh * GDN_DV:(hh + 1) * GDN_DV]

    cpb = 2 if n_chunks % 2 == 0 else 1

    def prep(nb, carry):
        items = []
        kk, qk, ks, qs = {}, {}, {}, {}
        for ch in range(cpb):
            n = nb * cpb + ch
            r0 = pl.multiple_of(n * c, c)
            gcol = gc_ref[0, 0, pl.ds(r0, c), :]
            bcol = bc_ref[0, 0, pl.ds(r0, c), :]
            grow = gr_ref[0, 0, n]
            brow = br_ref[0, 0, n]
            for qh in range(hb // 2):
                ks[ch, qh] = qk_slice(k_ref, r0, qh)
                qs[ch, qh] = qk_slice(q_ref, r0, qh)
            for hh in range(hb):
                items.append(dict(ch=ch, hh=hh, r0=r0, v=v_slice(v_ref, r0, hh),
                                  gcb=jnp.broadcast_to(gcol[:, hh:hh + 1], (c, GDN_DK)),
                                  bcb=jnp.broadcast_to(bcol[:, hh:hh + 1], (c, c)),
                                  gr=grow[hh:hh + 1, :], br=brow[hh:hh + 1, :]))
        for key in ks:
            kk[key] = _mm_nt(ks[key], ks[key])
        for key in ks:
            qk[key] = _mm_nt(qs[key], ks[key])
        lows = []
        for it in items:
            key = (it["ch"], it["hh"] // 2)
            it["decay"] = jnp.where(tri, jnp.exp(jnp.where(tri, it["gcb"][:, :c] - it["gr"], 0.0)), 0.0)
            lows.append(jnp.where(stri, (kk[key] * it["bcb"]) * it["decay"], 0.0))
        tinvs = _unit_lower_inverses(lows, c)
        us = [_mm(tinv * it["br"], it["v"]) for tinv, it in zip(tinvs, items)]
        ws = [_mm(tinv * (it["br"] * jnp.exp(it["gr"])), ks[it["ch"], it["hh"] // 2])
              for tinv, it in zip(tinvs, items)]
        for it, u, w in zip(items, us, ws):
            key = (it["ch"], it["hh"] // 2)
            hh, r0, gcb = it["hh"], it["r0"], it["gcb"]
            u_ref[hh, pl.ds(r0, c), :] = u
            w_ref[hh, pl.ds(r0, c), :] = w.astype(BF16)
            qk_ref[hh, pl.ds(r0, c), :] = jnp.where(tri, qk[key] * it["decay"], 0.0).astype(BF16)
            qd_ref[hh, pl.ds(r0, c), :] = (qs[key] * jnp.exp(gcb)).astype(BF16)
            kd_ref[hh, pl.ds(r0, c), :] = (ks[key] * jnp.exp(gcb[c - 1:c, :] - gcb)).astype(BF16)
        return carry

    lax.fori_loop(0, n_chunks // cpb, prep, 0)

    def scan(n, carry):
        r0 = pl.multiple_of(n * c, c)
        g_last = gc_ref[0, 0, pl.ds(r0 + (c - 1), 1), :]
        loaded = []
        for hh in range(hb):
            loaded.append((u_ref[hh, pl.ds(r0, c), :], w_ref[hh, pl.ds(r0, c), :], qk_ref[hh, pl.ds(r0, c), :],
                           qd_ref[hh, pl.ds(r0, c), :], kd_ref[hh, pl.ds(r0, c), :], v_slice(z_ref, r0, hh),
                           s_ref[0, hh]))
        ws_s = [_mm(w, s) for (u, w, qkm, qd, kd, z, s) in loaded]
        qd_s = [_mm(qd, s) for (u, w, qkm, qd, kd, z, s) in loaded]
        v_news = [ld[0] - ws for ld, ws in zip(loaded, ws_s)]
        qk_v = [_mm(ld[2], vn) for ld, vn in zip(loaded, v_news)]
        kd_v = [_mm_tn(ld[4], vn) for ld, vn in zip(loaded, v_news)]
        results = []
        for hh, ld in enumerate(loaded):
            o = qd_s[hh] + qk_v[hh]
            s_new = ld[6] * jnp.exp(g_last[:, hh:hh + 1]) + kd_v[hh]
            inv = lax.rsqrt(jnp.mean(o * o, axis=-1, keepdims=True) + RMS_EPS)
            results.append((((o * inv) * onw) * _silu(ld[5]), s_new))
        for hh, (og, s_new) in enumerate(results):
            s_ref[0, hh] = s_new
            o_ref[0, pl.ds(r0, c), hh * GDN_DV:(hh + 1) * GDN_DV] = og.astype(o_ref.dtype)
        return carry

    lax.fori_loop(0, n_chunks, scan, 0)


def _gdn_scan(qkv, proj, gcol, bcol, grow, brow, s0, onw):
    b, t, _ = qkv.shape
    hb = GDN_HEADS_PER_STEP
    hg = GDN_V_HEADS // hb
    qw = hb // 2 * GDN_DK
    vw = hb * GDN_DV
    n = t // GDN_CHUNK
    return pl.pallas_call(
        _gdn_scan_body,
        grid=(b, hg),
        in_specs=[pl.BlockSpec((1, t, qw), lambda i, j: (i, 0, j)),
                  pl.BlockSpec((1, t, qw), lambda i, j: (i, 0, GDN_QK_DIM // qw + j)),
                  pl.BlockSpec((1, t, vw), lambda i, j: (i, 0, 2 * GDN_QK_DIM // vw + j)),
                  pl.BlockSpec((1, t, vw), lambda i, j: (i, 0, GDN_CONV_CH // vw + j)),
                  pl.BlockSpec((1, 1, t, hb), lambda i, j: (i, j, 0, 0)),
                  pl.BlockSpec((1, 1, t, hb), lambda i, j: (i, j, 0, 0)),
                  pl.BlockSpec((1, 1, n, hb, GDN_CHUNK), lambda i, j: (i, j, 0, 0, 0)),
                  pl.BlockSpec((1, 1, n, hb, GDN_CHUNK), lambda i, j: (i, j, 0, 0, 0)),
                  pl.BlockSpec((1, hb, GDN_DK, GDN_DV), lambda i, j: (i, j, 0, 0)),
                  pl.BlockSpec((1, GDN_DV), lambda i, j: (0, 0))],
        out_specs=[pl.BlockSpec((1, t, vw), lambda i, j: (i, 0, j)),
                   pl.BlockSpec((1, hb, GDN_DK, GDN_DV), lambda i, j: (i, j, 0, 0))],
        out_shape=[jax.ShapeDtypeStruct((b, t, GDN_V_DIM), BF16),
                   jax.ShapeDtypeStruct((b, GDN_V_HEADS, GDN_DK, GDN_DV), F32)],
        scratch_shapes=[pltpu.VMEM((hb, t, GDN_DV), F32),
                        pltpu.VMEM((hb, t, GDN_DK), BF16),
                        pltpu.VMEM((hb, t, GDN_CHUNK), BF16),
                        pltpu.VMEM((hb, t, GDN_DK), BF16),
                        pltpu.VMEM((hb, t, GDN_DK), BF16)],
        compiler_params=_params("parallel", "parallel"),
        name="gdn_scan",
    )(qkv, qkv, qkv, proj, gcol, bcol, grow, brow, s0, onw.reshape(1, GDN_DV))


def _gdn_layer(x, s0, conv_buf, norm_w, w_in, conv_w, alog_pad, dtb_pad, out_norm_w, w_out):
    b, t, d = x.shape
    xf = x.reshape(b * t, d)
    proj = _rms_matmul(xf, norm_w, w_in, 896).reshape(b, t, -1)
    qkv, new_buf = _gdn_pre(proj, conv_buf, conv_w)
    tail_block = (GDN_CONV_CH + GDN_V_DIM) // 128
    _, gcum, beta = _gdn_gate(proj, alog_pad, dtb_pad, tail_block)
    gcum = gcum[:, :, :GDN_V_HEADS]
    beta = beta[:, :, GDN_V_HEADS:2 * GDN_V_HEADS]
    tp = -(-t // GDN_CHUNK) * GDN_CHUNK
    if tp != t:
        pad = [(0, 0), (0, tp - t), (0, 0)]
        qkv = jnp.pad(qkv, pad)
        proj_z = jnp.pad(proj, pad)
        gcum = jnp.pad(gcum, pad, mode="edge")
        beta = jnp.pad(beta, pad)
    else:
        proj_z = proj
    hb = GDN_HEADS_PER_STEP
    hg = GDN_V_HEADS // hb
    n = tp // GDN_CHUNK
    gcol = gcum.reshape(b, tp, hg, hb).transpose(0, 2, 1, 3)
    bcol = beta.reshape(b, tp, hg, hb).transpose(0, 2, 1, 3)
    grow = gcum.reshape(b, n, GDN_CHUNK, hg, hb).transpose(0, 3, 1, 4, 2)
    brow = beta.reshape(b, n, GDN_CHUNK, hg, hb).transpose(0, 3, 1, 4, 2)
    o, s_new = _gdn_scan(qkv, proj_z, gcol, bcol, grow, brow, s0, out_norm_w)
    o = o[:, :t].reshape(b * t, GDN_V_DIM)
    y = _matmul_res(o, w_out, xf).reshape(b, t, d)
    return y, s_new, new_buf


def _page_gather_body(pt_ref, *refs, n_steps, pps):
    page_refs, new_ref, o_ref = refs[:pps], refs[pps], refs[pps + 1]
    p = pl.program_id(1)
    page = page_refs[0].shape[3]

    @pl.when(p < n_steps)
    def _():
        for s, ref in enumerate(page_refs):
            o_ref[0, :, s * page:(s + 1) * page, :] = ref[0, 0]

    @pl.when(p == n_steps)
    def _():
        o_ref[0] = new_ref[0]


def _page_gather(cache_hm, layer, page_table, new_hm):
    b, n_pages = page_table.shape
    _, _, n_cg, page, dh = cache_hm.shape
    pps = new_hm.shape[2] // page
    n_steps = n_pages // pps

    def page_map(s):
        return lambda i, p, pt: (layer, pt[i, jnp.minimum(p, n_steps - 1) * pps + s], 0, 0, 0)

    grid_spec = pltpu.PrefetchScalarGridSpec(
        num_scalar_prefetch=1,
        grid=(b, n_steps + 1),
        in_specs=[pl.BlockSpec((1, 1, n_cg, page, dh), page_map(s)) for s in range(pps)]
        + [pl.BlockSpec((1, n_cg, pps * page, dh), lambda i, p, pt: (i, 0, 0, 0))],
        out_specs=pl.BlockSpec((1, n_cg, pps * page, dh), lambda i, p, pt: (i, 0, p, 0)),
    )
    return pl.pallas_call(
        functools.partial(_page_gather_body, n_steps=n_steps, pps=pps),
        grid_spec=grid_spec,
        out_shape=jax.ShapeDtypeStruct((b, n_cg, (n_pages + pps) * page, dh), cache_hm.dtype),
        compiler_params=_params("parallel", "arbitrary"),
        name="page_gather",
    )(page_table, *([cache_hm] * pps), new_hm)


def _compress_body(x_ref, w1_ref, pe_ref, w2_ref, o_ref, *, n_sub):
    hd = CMP_HIDDEN
    acc = jnp.dot(x_ref[0, 0], w1_ref[0], preferred_element_type=F32)
    pe = pe_ref[0]
    pe0 = jnp.broadcast_to(pe[0:1], (8, pe.shape[1]))
    pe1 = jnp.broadcast_to(pe[1:2], (8, pe.shape[1]))
    b0 = jnp.dot(pe0.astype(BF16), w1_ref[0, :, :hd], preferred_element_type=F32)[0:1]
    b1 = jnp.dot(pe1.astype(BF16), w1_ref[0, :, hd:], preferred_element_type=F32)[0:1]
    first = acc[:, :hd] + b0
    second = acc[:, hd:] + b1
    hid = _silu(first + pltpu.roll(second, n_sub - 1, 0))
    o_ref[0, 0] = jnp.dot(hid.astype(BF16), w2_ref[0], preferred_element_type=F32)


def _compress(kv_hm, n_sub, w1cat, pe_flat, w2):
    b, n_cg, rows, dh = kv_hm.shape
    flat = CMP_STRIDE * dh
    x = kv_hm.reshape(b, n_cg, rows // CMP_STRIDE, flat)
    return pl.pallas_call(
        functools.partial(_compress_body, n_sub=n_sub),
        grid=(b, 2 * NSA_G),
        in_specs=[pl.BlockSpec((1, 1, n_sub, flat), lambda i, j: (i, j, 0, 0)),
                  pl.BlockSpec((1, flat, 2 * CMP_HIDDEN), lambda i, j: (j // NSA_G, 0, 0)),
                  pl.BlockSpec((1, 2, flat), lambda i, j: (j // NSA_G, 0, 0)),
                  pl.BlockSpec((1, CMP_HIDDEN, dh), lambda i, j: (j // NSA_G, 0, 0))],
        out_specs=pl.BlockSpec((1, 1, n_sub, dh), lambda i, j: (i, j, 0, 0)),
        out_shape=jax.ShapeDtypeStruct((b, 2 * NSA_G, n_sub, dh), F32),
        compiler_params=_params("parallel", "parallel"),
        name="nsa_compress",
    )(x, w1cat, pe_flat, w2)


def _inv_or_zero(l):
    return jnp.where(l > 0.0, 1.0 / jnp.where(l > 0.0, l, 1.0), 0.0)


def _nsa_attn_body(q_ref, gate_ref, slope_ref, ck_ref, cv_ref, ks_ref, vs_ref, kw_ref, vw_ref, o_ref, osel_ref,
                   *, qb, p_len, n_cmp, n_blk, w0, sel_classes, kw_len):
    i = pl.program_id(2)
    r = NSA_R
    rows = r * qb
    q = (q_ref[0, 0].reshape(rows, NSA_DH) * (NSA_DH ** -0.5)).astype(BF16)
    slopes = slope_ref[0]
    q0 = p_len + i * qb
    t_pos = q0 + lax.broadcasted_iota(jnp.int32, (1, qb, 1), 1)

    def branch(k, v, kp, ok_fn):
        nk = k.shape[0]
        dist = t_pos - kp
        ok = ok_fn(dist)
        s = _mm_nt(q, k).reshape(r, qb, nk) - slopes * dist.astype(F32)
        s = jnp.where(ok, s, NEG_INF)
        m = jnp.max(s, axis=-1, keepdims=True)
        p = jnp.where(ok, jnp.exp(s - m), 0.0)
        inv = _inv_or_zero(jnp.sum(p, axis=-1, keepdims=True))
        o = _mm(p.reshape(rows, nk), v) * inv.reshape(rows, 1)
        return o, p, inv

    nc = ck_ref.shape[2]
    n_idx = lax.broadcasted_iota(jnp.int32, (1, 1, nc), 2)
    o_cmp, p_cmp, inv_cmp = branch(ck_ref[0, 0], cv_ref[0, 0], n_idx * CMP_STRIDE + (CMP_LEN - 1),
                                   lambda dist: (dist >= 0) & (n_idx < n_cmp))
    p_sum = jnp.sum(p_cmp * inv_cmp, axis=0)

    nb8 = -(-n_blk // 8) * 8
    jj = lax.broadcasted_iota(jnp.int32, (nb8, nc), 0)
    nn = lax.broadcasted_iota(jnp.int32, (nb8, nc), 1)
    overlap_t = ((nn * CMP_STRIDE < (jj + 1) * SEL_BLOCK) & (nn * CMP_STRIDE + CMP_LEN > jj * SEL_BLOCK)
                 & (nn < n_cmp) & (jj < n_blk)).astype(F32)
    imp = lax.dot_general(overlap_t, p_sum, (((1,), (1,)), ((), ())), preferred_element_type=F32,
                          precision=lax.Precision.HIGHEST)
    j = lax.broadcasted_iota(jnp.int32, (nb8, qb), 0)
    cur = (q0 + lax.broadcasted_iota(jnp.int32, (1, qb), 1)) // SEL_BLOCK
    imp = jnp.where((j == 0) | (j == cur) | (j == cur - 1), FORCE_SCORE, imp)
    imp = jnp.where(j > cur, -FORCE_SCORE, imp)
    imp = jnp.where(j >= n_blk, NEVER, imp)
    rank = jnp.zeros((nb8, qb), jnp.int32)
    for jp in range(n_blk):
        row = imp[jp:jp + 1, :]
        rank = rank + ((row > imp) | ((row == imp) & (j > jp))).astype(jnp.int32)
    picked_t = (rank < SEL_TOPN).astype(BF16)
    eye = (lax.broadcasted_iota(jnp.int32, (qb, qb), 0) == lax.broadcasted_iota(jnp.int32, (qb, qb), 1))
    picked = _mm_nt(eye.astype(BF16), picked_t)

    def sel_branch(nk):
        blk_of_key = lax.broadcasted_iota(jnp.int32, (nb8, nk), 1) // SEL_BLOCK
        expand = (blk_of_key == lax.broadcasted_iota(jnp.int32, (nb8, nk), 0)).astype(BF16)
        key_picked = (_mm(picked, expand) > 0.5).reshape(1, qb, nk)
        o, _, _ = branch(ks_ref[0, 0, 0:nk, :], vs_ref[0, 0, 0:nk, :],
                         lax.broadcasted_iota(jnp.int32, (1, 1, nk), 2),
                         lambda dist: key_picked & (dist >= 0))
        osel_ref[...] = o

    if len(sel_classes) == 1:
        sel_branch(sel_classes[0])
    else:
        need = q0 + qb
        prev = 0
        for nk in sel_classes:
            pl.when((need > prev) & (need <= nk))(functools.partial(sel_branch, nk))
            prev = nk

    tw = kw_ref.shape[2]
    if tw == kw_len:
        start = 0
    else:
        start = pl.multiple_of(jnp.clip(q0 + qb - w0 - kw_len, 0, tw - kw_len), SEL_BLOCK)
    kp_win = w0 + start + lax.broadcasted_iota(jnp.int32, (1, 1, kw_len), 2)
    o_win, _, _ = branch(kw_ref[0, 0, pl.ds(start, kw_len), :], vw_ref[0, 0, pl.ds(start, kw_len), :], kp_win,
                         lambda dist: (dist >= 0) & (dist < WINDOW) & (kp_win >= 0))

    gates = _sigmoid(gate_ref[0, 0]).reshape(rows, 3)
    o = gates[:, 0:1] * o_cmp + gates[:, 1:2] * osel_ref[...] + gates[:, 2:3] * o_win
    o_ref[0, 0] = o.reshape(r, qb, NSA_DH)


def _nsa_attn(q_t, gates_t, slopes, cmp_kv, sel_arr, sel_k0, sel_v0, win_arr, win_k0, win_v0,
              *, p_len, n_cmp, n_blk, w0):
    b, g, r, tq, dh = q_t.shape
    qb = NSA_Q_BLOCK if tq % NSA_Q_BLOCK == 0 else tq
    nc = cmp_kv.shape[2]
    tk = sel_arr.shape[2]
    tw = win_arr.shape[2]
    if tq == qb or tk % NSA_SEL_CLASS != 0:
        sel_classes = (tk,)
    else:
        sel_classes = tuple(range(NSA_SEL_CLASS, tk + 1, NSA_SEL_CLASS))
    kw_len = min(tw, -(-(WINDOW - 1 + qb) // KEY_CHUNK) * KEY_CHUNK)
    body = functools.partial(_nsa_attn_body, qb=qb, p_len=p_len, n_cmp=n_cmp, n_blk=n_blk, w0=w0,
                             sel_classes=sel_classes, kw_len=kw_len)
    return pl.pallas_call(
        body,
        grid=(b, g, tq // qb),
        in_specs=[pl.BlockSpec((1, 1, r, qb, dh), lambda bi, gi, i: (bi, gi, 0, i, 0)),
                  pl.BlockSpec((1, 1, r, qb, 3), lambda bi, gi, i: (bi, gi, 0, i, 0)),
                  pl.BlockSpec((1, r, 1, 1), lambda bi, gi, i: (gi, 0, 0, 0)),
                  pl.BlockSpec((1, 1, nc, dh), lambda bi, gi, i: (bi, gi, 0, 0)),
                  pl.BlockSpec((1, 1, nc, dh), lambda bi, gi, i: (bi, NSA_G + gi, 0, 0)),
                  pl.BlockSpec((1, 1, tk, dh), lambda bi, gi, i: (bi, sel_k0 + gi, 0, 0)),
                  pl.BlockSpec((1, 1, tk, dh), lambda bi, gi, i: (bi, sel_v0 + gi, 0, 0)),
                  pl.BlockSpec((1, 1, tw, dh), lambda bi, gi, i: (bi, win_k0 + gi, 0, 0)),
                  pl.BlockSpec((1, 1, tw, dh), lambda bi, gi, i: (bi, win_v0 + gi, 0, 0))],
        out_specs=pl.BlockSpec((1, 1, r, qb, dh), lambda bi, gi, i: (bi, gi, 0, i, 0)),
        out_shape=jax.ShapeDtypeStruct((b, g, r, tq, dh), F32),
        scratch_shapes=[pltpu.VMEM((r * qb, dh), F32)],
        compiler_params=_params("parallel", "parallel", "arbitrary"),
        name="nsa_attn",
    )(q_t, gates_t, slopes, cmp_kv, cmp_kv, sel_arr, sel_arr, win_arr, win_arr)


def _nsa_layer(x, cache_hm, layer, page_table, win_buf, norm_w, w_in, w1cat, pe_flat, w2, w_out, slopes):
    b, t, d = x.shape
    g, r, dh = NSA_G, NSA_R, NSA_DH
    xf = x.reshape(b * t, d)
    proj = _rms_matmul(xf, norm_w, w_in, 896).reshape(b, t, -1)
    q_t = proj[..., :NSA_Q_DIM].reshape(b, t, g, r, dh).transpose(0, 2, 3, 1, 4)
    kv = proj[..., NSA_Q_DIM:NSA_Q_DIM + 6 * NSA_KV_DIM]
    gates_t = proj[..., NSA_Q_DIM + 6 * NSA_KV_DIM:NSA_Q_DIM + 6 * NSA_KV_DIM + 3 * NSA_HEADS]
    gates_t = gates_t.reshape(b, t, g, r, 3).transpose(0, 2, 3, 1, 4)
    kv6 = kv.reshape(b, t, 6, g, dh)
    new_rows = kv6[:, :, :4]
    if cache_hm is None:
        p_len = 0
        tk = t
        kv_hm = kv6.astype(BF16).transpose(0, 2, 3, 1, 4).reshape(b, 6 * g, t, dh)
        sel_arr, sel_k0, sel_v0 = kv_hm, 2 * g, 3 * g
        win_arr, win_k0, win_v0 = kv_hm, 4 * g, 5 * g
        w0 = 0
        new_win = kv6[:, t - min(WINDOW, t):, 4:]
    else:
        n_pages = page_table.shape[1]
        page = cache_hm.shape[3]
        p_len = n_pages * page
        tk = p_len + t
        pps = 4 if n_pages % 4 == 0 else 1
        new_hm = kv6[:, :, :4].astype(BF16).transpose(0, 2, 3, 1, 4).reshape(b, 4 * g, t, dh)
        new_hm = jnp.pad(new_hm, [(0, 0), (0, 0), (0, pps * page - t), (0, 0)])
        kv_hm = _page_gather(cache_hm, layer, page_table, new_hm)
        sel_arr, sel_k0, sel_v0 = kv_hm, 2 * g, 3 * g
        win_all = jnp.concatenate([win_buf, kv6[:, :, 4:]], axis=1)
        wl = win_all.shape[1]
        wlp = -(-wl // KEY_CHUNK) * KEY_CHUNK
        win_arr = jnp.pad(win_all.astype(BF16), [(0, 0), (0, wlp - wl), (0, 0), (0, 0), (0, 0)])
        win_arr = win_arr.transpose(0, 2, 3, 1, 4).reshape(b, 2 * g, wlp, dh)
        win_k0, win_v0 = 0, g
        w0 = p_len - win_buf.shape[1]
        new_win = win_all[:, wl - min(WINDOW, tk):]
    n_sub = tk // CMP_STRIDE
    n_cmp = n_sub - 1
    n_blk = -(-tk // SEL_BLOCK)
    cmp_kv = _compress(kv_hm, n_sub, w1cat, pe_flat, w2)
    o_t = _nsa_attn(q_t, gates_t, slopes, cmp_kv, sel_arr, sel_k0, sel_v0, win_arr, win_k0, win_v0,
                    p_len=p_len, n_cmp=n_cmp, n_blk=n_blk, w0=w0)
    o = o_t.transpose(0, 3, 1, 2, 4).reshape(b * t, NSA_Q_DIM)
    y = _matmul_res(o, w_out, xf).reshape(b, t, d)
    return y, new_rows, new_win


def _ffn_layer(x, buf, norm_w, w_up, conv_w, w_down):
    b, t, d = x.shape
    xf = x.reshape(b * t, d)
    h = _rms_matmul(xf, norm_w, w_up, 512).reshape(b, t, -1)
    act, new_buf = _ffn_act(h, buf, conv_w)
    y = _matmul_res(act.reshape(b * t, D_FF), w_down, xf).reshape(b, t, d)
    return y, new_buf


def _pad_cols(w, n):
    return jnp.pad(w, [(0, 0)] * (w.ndim - 1) + [(0, n - w.shape[-1])])


def _trunk(x, cache_hm, page_table, nsa_win, gdn_state, gdn_conv, ffn_conv, wts):
    depth = wts["ffn_w_up"].shape[0]
    rows_l, win_l, s_l, gconv_l, fconv_l = [], [], [], [], []
    for i in range(depth):
        j = i // 2
        if i % 2 == 0:
            x, s_new, cb = _gdn_layer(x, gdn_state[j], gdn_conv[j], wts["gdn_norm"][j], wts["gdn_w_in"][j],
                                      wts["gdn_conv_w"][j], wts["gdn_a_log"][j], wts["gdn_dt_bias"][j],
                                      wts["gdn_out_norm"][j], wts["gdn_w_out"][j])
            s_l.append(s_new)
            gconv_l.append(cb)
        else:
            x, rows, wb = _nsa_layer(x, cache_hm, j, page_table, None if nsa_win is None else nsa_win[j],
                                     wts["nsa_norm"][j], wts["nsa_w_in"][j], wts["nsa_w1cat"][j],
                                     wts["nsa_pe_flat"][j], wts["nsa_w2"][j], wts["nsa_w_out"][j],
                                     wts["slopes"])
            rows_l.append(rows)
            win_l.append(wb)
        x, fb = _ffn_layer(x, ffn_conv[i], wts["ffn_norm"][i], wts["ffn_w_up"][i], wts["ffn_conv_w"][i],
                           wts["ffn_w_down"][i])
        fconv_l.append(fb)
    b, t, d = x.shape
    y = _rms(x.reshape(b * t, d), wts["final_norm"]).reshape(b, t, d)
    return (y, jnp.stack(rows_l), jnp.stack(win_l), jnp.stack(s_l), jnp.stack(gconv_l), jnp.stack(fconv_l))


def kernel(x_prompt, x_sample, cache_nsa_kv, cache_nsa_win, state_gdn_s, state_gdn_conv, state_ffn_conv,
           page_table, gdn_norm, gdn_w_in, gdn_conv_w, gdn_a_log, gdn_dt_bias, gdn_out_norm, gdn_w_out,
           nsa_norm, nsa_w_in, nsa_cmp_pe, nsa_cmp_w1, nsa_cmp_w2, nsa_w_out,
           ffn_norm, ffn_w_up, ffn_conv_w, ffn_w_down, final_norm):
    n_gdn = gdn_w_in.shape[0]
    n_nsa = nsa_w_in.shape[0]
    depth = ffn_w_up.shape[0]
    bp = x_prompt.shape[0]
    gdn_cols = -(-gdn_w_in.shape[2] // 896) * 896
    nsa_cols = -(-nsa_w_in.shape[2] // 896) * 896
    w1 = nsa_cmp_w1.reshape(n_nsa, 2, 2, CMP_STRIDE * NSA_DH, CMP_HIDDEN)
    w1cat = jnp.concatenate([w1[:, :, 0], w1[:, :, 1]], axis=-1).astype(BF16)
    slopes = 2.0 ** (-8.0 * jnp.arange(1, NSA_HEADS + 1, dtype=F32) / NSA_HEADS)
    wts = {
        "gdn_norm": gdn_norm,
        "gdn_w_in": _pad_cols(gdn_w_in, gdn_cols).astype(BF16),
        "gdn_conv_w": gdn_conv_w,
        "gdn_a_log": _pad_cols(gdn_a_log, 128).reshape(n_gdn, 1, 128),
        "gdn_dt_bias": _pad_cols(gdn_dt_bias, 128).reshape(n_gdn, 1, 128),
        "gdn_out_norm": gdn_out_norm,
        "gdn_w_out": gdn_w_out.astype(BF16),
        "nsa_norm": nsa_norm,
        "nsa_w_in": _pad_cols(nsa_w_in, nsa_cols).astype(BF16),
        "nsa_w1cat": w1cat,
        "nsa_pe_flat": nsa_cmp_pe.reshape(n_nsa, 2, 2, CMP_STRIDE * NSA_DH),
        "nsa_w2": nsa_cmp_w2.astype(BF16),
        "nsa_w_out": nsa_w_out.astype(BF16),
        "slopes": slopes.reshape(NSA_G, NSA_R, 1, 1),
        "ffn_norm": ffn_norm,
        "ffn_w_up": ffn_w_up.astype(BF16),
        "ffn_conv_w": ffn_conv_w,
        "ffn_w_down": ffn_w_down.astype(BF16),
        "final_norm": final_norm,
    }
    n_l, pool, page = cache_nsa_kv.shape[:3]
    cache_hm = cache_nsa_kv.astype(BF16).transpose(0, 1, 3, 4, 2, 5).reshape(n_l, pool, 4 * NSA_G, page, NSA_DH)

    zeros = functools.partial(jnp.zeros, dtype=F32)
    prompt = _trunk(x_prompt, None, None, None,
                    zeros((n_gdn, bp, GDN_V_HEADS, GDN_DK, GDN_DV)),
                    zeros((n_gdn, bp, GDN_CONV_W - 1, GDN_CONV_CH)),
                    zeros((depth, bp, FFN_CONV_W - 1, D_FF)), wts)
    sample = _trunk(x_sample, cache_hm, page_table, cache_nsa_win, state_gdn_s, state_gdn_conv, state_ffn_conv, wts)
    out = []
    for p, s in zip(prompt, sample):
        out.extend([p, s])
    return tuple(out)
```

```python
import functools

import jax
import jax.numpy as jnp
from jax import lax
from jax.experimental import pallas as pl
from jax.experimental.pallas import tpu as pltpu

F32 = jnp.float32
BF16 = jnp.bfloat16

RMS_EPS = 1e-6
L2_EPS = 1e-6
NEG_INF = -1e30
FORCE_SCORE = 1e9
NEVER = -3e38
MASKED_DIST = 1e30
LOG2E = 1.4426950408889634

GDN_QK_HEADS = 8
GDN_V_HEADS = 16
GDN_DK = 128
GDN_DV = 128
GDN_QK_DIM = GDN_QK_HEADS * GDN_DK
GDN_V_DIM = GDN_V_HEADS * GDN_DV
GDN_CONV_CH = 2 * GDN_QK_DIM + GDN_V_DIM
GDN_CONV_W = 4
GDN_CHUNK = 64
GDN_HEADS_PER_STEP = 4
NSA_HEADS = 16
NSA_G = 4
NSA_R = 4
NSA_DH = 64
NSA_Q_DIM = NSA_HEADS * NSA_DH
NSA_KV_DIM = NSA_G * NSA_DH
CMP_STRIDE = 16
CMP_LEN = 32
CMP_HIDDEN = 256
SEL_BLOCK = 64
SEL_TOPN = 8
WINDOW = 512
NSA_Q_BLOCK = 64
KEY_CHUNK = 128
NSA_SEL_CLASS = 512
D_FF = 2816
FFN_CONV_W = 3
FFN_TC = 256
SEQ_ROWS_PER_STEP = 2048

VMEM_LIMIT = 52 * 1024 * 1024


def _params(*sem):
    return pltpu.CompilerParams(dimension_semantics=sem, vmem_limit_bytes=VMEM_LIMIT)


def _mm(a, b):
    return jnp.dot(a.astype(BF16), b.astype(BF16), preferred_element_type=F32)


def _mm_nt(a, b):
    return lax.dot_general(a.astype(BF16), b.astype(BF16), (((1,), (1,)), ((), ())),
                           preferred_element_type=F32)


def _mm_tn(a, b):
    return lax.dot_general(a.astype(BF16), b.astype(BF16), (((0,), (0,)), ((), ())),
                           preferred_element_type=F32)


def _sigmoid(x):
    return 1.0 / (1.0 + jnp.exp(-x))


def _silu(x):
    return x * _sigmoid(x)


def _rms_matmul_body(x_ref, nw_ref, w_ref, o_ref, xn_ref):
    @pl.when(pl.program_id(1) == 0)
    def _():
        x = x_ref[...]
        inv = lax.rsqrt(jnp.mean(x * x, axis=-1, keepdims=True) + RMS_EPS)
        xn_ref[...] = ((x * inv) * nw_ref[...]).astype(BF16)

    o_ref[...] = jnp.dot(xn_ref[...], w_ref[...], preferred_element_type=F32)


def _rms_matmul(x, nw, w, tn):
    m, k = x.shape
    n = w.shape[1]
    tm = min(m, 1024)
    return pl.pallas_call(
        _rms_matmul_body,
        grid=(m // tm, n // tn),
        in_specs=[pl.BlockSpec((tm, k), lambda i, j: (i, 0)),
                  pl.BlockSpec((1, k), lambda i, j: (0, 0)),
                  pl.BlockSpec((k, tn), lambda i, j: (0, j))],
        out_specs=pl.BlockSpec((tm, tn), lambda i, j: (i, j)),
        out_shape=jax.ShapeDtypeStruct((m, n), F32),
        scratch_shapes=[pltpu.VMEM((tm, k), BF16)],
        compiler_params=_params("parallel", "arbitrary"),
        name="rms_matmul",
    )(x, nw.reshape(1, k), w)


def _matmul_res_body(a_ref, w_ref, r_ref, o_ref):
    o_ref[...] = r_ref[...] + jnp.dot(a_ref[...].astype(BF16), w_ref[...], preferred_element_type=F32)


def _matmul_res(a, w, res):
    m, k = a.shape
    n = w.shape[1]
    tm = min(m, 512)
    return pl.pallas_call(
        _matmul_res_body,
        grid=(m // tm,),
        in_specs=[pl.BlockSpec((tm, k), lambda i: (i, 0)),
                  pl.BlockSpec((k, n), lambda i: (0, 0)),
                  pl.BlockSpec((tm, n), lambda i: (i, 0))],
        out_specs=pl.BlockSpec((tm, n), lambda i: (i, 0)),
        out_shape=jax.ShapeDtypeStruct((m, n), F32),
        compiler_params=_params("parallel"),
        name="matmul_res",
    )(a, w, res)


def _rms_body(x_ref, nw_ref, o_ref):
    x = x_ref[...]
    inv = lax.rsqrt(jnp.mean(x * x, axis=-1, keepdims=True) + RMS_EPS)
    o_ref[...] = (x * inv) * nw_ref[...]


def _rms(x, nw):
    m, k = x.shape
    tm = min(m, 1024)
    return pl.pallas_call(
        _rms_body,
        grid=(m // tm,),
        in_specs=[pl.BlockSpec((tm, k), lambda i: (i, 0)), pl.BlockSpec((1, k), lambda i: (0, 0))],
        out_specs=pl.BlockSpec((tm, k), lambda i: (i, 0)),
        out_shape=jax.ShapeDtypeStruct((m, k), F32),
        compiler_params=_params("parallel"),
        name="final_rms",
    )(x, nw.reshape(1, k))


def _shifted(x, prev_rows, shift, row):
    nb = prev_rows.shape[1]
    y = pltpu.roll(x, shift, 1)
    for r in range(shift):
        y = jnp.where(row == r, prev_rows[:, nb - shift + r:nb - shift + r + 1], y)
    return y


def _batch_block(b, t):
    bb = max(1, min(b, SEQ_ROWS_PER_STEP // t))
    while b % bb:
        bb -= 1
    return bb


def _ffn_act_body(a_ref, g_ref, buf_ref, cw_ref, act_ref, nb_ref):
    a = a_ref[...]
    t = a.shape[1]
    buf = buf_ref[...]
    w = cw_ref[...]
    row = lax.broadcasted_iota(jnp.int32, a.shape, 1)
    y = _shifted(a, buf, 2, row) * w[0:1] + _shifted(a, buf, 1, row) * w[1:2] + a * w[2:3]
    act_ref[...] = (_silu(y) * g_ref[...]).astype(act_ref.dtype)
    nb_ref[...] = a_ref[:, t - (FFN_CONV_W - 1):t, :]


def _ffn_act(h, buf, cw):
    b, t, _ = h.shape
    nj = D_FF // FFN_TC
    bb = _batch_block(b, t)
    return pl.pallas_call(
        _ffn_act_body,
        grid=(b // bb, nj),
        in_specs=[pl.BlockSpec((bb, t, FFN_TC), lambda i, j: (i, 0, j)),
                  pl.BlockSpec((bb, t, FFN_TC), lambda i, j: (i, 0, j + nj)),
                  pl.BlockSpec((bb, FFN_CONV_W - 1, FFN_TC), lambda i, j: (i, 0, j)),
                  pl.BlockSpec((FFN_CONV_W, FFN_TC), lambda i, j: (0, j))],
        out_specs=[pl.BlockSpec((bb, t, FFN_TC), lambda i, j: (i, 0, j)),
                   pl.BlockSpec((bb, FFN_CONV_W - 1, FFN_TC), lambda i, j: (i, 0, j))],
        out_shape=[jax.ShapeDtypeStruct((b, t, D_FF), BF16),
                   jax.ShapeDtypeStruct((b, FFN_CONV_W - 1, D_FF), F32)],
        compiler_params=_params("parallel", "parallel"),
        name="ffn_act",
    )(h, h, buf, cw)


GDN_PRE_TC = 512


def _gdn_pre_body(x_ref, buf_ref, cw_ref, o_ref, nb_ref):
    j = pl.program_id(1)
    x = x_ref[...]
    t = x.shape[1]
    buf = buf_ref[...]
    w = cw_ref[...]
    row = lax.broadcasted_iota(jnp.int32, x.shape, 1)
    y = (_shifted(x, buf, 3, row) * w[0:1] + _shifted(x, buf, 2, row) * w[1:2]
         + _shifted(x, buf, 1, row) * w[2:3] + x * w[3:4])
    y = _silu(y)
    is_q = j < GDN_QK_DIM // GDN_PRE_TC
    is_v = j >= 2 * GDN_QK_DIM // GDN_PRE_TC
    qscale = jnp.where(is_q, GDN_DK ** -0.5, 1.0).astype(F32)
    for h in range(GDN_PRE_TC // GDN_DK):
        yh = y[:, :, h * GDN_DK:(h + 1) * GDN_DK]
        inv = lax.rsqrt(jnp.sum(yh * yh, axis=-1, keepdims=True) + L2_EPS)
        o_ref[:, :, h * GDN_DK:(h + 1) * GDN_DK] = jnp.where(is_v, yh, (yh * inv) * qscale)
    nb_ref[...] = x_ref[:, t - (GDN_CONV_W - 1):t, :]


def _gdn_pre(proj, buf, cw):
    b, t, _ = proj.shape
    nj = GDN_CONV_CH // GDN_PRE_TC
    bb = _batch_block(b, t)
    return pl.pallas_call(
        _gdn_pre_body,
        grid=(b // bb, nj),
        in_specs=[pl.BlockSpec((bb, t, GDN_PRE_TC), lambda i, j: (i, 0, j)),
                  pl.BlockSpec((bb, GDN_CONV_W - 1, GDN_PRE_TC), lambda i, j: (i, 0, j)),
                  pl.BlockSpec((GDN_CONV_W, GDN_PRE_TC), lambda i, j: (0, j))],
        out_specs=[pl.BlockSpec((bb, t, GDN_PRE_TC), lambda i, j: (i, 0, j)),
                   pl.BlockSpec((bb, GDN_CONV_W - 1, GDN_PRE_TC), lambda i, j: (i, 0, j))],
        out_shape=[jax.ShapeDtypeStruct((b, t, GDN_CONV_CH), F32),
                   jax.ShapeDtypeStruct((b, GDN_CONV_W - 1, GDN_CONV_CH), F32)],
        compiler_params=_params("parallel", "parallel"),
        name="gdn_pre",
    )(proj, buf, cw)


def _gdn_gate_body(x_ref, alog_ref, dtb_ref, g_ref, gcum_ref, beta_ref, *, chunk):
    x = x_ref[...]
    z = x + dtb_ref[...]
    softplus = jnp.maximum(z, 0.0) + jnp.log(1.0 + jnp.exp(-jnp.abs(z)))
    g = -jnp.exp(alog_ref[...]) * softplus
    g_ref[...] = g
    row = lax.broadcasted_iota(jnp.int32, x.shape, 1) % chunk
    acc = g
    s = 1
    while s < chunk:
        acc = acc + jnp.where(row >= s, pltpu.roll(acc, s, 1), 0.0)
        s *= 2
    gcum_ref[...] = acc
    beta_ref[...] = _sigmoid(x)


def _gdn_gate(proj, alog_pad, dtb_pad, lane_block):
    b, t, _ = proj.shape
    chunk = min(GDN_CHUNK, t)
    bb = _batch_block(b, t)
    spec = pl.BlockSpec((bb, t, 128), lambda i: (i, 0, 0))
    return pl.pallas_call(
        functools.partial(_gdn_gate_body, chunk=chunk),
        grid=(b // bb,),
        in_specs=[pl.BlockSpec((bb, t, 128), lambda i: (i, 0, lane_block)),
                  pl.BlockSpec((1, 128), lambda i: (0, 0)),
                  pl.BlockSpec((1, 128), lambda i: (0, 0))],
        out_specs=[spec, spec, spec],
        out_shape=[jax.ShapeDtypeStruct((b, t, 128), F32)] * 3,
        compiler_params=_params("parallel"),
        name="gdn_gate",
    )(proj, alog_pad, dtb_pad)


def _unit_lower_inverses(lows, n):
    eye = (lax.broadcasted_iota(jnp.int32, (n, n), 0) == lax.broadcasted_iota(jnp.int32, (n, n), 1)).astype(F32)
    ps = [eye - low for low in lows]
    ms = [_mm(low, low) for low in lows]
    k = 2
    while True:
        ps = [p + _mm(p, m) for p, m in zip(ps, ms)]
        k *= 2
        if k >= n:
            break
        ms = [_mm(m, m) for m in ms]
    return ps


def _gdn_scan_body(q_ref, k_ref, v_ref, z_ref, gc_ref, bc_ref, gr_ref, br_ref, s0_ref, onw_ref, o_ref, s_ref,
                   u_ref, w_ref, qk_ref, qd_ref, kd_ref):
    c = GDN_CHUNK
    hb = GDN_HEADS_PER_STEP
    n_chunks = q_ref.shape[1] // c
    s_ref[...] = s0_ref[...]
    ri = lax.broadcasted_iota(jnp.int32, (c, c), 0)
    ci = lax.broadcasted_iota(jnp.int32, (c, c), 1)
    tri = ri >= ci
    stri = ri > ci
    onw = onw_ref[...]

    def qk_slice(ref, r0, qh):
        return ref[0, pl.ds(r0, c), qh * GDN_DK:(qh + 1) * GDN_DK]

    def v_slice(ref, r0, hh):
        return ref[0, pl.ds(r0, c), hh * GDN_DV:(hh + 1) * GDN_DV]

    cpb = 2 if n_chunks % 2 == 0 else 1

    def prep(nb, carry):
        items = []
        kk, qk, ks, qs = {}, {}, {}, {}
        for ch in range(cpb):
            n = nb * cpb + ch
            r0 = pl.multiple_of(n * c, c)
            gcol = gc_ref[0, 0, pl.ds(r0, c), :]
            bcol = bc_ref[0, 0, pl.ds(r0, c), :]
            grow = gr_ref[0, 0, n]
            brow = br_ref[0, 0, n]
            for qh in range(hb // 2):
                ks[ch, qh] = qk_slice(k_ref, r0, qh)
                qs[ch, qh] = qk_slice(q_ref, r0, qh)
            for hh in range(hb):
                items.append(dict(ch=ch, hh=hh, r0=r0, v=v_slice(v_ref, r0, hh),
                                  gcb=jnp.broadcast_to(gcol[:, hh:hh + 1], (c, GDN_DK)),
                                  bcb=jnp.broadcast_to(bcol[:, hh:hh + 1], (c, c)),
                                  gr=grow[hh:hh + 1, :], br=brow[hh:hh + 1, :]))
        for key in ks:
            kk[key] = _mm_nt(ks[key], ks[key])
        for key in ks:
            qk[key] = _mm_nt(qs[key], ks[key])
        lows = []
        for it in items:
            key = (it["ch"], it["hh"] // 2)
            it["decay"] = jnp.where(tri, jnp.exp(jnp.where(tri, it["gcb"][:, :c] - it["gr"], 0.0)), 0.0)
            lows.append(jnp.where(stri, (kk[key] * it["bcb"]) * it["decay"], 0.0))
        tinvs = _unit_lower_inverses(lows, c)
        us = [_mm(tinv * it["br"], it["v"]) for tinv, it in zip(tinvs, items)]
        ws = [_mm(tinv * (it["br"] * jnp.exp(it["gr"])), ks[it["ch"], it["hh"] // 2])
              for tinv, it in zip(tinvs, items)]
        for it, u, w in zip(items, us, ws):
            key = (it["ch"], it["hh"] // 2)
            hh, r0, gcb = it["hh"], it["r0"], it["gcb"]
            u_ref[hh, pl.ds(r0, c), :] = u
            w_ref[hh, pl.ds(r0, c), :] = w.astype(BF16)
            qk_ref[hh, pl.ds(r0, c), :] = jnp.where(tri, qk[key] * it["decay"], 0.0).astype(BF16)
            qd_ref[hh, pl.ds(r0, c), :] = (qs[key] * jnp.exp(gcb)).astype(BF16)
            kd_ref[hh, pl.ds(r0, c), :] = (ks[key] * jnp.exp(gcb[c - 1:c, :] - gcb)).astype(BF16)
        return carry

    lax.fori_loop(0, n_chunks // cpb, prep, 0)

    def scan(n, carry):
        r0 = pl.multiple_of(n * c, c)
        g_last = gc_ref[0, 0, pl.ds(r0 + (c - 1), 1), :]
        loaded = []
        for hh in range(hb):
            loaded.append((u_ref[hh, pl.ds(r0, c), :], w_ref[hh, pl.ds(r0, c), :], qk_ref[hh, pl.ds(r0, c), :],
                           qd_ref[hh, pl.ds(r0, c), :], kd_ref[hh, pl.ds(r0, c), :], v_slice(z_ref, r0, hh),
                           s_ref[0, hh]))
        ws_s = [_mm(w, s) for (u, w, qkm, qd, kd, z, s) in loaded]
        qd_s = [_mm(qd, s) for (u, w, qkm, qd, kd, z, s) in loaded]
        v_news = [ld[0] - ws for ld, ws in zip(loaded, ws_s)]
        qk_v = [_mm(ld[2], vn) for ld, vn in zip(loaded, v_news)]
        kd_v = [_mm_tn(ld[4], vn) for ld, vn in zip(loaded, v_news)]
        results = []
        for hh, ld in enumerate(loaded):
            o = qd_s[hh] + qk_v[hh]
            s_new = ld[6] * jnp.exp(g_last[:, hh:hh + 1]) + kd_v[hh]
            inv = lax.rsqrt(jnp.mean(o * o, axis=-1, keepdims=True) + RMS_EPS)
            results.append((((o * inv) * onw) * _silu(ld[5]), s_new))
        for hh, (og, s_new) in enumerate(results):
            s_ref[0, hh] = s_new
            o_ref[0, pl.ds(r0, c), hh * GDN_DV:(hh + 1) * GDN_DV] = og.astype(o_ref.dtype)
        return carry

    lax.fori_loop(0, n_chunks, scan, 0)


def _gdn_scan(qkv, proj, gcol, bcol, grow, brow, s0, onw):
    b, t, _ = qkv.shape
    hb = GDN_HEADS_PER_STEP
    hg = GDN_V_HEADS // hb
    qw = hb // 2 * GDN_DK
    vw = hb * GDN_DV
    n = t // GDN_CHUNK
    return pl.pallas_call(
        _gdn_scan_body,
        grid=(b, hg),
        in_specs=[pl.BlockSpec((1, t, qw), lambda i, j: (i, 0, j)),
                  pl.BlockSpec((1, t, qw), lambda i, j: (i, 0, GDN_QK_DIM // qw + j)),
                  pl.BlockSpec((1, t, vw), lambda i, j: (i, 0, 2 * GDN_QK_DIM // vw + j)),
                  pl.BlockSpec((1, t, vw), lambda i, j: (i, 0, GDN_CONV_CH // vw + j)),
                  pl.BlockSpec((1, 1, t, hb), lambda i, j: (i, j, 0, 0)),
                  pl.BlockSpec((1, 1, t, hb), lambda i, j: (i, j, 0, 0)),
                  pl.BlockSpec((1, 1, n, hb, GDN_CHUNK), lambda i, j: (i, j, 0, 0, 0)),
                  pl.BlockSpec((1, 1, n, hb, GDN_CHUNK), lambda i, j: (i, j, 0, 0, 0)),
                  pl.BlockSpec((1, hb, GDN_DK, GDN_DV), lambda i, j: (i, j, 0, 0)),
                  pl.BlockSpec((1, GDN_DV), lambda i, j: (0, 0))],
        out_specs=[pl.BlockSpec((1, t, vw), lambda i, j: (i, 0, j)),
                   pl.BlockSpec((1, hb, GDN_DK, GDN_DV), lambda i, j: (i, j, 0, 0))],
        out_shape=[jax.ShapeDtypeStruct((b, t, GDN_V_DIM), BF16),
                   jax.ShapeDtypeStruct((b, GDN_V_HEADS, GDN_DK, GDN_DV), F32)],
        scratch_shapes=[pltpu.VMEM((hb, t, GDN_DV), F32),
                        pltpu.VMEM((hb, t, GDN_DK), BF16),
                        pltpu.VMEM((hb, t, GDN_CHUNK), BF16),
                        pltpu.VMEM((hb, t, GDN_DK), BF16),
                        pltpu.VMEM((hb, t, GDN_DK), BF16)],
        compiler_params=_params("parallel", "parallel"),
        name="gdn_scan",
    )(qkv, qkv, qkv, proj, gcol, bcol, grow, brow, s0, onw.reshape(1, GDN_DV))


def _gdn_layer(x, s0, conv_buf, norm_w, w_in, conv_w, alog_pad, dtb_pad, out_norm_w, w_out):
    b, t, d = x.shape
    xf = x.reshape(b * t, d)
    proj = _rms_matmul(xf, norm_w, w_in, 896).reshape(b, t, -1)
    qkv, new_buf = _gdn_pre(proj, conv_buf, conv_w)
    tail_block = (GDN_CONV_CH + GDN_V_DIM) // 128
    _, gcum, beta = _gdn_gate(proj, alog_pad, dtb_pad, tail_block)
    gcum = gcum[:, :, :GDN_V_HEADS]
    beta = beta[:, :, GDN_V_HEADS:2 * GDN_V_HEADS]
    tp = -(-t // GDN_CHUNK) * GDN_CHUNK
    if tp != t:
        pad = [(0, 0), (0, tp - t), (0, 0)]
        qkv = jnp.pad(qkv, pad)
        proj_z = jnp.pad(proj, pad)
        gcum = jnp.pad(gcum, pad, mode="edge")
        beta = jnp.pad(beta, pad)
    else:
        proj_z = proj
    hb = GDN_HEADS_PER_STEP
    hg = GDN_V_HEADS // hb
    n = tp // GDN_CHUNK
    gcol = gcum.reshape(b, tp, hg, hb).transpose(0, 2, 1, 3)
    bcol = beta.reshape(b, tp, hg, hb).transpose(0, 2, 1, 3)
    grow = gcum.reshape(b, n, GDN_CHUNK, hg, hb).transpose(0, 3, 1, 4, 2)
    brow = beta.reshape(b, n, GDN_CHUNK, hg, hb).transpose(0, 3, 1, 4, 2)
    o, s_new = _gdn_scan(qkv, proj_z, gcol, bcol, grow, brow, s0, out_norm_w)
    o = o[:, :t].reshape(b * t, GDN_V_DIM)
    y = _matmul_res(o, w_out, xf).reshape(b, t, d)
    return y, s_new, new_buf


def _page_gather_body(pt_ref, *refs, n_steps, pps):
    cmp_refs, sel_refs = refs[:pps], refs[pps:2 * pps]
    new_ref, ocmp_ref, osel_ref = refs[2 * pps:]
    p = pl.program_id(1)
    sub = cmp_refs[0].shape[3]
    page = sel_refs[0].shape[3]

    @pl.when(p < n_steps)
    def _():
        for s in range(pps):
            ocmp_ref[0, :, s * sub:(s + 1) * sub, :] = cmp_refs[s][0, 0]
            osel_ref[0, :, s * page:(s + 1) * page, :] = sel_refs[s][0, 0]

    @pl.when(p == n_steps)
    def _():
        osel_ref[0] = new_ref[0]


def _page_gather(cache_cmp, cache_sel, layer, page_table, new_sel):
    b, n_pages = page_table.shape
    _, _, n_cg, sub, flat = cache_cmp.shape
    _, _, _, page, dh = cache_sel.shape
    pps = new_sel.shape[2] // page
    n_steps = n_pages // pps

    def page_map(s):
        return lambda i, p, pt: (layer, pt[i, jnp.minimum(p, n_steps - 1) * pps + s], 0, 0, 0)

    grid_spec = pltpu.PrefetchScalarGridSpec(
        num_scalar_prefetch=1,
        grid=(b, n_steps + 1),
        in_specs=[pl.BlockSpec((1, 1, n_cg, sub, flat), page_map(s)) for s in range(pps)]
        + [pl.BlockSpec((1, 1, n_cg, page, dh), page_map(s)) for s in range(pps)]
        + [pl.BlockSpec((1, n_cg, pps * page, dh), lambda i, p, pt: (i, 0, 0, 0))],
        out_specs=[pl.BlockSpec((1, n_cg, pps * sub, flat),
                                lambda i, p, pt: (i, 0, jnp.minimum(p, n_steps - 1), 0)),
                   pl.BlockSpec((1, n_cg, pps * page, dh), lambda i, p, pt: (i, 0, p, 0))],
    )
    return pl.pallas_call(
        functools.partial(_page_gather_body, n_steps=n_steps, pps=pps),
        grid_spec=grid_spec,
        out_shape=[jax.ShapeDtypeStruct((b, n_cg, n_pages * sub, flat), cache_cmp.dtype),
                   jax.ShapeDtypeStruct((b, n_cg, (n_pages + pps) * page, dh), cache_sel.dtype)],
        compiler_params=_params("parallel", "arbitrary"),
        name="page_gather",
    )(page_table, *([cache_cmp] * pps), *([cache_sel] * pps), new_sel)


def _compress_body(x_ref, w1_ref, pe_ref, w2_ref, o_ref, *, n_sub):
    hd = CMP_HIDDEN
    acc = jnp.dot(x_ref[0, 0], w1_ref[0], preferred_element_type=F32)
    pe = pe_ref[0]
    pe0 = jnp.broadcast_to(pe[0:1], (8, pe.shape[1]))
    pe1 = jnp.broadcast_to(pe[1:2], (8, pe.shape[1]))
    b0 = jnp.dot(pe0.astype(BF16), w1_ref[0, :, :hd], preferred_element_type=F32)[0:1]
    b1 = jnp.dot(pe1.astype(BF16), w1_ref[0, :, hd:], preferred_element_type=F32)[0:1]
    first = acc[:, :hd] + b0
    second = acc[:, hd:] + b1
    hid = _silu(first + pltpu.roll(second, n_sub - 1, 0))
    o_ref[0, 0] = jnp.dot(hid.astype(BF16), w2_ref[0], preferred_element_type=F32)


def _compress(x, n_sub, w1cat, pe_flat, w2):
    b, _, _, flat = x.shape
    dh = flat // CMP_STRIDE
    return pl.pallas_call(
        functools.partial(_compress_body, n_sub=n_sub),
        grid=(b, 2 * NSA_G),
        in_specs=[pl.BlockSpec((1, 1, n_sub, flat), lambda i, j: (i, j, 0, 0)),
                  pl.BlockSpec((1, flat, 2 * CMP_HIDDEN), lambda i, j: (j // NSA_G, 0, 0)),
                  pl.BlockSpec((1, 2, flat), lambda i, j: (j // NSA_G, 0, 0)),
                  pl.BlockSpec((1, CMP_HIDDEN, dh), lambda i, j: (j // NSA_G, 0, 0))],
        out_specs=pl.BlockSpec((1, 1, n_sub, dh), lambda i, j: (i, j, 0, 0)),
        out_shape=jax.ShapeDtypeStruct((b, 2 * NSA_G, n_sub, dh), F32),
        compiler_params=_params("parallel", "parallel"),
        name="nsa_compress",
    )(x, w1cat, pe_flat, w2)


def _inv_or_zero(l):
    return jnp.where(l > 0.0, 1.0 / jnp.where(l > 0.0, l, 1.0), 0.0)


def _nsa_attn_body(q_ref, gate_ref, slope_ref, ck_ref, cv_ref, ks_ref, vs_ref, kw_ref, vw_ref, o_ref, osel_ref,
                   *, qb, p_len, n_cmp, n_blk, w0, sel_classes, kw_len):
    i = pl.program_id(2)
    r = NSA_R
    rows = r * qb
    q = (q_ref[0, 0].reshape(rows, NSA_DH) * (NSA_DH ** -0.5 * LOG2E)).astype(BF16)
    slopes = slope_ref[0] * LOG2E
    q0 = p_len + i * qb
    t_pos = q0 + lax.broadcasted_iota(jnp.int32, (1, qb, 1), 1)

    def branch(k, v, kp, ok_fn, may_be_empty=False):
        nk = k.shape[0]
        dist = t_pos - kp
        ok = ok_fn(dist)
        pen = jnp.where(ok, dist.astype(F32), MASKED_DIST)
        s = _mm_nt(q, k).reshape(r, qb, nk) - slopes * pen
        m = jnp.max(s, axis=-1, keepdims=True)
        p = jnp.exp2(s - m)
        if may_be_empty:
            p = jnp.where(ok, p, 0.0)
        inv = _inv_or_zero(jnp.sum(p, axis=-1, keepdims=True))
        o = _mm(p.reshape(rows, nk), v) * inv.reshape(rows, 1)
        return o, p, inv

    nc = ck_ref.shape[2]
    n_idx = lax.broadcasted_iota(jnp.int32, (1, 1, nc), 2)
    o_cmp, p_cmp, inv_cmp = branch(ck_ref[0, 0], cv_ref[0, 0], n_idx * CMP_STRIDE + (CMP_LEN - 1),
                                   lambda dist: (dist >= 0) & (n_idx < n_cmp), may_be_empty=True)
    p_sum = jnp.sum(p_cmp * inv_cmp, axis=0)

    nb8 = -(-n_blk // 8) * 8
    jj = lax.broadcasted_iota(jnp.int32, (nb8, nc), 0)
    nn = lax.broadcasted_iota(jnp.int32, (nb8, nc), 1)
    overlap_t = ((nn * CMP_STRIDE < (jj + 1) * SEL_BLOCK) & (nn * CMP_STRIDE + CMP_LEN > jj * SEL_BLOCK)
                 & (nn < n_cmp) & (jj < n_blk)).astype(F32)
    imp = lax.dot_general(overlap_t, p_sum, (((1,), (1,)), ((), ())), preferred_element_type=F32,
                          precision=lax.Precision.HIGHEST)
    j = lax.broadcasted_iota(jnp.int32, (nb8, qb), 0)
    cur = (q0 + lax.broadcasted_iota(jnp.int32, (1, qb), 1)) // SEL_BLOCK
    imp = jnp.where((j == 0) | (j == cur) | (j == cur - 1), FORCE_SCORE, imp)
    imp = jnp.where(j > cur, -FORCE_SCORE, imp)
    imp = jnp.where(j >= n_blk, NEVER, imp)
    rank = jnp.zeros((nb8, qb), jnp.int32)
    for jp in range(n_blk):
        row = imp[jp:jp + 1, :]
        rank = rank + ((row > imp) | ((row == imp) & (j > jp))).astype(jnp.int32)
    picked_t = (rank < SEL_TOPN).astype(BF16)
    eye = (lax.broadcasted_iota(jnp.int32, (qb, qb), 0) == lax.broadcasted_iota(jnp.int32, (qb, qb), 1))
    picked = _mm_nt(eye.astype(BF16), picked_t)

    def sel_branch(nk):
        blk_of_key = lax.broadcasted_iota(jnp.int32, (nb8, nk), 1) // SEL_BLOCK
        expand = (blk_of_key == lax.broadcasted_iota(jnp.int32, (nb8, nk), 0)).astype(BF16)
        key_picked = (_mm(picked, expand) > 0.5).reshape(1, qb, nk)
        o, _, _ = branch(ks_ref[0, 0, 0:nk, :], vs_ref[0, 0, 0:nk, :],
                         lax.broadcasted_iota(jnp.int32, (1, 1, nk), 2),
                         lambda dist: key_picked & (dist >= 0))
        osel_ref[...] = o

    if len(sel_classes) == 1:
        sel_branch(sel_classes[0])
    else:
        need = q0 + qb
        prev = 0
        for nk in sel_classes:
            pl.when((need > prev) & (need <= nk))(functools.partial(sel_branch, nk))
            prev = nk

    tw = kw_ref.shape[2]
    if tw == kw_len:
        start = 0
    else:
        start = pl.multiple_of(jnp.clip(q0 + qb - w0 - kw_len, 0, tw - kw_len), SEL_BLOCK)
    kp_win = w0 + start + lax.broadcasted_iota(jnp.int32, (1, 1, kw_len), 2)
    o_win, _, _ = branch(kw_ref[0, 0, pl.ds(start, kw_len), :], vw_ref[0, 0, pl.ds(start, kw_len), :], kp_win,
                         lambda dist: (dist >= 0) & (dist < WINDOW) & (kp_win >= 0))

    gates = _sigmoid(gate_ref[0, 0]).reshape(rows, 3)
    o = gates[:, 0:1] * o_cmp + gates[:, 1:2] * osel_ref[...] + gates[:, 2:3] * o_win
    o_ref[0, 0] = o.reshape(r, qb, NSA_DH)


def _nsa_attn(q_t, gates_t, slopes, cmp_kv, sel_arr, sel_k0, sel_v0, win_arr, win_k0, win_v0,
              *, p_len, n_cmp, n_blk, w0):
    b, g, r, tq, dh = q_t.shape
    qb = NSA_Q_BLOCK if tq % NSA_Q_BLOCK == 0 else tq
    nc = cmp_kv.shape[2]
    tk = sel_arr.shape[2]
    tw = win_arr.shape[2]
    if tq == qb or tk % NSA_SEL_CLASS != 0:
        sel_classes = (tk,)
    else:
        sel_classes = tuple(range(NSA_SEL_CLASS, tk + 1, NSA_SEL_CLASS))
    kw_len = min(tw, -(-(WINDOW - 1 + qb) // KEY_CHUNK) * KEY_CHUNK)
    body = functools.partial(_nsa_attn_body, qb=qb, p_len=p_len, n_cmp=n_cmp, n_blk=n_blk, w0=w0,
                             sel_classes=sel_classes, kw_len=kw_len)
    return pl.pallas_call(
        body,
        grid=(b, g, tq // qb),
        in_specs=[pl.BlockSpec((1, 1, r, qb, dh), lambda bi, gi, i: (bi, gi, 0, i, 0)),
                  pl.BlockSpec((1, 1, r, qb, 3), lambda bi, gi, i: (bi, gi, 0, i, 0)),
                  pl.BlockSpec((1, r, 1, 1), lambda bi, gi, i: (gi, 0, 0, 0)),
                  pl.BlockSpec((1, 1, nc, dh), lambda bi, gi, i: (bi, gi, 0, 0)),
                  pl.BlockSpec((1, 1, nc, dh), lambda bi, gi, i: (bi, NSA_G + gi, 0, 0)),
                  pl.BlockSpec((1, 1, tk, dh), lambda bi, gi, i: (bi, sel_k0 + gi, 0, 0)),
                  pl.BlockSpec((1, 1, tk, dh), lambda bi, gi, i: (bi, sel_v0 + gi, 0, 0)),
                  pl.BlockSpec((1, 1, tw, dh), lambda bi, gi, i: (bi, win_k0 + gi, 0, 0)),
                  pl.BlockSpec((1, 1, tw, dh), lambda bi, gi, i: (bi, win_v0 + gi, 0, 0))],
        out_specs=pl.BlockSpec((1, 1, r, qb, dh), lambda bi, gi, i: (bi, gi, 0, i, 0)),
        out_shape=jax.ShapeDtypeStruct((b, g, r, tq, dh), F32),
        scratch_shapes=[pltpu.VMEM((r * qb, dh), F32)],
        compiler_params=_params("parallel", "parallel", "arbitrary"),
        name="nsa_attn",
    )(q_t, gates_t, slopes, cmp_kv, cmp_kv, sel_arr, sel_arr, win_arr, win_arr)


def _nsa_layer(x, cache_hm, layer, page_table, win_buf, norm_w, w_in, w1cat, pe_flat, w2, w_out, slopes):
    b, t, d = x.shape
    g, r, dh = NSA_G, NSA_R, NSA_DH
    xf = x.reshape(b * t, d)
    proj = _rms_matmul(xf, norm_w, w_in, 896).reshape(b, t, -1)
    q_t = proj[..., :NSA_Q_DIM].reshape(b, t, g, r, dh).transpose(0, 2, 3, 1, 4)
    kv = proj[..., NSA_Q_DIM:NSA_Q_DIM + 6 * NSA_KV_DIM]
    gates_t = proj[..., NSA_Q_DIM + 6 * NSA_KV_DIM:NSA_Q_DIM + 6 * NSA_KV_DIM + 3 * NSA_HEADS]
    gates_t = gates_t.reshape(b, t, g, r, 3).transpose(0, 2, 3, 1, 4)
    kv6 = kv.reshape(b, t, 6, g, dh)
    new_rows = kv6[:, :, :4]
    if cache_hm is None:
        p_len = 0
        tk = t
        assert t % CMP_STRIDE == 0
        cmp_x = kv6[:, :, :2].astype(BF16).transpose(0, 2, 3, 1, 4).reshape(b, 2 * g, t // CMP_STRIDE,
                                                                           CMP_STRIDE * dh)
        kv_hm = kv6[:, :, 2:].astype(BF16).transpose(0, 2, 3, 1, 4).reshape(b, 4 * g, t, dh)
        sel_arr, sel_k0, sel_v0 = kv_hm, 0, g
        win_arr, win_k0, win_v0 = kv_hm, 2 * g, 3 * g
        w0 = 0
        new_win = kv6[:, t - min(WINDOW, t):, 4:]
    else:
        cache_cmp, cache_sel = cache_hm
        n_pages = page_table.shape[1]
        page = cache_sel.shape[3]
        p_len = n_pages * page
        tk = p_len + t
        assert p_len % CMP_STRIDE == 0 and t < CMP_STRIDE
        pps = 4 if n_pages % 4 == 0 else 1
        new_sel = kv6[:, :, 2:4].astype(BF16).transpose(0, 2, 3, 1, 4).reshape(b, 2 * g, t, dh)
        new_sel = jnp.pad(new_sel, [(0, 0), (0, 0), (0, pps * page - t), (0, 0)])
        cmp_x, sel_arr = _page_gather(cache_cmp, cache_sel, layer, page_table, new_sel)
        sel_k0, sel_v0 = 0, g
        win_all = jnp.concatenate([win_buf, kv6[:, :, 4:]], axis=1)
        wl = win_all.shape[1]
        wlp = -(-wl // KEY_CHUNK) * KEY_CHUNK
        win_arr = jnp.pad(win_all.astype(BF16), [(0, 0), (0, wlp - wl), (0, 0), (0, 0), (0, 0)])
        win_arr = win_arr.transpose(0, 2, 3, 1, 4).reshape(b, 2 * g, wlp, dh)
        win_k0, win_v0 = 0, g
        w0 = p_len - win_buf.shape[1]
        new_win = win_all[:, wl - min(WINDOW, tk):]
    n_sub = tk // CMP_STRIDE
    n_cmp = n_sub - 1
    n_blk = -(-tk // SEL_BLOCK)
    cmp_kv = _compress(cmp_x, n_sub, w1cat, pe_flat, w2)
    o_t = _nsa_attn(q_t, gates_t, slopes, cmp_kv, sel_arr, sel_k0, sel_v0, win_arr, win_k0, win_v0,
                    p_len=p_len, n_cmp=n_cmp, n_blk=n_blk, w0=w0)
    o = o_t.transpose(0, 3, 1, 2, 4).reshape(b * t, NSA_Q_DIM)
    y = _matmul_res(o, w_out, xf).reshape(b, t, d)
    return y, new_rows, new_win


def _ffn_layer(x, buf, norm_w, w_up, conv_w, w_down):
    b, t, d = x.shape
    xf = x.reshape(b * t, d)
    h = _rms_matmul(xf, norm_w, w_up, 512).reshape(b, t, -1)
    act, new_buf = _ffn_act(h, buf, conv_w)
    y = _matmul_res(act.reshape(b * t, D_FF), w_down, xf).reshape(b, t, d)
    return y, new_buf


def _pad_cols(w, n):
    return jnp.pad(w, [(0, 0)] * (w.ndim - 1) + [(0, n - w.shape[-1])])


def _trunk(x, cache_hm, page_table, nsa_win, gdn_state, gdn_conv, ffn_conv, wts):
    depth = wts["ffn_w_up"].shape[0]
    rows_l, win_l, s_l, gconv_l, fconv_l = [], [], [], [], []
    for i in range(depth):
        j = i // 2
        if i % 2 == 0:
            x, s_new, cb = _gdn_layer(x, gdn_state[j], gdn_conv[j], wts["gdn_norm"][j], wts["gdn_w_in"][j],
                                      wts["gdn_conv_w"][j], wts["gdn_a_log"][j], wts["gdn_dt_bias"][j],
                                      wts["gdn_out_norm"][j], wts["gdn_w_out"][j])
            s_l.append(s_new)
            gconv_l.append(cb)
        else:
            x, rows, wb = _nsa_layer(x, cache_hm, j, page_table, None if nsa_win is None else nsa_win[j],
                                     wts["nsa_norm"][j], wts["nsa_w_in"][j], wts["nsa_w1cat"][j],
                                     wts["nsa_pe_flat"][j], wts["nsa_w2"][j], wts["nsa_w_out"][j],
                                     wts["slopes"])
            rows_l.append(rows)
            win_l.append(wb)
        x, fb = _ffn_layer(x, ffn_conv[i], wts["ffn_norm"][i], wts["ffn_w_up"][i], wts["ffn_conv_w"][i],
                           wts["ffn_w_down"][i])
        fconv_l.append(fb)
    b, t, d = x.shape
    y = _rms(x.reshape(b * t, d), wts["final_norm"]).reshape(b, t, d)
    return (y, jnp.stack(rows_l), jnp.stack(win_l), jnp.stack(s_l), jnp.stack(gconv_l), jnp.stack(fconv_l))


def kernel(x_prompt, x_sample, cache_nsa_kv, cache_nsa_win, state_gdn_s, state_gdn_conv, state_ffn_conv,
           page_table, gdn_norm, gdn_w_in, gdn_conv_w, gdn_a_log, gdn_dt_bias, gdn_out_norm, gdn_w_out,
           nsa_norm, nsa_w_in, nsa_cmp_pe, nsa_cmp_w1, nsa_cmp_w2, nsa_w_out,
           ffn_norm, ffn_w_up, ffn_conv_w, ffn_w_down, final_norm):
    n_gdn = gdn_w_in.shape[0]
    n_nsa = nsa_w_in.shape[0]
    depth = ffn_w_up.shape[0]
    bp = x_prompt.shape[0]
    gdn_cols = -(-gdn_w_in.shape[2] // 896) * 896
    nsa_cols = -(-nsa_w_in.shape[2] // 896) * 896
    w1 = nsa_cmp_w1.reshape(n_nsa, 2, 2, CMP_STRIDE * NSA_DH, CMP_HIDDEN)
    w1cat = jnp.concatenate([w1[:, :, 0], w1[:, :, 1]], axis=-1).astype(BF16)
    slopes = 2.0 ** (-8.0 * jnp.arange(1, NSA_HEADS + 1, dtype=F32) / NSA_HEADS)
    wts = {
        "gdn_norm": gdn_norm,
        "gdn_w_in": _pad_cols(gdn_w_in, gdn_cols).astype(BF16),
        "gdn_conv_w": gdn_conv_w,
        "gdn_a_log": _pad_cols(gdn_a_log, 128).reshape(n_gdn, 1, 128),
        "gdn_dt_bias": _pad_cols(gdn_dt_bias, 128).reshape(n_gdn, 1, 128),
        "gdn_out_norm": gdn_out_norm,
        "gdn_w_out": gdn_w_out.astype(BF16),
        "nsa_norm": nsa_norm,
        "nsa_w_in": _pad_cols(nsa_w_in, nsa_cols).astype(BF16),
        "nsa_w1cat": w1cat,
        "nsa_pe_flat": nsa_cmp_pe.reshape(n_nsa, 2, 2, CMP_STRIDE * NSA_DH),
        "nsa_w2": nsa_cmp_w2.astype(BF16),
        "nsa_w_out": nsa_w_out.astype(BF16),
        "slopes": slopes.reshape(NSA_G, NSA_R, 1, 1),
        "ffn_norm": ffn_norm,
        "ffn_w_up": ffn_w_up.astype(BF16),
        "ffn_conv_w": ffn_conv_w,
        "ffn_w_down": ffn_w_down.astype(BF16),
        "final_norm": final_norm,
    }
    n_l, pool, page = cache_nsa_kv.shape[:3]
    cache_t = cache_nsa_kv.astype(BF16).transpose(0, 1, 3, 4, 2, 5)
    cache_hm = (cache_t[:, :, :2].reshape(n_l, pool, 2 * NSA_G, page // CMP_STRIDE, CMP_STRIDE * NSA_DH),
                cache_t[:, :, 2:].reshape(n_l, pool, 2 * NSA_G, page, NSA_DH))

    zeros = functools.partial(jnp.zeros, dtype=F32)
    prompt = _trunk(x_prompt, None, None, None,
                    zeros((n_gdn, bp, GDN_V_HEADS, GDN_DK, GDN_DV)),
                    zeros((n_gdn, bp, GDN_CONV_W - 1, GDN_CONV_CH)),
                    zeros((depth, bp, FFN_CONV_W - 1, D_FF)), wts)
    sample = _trunk(x_sample, cache_hm, page_table, cache_nsa_win, state_gdn_s, state_gdn_conv, state_ffn_conv, wts)
    out = []
    for p, s in zip(prompt, sample):
        out.extend([p, s])
    return tuple(out)
```

```python
import functools

import jax
import jax.numpy as jnp
from jax import lax
from jax.experimental import pallas as pl
from jax.experimental.pallas import tpu as pltpu

F32 = jnp.float32
BF16 = jnp.bfloat16

RMS_EPS = 1e-6
L2_EPS = 1e-6
NEG_INF = -1e30
FORCE_SCORE = 1e9
NEVER = -3e38
MASKED_DIST = 1e30
LOG2E = 1.4426950408889634

GDN_QK_HEADS = 8
GDN_V_HEADS = 16
GDN_DK = 128
GDN_DV = 128
GDN_QK_DIM = GDN_QK_HEADS * GDN_DK
GDN_V_DIM = GDN_V_HEADS * GDN_DV
GDN_CONV_CH = 2 * GDN_QK_DIM + GDN_V_DIM
GDN_CONV_W = 4
GDN_CHUNK = 64
GDN_HEADS_PER_STEP = 4
NSA_HEADS = 16
NSA_G = 4
NSA_R = 4
NSA_DH = 64
NSA_Q_DIM = NSA_HEADS * NSA_DH
NSA_KV_DIM = NSA_G * NSA_DH
CMP_STRIDE = 16
CMP_LEN = 32
CMP_HIDDEN = 256
SEL_BLOCK = 64
SEL_TOPN = 8
WINDOW = 512
NSA_Q_BLOCK = 64
KEY_CHUNK = 128
NSA_SEL_CLASS = 512
D_FF = 2816
FFN_CONV_W = 3
FFN_TC = 256
SEQ_ROWS_PER_STEP = 2048

VMEM_LIMIT = 52 * 1024 * 1024


def _params(*sem):
    return pltpu.CompilerParams(dimension_semantics=sem, vmem_limit_bytes=VMEM_LIMIT)


def _mm(a, b):
    return jnp.dot(a.astype(BF16), b.astype(BF16), preferred_element_type=F32)


def _mm_nt(a, b):
    return lax.dot_general(a.astype(BF16), b.astype(BF16), (((1,), (1,)), ((), ())),
                           preferred_element_type=F32)


def _mm_tn(a, b):
    return lax.dot_general(a.astype(BF16), b.astype(BF16), (((0,), (0,)), ((), ())),
                           preferred_element_type=F32)


def _sigmoid(x):
    return 1.0 / (1.0 + jnp.exp(-x))


def _silu(x):
    return x * _sigmoid(x)


def _rms_matmul_body(x_ref, nw_ref, w_ref, o_ref, xn_ref):
    @pl.when(pl.program_id(1) == 0)
    def _():
        x = x_ref[...]
        inv = lax.rsqrt(jnp.mean(x * x, axis=-1, keepdims=True) + RMS_EPS)
        xn_ref[...] = ((x * inv) * nw_ref[...]).astype(BF16)

    o_ref[...] = jnp.dot(xn_ref[...], w_ref[...], preferred_element_type=F32)


def _rms_matmul(x, nw, w, tn):
    m, k = x.shape
    n = w.shape[1]
    tm = min(m, 1024)
    return pl.pallas_call(
        _rms_matmul_body,
        grid=(m // tm, n // tn),
        in_specs=[pl.BlockSpec((tm, k), lambda i, j: (i, 0)),
                  pl.BlockSpec((1, k), lambda i, j: (0, 0)),
                  pl.BlockSpec((k, tn), lambda i, j: (0, j))],
        out_specs=pl.BlockSpec((tm, tn), lambda i, j: (i, j)),
        out_shape=jax.ShapeDtypeStruct((m, n), F32),
        scratch_shapes=[pltpu.VMEM((tm, k), BF16)],
        compiler_params=_params("parallel", "arbitrary"),
        name="rms_matmul",
    )(x, nw.reshape(1, k), w)


def _matmul_res_body(a_ref, w_ref, r_ref, o_ref):
    o_ref[...] = r_ref[...] + jnp.dot(a_ref[...].astype(BF16), w_ref[...], preferred_element_type=F32)


def _matmul_res(a, w, res):
    m, k = a.shape
    n = w.shape[1]
    tm = min(m, 512)
    return pl.pallas_call(
        _matmul_res_body,
        grid=(m // tm,),
        in_specs=[pl.BlockSpec((tm, k), lambda i: (i, 0)),
                  pl.BlockSpec((k, n), lambda i: (0, 0)),
                  pl.BlockSpec((tm, n), lambda i: (i, 0))],
        out_specs=pl.BlockSpec((tm, n), lambda i: (i, 0)),
        out_shape=jax.ShapeDtypeStruct((m, n), F32),
        compiler_params=_params("parallel"),
        name="matmul_res",
    )(a, w, res)


def _rms_body(x_ref, nw_ref, o_ref):
    x = x_ref[...]
    inv = lax.rsqrt(jnp.mean(x * x, axis=-1, keepdims=True) + RMS_EPS)
    o_ref[...] = (x * inv) * nw_ref[...]


def _rms(x, nw):
    m, k = x.shape
    tm = min(m, 1024)
    return pl.pallas_call(
        _rms_body,
        grid=(m // tm,),
        in_specs=[pl.BlockSpec((tm, k), lambda i: (i, 0)), pl.BlockSpec((1, k), lambda i: (0, 0))],
        out_specs=pl.BlockSpec((tm, k), lambda i: (i, 0)),
        out_shape=jax.ShapeDtypeStruct((m, k), F32),
        compiler_params=_params("parallel"),
        name="final_rms",
    )(x, nw.reshape(1, k))


def _shifted(x, prev_rows, shift, row):
    nb = prev_rows.shape[1]
    y = pltpu.roll(x, shift, 1)
    for r in range(shift):
        y = jnp.where(row == r, prev_rows[:, nb - shift + r:nb - shift + r + 1], y)
    return y


def _batch_block(b, t):
    bb = max(1, min(b, SEQ_ROWS_PER_STEP // t))
    while b % bb:
        bb -= 1
    return bb


def _ffn_act_body(a_ref, g_ref, buf_ref, cw_ref, act_ref, nb_ref):
    a = a_ref[...]
    t = a.shape[1]
    buf = buf_ref[...]
    w = cw_ref[...]
    row = lax.broadcasted_iota(jnp.int32, a.shape, 1)
    y = _shifted(a, buf, 2, row) * w[0:1] + _shifted(a, buf, 1, row) * w[1:2] + a * w[2:3]
    act_ref[...] = (_silu(y) * g_ref[...]).astype(act_ref.dtype)
    nb_ref[...] = a_ref[:, t - (FFN_CONV_W - 1):t, :]


def _ffn_act(h, buf, cw):
    b, t, _ = h.shape
    nj = D_FF // FFN_TC
    bb = _batch_block(b, t)
    return pl.pallas_call(
        _ffn_act_body,
        grid=(b // bb, nj),
        in_specs=[pl.BlockSpec((bb, t, FFN_TC), lambda i, j: (i, 0, j)),
                  pl.BlockSpec((bb, t, FFN_TC), lambda i, j: (i, 0, j + nj)),
                  pl.BlockSpec((bb, FFN_CONV_W - 1, FFN_TC), lambda i, j: (i, 0, j)),
                  pl.BlockSpec((FFN_CONV_W, FFN_TC), lambda i, j: (0, j))],
        out_specs=[pl.BlockSpec((bb, t, FFN_TC), lambda i, j: (i, 0, j)),
                   pl.BlockSpec((bb, FFN_CONV_W - 1, FFN_TC), lambda i, j: (i, 0, j))],
        out_shape=[jax.ShapeDtypeStruct((b, t, D_FF), BF16),
                   jax.ShapeDtypeStruct((b, FFN_CONV_W - 1, D_FF), F32)],
        compiler_params=_params("parallel", "parallel"),
        name="ffn_act",
    )(h, h, buf, cw)


GDN_PRE_TC = 512


def _gdn_pre_body(x_ref, buf_ref, cw_ref, o_ref, nb_ref):
    j = pl.program_id(1)
    x = x_ref[...]
    t = x.shape[1]
    buf = buf_ref[...]
    w = cw_ref[...]
    row = lax.broadcasted_iota(jnp.int32, x.shape, 1)
    y = (_shifted(x, buf, 3, row) * w[0:1] + _shifted(x, buf, 2, row) * w[1:2]
         + _shifted(x, buf, 1, row) * w[2:3] + x * w[3:4])
    y = _silu(y)
    is_q = j < GDN_QK_DIM // GDN_PRE_TC
    is_v = j >= 2 * GDN_QK_DIM // GDN_PRE_TC
    qscale = jnp.where(is_q, GDN_DK ** -0.5, 1.0).astype(F32)
    for h in range(GDN_PRE_TC // GDN_DK):
        yh = y[:, :, h * GDN_DK:(h + 1) * GDN_DK]
        inv = lax.rsqrt(jnp.sum(yh * yh, axis=-1, keepdims=True) + L2_EPS)
        o_ref[:, :, h * GDN_DK:(h + 1) * GDN_DK] = jnp.where(is_v, yh, (yh * inv) * qscale)
    nb_ref[...] = x_ref[:, t - (GDN_CONV_W - 1):t, :]


def _gdn_pre(proj, buf, cw):
    b, t, _ = proj.shape
    nj = GDN_CONV_CH // GDN_PRE_TC
    bb = _batch_block(b, t)
    return pl.pallas_call(
        _gdn_pre_body,
        grid=(b // bb, nj),
        in_specs=[pl.BlockSpec((bb, t, GDN_PRE_TC), lambda i, j: (i, 0, j)),
                  pl.BlockSpec((bb, GDN_CONV_W - 1, GDN_PRE_TC), lambda i, j: (i, 0, j)),
                  pl.BlockSpec((GDN_CONV_W, GDN_PRE_TC), lambda i, j: (0, j))],
        out_specs=[pl.BlockSpec((bb, t, GDN_PRE_TC), lambda i, j: (i, 0, j)),
                   pl.BlockSpec((bb, GDN_CONV_W - 1, GDN_PRE_TC), lambda i, j: (i, 0, j))],
        out_shape=[jax.ShapeDtypeStruct((b, t, GDN_CONV_CH), F32),
                   jax.ShapeDtypeStruct((b, GDN_CONV_W - 1, GDN_CONV_CH), F32)],
        compiler_params=_params("parallel", "parallel"),
        name="gdn_pre",
    )(proj, buf, cw)


def _gdn_gate_body(x_ref, alog_ref, dtb_ref, g_ref, gcum_ref, beta_ref, *, chunk):
    x = x_ref[...]
    z = x + dtb_ref[...]
    softplus = jnp.maximum(z, 0.0) + jnp.log(1.0 + jnp.exp(-jnp.abs(z)))
    g = -jnp.exp(alog_ref[...]) * softplus
    g_ref[...] = g
    row = lax.broadcasted_iota(jnp.int32, x.shape, 1) % chunk
    acc = g
    s = 1
    while s < chunk:
        acc = acc + jnp.where(row >= s, pltpu.roll(acc, s, 1), 0.0)
        s *= 2
    gcum_ref[...] = acc
    beta_ref[...] = _sigmoid(x)


def _gdn_gate(proj, alog_pad, dtb_pad, lane_block):
    b, t, _ = proj.shape
    chunk = min(GDN_CHUNK, t)
    bb = _batch_block(b, t)
    spec = pl.BlockSpec((bb, t, 128), lambda i: (i, 0, 0))
    return pl.pallas_call(
        functools.partial(_gdn_gate_body, chunk=chunk),
        grid=(b // bb,),
        in_specs=[pl.BlockSpec((bb, t, 128), lambda i: (i, 0, lane_block)),
                  pl.BlockSpec((1, 128), lambda i: (0, 0)),
                  pl.BlockSpec((1, 128), lambda i: (0, 0))],
        out_specs=[spec, spec, spec],
        out_shape=[jax.ShapeDtypeStruct((b, t, 128), F32)] * 3,
        compiler_params=_params("parallel"),
        name="gdn_gate",
    )(proj, alog_pad, dtb_pad)


def _unit_lower_inverses(lows, n):
    eye = (lax.broadcasted_iota(jnp.int32, (n, n), 0) == lax.broadcasted_iota(jnp.int32, (n, n), 1)).astype(F32)
    ps = [eye - low for low in lows]
    ms = [_mm(low, low) for low in lows]
    k = 2
    while True:
        ps = [p + _mm(p, m) for p, m in zip(ps, ms)]
        k *= 2
        if k >= n:
            break
        ms = [_mm(m, m) for m in ms]
    return ps


def _gdn_scan_body(q_ref, k_ref, v_ref, z_ref, gc_ref, bc_ref, gr_ref, br_ref, s0_ref, onw_ref, o_ref, s_ref,
                   u_ref, w_ref, qk_ref, qd_ref, kd_ref):
    c = GDN_CHUNK
    hb = GDN_HEADS_PER_STEP
    n_chunks = q_ref.shape[1] // c
    s_ref[...] = s0_ref[...]
    ri = lax.broadcasted_iota(jnp.int32, (c, c), 0)
    ci = lax.broadcasted_iota(jnp.int32, (c, c), 1)
    tri = ri >= ci
    stri = ri > ci
    onw = onw_ref[...]

    def qk_slice(ref, r0, qh):
        return ref[0, pl.ds(r0, c), qh * GDN_DK:(qh + 1) * GDN_DK]

    def v_slice(ref, r0, hh):
        return ref[0, pl.ds(r0, c), hh * GDN_DV:(hh + 1) * GDN_DV]

    cpb = 2 if n_chunks % 2 == 0 else 1

    def prep(nb, carry):
        items = []
        kk, qk, ks, qs = {}, {}, {}, {}
        for ch in range(cpb):
            n = nb * cpb + ch
            r0 = pl.multiple_of(n * c, c)
            gcol = gc_ref[0, 0, pl.ds(r0, c), :]
            bcol = bc_ref[0, 0, pl.ds(r0, c), :]
            grow = gr_ref[0, 0, n]
            brow = br_ref[0, 0, n]
            for qh in range(hb // 2):
                ks[ch, qh] = qk_slice(k_ref, r0, qh)
                qs[ch, qh] = qk_slice(q_ref, r0, qh)
            for hh in range(hb):
                items.append(dict(ch=ch, hh=hh, r0=r0, v=v_slice(v_ref, r0, hh),
                                  gcb=jnp.broadcast_to(gcol[:, hh:hh + 1], (c, GDN_DK)),
                                  bcb=jnp.broadcast_to(bcol[:, hh:hh + 1], (c, c)),
                                  gr=grow[hh:hh + 1, :], br=brow[hh:hh + 1, :]))
        for key in ks:
            kk[key] = _mm_nt(ks[key], ks[key])
        for key in ks:
            qk[key] = _mm_nt(qs[key], ks[key])
        lows = []
        for it in items:
            key = (it["ch"], it["hh"] // 2)
            it["decay"] = jnp.where(tri, jnp.exp(jnp.where(tri, it["gcb"][:, :c] - it["gr"], 0.0)), 0.0)
            lows.append(jnp.where(stri, (kk[key] * it["bcb"]) * it["decay"], 0.0))
        tinvs = _unit_lower_inverses(lows, c)
        us = [_mm(tinv * it["br"], it["v"]) for tinv, it in zip(tinvs, items)]
        ws = [_mm(tinv * (it["br"] * jnp.exp(it["gr"])), ks[it["ch"], it["hh"] // 2])
              for tinv, it in zip(tinvs, items)]
        for it, u, w in zip(items, us, ws):
            key = (it["ch"], it["hh"] // 2)
            hh, r0, gcb = it["hh"], it["r0"], it["gcb"]
            u_ref[hh, pl.ds(r0, c), :] = u
            w_ref[hh, pl.ds(r0, c), :] = w.astype(BF16)
            qk_ref[hh, pl.ds(r0, c), :] = jnp.where(tri, qk[key] * it["decay"], 0.0).astype(BF16)
            qd_ref[hh, pl.ds(r0, c), :] = (qs[key] * jnp.exp(gcb)).astype(BF16)
            kd_ref[hh, pl.ds(r0, c), :] = (ks[key] * jnp.exp(gcb[c - 1:c, :] - gcb)).astype(BF16)
        return carry

    lax.fori_loop(0, n_chunks // cpb, prep, 0)

    def scan(n, carry):
        r0 = pl.multiple_of(n * c, c)
        g_last = gc_ref[0, 0, pl.ds(r0 + (c - 1), 1), :]
        loaded = []
        for hh in range(hb):
            loaded.append((u_ref[hh, pl.ds(r0, c), :], w_ref[hh, pl.ds(r0, c), :], qk_ref[hh, pl.ds(r0, c), :],
                           qd_ref[hh, pl.ds(r0, c), :], kd_ref[hh, pl.ds(r0, c), :], v_slice(z_ref, r0, hh),
                           s_ref[0, hh]))
        ws_s = [_mm(w, s) for (u, w, qkm, qd, kd, z, s) in loaded]
        qd_s = [_mm(qd, s) for (u, w, qkm, qd, kd, z, s) in loaded]
        v_news = [ld[0] - ws for ld, ws in zip(loaded, ws_s)]
        qk_v = [_mm(ld[2], vn) for ld, vn in zip(loaded, v_news)]
        kd_v = [_mm_tn(ld[4], vn) for ld, vn in zip(loaded, v_news)]
        results = []
        for hh, ld in enumerate(loaded):
            o = qd_s[hh] + qk_v[hh]
            s_new = ld[6] * jnp.exp(g_last[:, hh:hh + 1]) + kd_v[hh]
            inv = lax.rsqrt(jnp.mean(o * o, axis=-1, keepdims=True) + RMS_EPS)
            results.append((((o * inv) * onw) * _silu(ld[5]), s_new))
        for hh, (og, s_new) in enumerate(results):
            s_ref[0, hh] = s_new
            o_ref[0, pl.ds(r0, c), hh * GDN_DV:(hh + 1) * GDN_DV] = og.astype(o_ref.dtype)
        return carry

    lax.fori_loop(0, n_chunks, scan, 0)


def _gdn_scan(qkv, proj, gcol, bcol, grow, brow, s0, onw):
    b, t, _ = qkv.shape
    hb = GDN_HEADS_PER_STEP
    hg = GDN_V_HEADS // hb
    qw = hb // 2 * GDN_DK
    vw = hb * GDN_DV
    n = t // GDN_CHUNK
    return pl.pallas_call(
        _gdn_scan_body,
        grid=(b, hg),
        in_specs=[pl.BlockSpec((1, t, qw), lambda i, j: (i, 0, j)),
                  pl.BlockSpec((1, t, qw), lambda i, j: (i, 0, GDN_QK_DIM // qw + j)),
                  pl.BlockSpec((1, t, vw), lambda i, j: (i, 0, 2 * GDN_QK_DIM // vw + j)),
                  pl.BlockSpec((1, t, vw), lambda i, j: (i, 0, GDN_CONV_CH // vw + j)),
                  pl.BlockSpec((1, 1, t, hb), lambda i, j: (i, j, 0, 0)),
                  pl.BlockSpec((1, 1, t, hb), lambda i, j: (i, j, 0, 0)),
                  pl.BlockSpec((1, 1, n, hb, GDN_CHUNK), lambda i, j: (i, j, 0, 0, 0)),
                  pl.BlockSpec((1, 1, n, hb, GDN_CHUNK), lambda i, j: (i, j, 0, 0, 0)),
                  pl.BlockSpec((1, hb, GDN_DK, GDN_DV), lambda i, j: (i, j, 0, 0)),
                  pl.BlockSpec((1, GDN_DV), lambda i, j: (0, 0))],
        out_specs=[pl.BlockSpec((1, t, vw), lambda i, j: (i, 0, j)),
                   pl.BlockSpec((1, hb, GDN_DK, GDN_DV), lambda i, j: (i, j, 0, 0))],
        out_shape=[jax.ShapeDtypeStruct((b, t, GDN_V_DIM), BF16),
                   jax.ShapeDtypeStruct((b, GDN_V_HEADS, GDN_DK, GDN_DV), F32)],
        scratch_shapes=[pltpu.VMEM((hb, t, GDN_DV), F32),
                        pltpu.VMEM((hb, t, GDN_DK), BF16),
                        pltpu.VMEM((hb, t, GDN_CHUNK), BF16),
                        pltpu.VMEM((hb, t, GDN_DK), BF16),
                        pltpu.VMEM((hb, t, GDN_DK), BF16)],
        compiler_params=_params("parallel", "parallel"),
        name="gdn_scan",
    )(qkv, qkv, qkv, proj, gcol, bcol, grow, brow, s0, onw.reshape(1, GDN_DV))


def _gdn_layer(x, s0, conv_buf, norm_w, w_in, conv_w, alog_pad, dtb_pad, out_norm_w, w_out):
    b, t, d = x.shape
    xf = x.reshape(b * t, d)
    proj = _rms_matmul(xf, norm_w, w_in, 896).reshape(b, t, -1)
    qkv, new_buf = _gdn_pre(proj, conv_buf, conv_w)
    tail_block = (GDN_CONV_CH + GDN_V_DIM) // 128
    _, gcum, beta = _gdn_gate(proj, alog_pad, dtb_pad, tail_block)
    gcum = gcum[:, :, :GDN_V_HEADS]
    beta = beta[:, :, GDN_V_HEADS:2 * GDN_V_HEADS]
    tp = -(-t // GDN_CHUNK) * GDN_CHUNK
    if tp != t:
        pad = [(0, 0), (0, tp - t), (0, 0)]
        qkv = jnp.pad(qkv, pad)
        proj_z = jnp.pad(proj, pad)
        gcum = jnp.pad(gcum, pad, mode="edge")
        beta = jnp.pad(beta, pad)
    else:
        proj_z = proj
    hb = GDN_HEADS_PER_STEP
    hg = GDN_V_HEADS // hb
    n = tp // GDN_CHUNK
    gcol = gcum.reshape(b, tp, hg, hb).transpose(0, 2, 1, 3)
    bcol = beta.reshape(b, tp, hg, hb).transpose(0, 2, 1, 3)
    grow = gcum.reshape(b, n, GDN_CHUNK, hg, hb).transpose(0, 3, 1, 4, 2)
    brow = beta.reshape(b, n, GDN_CHUNK, hg, hb).transpose(0, 3, 1, 4, 2)
    o, s_new = _gdn_scan(qkv, proj_z, gcol, bcol, grow, brow, s0, out_norm_w)
    o = o[:, :t].reshape(b * t, GDN_V_DIM)
    y = _matmul_res(o, w_out, xf).reshape(b, t, d)
    return y, s_new, new_buf


def _page_gather_body(pt_ref, *refs, pps):
    cmp_refs, o_ref = refs[:pps], refs[pps]
    sub = cmp_refs[0].shape[3]
    for s in range(pps):
        o_ref[0, :, s * sub:(s + 1) * sub, :] = cmp_refs[s][0, 0]


def _page_gather(cache_cmp, layer, page_table):
    b, n_pages = page_table.shape
    _, _, n_cg, sub, flat = cache_cmp.shape
    pps = 8 if n_pages % 8 == 0 else 1
    n_steps = n_pages // pps

    def page_map(s):
        return lambda i, p, pt: (layer, pt[i, p * pps + s], 0, 0, 0)

    grid_spec = pltpu.PrefetchScalarGridSpec(
        num_scalar_prefetch=1,
        grid=(b, n_steps),
        in_specs=[pl.BlockSpec((1, 1, n_cg, sub, flat), page_map(s)) for s in range(pps)],
        out_specs=pl.BlockSpec((1, n_cg, pps * sub, flat), lambda i, p, pt: (i, 0, p, 0)),
    )
    return pl.pallas_call(
        functools.partial(_page_gather_body, pps=pps),
        grid_spec=grid_spec,
        out_shape=jax.ShapeDtypeStruct((b, n_cg, n_pages * sub, flat), cache_cmp.dtype),
        compiler_params=_params("parallel", "arbitrary"),
        name="page_gather",
    )(page_table, *([cache_cmp] * pps))


def _compress_body(x_ref, w1_ref, pe_ref, w2_ref, o_ref, *, n_sub):
    hd = CMP_HIDDEN
    acc = jnp.dot(x_ref[0, 0], w1_ref[0], preferred_element_type=F32)
    pe = pe_ref[0]
    pe0 = jnp.broadcast_to(pe[0:1], (8, pe.shape[1]))
    pe1 = jnp.broadcast_to(pe[1:2], (8, pe.shape[1]))
    b0 = jnp.dot(pe0.astype(BF16), w1_ref[0, :, :hd], preferred_element_type=F32)[0:1]
    b1 = jnp.dot(pe1.astype(BF16), w1_ref[0, :, hd:], preferred_element_type=F32)[0:1]
    first = acc[:, :hd] + b0
    second = acc[:, hd:] + b1
    hid = _silu(first + pltpu.roll(second, n_sub - 1, 0))
    o_ref[0, 0] = jnp.dot(hid.astype(BF16), w2_ref[0], preferred_element_type=F32)


def _compress(x, n_sub, w1cat, pe_flat, w2):
    b, _, _, flat = x.shape
    dh = flat // CMP_STRIDE
    return pl.pallas_call(
        functools.partial(_compress_body, n_sub=n_sub),
        grid=(b, 2 * NSA_G),
        in_specs=[pl.BlockSpec((1, 1, n_sub, flat), lambda i, j: (i, j, 0, 0)),
                  pl.BlockSpec((1, flat, 2 * CMP_HIDDEN), lambda i, j: (j // NSA_G, 0, 0)),
                  pl.BlockSpec((1, 2, flat), lambda i, j: (j // NSA_G, 0, 0)),
                  pl.BlockSpec((1, CMP_HIDDEN, dh), lambda i, j: (j // NSA_G, 0, 0))],
        out_specs=pl.BlockSpec((1, 1, n_sub, dh), lambda i, j: (i, j, 0, 0)),
        out_shape=jax.ShapeDtypeStruct((b, 2 * NSA_G, n_sub, dh), F32),
        compiler_params=_params("parallel", "parallel"),
        name="nsa_compress",
    )(x, w1cat, pe_flat, w2)


def _inv_or_zero(l):
    return jnp.where(l > 0.0, 1.0 / jnp.where(l > 0.0, l, 1.0), 0.0)


def _attn_branch(q, slopes, t_pos, k, v, kp, ok_fn, may_be_empty=False):
    r, qb, nk = slopes.shape[0], t_pos.shape[1], k.shape[0]
    dist = t_pos - kp
    ok = ok_fn(dist)
    pen = jnp.where(ok, dist.astype(F32), MASKED_DIST)
    s = _mm_nt(q, k).reshape(r, qb, nk) - slopes * pen
    m = jnp.max(s, axis=-1, keepdims=True)
    p = jnp.exp2(s - m)
    if may_be_empty:
        p = jnp.where(ok, p, 0.0)
    inv = _inv_or_zero(jnp.sum(p, axis=-1, keepdims=True))
    o = _mm(p.reshape(r * qb, nk), v) * inv.reshape(r * qb, 1)
    return o, p, inv


def _pick_blocks(p_sum, t_lane, n_cmp, n_blk, nb_rows):
    nq, nc = p_sum.shape
    jj = lax.broadcasted_iota(jnp.int32, (nb_rows, nc), 0)
    nn = lax.broadcasted_iota(jnp.int32, (nb_rows, nc), 1)
    overlap_t = ((nn * CMP_STRIDE < (jj + 1) * SEL_BLOCK) & (nn * CMP_STRIDE + CMP_LEN > jj * SEL_BLOCK)
                 & (nn < n_cmp) & (jj < n_blk)).astype(F32)
    imp = lax.dot_general(overlap_t, p_sum, (((1,), (1,)), ((), ())), preferred_element_type=F32,
                          precision=lax.Precision.HIGHEST)
    j = lax.broadcasted_iota(jnp.int32, (nb_rows, nq), 0)
    cur = t_lane // SEL_BLOCK
    imp = jnp.where((j == 0) | (j == cur) | (j == cur - 1), FORCE_SCORE, imp)
    imp = jnp.where(j > cur, -FORCE_SCORE, imp)
    imp = jnp.where(j >= n_blk, NEVER, imp)
    rank = jnp.zeros((nb_rows, nq), jnp.int32)
    for jp in range(n_blk):
        row = imp[jp:jp + 1, :]
        rank = rank + ((row > imp) | ((row == imp) & (j > jp))).astype(jnp.int32)
    return (rank < SEL_TOPN).astype(F32)


def _transpose_01(x_t):
    m = x_t.shape[1]
    eye = (lax.broadcasted_iota(jnp.int32, (m, m), 0) == lax.broadcasted_iota(jnp.int32, (m, m), 1))
    return _mm_nt(eye.astype(BF16), x_t)


def _nsa_attn_body(q_ref, gate_ref, slope_ref, ck_ref, cv_ref, ks_ref, vs_ref, kw_ref, vw_ref, o_ref, osel_ref,
                   *, qb, p_len, n_cmp, n_blk, w0, sel_classes, kw_len):
    i = pl.program_id(2)
    r = NSA_R
    rows = r * qb
    q = (q_ref[0, 0].reshape(rows, NSA_DH) * (NSA_DH ** -0.5 * LOG2E)).astype(BF16)
    slopes = slope_ref[0] * LOG2E
    q0 = p_len + i * qb
    t_pos = q0 + lax.broadcasted_iota(jnp.int32, (1, qb, 1), 1)
    branch = functools.partial(_attn_branch, q, slopes, t_pos)

    nc = ck_ref.shape[2]
    n_idx = lax.broadcasted_iota(jnp.int32, (1, 1, nc), 2)
    o_cmp, p_cmp, inv_cmp = branch(ck_ref[0, 0], cv_ref[0, 0], n_idx * CMP_STRIDE + (CMP_LEN - 1),
                                   lambda dist: (dist >= 0) & (n_idx < n_cmp), may_be_empty=True)
    p_sum = jnp.sum(p_cmp * inv_cmp, axis=0)
    nb8 = -(-n_blk // 16) * 16
    picked_t = _pick_blocks(p_sum, q0 + lax.broadcasted_iota(jnp.int32, (1, qb), 1), n_cmp, n_blk, nb8)
    picked = _transpose_01(picked_t)

    def sel_branch(nk):
        blk_of_key = lax.broadcasted_iota(jnp.int32, (nb8, nk), 1) // SEL_BLOCK
        expand = (blk_of_key == lax.broadcasted_iota(jnp.int32, (nb8, nk), 0)).astype(BF16)
        key_picked = (_mm(picked, expand) > 0.5).reshape(1, qb, nk)
        o, _, _ = branch(ks_ref[0, 0, 0:nk, :], vs_ref[0, 0, 0:nk, :],
                         lax.broadcasted_iota(jnp.int32, (1, 1, nk), 2),
                         lambda dist: key_picked & (dist >= 0))
        osel_ref[...] = o

    if len(sel_classes) == 1:
        sel_branch(sel_classes[0])
    else:
        need = q0 + qb
        prev = 0
        for nk in sel_classes:
            pl.when((need > prev) & (need <= nk))(functools.partial(sel_branch, nk))
            prev = nk

    tw = kw_ref.shape[2]
    if tw == kw_len:
        start = 0
    else:
        start = pl.multiple_of(jnp.clip(q0 + qb - w0 - kw_len, 0, tw - kw_len), SEL_BLOCK)
    kp_win = w0 + start + lax.broadcasted_iota(jnp.int32, (1, 1, kw_len), 2)
    o_win, _, _ = branch(kw_ref[0, 0, pl.ds(start, kw_len), :], vw_ref[0, 0, pl.ds(start, kw_len), :], kp_win,
                         lambda dist: (dist >= 0) & (dist < WINDOW) & (kp_win >= 0))

    gates = _sigmoid(gate_ref[0, 0]).reshape(rows, 3)
    o = gates[:, 0:1] * o_cmp + gates[:, 1:2] * osel_ref[...] + gates[:, 2:3] * o_win
    o_ref[0, 0] = o.reshape(r, qb, NSA_DH)


def _nsa_attn(q_t, gates_t, slopes, cmp_kv, sel_arr, sel_k0, sel_v0, win_arr, win_k0, win_v0,
              *, p_len, n_cmp, n_blk, w0):
    b, g, r, tq, dh = q_t.shape
    qb = NSA_Q_BLOCK if tq % NSA_Q_BLOCK == 0 else tq
    nc = cmp_kv.shape[2]
    tk = sel_arr.shape[2]
    tw = win_arr.shape[2]
    if tq == qb or tk % NSA_SEL_CLASS != 0:
        sel_classes = (tk,)
    else:
        sel_classes = tuple(range(NSA_SEL_CLASS, tk + 1, NSA_SEL_CLASS))
    kw_len = min(tw, -(-(WINDOW - 1 + qb) // KEY_CHUNK) * KEY_CHUNK)
    body = functools.partial(_nsa_attn_body, qb=qb, p_len=p_len, n_cmp=n_cmp, n_blk=n_blk, w0=w0,
                             sel_classes=sel_classes, kw_len=kw_len)
    return pl.pallas_call(
        body,
        grid=(b, g, tq // qb),
        in_specs=[pl.BlockSpec((1, 1, r, qb, dh), lambda bi, gi, i: (bi, gi, 0, i, 0)),
                  pl.BlockSpec((1, 1, r, qb, 3), lambda bi, gi, i: (bi, gi, 0, i, 0)),
                  pl.BlockSpec((1, r, 1, 1), lambda bi, gi, i: (gi, 0, 0, 0)),
                  pl.BlockSpec((1, 1, nc, dh), lambda bi, gi, i: (bi, gi, 0, 0)),
                  pl.BlockSpec((1, 1, nc, dh), lambda bi, gi, i: (bi, NSA_G + gi, 0, 0)),
                  pl.BlockSpec((1, 1, tk, dh), lambda bi, gi, i: (bi, sel_k0 + gi, 0, 0)),
                  pl.BlockSpec((1, 1, tk, dh), lambda bi, gi, i: (bi, sel_v0 + gi, 0, 0)),
                  pl.BlockSpec((1, 1, tw, dh), lambda bi, gi, i: (bi, win_k0 + gi, 0, 0)),
                  pl.BlockSpec((1, 1, tw, dh), lambda bi, gi, i: (bi, win_v0 + gi, 0, 0))],
        out_specs=pl.BlockSpec((1, 1, r, qb, dh), lambda bi, gi, i: (bi, gi, 0, i, 0)),
        out_shape=jax.ShapeDtypeStruct((b, g, r, tq, dh), F32),
        scratch_shapes=[pltpu.VMEM((r * qb, dh), F32)],
        compiler_params=_params("parallel", "parallel", "arbitrary"),
        name="nsa_attn",
    )(q_t, gates_t, slopes, cmp_kv, cmp_kv, sel_arr, sel_arr, win_arr, win_arr)


def _nsa_paged_body(pt_ref, *refs, tq, p_len, n_cmp, n_blk, w0, pps, n_steps):
    k_refs, v_refs = refs[:pps], refs[pps:2 * pps]
    (q_ref, gate_ref, slope_ref, cmp_ref, new_ref, win_ref, o_ref,
     qbd_ref, pen_ref, m_ref, l_ref, acc_ref, ocmp_ref) = refs[2 * pps:]
    p = pl.program_id(1)
    g_n, r = NSA_G, NSA_R
    dh = NSA_DH
    rg = r * tq
    rows = g_n * rg
    page = k_refs[0].shape[5]
    chunk = pps * page
    n_keys = pen_ref.shape[1]
    t_pos = p_len + lax.broadcasted_iota(jnp.int32, (1, tq, 1), 1)

    @pl.when(p == 0)
    def _():
        qbd_ref[...] = jnp.zeros(qbd_ref.shape, qbd_ref.dtype)
        nc = cmp_ref.shape[2]
        n_idx = lax.broadcasted_iota(jnp.int32, (1, 1, nc), 2)
        p_sums = []
        for g in range(g_n):
            q = (q_ref[0, g].reshape(rg, dh) * (dh ** -0.5 * LOG2E)).astype(BF16)
            qbd_ref[g * rg:(g + 1) * rg, g * dh:(g + 1) * dh] = q
            o_cmp, p_cmp, inv_cmp = _attn_branch(
                q, slope_ref[g] * LOG2E, t_pos, cmp_ref[0, g], cmp_ref[0, g_n + g],
                n_idx * CMP_STRIDE + (CMP_LEN - 1), lambda dist: (dist >= 0) & (n_idx < n_cmp), may_be_empty=True)
            ocmp_ref[g * rg:(g + 1) * rg, :] = o_cmp
            p_sums.append(jnp.sum(p_cmp * inv_cmp, axis=0))
        nbp = -(-n_blk // 128) * 128
        nq = g_n * tq
        t_lane = p_len + lax.broadcasted_iota(jnp.int32, (1, nq), 1) % tq
        picked_t = _pick_blocks(jnp.concatenate(p_sums, axis=0), t_lane, n_cmp, n_blk, nbp)
        picked = _transpose_01(picked_t.astype(BF16))
        picked_rows = jnp.concatenate([picked[g * tq:(g + 1) * tq] for g in range(g_n) for _ in range(r)], axis=0)
        slope_rows = jnp.broadcast_to(slope_ref[...] * LOG2E, (g_n, r, tq, 1)).reshape(rows, 1)
        t_rows = p_len + lax.broadcasted_iota(jnp.int32, (rows, 1), 0) % tq

        def fill(k0, width):
            kp = k0 + lax.broadcasted_iota(jnp.int32, (1, width), 1)
            blk = lax.broadcasted_iota(jnp.int32, (nbp, width), 0)
            expand = ((k0 + lax.broadcasted_iota(jnp.int32, (nbp, width), 1)) // SEL_BLOCK == blk).astype(BF16)
            key_picked = _mm(picked_rows, expand) > 0.5
            dist = t_rows - kp
            return slope_rows * jnp.where(key_picked & (dist >= 0), dist.astype(F32), MASKED_DIST)

        def fill_chunk(c, carry):
            k0 = pl.multiple_of(c * chunk, chunk)
            pen_ref[:, pl.ds(k0, chunk)] = fill(k0, chunk)
            return carry

        lax.fori_loop(0, n_steps, fill_chunk, 0)
        pen_ref[:, n_steps * chunk:n_keys] = fill(n_steps * chunk, n_keys - n_steps * chunk)
        m_ref[...] = jnp.full(m_ref.shape, NEVER, F32)
        l_ref[...] = jnp.zeros(l_ref.shape, F32)
        acc_ref[...] = jnp.zeros(acc_ref.shape, F32)

    def online_update(kts, vts, k0):
        width = len(kts) * page
        qbd = qbd_ref[...]
        s = jnp.concatenate([jnp.dot(qbd, kt.astype(BF16), preferred_element_type=F32) for kt in kts], axis=1)
        s = s - pen_ref[:, pl.ds(k0, width)]
        m_prev = m_ref[...]
        m_new = jnp.maximum(m_prev, jnp.max(s, axis=-1, keepdims=True))
        alpha = jnp.exp2(m_prev - m_new)
        prob = jnp.exp2(s - m_new)
        l_ref[...] = alpha * l_ref[...] + jnp.sum(prob, axis=-1, keepdims=True)
        pv = [_mm_nt(prob[:, i * page:(i + 1) * page], vt) for i, vt in enumerate(vts)]
        acc_ref[...] = alpha * acc_ref[...] + sum(pv[1:], pv[0])
        m_ref[...] = m_new

    @pl.when(p < n_steps)
    def _():
        online_update([ref[0, 0, 0].reshape(g_n * dh, page) for ref in k_refs],
                      [ref[0, 0, 0].reshape(g_n * dh, page) for ref in v_refs], pl.multiple_of(p * chunk, chunk))

    @pl.when(p == n_steps)
    def _():
        online_update([new_ref[0, 0].reshape(g_n * dh, page)], [new_ref[0, 1].reshape(g_n * dh, page)],
                      n_steps * chunk)
        acc = acc_ref[...] * _inv_or_zero(l_ref[...])
        tw = win_ref.shape[2]
        kp_win = w0 + lax.broadcasted_iota(jnp.int32, (1, 1, tw), 2)
        for g in range(g_n):
            q = qbd_ref[g * rg:(g + 1) * rg, g * dh:(g + 1) * dh]
            o_win, _, _ = _attn_branch(q, slope_ref[g] * LOG2E, t_pos, win_ref[0, g], win_ref[0, g_n + g], kp_win,
                                       lambda dist: (dist >= 0) & (dist < WINDOW) & (kp_win >= 0))
            gates = _sigmoid(gate_ref[0, g]).reshape(rg, 3)
            o = (gates[:, 0:1] * ocmp_ref[g * rg:(g + 1) * rg, :]
                 + gates[:, 1:2] * acc[g * rg:(g + 1) * rg, g * dh:(g + 1) * dh] + gates[:, 2:3] * o_win)
            o_ref[0, g] = o.reshape(r, tq, dh)


def _nsa_paged(q_t, gates_t, slopes, cmp_kv, cache_nt, layer, page_table, new_t, win_arr,
               *, p_len, n_cmp, n_blk, w0):
    b, g, r, tq, dh = q_t.shape
    n_pages = page_table.shape[1]
    page = cache_nt.shape[5]
    pps = 4 if n_pages % 4 == 0 else 1
    n_steps = n_pages // pps
    nc = cmp_kv.shape[2]
    tw = win_arr.shape[2]
    rows = g * r * tq
    n_keys = (n_pages + 1) * page

    def page_map(c, s):
        return lambda i, p, pt: (layer, pt[i, jnp.minimum(p, n_steps - 1) * pps + s], c, 0, 0, 0)

    const = lambda i, p, pt: (i, 0, 0, 0, 0)
    grid_spec = pltpu.PrefetchScalarGridSpec(
        num_scalar_prefetch=1,
        grid=(b, n_steps + 1),
        in_specs=[pl.BlockSpec((1, 1, 1, g, dh, page), page_map(2, s)) for s in range(pps)]
        + [pl.BlockSpec((1, 1, 1, g, dh, page), page_map(3, s)) for s in range(pps)]
        + [pl.BlockSpec((1, g, r, tq, dh), const),
           pl.BlockSpec((1, g, r, tq, 3), const),
           pl.BlockSpec((g, r, 1, 1), lambda i, p, pt: (0, 0, 0, 0)),
           pl.BlockSpec((1, 2 * g, nc, dh), lambda i, p, pt: (i, 0, 0, 0)),
           pl.BlockSpec((1, 2, g, dh, page), const),
           pl.BlockSpec((1, 2 * g, tw, dh), lambda i, p, pt: (i, 0, 0, 0))],
        out_specs=pl.BlockSpec((1, g, r, tq, dh), const),
        scratch_shapes=[pltpu.VMEM((rows, g * dh), BF16),
                        pltpu.VMEM((rows, n_keys), F32),
                        pltpu.VMEM((rows, 1), F32),
                        pltpu.VMEM((rows, 1), F32),
                        pltpu.VMEM((rows, g * dh), F32),
                        pltpu.VMEM((rows, dh), F32)],
    )
    body = functools.partial(_nsa_paged_body, tq=tq, p_len=p_len, n_cmp=n_cmp, n_blk=n_blk, w0=w0, pps=pps,
                             n_steps=n_steps)
    return pl.pallas_call(
        body,
        grid_spec=grid_spec,
        out_shape=jax.ShapeDtypeStruct((b, g, r, tq, dh), F32),
        compiler_params=_params("parallel", "arbitrary"),
        name="nsa_paged",
    )(page_table, *([cache_nt] * (2 * pps)), q_t, gates_t, slopes, cmp_kv, new_t, win_arr)


def _nsa_layer(x, cache_hm, layer, page_table, win_buf, norm_w, w_in, w1cat, pe_flat, w2, w_out, slopes):
    b, t, d = x.shape
    g, r, dh = NSA_G, NSA_R, NSA_DH
    xf = x.reshape(b * t, d)
    proj = _rms_matmul(xf, norm_w, w_in, 896).reshape(b, t, -1)
    q_t = proj[..., :NSA_Q_DIM].reshape(b, t, g, r, dh).transpose(0, 2, 3, 1, 4)
    kv = proj[..., NSA_Q_DIM:NSA_Q_DIM + 6 * NSA_KV_DIM]
    gates_t = proj[..., NSA_Q_DIM + 6 * NSA_KV_DIM:NSA_Q_DIM + 6 * NSA_KV_DIM + 3 * NSA_HEADS]
    gates_t = gates_t.reshape(b, t, g, r, 3).transpose(0, 2, 3, 1, 4)
    kv6 = kv.reshape(b, t, 6, g, dh)
    new_rows = kv6[:, :, :4]
    if cache_hm is None:
        p_len = 0
        tk = t
        assert t % CMP_STRIDE == 0
        cmp_x = kv6[:, :, :2].astype(BF16).transpose(0, 2, 3, 1, 4).reshape(b, 2 * g, t // CMP_STRIDE,
                                                                           CMP_STRIDE * dh)
        kv_hm = kv6[:, :, 2:].astype(BF16).transpose(0, 2, 3, 1, 4).reshape(b, 4 * g, t, dh)
        new_win = kv6[:, t - min(WINDOW, t):, 4:]
    else:
        cache_cmp, cache_nt = cache_hm
        n_pages = page_table.shape[1]
        page = cache_nt.shape[5]
        p_len = n_pages * page
        tk = p_len + t
        assert p_len % CMP_STRIDE == 0 and t < CMP_STRIDE and t <= page
        cmp_x = _page_gather(cache_cmp, layer, page_table)
        new_t = jnp.pad(kv6[:, :, 2:4].transpose(0, 2, 3, 4, 1), [(0, 0)] * 4 + [(0, page - t)])
        win_all = jnp.concatenate([win_buf, kv6[:, :, 4:]], axis=1)
        wl = win_all.shape[1]
        wlp = -(-wl // KEY_CHUNK) * KEY_CHUNK
        win_arr = jnp.pad(win_all.astype(BF16), [(0, 0), (0, wlp - wl), (0, 0), (0, 0), (0, 0)])
        win_arr = win_arr.transpose(0, 2, 3, 1, 4).reshape(b, 2 * g, wlp, dh)
        new_win = win_all[:, wl - min(WINDOW, tk):]
    n_sub = tk // CMP_STRIDE
    n_cmp = n_sub - 1
    n_blk = -(-tk // SEL_BLOCK)
    cmp_kv = _compress(cmp_x, n_sub, w1cat, pe_flat, w2)
    if cache_hm is None:
        o_t = _nsa_attn(q_t, gates_t, slopes, cmp_kv, kv_hm, 0, g, kv_hm, 2 * g, 3 * g,
                        p_len=0, n_cmp=n_cmp, n_blk=n_blk, w0=0)
    else:
        o_t = _nsa_paged(q_t, gates_t, slopes, cmp_kv, cache_nt, layer, page_table, new_t, win_arr,
                         p_len=p_len, n_cmp=n_cmp, n_blk=n_blk, w0=p_len - win_buf.shape[1])
    o = o_t.transpose(0, 3, 1, 2, 4).reshape(b * t, NSA_Q_DIM)
    y = _matmul_res(o, w_out, xf).reshape(b, t, d)
    return y, new_rows, new_win


def _ffn_layer(x, buf, norm_w, w_up, conv_w, w_down):
    b, t, d = x.shape
    xf = x.reshape(b * t, d)
    h = _rms_matmul(xf, norm_w, w_up, 512).reshape(b, t, -1)
    act, new_buf = _ffn_act(h, buf, conv_w)
    y = _matmul_res(act.reshape(b * t, D_FF), w_down, xf).reshape(b, t, d)
    return y, new_buf


def _pad_cols(w, n):
    return jnp.pad(w, [(0, 0)] * (w.ndim - 1) + [(0, n - w.shape[-1])])


def _trunk(x, cache_hm, page_table, nsa_win, gdn_state, gdn_conv, ffn_conv, wts):
    depth = wts["ffn_w_up"].shape[0]
    rows_l, win_l, s_l, gconv_l, fconv_l = [], [], [], [], []
    for i in range(depth):
        j = i // 2
        if i % 2 == 0:
            x, s_new, cb = _gdn_layer(x, gdn_state[j], gdn_conv[j], wts["gdn_norm"][j], wts["gdn_w_in"][j],
                                      wts["gdn_conv_w"][j], wts["gdn_a_log"][j], wts["gdn_dt_bias"][j],
                                      wts["gdn_out_norm"][j], wts["gdn_w_out"][j])
            s_l.append(s_new)
            gconv_l.append(cb)
        else:
            x, rows, wb = _nsa_layer(x, cache_hm, j, page_table, None if nsa_win is None else nsa_win[j],
                                     wts["nsa_norm"][j], wts["nsa_w_in"][j], wts["nsa_w1cat"][j],
                                     wts["nsa_pe_flat"][j], wts["nsa_w2"][j], wts["nsa_w_out"][j],
                                     wts["slopes"])
            rows_l.append(rows)
            win_l.append(wb)
        x, fb = _ffn_layer(x, ffn_conv[i], wts["ffn_norm"][i], wts["ffn_w_up"][i], wts["ffn_conv_w"][i],
                           wts["ffn_w_down"][i])
        fconv_l.append(fb)
    b, t, d = x.shape
    y = _rms(x.reshape(b * t, d), wts["final_norm"]).reshape(b, t, d)
    return (y, jnp.stack(rows_l), jnp.stack(win_l), jnp.stack(s_l), jnp.stack(gconv_l), jnp.stack(fconv_l))


def kernel(x_prompt, x_sample, cache_nsa_kv, cache_nsa_win, state_gdn_s, state_gdn_conv, state_ffn_conv,
           page_table, gdn_norm, gdn_w_in, gdn_conv_w, gdn_a_log, gdn_dt_bias, gdn_out_norm, gdn_w_out,
           nsa_norm, nsa_w_in, nsa_cmp_pe, nsa_cmp_w1, nsa_cmp_w2, nsa_w_out,
           ffn_norm, ffn_w_up, ffn_conv_w, ffn_w_down, final_norm):
    n_gdn = gdn_w_in.shape[0]
    n_nsa = nsa_w_in.shape[0]
    depth = ffn_w_up.shape[0]
    bp = x_prompt.shape[0]
    gdn_cols = -(-gdn_w_in.shape[2] // 896) * 896
    nsa_cols = -(-nsa_w_in.shape[2] // 896) * 896
    w1 = nsa_cmp_w1.reshape(n_nsa, 2, 2, CMP_STRIDE * NSA_DH, CMP_HIDDEN)
    w1cat = jnp.concatenate([w1[:, :, 0], w1[:, :, 1]], axis=-1).astype(BF16)
    slopes = 2.0 ** (-8.0 * jnp.arange(1, NSA_HEADS + 1, dtype=F32) / NSA_HEADS)
    wts = {
        "gdn_norm": gdn_norm,
        "gdn_w_in": _pad_cols(gdn_w_in, gdn_cols).astype(BF16),
        "gdn_conv_w": gdn_conv_w,
        "gdn_a_log": _pad_cols(gdn_a_log, 128).reshape(n_gdn, 1, 128),
        "gdn_dt_bias": _pad_cols(gdn_dt_bias, 128).reshape(n_gdn, 1, 128),
        "gdn_out_norm": gdn_out_norm,
        "gdn_w_out": gdn_w_out.astype(BF16),
        "nsa_norm": nsa_norm,
        "nsa_w_in": _pad_cols(nsa_w_in, nsa_cols).astype(BF16),
        "nsa_w1cat": w1cat,
        "nsa_pe_flat": nsa_cmp_pe.reshape(n_nsa, 2, 2, CMP_STRIDE * NSA_DH),
        "nsa_w2": nsa_cmp_w2.astype(BF16),
        "nsa_w_out": nsa_w_out.astype(BF16),
        "slopes": slopes.reshape(NSA_G, NSA_R, 1, 1),
        "ffn_norm": ffn_norm,
        "ffn_w_up": ffn_w_up.astype(BF16),
        "ffn_conv_w": ffn_conv_w,
        "ffn_w_down": ffn_w_down.astype(BF16),
        "final_norm": final_norm,
    }
    n_l, pool, page = cache_nsa_kv.shape[:3]
    cache_cmp = cache_nsa_kv[:, :, :, :2].astype(BF16).transpose(0, 1, 3, 4, 2, 5)
    cache_hm = (cache_cmp.reshape(n_l, pool, 2 * NSA_G, page // CMP_STRIDE, CMP_STRIDE * NSA_DH),
                cache_nsa_kv.transpose(0, 1, 3, 4, 5, 2))

    zeros = functools.partial(jnp.zeros, dtype=F32)
    prompt = _trunk(x_prompt, None, None, None,
                    zeros((n_gdn, bp, GDN_V_HEADS, GDN_DK, GDN_DV)),
                    zeros((n_gdn, bp, GDN_CONV_W - 1, GDN_CONV_CH)),
                    zeros((depth, bp, FFN_CONV_W - 1, D_FF)), wts)
    sample = _trunk(x_sample, cache_hm, page_table, cache_nsa_win, state_gdn_s, state_gdn_conv, state_ffn_conv, wts)
    out = []
    for p, s in zip(prompt, sample):
        out.extend([p, s])
    return tuple(out)
```

```python
import functools

import jax
import jax.numpy as jnp
from jax import lax
from jax.experimental import pallas as pl
from jax.experimental.pallas import tpu as pltpu

F32 = jnp.float32
BF16 = jnp.bfloat16

RMS_EPS = 1e-6
L2_EPS = 1e-6
NEG_INF = -1e30
FORCE_SCORE = 1e9
NEVER = -3e38
MASKED_DIST = 1e30
LOG2E = 1.4426950408889634

GDN_QK_HEADS = 8
GDN_V_HEADS = 16
GDN_DK = 128
GDN_DV = 128
GDN_QK_DIM = GDN_QK_HEADS * GDN_DK
GDN_V_DIM = GDN_V_HEADS * GDN_DV
GDN_CONV_CH = 2 * GDN_QK_DIM + GDN_V_DIM
GDN_CONV_W = 4
GDN_CHUNK = 64
GDN_HEADS_PER_STEP = 4
NSA_HEADS = 16
NSA_G = 4
NSA_R = 4
NSA_DH = 64
NSA_Q_DIM = NSA_HEADS * NSA_DH
NSA_KV_DIM = NSA_G * NSA_DH
CMP_STRIDE = 16
CMP_LEN = 32
CMP_HIDDEN = 256
SEL_BLOCK = 64
SEL_TOPN = 8
WINDOW = 512
NSA_Q_BLOCK = 64
KEY_CHUNK = 128
NSA_SEL_CLASS = 512
D_FF = 2816
FFN_CONV_W = 3
FFN_TC = 256
SEQ_ROWS_PER_STEP = 2048

VMEM_LIMIT = 52 * 1024 * 1024


def _params(*sem):
    return pltpu.CompilerParams(dimension_semantics=sem, vmem_limit_bytes=VMEM_LIMIT)


def _mm(a, b):
    return jnp.dot(a.astype(BF16), b.astype(BF16), preferred_element_type=F32)


def _mm_nt(a, b):
    return lax.dot_general(a.astype(BF16), b.astype(BF16), (((1,), (1,)), ((), ())),
                           preferred_element_type=F32)


def _mm_tn(a, b):
    return lax.dot_general(a.astype(BF16), b.astype(BF16), (((0,), (0,)), ((), ())),
                           preferred_element_type=F32)


def _sigmoid(x):
    return 1.0 / (1.0 + jnp.exp(-x))


def _silu(x):
    return x * _sigmoid(x)


def _rms_matmul_body(x_ref, nw_ref, w_ref, o_ref, xn_ref):
    @pl.when(pl.program_id(1) == 0)
    def _():
        x = x_ref[...]
        inv = lax.rsqrt(jnp.mean(x * x, axis=-1, keepdims=True) + RMS_EPS)
        xn_ref[...] = ((x * inv) * nw_ref[...]).astype(BF16)

    o_ref[...] = jnp.dot(xn_ref[...], w_ref[...], preferred_element_type=F32).astype(o_ref.dtype)


def _rms_matmul(x, nw, w, tn, out_dtype=F32):
    m, k = x.shape
    n = w.shape[1]
    tm = min(m, 1024)
    return pl.pallas_call(
        _rms_matmul_body,
        grid=(m // tm, n // tn),
        in_specs=[pl.BlockSpec((tm, k), lambda i, j: (i, 0)),
                  pl.BlockSpec((1, k), lambda i, j: (0, 0)),
                  pl.BlockSpec((k, tn), lambda i, j: (0, j))],
        out_specs=pl.BlockSpec((tm, tn), lambda i, j: (i, j)),
        out_shape=jax.ShapeDtypeStruct((m, n), out_dtype),
        scratch_shapes=[pltpu.VMEM((tm, k), BF16)],
        compiler_params=_params("parallel", "arbitrary"),
        name="rms_matmul",
    )(x, nw.reshape(1, k), w)


def _matmul_res_body(a_ref, w_ref, r_ref, o_ref):
    o_ref[...] = r_ref[...] + jnp.dot(a_ref[...].astype(BF16), w_ref[...], preferred_element_type=F32)


def _matmul_res(a, w, res):
    m, k = a.shape
    n = w.shape[1]
    tm = min(m, 512)
    return pl.pallas_call(
        _matmul_res_body,
        grid=(m // tm,),
        in_specs=[pl.BlockSpec((tm, k), lambda i: (i, 0)),
                  pl.BlockSpec((k, n), lambda i: (0, 0)),
                  pl.BlockSpec((tm, n), lambda i: (i, 0))],
        out_specs=pl.BlockSpec((tm, n), lambda i: (i, 0)),
        out_shape=jax.ShapeDtypeStruct((m, n), F32),
        compiler_params=_params("parallel"),
        name="matmul_res",
    )(a, w, res)


def _rms_body(x_ref, nw_ref, o_ref):
    x = x_ref[...]
    inv = lax.rsqrt(jnp.mean(x * x, axis=-1, keepdims=True) + RMS_EPS)
    o_ref[...] = (x * inv) * nw_ref[...]


def _rms(x, nw):
    m, k = x.shape
    tm = min(m, 1024)
    return pl.pallas_call(
        _rms_body,
        grid=(m // tm,),
        in_specs=[pl.BlockSpec((tm, k), lambda i: (i, 0)), pl.BlockSpec((1, k), lambda i: (0, 0))],
        out_specs=pl.BlockSpec((tm, k), lambda i: (i, 0)),
        out_shape=jax.ShapeDtypeStruct((m, k), F32),
        compiler_params=_params("parallel"),
        name="final_rms",
    )(x, nw.reshape(1, k))


def _shifted(x, prev_rows, shift, row):
    nb = prev_rows.shape[1]
    y = pltpu.roll(x, shift, 1)
    for r in range(shift):
        y = jnp.where(row == r, prev_rows[:, nb - shift + r:nb - shift + r + 1], y)
    return y


def _batch_block(b, t):
    bb = max(1, min(b, SEQ_ROWS_PER_STEP // t))
    while b % bb:
        bb -= 1
    return bb


def _ffn_act_body(a_ref, g_ref, buf_ref, cw_ref, act_ref, nb_ref):
    a = a_ref[...].astype(F32)
    t = a.shape[1]
    buf = buf_ref[...]
    w = cw_ref[...]
    row = lax.broadcasted_iota(jnp.int32, a.shape, 1)
    y = _shifted(a, buf, 2, row) * w[0:1] + _shifted(a, buf, 1, row) * w[1:2] + a * w[2:3]
    act_ref[...] = (_silu(y) * g_ref[...].astype(F32)).astype(act_ref.dtype)
    nb_ref[...] = a[:, t - (FFN_CONV_W - 1):t, :]


def _ffn_act(h, buf, cw):
    b, t, _ = h.shape
    nj = D_FF // FFN_TC
    bb = _batch_block(b, t)
    return pl.pallas_call(
        _ffn_act_body,
        grid=(b // bb, nj),
        in_specs=[pl.BlockSpec((bb, t, FFN_TC), lambda i, j: (i, 0, j)),
                  pl.BlockSpec((bb, t, FFN_TC), lambda i, j: (i, 0, j + nj)),
                  pl.BlockSpec((bb, FFN_CONV_W - 1, FFN_TC), lambda i, j: (i, 0, j)),
                  pl.BlockSpec((FFN_CONV_W, FFN_TC), lambda i, j: (0, j))],
        out_specs=[pl.BlockSpec((bb, t, FFN_TC), lambda i, j: (i, 0, j)),
                   pl.BlockSpec((bb, FFN_CONV_W - 1, FFN_TC), lambda i, j: (i, 0, j))],
        out_shape=[jax.ShapeDtypeStruct((b, t, D_FF), BF16),
                   jax.ShapeDtypeStruct((b, FFN_CONV_W - 1, D_FF), F32)],
        compiler_params=_params("parallel", "parallel"),
        name="ffn_act",
    )(h, h, buf, cw)


GDN_PRE_TC = 512


def _gdn_pre_body(x_ref, buf_ref, cw_ref, o_ref, nb_ref):
    j = pl.program_id(1)
    x = x_ref[...].astype(F32)
    t = x.shape[1]
    buf = buf_ref[...]
    w = cw_ref[...]
    row = lax.broadcasted_iota(jnp.int32, x.shape, 1)
    y = (_shifted(x, buf, 3, row) * w[0:1] + _shifted(x, buf, 2, row) * w[1:2]
         + _shifted(x, buf, 1, row) * w[2:3] + x * w[3:4])
    y = _silu(y)
    is_q = j < GDN_QK_DIM // GDN_PRE_TC
    is_v = j >= 2 * GDN_QK_DIM // GDN_PRE_TC
    qscale = jnp.where(is_q, GDN_DK ** -0.5, 1.0).astype(F32)
    for h in range(GDN_PRE_TC // GDN_DK):
        yh = y[:, :, h * GDN_DK:(h + 1) * GDN_DK]
        inv = lax.rsqrt(jnp.sum(yh * yh, axis=-1, keepdims=True) + L2_EPS)
        o_ref[:, :, h * GDN_DK:(h + 1) * GDN_DK] = jnp.where(is_v, yh, (yh * inv) * qscale).astype(o_ref.dtype)
    nb_ref[...] = x[:, t - (GDN_CONV_W - 1):t, :]


def _gdn_pre(proj, buf, cw):
    b, t, _ = proj.shape
    nj = GDN_CONV_CH // GDN_PRE_TC
    bb = _batch_block(b, t)
    return pl.pallas_call(
        _gdn_pre_body,
        grid=(b // bb, nj),
        in_specs=[pl.BlockSpec((bb, t, GDN_PRE_TC), lambda i, j: (i, 0, j)),
                  pl.BlockSpec((bb, GDN_CONV_W - 1, GDN_PRE_TC), lambda i, j: (i, 0, j)),
                  pl.BlockSpec((GDN_CONV_W, GDN_PRE_TC), lambda i, j: (0, j))],
        out_specs=[pl.BlockSpec((bb, t, GDN_PRE_TC), lambda i, j: (i, 0, j)),
                   pl.BlockSpec((bb, GDN_CONV_W - 1, GDN_PRE_TC), lambda i, j: (i, 0, j))],
        out_shape=[jax.ShapeDtypeStruct((b, t, GDN_CONV_CH), BF16),
                   jax.ShapeDtypeStruct((b, GDN_CONV_W - 1, GDN_CONV_CH), F32)],
        compiler_params=_params("parallel", "parallel"),
        name="gdn_pre",
    )(proj, buf, cw)


def _gdn_gate_body(x_ref, alog_ref, dtb_ref, g_ref, gcum_ref, beta_ref, *, chunk):
    x = x_ref[...]
    z = x + dtb_ref[...]
    softplus = jnp.maximum(z, 0.0) + jnp.log(1.0 + jnp.exp(-jnp.abs(z)))
    g = -jnp.exp(alog_ref[...]) * softplus
    g_ref[...] = g
    row = lax.broadcasted_iota(jnp.int32, x.shape, 1) % chunk
    acc = g
    s = 1
    while s < chunk:
        acc = acc + jnp.where(row >= s, pltpu.roll(acc, s, 1), 0.0)
        s *= 2
    gcum_ref[...] = acc
    beta_ref[...] = _sigmoid(x)


def _gdn_gate(proj, alog_pad, dtb_pad, lane_block):
    b, t, _ = proj.shape
    chunk = min(GDN_CHUNK, t)
    bb = _batch_block(b, t)
    spec = pl.BlockSpec((bb, t, 128), lambda i: (i, 0, 0))
    return pl.pallas_call(
        functools.partial(_gdn_gate_body, chunk=chunk),
        grid=(b // bb,),
        in_specs=[pl.BlockSpec((bb, t, 128), lambda i: (i, 0, lane_block)),
                  pl.BlockSpec((1, 128), lambda i: (0, 0)),
                  pl.BlockSpec((1, 128), lambda i: (0, 0))],
        out_specs=[spec, spec, spec],
        out_shape=[jax.ShapeDtypeStruct((b, t, 128), F32)] * 3,
        compiler_params=_params("parallel"),
        name="gdn_gate",
    )(proj, alog_pad, dtb_pad)


def _unit_lower_inverses(lows, n):
    eye = (lax.broadcasted_iota(jnp.int32, (n, n), 0) == lax.broadcasted_iota(jnp.int32, (n, n), 1)).astype(F32)
    ps = [eye - low for low in lows]
    ms = [_mm(low, low) for low in lows]
    k = 2
    while True:
        ps = [p + _mm(p, m) for p, m in zip(ps, ms)]
        k *= 2
        if k >= n:
            break
        ms = [_mm(m, m) for m in ms]
    return ps


def _gdn_scan_body(q_ref, k_ref, v_ref, z_ref, gc_ref, bc_ref, gr_ref, br_ref, s0_ref, onw_ref, o_ref, s_ref,
                   u_ref, w_ref, qk_ref, qd_ref, kd_ref):
    c = GDN_CHUNK
    hb = GDN_HEADS_PER_STEP
    n_chunks = q_ref.shape[1] // c
    s_ref[...] = s0_ref[...]
    ri = lax.broadcasted_iota(jnp.int32, (c, c), 0)
    ci = lax.broadcasted_iota(jnp.int32, (c, c), 1)
    tri = ri >= ci
    stri = ri > ci
    onw = onw_ref[...]

    def qk_slice(ref, r0, qh):
        return ref[0, pl.ds(r0, c), qh * GDN_DK:(qh + 1) * GDN_DK]

    def v_slice(ref, r0, hh):
        return ref[0, pl.ds(r0, c), hh * GDN_DV:(hh + 1) * GDN_DV]

    cpb = 2 if n_chunks % 2 == 0 else 1

    def prep(nb, carry):
        items = []
        kk, qk, ks, qs = {}, {}, {}, {}
        for ch in range(cpb):
            n = nb * cpb + ch
            r0 = pl.multiple_of(n * c, c)
            gcol = gc_ref[0, 0, pl.ds(r0, c), :]
            bcol = bc_ref[0, 0, pl.ds(r0, c), :]
            grow = gr_ref[0, 0, n]
            brow = br_ref[0, 0, n]
            for qh in range(hb // 2):
                ks[ch, qh] = qk_slice(k_ref, r0, qh)
                qs[ch, qh] = qk_slice(q_ref, r0, qh)
            for hh in range(hb):
                items.append(dict(ch=ch, hh=hh, r0=r0, v=v_slice(v_ref, r0, hh),
                                  gcb=jnp.broadcast_to(gcol[:, hh:hh + 1], (c, GDN_DK)),
                                  bcb=jnp.broadcast_to(bcol[:, hh:hh + 1], (c, c)),
                                  gr=grow[hh:hh + 1, :], br=brow[hh:hh + 1, :]))
        for key in ks:
            kk[key] = _mm_nt(ks[key], ks[key])
        for key in ks:
            qk[key] = _mm_nt(qs[key], ks[key])
        lows = []
        for it in items:
            key = (it["ch"], it["hh"] // 2)
            it["decay"] = jnp.where(tri, jnp.exp(jnp.where(tri, it["gcb"][:, :c] - it["gr"], 0.0)), 0.0)
            lows.append(jnp.where(stri, (kk[key] * it["bcb"]) * it["decay"], 0.0))
        tinvs = _unit_lower_inverses(lows, c)
        us = [_mm(tinv * it["br"], it["v"]) for tinv, it in zip(tinvs, items)]
        ws = [_mm(tinv * (it["br"] * jnp.exp(it["gr"])), ks[it["ch"], it["hh"] // 2])
              for tinv, it in zip(tinvs, items)]
        for it, u, w in zip(items, us, ws):
            key = (it["ch"], it["hh"] // 2)
            hh, r0, gcb = it["hh"], it["r0"], it["gcb"]
            u_ref[hh, pl.ds(r0, c), :] = u
            w_ref[hh, pl.ds(r0, c), :] = w.astype(BF16)
            qk_ref[hh, pl.ds(r0, c), :] = jnp.where(tri, qk[key] * it["decay"], 0.0).astype(BF16)
            qd_ref[hh, pl.ds(r0, c), :] = (qs[key] * jnp.exp(gcb)).astype(BF16)
            kd_ref[hh, pl.ds(r0, c), :] = (ks[key] * jnp.exp(gcb[c - 1:c, :] - gcb)).astype(BF16)
        return carry

    lax.fori_loop(0, n_chunks // cpb, prep, 0)

    def scan(n, carry):
        r0 = pl.multiple_of(n * c, c)
        g_last = gc_ref[0, 0, pl.ds(r0 + (c - 1), 1), :]
        loaded = []
        for hh in range(hb):
            loaded.append((u_ref[hh, pl.ds(r0, c), :], w_ref[hh, pl.ds(r0, c), :], qk_ref[hh, pl.ds(r0, c), :],
                           qd_ref[hh, pl.ds(r0, c), :], kd_ref[hh, pl.ds(r0, c), :], v_slice(z_ref, r0, hh),
                           s_ref[0, hh]))
        ws_s = [_mm(w, s) for (u, w, qkm, qd, kd, z, s) in loaded]
        qd_s = [_mm(qd, s) for (u, w, qkm, qd, kd, z, s) in loaded]
        v_news = [ld[0] - ws for ld, ws in zip(loaded, ws_s)]
        qk_v = [_mm(ld[2], vn) for ld, vn in zip(loaded, v_news)]
        kd_v = [_mm_tn(ld[4], vn) for ld, vn in zip(loaded, v_news)]
        results = []
        for hh, ld in enumerate(loaded):
            o = qd_s[hh] + qk_v[hh]
            s_new = ld[6] * jnp.exp(g_last[:, hh:hh + 1]) + kd_v[hh]
            inv = lax.rsqrt(jnp.mean(o * o, axis=-1, keepdims=True) + RMS_EPS)
            results.append((((o * inv) * onw) * _silu(ld[5].astype(F32)), s_new))
        for hh, (og, s_new) in enumerate(results):
            s_ref[0, hh] = s_new
            o_ref[0, pl.ds(r0, c), hh * GDN_DV:(hh + 1) * GDN_DV] = og.astype(o_ref.dtype)
        return carry

    lax.fori_loop(0, n_chunks, scan, 0)


def _gdn_scan(qkv, proj, gcol, bcol, grow, brow, s0, onw):
    b, t, _ = qkv.shape
    hb = GDN_HEADS_PER_STEP
    hg = GDN_V_HEADS // hb
    qw = hb // 2 * GDN_DK
    vw = hb * GDN_DV
    n = t // GDN_CHUNK
    return pl.pallas_call(
        _gdn_scan_body,
        grid=(b, hg),
        in_specs=[pl.BlockSpec((1, t, qw), lambda i, j: (i, 0, j)),
                  pl.BlockSpec((1, t, qw), lambda i, j: (i, 0, GDN_QK_DIM // qw + j)),
                  pl.BlockSpec((1, t, vw), lambda i, j: (i, 0, 2 * GDN_QK_DIM // vw + j)),
                  pl.BlockSpec((1, t, vw), lambda i, j: (i, 0, GDN_CONV_CH // vw + j)),
                  pl.BlockSpec((1, 1, t, hb), lambda i, j: (i, j, 0, 0)),
                  pl.BlockSpec((1, 1, t, hb), lambda i, j: (i, j, 0, 0)),
                  pl.BlockSpec((1, 1, n, hb, GDN_CHUNK), lambda i, j: (i, j, 0, 0, 0)),
                  pl.BlockSpec((1, 1, n, hb, GDN_CHUNK), lambda i, j: (i, j, 0, 0, 0)),
                  pl.BlockSpec((1, hb, GDN_DK, GDN_DV), lambda i, j: (i, j, 0, 0)),
                  pl.BlockSpec((1, GDN_DV), lambda i, j: (0, 0))],
        out_specs=[pl.BlockSpec((1, t, vw), lambda i, j: (i, 0, j)),
                   pl.BlockSpec((1, hb, GDN_DK, GDN_DV), lambda i, j: (i, j, 0, 0))],
        out_shape=[jax.ShapeDtypeStruct((b, t, GDN_V_DIM), BF16),
                   jax.ShapeDtypeStruct((b, GDN_V_HEADS, GDN_DK, GDN_DV), F32)],
        scratch_shapes=[pltpu.VMEM((hb, t, GDN_DV), F32),
                        pltpu.VMEM((hb, t, GDN_DK), BF16),
                        pltpu.VMEM((hb, t, GDN_CHUNK), BF16),
                        pltpu.VMEM((hb, t, GDN_DK), BF16),
                        pltpu.VMEM((hb, t, GDN_DK), BF16)],
        compiler_params=_params("parallel", "parallel"),
        name="gdn_scan",
    )(qkv, qkv, qkv, proj, gcol, bcol, grow, brow, s0, onw.reshape(1, GDN_DV))


def _gdn_layer(x, s0, conv_buf, norm_w, w_in, conv_w, alog_pad, dtb_pad, out_norm_w, w_out):
    b, t, d = x.shape
    xf = x.reshape(b * t, d)
    w_main, w_tail = w_in
    proj = _rms_matmul(xf, norm_w, w_main, 1024, BF16).reshape(b, t, -1)
    tail = _rms_matmul(xf, norm_w, w_tail, 128).reshape(b, t, -1)
    qkv, new_buf = _gdn_pre(proj, conv_buf, conv_w)
    _, gcum, beta = _gdn_gate(tail, alog_pad, dtb_pad, 0)
    gcum = gcum[:, :, :GDN_V_HEADS]
    beta = beta[:, :, GDN_V_HEADS:2 * GDN_V_HEADS]
    tp = -(-t // GDN_CHUNK) * GDN_CHUNK
    if tp != t:
        pad = [(0, 0), (0, tp - t), (0, 0)]
        qkv = jnp.pad(qkv, pad)
        proj_z = jnp.pad(proj, pad)
        gcum = jnp.pad(gcum, pad, mode="edge")
        beta = jnp.pad(beta, pad)
    else:
        proj_z = proj
    hb = GDN_HEADS_PER_STEP
    hg = GDN_V_HEADS // hb
    n = tp // GDN_CHUNK
    gcol = gcum.reshape(b, tp, hg, hb).transpose(0, 2, 1, 3)
    bcol = beta.reshape(b, tp, hg, hb).transpose(0, 2, 1, 3)
    grow = gcum.reshape(b, n, GDN_CHUNK, hg, hb).transpose(0, 3, 1, 4, 2)
    brow = beta.reshape(b, n, GDN_CHUNK, hg, hb).transpose(0, 3, 1, 4, 2)
    o, s_new = _gdn_scan(qkv, proj_z, gcol, bcol, grow, brow, s0, out_norm_w)
    o = o[:, :t].reshape(b * t, GDN_V_DIM)
    y = _matmul_res(o, w_out, xf).reshape(b, t, d)
    return y, s_new, new_buf


def _page_gather_body(pt_ref, *refs, pps):
    cmp_refs, o_ref = refs[:pps], refs[pps]
    sub = cmp_refs[0].shape[3]
    for s in range(pps):
        o_ref[0, :, s * sub:(s + 1) * sub, :] = cmp_refs[s][0, 0]


def _page_gather(cache_cmp, layer, page_table):
    b, n_pages = page_table.shape
    _, _, n_cg, sub, flat = cache_cmp.shape
    pps = 8 if n_pages % 8 == 0 else 1
    n_steps = n_pages // pps

    def page_map(s):
        return lambda i, p, pt: (layer, pt[i, p * pps + s], 0, 0, 0)

    grid_spec = pltpu.PrefetchScalarGridSpec(
        num_scalar_prefetch=1,
        grid=(b, n_steps),
        in_specs=[pl.BlockSpec((1, 1, n_cg, sub, flat), page_map(s)) for s in range(pps)],
        out_specs=pl.BlockSpec((1, n_cg, pps * sub, flat), lambda i, p, pt: (i, 0, p, 0)),
    )
    return pl.pallas_call(
        functools.partial(_page_gather_body, pps=pps),
        grid_spec=grid_spec,
        out_shape=jax.ShapeDtypeStruct((b, n_cg, n_pages * sub, flat), cache_cmp.dtype),
        compiler_params=_params("parallel", "arbitrary"),
        name="page_gather",
    )(page_table, *([cache_cmp] * pps))


def _compress_body(x_ref, w1_ref, pe_ref, w2_ref, o_ref, *, n_sub):
    hd = CMP_HIDDEN
    acc = jnp.dot(x_ref[0, 0], w1_ref[0], preferred_element_type=F32)
    pe = pe_ref[0]
    pe0 = jnp.broadcast_to(pe[0:1], (8, pe.shape[1]))
    pe1 = jnp.broadcast_to(pe[1:2], (8, pe.shape[1]))
    b0 = jnp.dot(pe0.astype(BF16), w1_ref[0, :, :hd], preferred_element_type=F32)[0:1]
    b1 = jnp.dot(pe1.astype(BF16), w1_ref[0, :, hd:], preferred_element_type=F32)[0:1]
    first = acc[:, :hd] + b0
    second = acc[:, hd:] + b1
    hid = _silu(first + pltpu.roll(second, n_sub - 1, 0))
    o_ref[0, 0] = jnp.dot(hid.astype(BF16), w2_ref[0], preferred_element_type=F32)


def _compress(x, n_sub, w1cat, pe_flat, w2):
    b, _, _, flat = x.shape
    dh = flat // CMP_STRIDE
    return pl.pallas_call(
        functools.partial(_compress_body, n_sub=n_sub),
        grid=(b, 2 * NSA_G),
        in_specs=[pl.BlockSpec((1, 1, n_sub, flat), lambda i, j: (i, j, 0, 0)),
                  pl.BlockSpec((1, flat, 2 * CMP_HIDDEN), lambda i, j: (j // NSA_G, 0, 0)),
                  pl.BlockSpec((1, 2, flat), lambda i, j: (j // NSA_G, 0, 0)),
                  pl.BlockSpec((1, CMP_HIDDEN, dh), lambda i, j: (j // NSA_G, 0, 0))],
        out_specs=pl.BlockSpec((1, 1, n_sub, dh), lambda i, j: (i, j, 0, 0)),
        out_shape=jax.ShapeDtypeStruct((b, 2 * NSA_G, n_sub, dh), F32),
        compiler_params=_params("parallel", "parallel"),
        name="nsa_compress",
    )(x, w1cat, pe_flat, w2)


def _inv_or_zero(l):
    return jnp.where(l > 0.0, 1.0 / jnp.where(l > 0.0, l, 1.0), 0.0)


def _attn_branch(q, slopes, t_pos, k, v, kp, ok_fn, may_be_empty=False):
    r, qb, nk = slopes.shape[0], t_pos.shape[1], k.shape[0]
    dist = t_pos - kp
    ok = ok_fn(dist)
    pen = jnp.where(ok, dist.astype(F32), MASKED_DIST)
    s = _mm_nt(q, k).reshape(r, qb, nk) - slopes * pen
    m = jnp.max(s, axis=-1, keepdims=True)
    p = jnp.exp2(s - m)
    if may_be_empty:
        p = jnp.where(ok, p, 0.0)
    inv = _inv_or_zero(jnp.sum(p, axis=-1, keepdims=True))
    o = _mm(p.reshape(r * qb, nk), v) * inv.reshape(r * qb, 1)
    return o, p, inv


def _pick_blocks(p_sum, t_lane, n_cmp, n_blk, nb_rows):
    nq, nc = p_sum.shape
    nb = -(-n_blk // 8) * 8
    jj = lax.broadcasted_iota(jnp.int32, (nb, nc), 0)
    nn = lax.broadcasted_iota(jnp.int32, (nb, nc), 1)
    overlap_t = ((nn * CMP_STRIDE < (jj + 1) * SEL_BLOCK) & (nn * CMP_STRIDE + CMP_LEN > jj * SEL_BLOCK)
                 & (nn < n_cmp) & (jj < n_blk)).astype(F32)
    imp = lax.dot_general(overlap_t, p_sum, (((1,), (1,)), ((), ())), preferred_element_type=F32,
                          precision=lax.Precision.HIGHEST)
    j = lax.broadcasted_iota(jnp.int32, (nb, nq), 0)
    cur = t_lane // SEL_BLOCK
    imp = jnp.where((j == 0) | (j == cur) | (j == cur - 1), FORCE_SCORE, imp)
    imp = jnp.where(j > cur, -FORCE_SCORE, imp)
    imp = jnp.where(j >= n_blk, NEVER, imp)
    rank = jnp.zeros((nb, nq), jnp.int32)
    for jp in range(n_blk):
        row = imp[jp:jp + 1, :]
        rank = rank + ((row > imp) | ((row == imp) & (j > jp))).astype(jnp.int32)
    picked_t = ((rank < SEL_TOPN) & (j < n_blk)).astype(F32)
    if nb_rows > nb:
        picked_t = jnp.concatenate([picked_t, jnp.zeros((nb_rows - nb, nq), F32)], axis=0)
    return picked_t


def _transpose_01(x_t):
    m = x_t.shape[1]
    eye = (lax.broadcasted_iota(jnp.int32, (m, m), 0) == lax.broadcasted_iota(jnp.int32, (m, m), 1))
    return _mm_nt(eye.astype(BF16), x_t)


def _nsa_attn_body(q_ref, gate_ref, slope_ref, ck_ref, cv_ref, ks_ref, vs_ref, kw_ref, vw_ref, o_ref, osel_ref,
                   *, qb, p_len, n_cmp, n_blk, w0, sel_classes, kw_len):
    i = pl.program_id(2)
    r = NSA_R
    rows = r * qb
    q = (q_ref[0, 0].astype(F32).reshape(rows, NSA_DH) * (NSA_DH ** -0.5 * LOG2E)).astype(BF16)
    slopes = slope_ref[0] * LOG2E
    q0 = p_len + i * qb
    t_pos = q0 + lax.broadcasted_iota(jnp.int32, (1, qb, 1), 1)
    branch = functools.partial(_attn_branch, q, slopes, t_pos)

    nc = ck_ref.shape[2]
    n_idx = lax.broadcasted_iota(jnp.int32, (1, 1, nc), 2)
    o_cmp, p_cmp, inv_cmp = branch(ck_ref[0, 0], cv_ref[0, 0], n_idx * CMP_STRIDE + (CMP_LEN - 1),
                                   lambda dist: (dist >= 0) & (n_idx < n_cmp), may_be_empty=True)
    p_sum = jnp.sum(p_cmp * inv_cmp, axis=0)
    nb8 = -(-n_blk // 16) * 16
    picked_t = _pick_blocks(p_sum, q0 + lax.broadcasted_iota(jnp.int32, (1, qb), 1), n_cmp, n_blk, nb8)
    picked = _transpose_01(picked_t)

    def sel_branch(nk):
        blk_of_key = lax.broadcasted_iota(jnp.int32, (nb8, nk), 1) // SEL_BLOCK
        expand = (blk_of_key == lax.broadcasted_iota(jnp.int32, (nb8, nk), 0)).astype(BF16)
        key_picked = (_mm(picked, expand) > 0.5).reshape(1, qb, nk)
        o, _, _ = branch(ks_ref[0, 0, 0:nk, :], vs_ref[0, 0, 0:nk, :],
                         lax.broadcasted_iota(jnp.int32, (1, 1, nk), 2),
                         lambda dist: key_picked & (dist >= 0))
        osel_ref[...] = o

    if len(sel_classes) == 1:
        sel_branch(sel_classes[0])
    else:
        need = q0 + qb
        prev = 0
        for nk in sel_classes:
            pl.when((need > prev) & (need <= nk))(functools.partial(sel_branch, nk))
            prev = nk

    tw = kw_ref.shape[2]
    if tw == kw_len:
        start = 0
    else:
        start = pl.multiple_of(jnp.clip(q0 + qb - w0 - kw_len, 0, tw - kw_len), SEL_BLOCK)
    kp_win = w0 + start + lax.broadcasted_iota(jnp.int32, (1, 1, kw_len), 2)
    o_win, _, _ = branch(kw_ref[0, 0, pl.ds(start, kw_len), :], vw_ref[0, 0, pl.ds(start, kw_len), :], kp_win,
                         lambda dist: (dist >= 0) & (dist < WINDOW) & (kp_win >= 0))

    gates = _sigmoid(gate_ref[0, 0]).reshape(rows, 3)
    o = gates[:, 0:1] * o_cmp + gates[:, 1:2] * osel_ref[...] + gates[:, 2:3] * o_win
    o_ref[0, 0] = o.reshape(r, qb, NSA_DH)


def _nsa_attn(q_t, gates_t, slopes, cmp_kv, sel_arr, sel_k0, sel_v0, win_arr, win_k0, win_v0,
              *, p_len, n_cmp, n_blk, w0):
    b, g, r, tq, dh = q_t.shape
    qb = NSA_Q_BLOCK if tq % NSA_Q_BLOCK == 0 else tq
    nc = cmp_kv.shape[2]
    tk = sel_arr.shape[2]
    tw = win_arr.shape[2]
    if tq == qb or tk % NSA_SEL_CLASS != 0:
        sel_classes = (tk,)
    else:
        sel_classes = tuple(range(NSA_SEL_CLASS, tk + 1, NSA_SEL_CLASS))
    kw_len = min(tw, -(-(WINDOW - 1 + qb) // KEY_CHUNK) * KEY_CHUNK)
    body = functools.partial(_nsa_attn_body, qb=qb, p_len=p_len, n_cmp=n_cmp, n_blk=n_blk, w0=w0,
                             sel_classes=sel_classes, kw_len=kw_len)
    return pl.pallas_call(
        body,
        grid=(b, g, tq // qb),
        in_specs=[pl.BlockSpec((1, 1, r, qb, dh), lambda bi, gi, i: (bi, gi, 0, i, 0)),
                  pl.BlockSpec((1, 1, r, qb, 3), lambda bi, gi, i: (bi, gi, 0, i, 0)),
                  pl.BlockSpec((1, r, 1, 1), lambda bi, gi, i: (gi, 0, 0, 0)),
                  pl.BlockSpec((1, 1, nc, dh), lambda bi, gi, i: (bi, gi, 0, 0)),
                  pl.BlockSpec((1, 1, nc, dh), lambda bi, gi, i: (bi, NSA_G + gi, 0, 0)),
                  pl.BlockSpec((1, 1, tk, dh), lambda bi, gi, i: (bi, sel_k0 + gi, 0, 0)),
                  pl.BlockSpec((1, 1, tk, dh), lambda bi, gi, i: (bi, sel_v0 + gi, 0, 0)),
                  pl.BlockSpec((1, 1, tw, dh), lambda bi, gi, i: (bi, win_k0 + gi, 0, 0)),
                  pl.BlockSpec((1, 1, tw, dh), lambda bi, gi, i: (bi, win_v0 + gi, 0, 0))],
        out_specs=pl.BlockSpec((1, 1, r, qb, dh), lambda bi, gi, i: (bi, gi, 0, i, 0)),
        out_shape=jax.ShapeDtypeStruct((b, g, r, tq, dh), F32),
        scratch_shapes=[pltpu.VMEM((r * qb, dh), F32)],
        compiler_params=_params("parallel", "parallel", "arbitrary"),
        name="nsa_attn",
    )(q_t, gates_t, slopes, cmp_kv, cmp_kv, sel_arr, sel_arr, win_arr, win_arr)


def _nsa_paged_body(pt_ref, *refs, tq, p_len, n_cmp, n_blk, w0, pps, n_steps):
    k_refs, v_refs = refs[:pps], refs[pps:2 * pps]
    (q_ref, gate_ref, slope_ref, cmp_ref, new_ref, win_ref, o_ref,
     qbd_ref, pen_ref, m_ref, l_ref, acc_ref, ocmp_ref) = refs[2 * pps:]
    p = pl.program_id(1)
    g_n, r = NSA_G, NSA_R
    dh = NSA_DH
    rg = r * tq
    rows = g_n * rg
    page = k_refs[0].shape[5]
    chunk = pps * page
    n_keys = pen_ref.shape[1]
    t_pos = p_len + lax.broadcasted_iota(jnp.int32, (1, tq, 1), 1)

    @pl.when(p == 0)
    def _():
        qbd_ref[...] = jnp.zeros(qbd_ref.shape, qbd_ref.dtype)
        nc = cmp_ref.shape[2]
        n_idx = lax.broadcasted_iota(jnp.int32, (1, 1, nc), 2)
        p_sums = []
        for g in range(g_n):
            q = (q_ref[0, g].astype(F32).reshape(rg, dh) * (dh ** -0.5 * LOG2E)).astype(BF16)
            qbd_ref[g * rg:(g + 1) * rg, g * dh:(g + 1) * dh] = q
            o_cmp, p_cmp, inv_cmp = _attn_branch(
                q, slope_ref[g] * LOG2E, t_pos, cmp_ref[0, g], cmp_ref[0, g_n + g],
                n_idx * CMP_STRIDE + (CMP_LEN - 1), lambda dist: (dist >= 0) & (n_idx < n_cmp), may_be_empty=True)
            ocmp_ref[g * rg:(g + 1) * rg, :] = o_cmp
            p_sums.append(jnp.sum(p_cmp * inv_cmp, axis=0))
        nbp = -(-n_blk // 128) * 128
        nq = g_n * tq
        t_lane = p_len + lax.broadcasted_iota(jnp.int32, (1, nq), 1) % tq
        picked_t = _pick_blocks(jnp.concatenate(p_sums, axis=0), t_lane, n_cmp, n_blk, nbp)
        picked = _transpose_01(picked_t.astype(BF16))
        picked_rows = jnp.concatenate([picked[g * tq:(g + 1) * tq] for g in range(g_n) for _ in range(r)], axis=0)
        slope_rows = jnp.broadcast_to(slope_ref[...] * LOG2E, (g_n, r, tq, 1)).reshape(rows, 1)
        t_rows = p_len + lax.broadcasted_iota(jnp.int32, (rows, 1), 0) % tq

        def fill(k0, width):
            kp = k0 + lax.broadcasted_iota(jnp.int32, (1, width), 1)
            blk = lax.broadcasted_iota(jnp.int32, (nbp, width), 0)
            expand = ((k0 + lax.broadcasted_iota(jnp.int32, (nbp, width), 1)) // SEL_BLOCK == blk).astype(BF16)
            key_picked = _mm(picked_rows, expand) > 0.5
            dist = t_rows - kp
            return slope_rows * jnp.where(key_picked & (dist >= 0), dist.astype(F32), MASKED_DIST)

        def fill_chunk(c, carry):
            k0 = pl.multiple_of(c * chunk, chunk)
            pen_ref[:, pl.ds(k0, chunk)] = fill(k0, chunk)
            return carry

        lax.fori_loop(0, n_steps, fill_chunk, 0)
        pen_ref[:, n_steps * chunk:n_keys] = fill(n_steps * chunk, n_keys - n_steps * chunk)
        m_ref[...] = jnp.full(m_ref.shape, NEVER, F32)
        l_ref[...] = jnp.zeros(l_ref.shape, F32)
        acc_ref[...] = jnp.zeros(acc_ref.shape, F32)

    def online_update(kts, vts, k0):
        width = len(kts) * page
        qbd = qbd_ref[...]
        s = jnp.concatenate([jnp.dot(qbd, kt.astype(BF16), preferred_element_type=F32) for kt in kts], axis=1)
        s = s - pen_ref[:, pl.ds(k0, width)]
        m_prev = m_ref[...]
        m_new = jnp.maximum(m_prev, jnp.max(s, axis=-1, keepdims=True))
        alpha = jnp.exp2(m_prev - m_new)
        prob = jnp.exp2(s - m_new)
        l_ref[...] = alpha * l_ref[...] + jnp.sum(prob, axis=-1, keepdims=True)
        pv = [_mm_nt(prob[:, i * page:(i + 1) * page], vt) for i, vt in enumerate(vts)]
        acc_ref[...] = alpha * acc_ref[...] + sum(pv[1:], pv[0])
        m_ref[...] = m_new

    @pl.when(p < n_steps)
    def _():
        online_update([ref[0, 0, 0].reshape(g_n * dh, page) for ref in k_refs],
                      [ref[0, 0, 0].reshape(g_n * dh, page) for ref in v_refs], pl.multiple_of(p * chunk, chunk))

    @pl.when(p == n_steps)
    def _():
        online_update([new_ref[0, 0].reshape(g_n * dh, page)], [new_ref[0, 1].reshape(g_n * dh, page)],
                      n_steps * chunk)
        acc = acc_ref[...] * _inv_or_zero(l_ref[...])
        tw = win_ref.shape[2]
        kp_win = w0 + lax.broadcasted_iota(jnp.int32, (1, 1, tw), 2)
        for g in range(g_n):
            q = qbd_ref[g * rg:(g + 1) * rg, g * dh:(g + 1) * dh]
            o_win, _, _ = _attn_branch(q, slope_ref[g] * LOG2E, t_pos, win_ref[0, g], win_ref[0, g_n + g], kp_win,
                                       lambda dist: (dist >= 0) & (dist < WINDOW) & (kp_win >= 0))
            gates = _sigmoid(gate_ref[0, g]).reshape(rg, 3)
            o = (gates[:, 0:1] * ocmp_ref[g * rg:(g + 1) * rg, :]
                 + gates[:, 1:2] * acc[g * rg:(g + 1) * rg, g * dh:(g + 1) * dh] + gates[:, 2:3] * o_win)
            o_ref[0, g] = o.reshape(r, tq, dh)


def _nsa_paged(q_t, gates_t, slopes, cmp_kv, cache_nt, layer, page_table, new_t, win_arr,
               *, p_len, n_cmp, n_blk, w0):
    b, g, r, tq, dh = q_t.shape
    n_pages = page_table.shape[1]
    page = cache_nt.shape[5]
    pps = 8 if n_pages % 8 == 0 else (4 if n_pages % 4 == 0 else 1)
    n_steps = n_pages // pps
    nc = cmp_kv.shape[2]
    tw = win_arr.shape[2]
    rows = g * r * tq
    n_keys = (n_pages + 1) * page

    def page_map(c, s):
        return lambda i, p, pt: (layer, pt[i, jnp.minimum(p, n_steps - 1) * pps + s], c, 0, 0, 0)

    const = lambda i, p, pt: (i, 0, 0, 0, 0)
    grid_spec = pltpu.PrefetchScalarGridSpec(
        num_scalar_prefetch=1,
        grid=(b, n_steps + 1),
        in_specs=[pl.BlockSpec((1, 1, 1, g, dh, page), page_map(2, s)) for s in range(pps)]
        + [pl.BlockSpec((1, 1, 1, g, dh, page), page_map(3, s)) for s in range(pps)]
        + [pl.BlockSpec((1, g, r, tq, dh), const),
           pl.BlockSpec((1, g, r, tq, 3), const),
           pl.BlockSpec((g, r, 1, 1), lambda i, p, pt: (0, 0, 0, 0)),
           pl.BlockSpec((1, 2 * g, nc, dh), lambda i, p, pt: (i, 0, 0, 0)),
           pl.BlockSpec((1, 2, g, dh, page), const),
           pl.BlockSpec((1, 2 * g, tw, dh), lambda i, p, pt: (i, 0, 0, 0))],
        out_specs=pl.BlockSpec((1, g, r, tq, dh), const),
        scratch_shapes=[pltpu.VMEM((rows, g * dh), BF16),
                        pltpu.VMEM((rows, n_keys), F32),
                        pltpu.VMEM((rows, 1), F32),
                        pltpu.VMEM((rows, 1), F32),
                        pltpu.VMEM((rows, g * dh), F32),
                        pltpu.VMEM((rows, dh), F32)],
    )
    body = functools.partial(_nsa_paged_body, tq=tq, p_len=p_len, n_cmp=n_cmp, n_blk=n_blk, w0=w0, pps=pps,
                             n_steps=n_steps)
    return pl.pallas_call(
        body,
        grid_spec=grid_spec,
        out_shape=jax.ShapeDtypeStruct((b, g, r, tq, dh), F32),
        compiler_params=_params("parallel", "arbitrary"),
        name="nsa_paged",
    )(page_table, *([cache_nt] * (2 * pps)), q_t, gates_t, slopes, cmp_kv, new_t, win_arr)


def _nsa_layer(x, cache_hm, layer, page_table, win_buf, norm_w, w_in, w1cat, pe_flat, w2, w_out, slopes):
    b, t, d = x.shape
    g, r, dh = NSA_G, NSA_R, NSA_DH
    xf = x.reshape(b * t, d)
    w_q, w_kv = w_in
    q_t = _rms_matmul(xf, norm_w, w_q, NSA_Q_DIM, BF16).reshape(b, t, g, r, dh).transpose(0, 2, 3, 1, 4)
    proj = _rms_matmul(xf, norm_w, w_kv, 896).reshape(b, t, -1)
    kv = proj[..., :6 * NSA_KV_DIM]
    gates_t = proj[..., 6 * NSA_KV_DIM:6 * NSA_KV_DIM + 3 * NSA_HEADS]
    gates_t = gates_t.reshape(b, t, g, r, 3).transpose(0, 2, 3, 1, 4)
    kv6 = kv.reshape(b, t, 6, g, dh)
    new_rows = kv6[:, :, :4]
    if cache_hm is None:
        p_len = 0
        tk = t
        assert t % CMP_STRIDE == 0
        cmp_x = kv6[:, :, :2].astype(BF16).transpose(0, 2, 3, 1, 4).reshape(b, 2 * g, t // CMP_STRIDE,
                                                                           CMP_STRIDE * dh)
        kv_hm = kv6[:, :, 2:].astype(BF16).transpose(0, 2, 3, 1, 4).reshape(b, 4 * g, t, dh)
        new_win = kv6[:, t - min(WINDOW, t):, 4:]
    else:
        cache_cmp, cache_nt = cache_hm
        n_pages = page_table.shape[1]
        page = cache_nt.shape[5]
        p_len = n_pages * page
        tk = p_len + t
        assert p_len % CMP_STRIDE == 0 and t < CMP_STRIDE and t <= page
        cmp_x = _page_gather(cache_cmp, layer, page_table)
        new_t = jnp.pad(kv6[:, :, 2:4].transpose(0, 2, 3, 4, 1), [(0, 0)] * 4 + [(0, page - t)])
        win_all = jnp.concatenate([win_buf, kv6[:, :, 4:]], axis=1)
        wl = win_all.shape[1]
        wlp = -(-wl // KEY_CHUNK) * KEY_CHUNK
        win_arr = jnp.pad(win_all.astype(BF16), [(0, 0), (0, wlp - wl), (0, 0), (0, 0), (0, 0)])
        win_arr = win_arr.transpose(0, 2, 3, 1, 4).reshape(b, 2 * g, wlp, dh)
        new_win = win_all[:, wl - min(WINDOW, tk):]
    n_sub = tk // CMP_STRIDE
    n_cmp = n_sub - 1
    n_blk = -(-tk // SEL_BLOCK)
    cmp_kv = _compress(cmp_x, n_sub, w1cat, pe_flat, w2)
    if cache_hm is None:
        o_t = _nsa_attn(q_t, gates_t, slopes, cmp_kv, kv_hm, 0, g, kv_hm, 2 * g, 3 * g,
                        p_len=0, n_cmp=n_cmp, n_blk=n_blk, w0=0)
    else:
        o_t = _nsa_paged(q_t, gates_t, slopes, cmp_kv, cache_nt, layer, page_table, new_t, win_arr,
                         p_len=p_len, n_cmp=n_cmp, n_blk=n_blk, w0=p_len - win_buf.shape[1])
    o = o_t.transpose(0, 3, 1, 2, 4).reshape(b * t, NSA_Q_DIM)
    y = _matmul_res(o, w_out, xf).reshape(b, t, d)
    return y, new_rows, new_win


def _ffn_layer(x, buf, norm_w, w_up, conv_w, w_down):
    b, t, d = x.shape
    xf = x.reshape(b * t, d)
    h = _rms_matmul(xf, norm_w, w_up, 512, BF16).reshape(b, t, -1)
    act, new_buf = _ffn_act(h, buf, conv_w)
    y = _matmul_res(act.reshape(b * t, D_FF), w_down, xf).reshape(b, t, d)
    return y, new_buf


def _pad_cols(w, n):
    return jnp.pad(w, [(0, 0)] * (w.ndim - 1) + [(0, n - w.shape[-1])])


def _trunk(x, cache_hm, page_table, nsa_win, gdn_state, gdn_conv, ffn_conv, wts):
    depth = wts["ffn_w_up"].shape[0]
    rows_l, win_l, s_l, gconv_l, fconv_l = [], [], [], [], []
    for i in range(depth):
        j = i // 2
        if i % 2 == 0:
            x, s_new, cb = _gdn_layer(x, gdn_state[j], gdn_conv[j], wts["gdn_norm"][j],
                                      (wts["gdn_w_main"][j], wts["gdn_w_tail"][j]),
                                      wts["gdn_conv_w"][j], wts["gdn_a_log"][j], wts["gdn_dt_bias"][j],
                                      wts["gdn_out_norm"][j], wts["gdn_w_out"][j])
            s_l.append(s_new)
            gconv_l.append(cb)
        else:
            x, rows, wb = _nsa_layer(x, cache_hm, j, page_table, None if nsa_win is None else nsa_win[j],
                                     wts["nsa_norm"][j], (wts["nsa_w_q"][j], wts["nsa_w_kv"][j]),
                                     wts["nsa_w1cat"][j],
                                     wts["nsa_pe_flat"][j], wts["nsa_w2"][j], wts["nsa_w_out"][j],
                                     wts["slopes"])
            rows_l.append(rows)
            win_l.append(wb)
        x, fb = _ffn_layer(x, ffn_conv[i], wts["ffn_norm"][i], wts["ffn_w_up"][i], wts["ffn_conv_w"][i],
                           wts["ffn_w_down"][i])
        fconv_l.append(fb)
    b, t, d = x.shape
    y = _rms(x.reshape(b * t, d), wts["final_norm"]).reshape(b, t, d)
    return (y, jnp.stack(rows_l), jnp.stack(win_l), jnp.stack(s_l), jnp.stack(gconv_l), jnp.stack(fconv_l))


def kernel(x_prompt, x_sample, cache_nsa_kv, cache_nsa_win, state_gdn_s, state_gdn_conv, state_ffn_conv,
           page_table, gdn_norm, gdn_w_in, gdn_conv_w, gdn_a_log, gdn_dt_bias, gdn_out_norm, gdn_w_out,
           nsa_norm, nsa_w_in, nsa_cmp_pe, nsa_cmp_w1, nsa_cmp_w2, nsa_w_out,
           ffn_norm, ffn_w_up, ffn_conv_w, ffn_w_down, final_norm):
    n_gdn = gdn_w_in.shape[0]
    n_nsa = nsa_w_in.shape[0]
    depth = ffn_w_up.shape[0]
    bp = x_prompt.shape[0]
    gdn_main = GDN_CONV_CH + GDN_V_DIM
    nsa_cols = -(-(nsa_w_in.shape[2] - NSA_Q_DIM) // 896) * 896
    w1 = nsa_cmp_w1.reshape(n_nsa, 2, 2, CMP_STRIDE * NSA_DH, CMP_HIDDEN)
    w1cat = jnp.concatenate([w1[:, :, 0], w1[:, :, 1]], axis=-1).astype(BF16)
    slopes = 2.0 ** (-8.0 * jnp.arange(1, NSA_HEADS + 1, dtype=F32) / NSA_HEADS)
    wts = {
        "gdn_norm": gdn_norm,
        "gdn_w_main": gdn_w_in[:, :, :gdn_main].astype(BF16),
        "gdn_w_tail": _pad_cols(gdn_w_in[:, :, gdn_main:], 128).astype(BF16),
        "gdn_conv_w": gdn_conv_w,
        "gdn_a_log": _pad_cols(gdn_a_log, 128).reshape(n_gdn, 1, 128),
        "gdn_dt_bias": _pad_cols(gdn_dt_bias, 128).reshape(n_gdn, 1, 128),
        "gdn_out_norm": gdn_out_norm,
        "gdn_w_out": gdn_w_out.astype(BF16),
        "nsa_norm": nsa_norm,
        "nsa_w_q": nsa_w_in[:, :, :NSA_Q_DIM].astype(BF16),
        "nsa_w_kv": _pad_cols(nsa_w_in[:, :, NSA_Q_DIM:], nsa_cols).astype(BF16),
        "nsa_w1cat": w1cat,
        "nsa_pe_flat": nsa_cmp_pe.reshape(n_nsa, 2, 2, CMP_STRIDE * NSA_DH),
        "nsa_w2": nsa_cmp_w2.astype(BF16),
        "nsa_w_out": nsa_w_out.astype(BF16),
        "slopes": slopes.reshape(NSA_G, NSA_R, 1, 1),
        "ffn_norm": ffn_norm,
        "ffn_w_up": ffn_w_up.astype(BF16),
        "ffn_conv_w": ffn_conv_w,
        "ffn_w_down": ffn_w_down.astype(BF16),
        "final_norm": final_norm,
    }
    n_l, pool, page = cache_nsa_kv.shape[:3]
    cache_cmp = cache_nsa_kv[:, :, :, :2].astype(BF16).transpose(0, 1, 3, 4, 2, 5)
    cache_hm = (cache_cmp.reshape(n_l, pool, 2 * NSA_G, page // CMP_STRIDE, CMP_STRIDE * NSA_DH),
                cache_nsa_kv.transpose(0, 1, 3, 4, 5, 2))

    zeros = functools.partial(jnp.zeros, dtype=F32)
    prompt = _trunk(x_prompt, None, None, None,
                    zeros((n_gdn, bp, GDN_V_HEADS, GDN_DK, GDN_DV)),
                    zeros((n_gdn, bp, GDN_CONV_W - 1, GDN_CONV_CH)),
                    zeros((depth, bp, FFN_CONV_W - 1, D_FF)), wts)
    sample = _trunk(x_sample, cache_hm, page_table, cache_nsa_win, state_gdn_s, state_gdn_conv, state_ffn_conv, wts)
    out = []
    for p, s in zip(prompt, sample):
        out.extend([p, s])
    return tuple(out)
```

```python
import functools

import jax
import jax.numpy as jnp
from jax import lax
from jax.experimental import pallas as pl
from jax.experimental.pallas import tpu as pltpu

F32 = jnp.float32
BF16 = jnp.bfloat16

RMS_EPS = 1e-6
L2_EPS = 1e-6
NEG_INF = -1e30
FORCE_SCORE = 1e9
NEVER = -3e38
MASKED_DIST = 1e30
LOG2E = 1.4426950408889634

GDN_QK_HEADS = 8
GDN_V_HEADS = 16
GDN_DK = 128
GDN_DV = 128
GDN_QK_DIM = GDN_QK_HEADS * GDN_DK
GDN_V_DIM = GDN_V_HEADS * GDN_DV
GDN_CONV_CH = 2 * GDN_QK_DIM + GDN_V_DIM
GDN_CONV_W = 4
GDN_CHUNK = 64
GDN_HEADS_PER_STEP = 4
NSA_HEADS = 16
NSA_G = 4
NSA_R = 4
NSA_DH = 64
NSA_Q_DIM = NSA_HEADS * NSA_DH
NSA_KV_DIM = NSA_G * NSA_DH
CMP_STRIDE = 16
CMP_LEN = 32
CMP_HIDDEN = 256
SEL_BLOCK = 64
SEL_TOPN = 8
WINDOW = 512
NSA_Q_BLOCK = 64
KEY_CHUNK = 128
NSA_SEL_CLASS = 512
D_FF = 2816
FFN_CONV_W = 3
FFN_TC = 256
SEQ_ROWS_PER_STEP = 2048

VMEM_LIMIT = 52 * 1024 * 1024


def _params(*sem):
    return pltpu.CompilerParams(dimension_semantics=sem, vmem_limit_bytes=VMEM_LIMIT)


def _mm(a, b):
    return jnp.dot(a.astype(BF16), b.astype(BF16), preferred_element_type=F32)


def _mm_nt(a, b):
    return lax.dot_general(a.astype(BF16), b.astype(BF16), (((1,), (1,)), ((), ())),
                           preferred_element_type=F32)


def _mm_tn(a, b):
    return lax.dot_general(a.astype(BF16), b.astype(BF16), (((0,), (0,)), ((), ())),
                           preferred_element_type=F32)


def _sigmoid(x):
    return 1.0 / (1.0 + jnp.exp(-x))


def _silu(x):
    return x * _sigmoid(x)


def _rms_matmul_body(x_ref, nw_ref, w_ref, o_ref, xn_ref):
    @pl.when(pl.program_id(1) == 0)
    def _():
        x = x_ref[...]
        inv = lax.rsqrt(jnp.mean(x * x, axis=-1, keepdims=True) + RMS_EPS)
        xn_ref[...] = ((x * inv) * nw_ref[...]).astype(BF16)

    o_ref[...] = jnp.dot(xn_ref[...], w_ref[...], preferred_element_type=F32).astype(o_ref.dtype)


def _rms_matmul(x, nw, w, tn, out_dtype=F32):
    m, k = x.shape
    n = w.shape[1]
    tm = min(m, 1024)
    return pl.pallas_call(
        _rms_matmul_body,
        grid=(m // tm, n // tn),
        in_specs=[pl.BlockSpec((tm, k), lambda i, j: (i, 0)),
                  pl.BlockSpec((1, k), lambda i, j: (0, 0)),
                  pl.BlockSpec((k, tn), lambda i, j: (0, j))],
        out_specs=pl.BlockSpec((tm, tn), lambda i, j: (i, j)),
        out_shape=jax.ShapeDtypeStruct((m, n), out_dtype),
        scratch_shapes=[pltpu.VMEM((tm, k), BF16)],
        compiler_params=_params("parallel", "arbitrary"),
        name="rms_matmul",
    )(x, nw.reshape(1, k), w)


def _matmul_res_body(a_ref, w_ref, r_ref, o_ref):
    o_ref[...] = r_ref[...] + jnp.dot(a_ref[...].astype(BF16), w_ref[...], preferred_element_type=F32)


def _matmul_res(a, w, res):
    m, k = a.shape
    n = w.shape[1]
    tm = min(m, 512)
    return pl.pallas_call(
        _matmul_res_body,
        grid=(m // tm,),
        in_specs=[pl.BlockSpec((tm, k), lambda i: (i, 0)),
                  pl.BlockSpec((k, n), lambda i: (0, 0)),
                  pl.BlockSpec((tm, n), lambda i: (i, 0))],
        out_specs=pl.BlockSpec((tm, n), lambda i: (i, 0)),
        out_shape=jax.ShapeDtypeStruct((m, n), F32),
        compiler_params=_params("parallel"),
        name="matmul_res",
    )(a, w, res)


def _rms_body(x_ref, nw_ref, o_ref):
    x = x_ref[...]
    inv = lax.rsqrt(jnp.mean(x * x, axis=-1, keepdims=True) + RMS_EPS)
    o_ref[...] = (x * inv) * nw_ref[...]


def _rms(x, nw):
    m, k = x.shape
    tm = min(m, 1024)
    return pl.pallas_call(
        _rms_body,
        grid=(m // tm,),
        in_specs=[pl.BlockSpec((tm, k), lambda i: (i, 0)), pl.BlockSpec((1, k), lambda i: (0, 0))],
        out_specs=pl.BlockSpec((tm, k), lambda i: (i, 0)),
        out_shape=jax.ShapeDtypeStruct((m, k), F32),
        compiler_params=_params("parallel"),
        name="final_rms",
    )(x, nw.reshape(1, k))


def _shifted(x, prev_rows, shift, row):
    nb = prev_rows.shape[1]
    y = pltpu.roll(x, shift, 1)
    for r in range(shift):
        y = jnp.where(row == r, prev_rows[:, nb - shift + r:nb - shift + r + 1], y)
    return y


def _batch_block(b, t):
    bb = max(1, min(b, SEQ_ROWS_PER_STEP // t))
    while b % bb:
        bb -= 1
    return bb


def _ffn_act_body(a_ref, g_ref, buf_ref, cw_ref, act_ref, nb_ref):
    a = a_ref[...].astype(F32)
    t = a.shape[1]
    buf = buf_ref[...]
    w = cw_ref[...]
    row = lax.broadcasted_iota(jnp.int32, a.shape, 1)
    y = _shifted(a, buf, 2, row) * w[0:1] + _shifted(a, buf, 1, row) * w[1:2] + a * w[2:3]
    act_ref[...] = (_silu(y) * g_ref[...].astype(F32)).astype(act_ref.dtype)
    nb_ref[...] = a[:, t - (FFN_CONV_W - 1):t, :]


def _ffn_act(h, buf, cw):
    b, t, _ = h.shape
    nj = D_FF // FFN_TC
    bb = _batch_block(b, t)
    return pl.pallas_call(
        _ffn_act_body,
        grid=(b // bb, nj),
        in_specs=[pl.BlockSpec((bb, t, FFN_TC), lambda i, j: (i, 0, j)),
                  pl.BlockSpec((bb, t, FFN_TC), lambda i, j: (i, 0, j + nj)),
                  pl.BlockSpec((bb, FFN_CONV_W - 1, FFN_TC), lambda i, j: (i, 0, j)),
                  pl.BlockSpec((FFN_CONV_W, FFN_TC), lambda i, j: (0, j))],
        out_specs=[pl.BlockSpec((bb, t, FFN_TC), lambda i, j: (i, 0, j)),
                   pl.BlockSpec((bb, FFN_CONV_W - 1, FFN_TC), lambda i, j: (i, 0, j))],
        out_shape=[jax.ShapeDtypeStruct((b, t, D_FF), BF16),
                   jax.ShapeDtypeStruct((b, FFN_CONV_W - 1, D_FF), F32)],
        compiler_params=_params("parallel", "parallel"),
        name="ffn_act",
    )(h, h, buf, cw)


GDN_PRE_TC = 512


def _gdn_pre_body(x_ref, buf_ref, cw_ref, o_ref, nb_ref):
    j = pl.program_id(1)
    x = x_ref[...].astype(F32)
    t = x.shape[1]
    buf = buf_ref[...]
    w = cw_ref[...]
    row = lax.broadcasted_iota(jnp.int32, x.shape, 1)
    y = (_shifted(x, buf, 3, row) * w[0:1] + _shifted(x, buf, 2, row) * w[1:2]
         + _shifted(x, buf, 1, row) * w[2:3] + x * w[3:4])
    y = _silu(y)
    is_q = j < GDN_QK_DIM // GDN_PRE_TC
    is_v = j >= 2 * GDN_QK_DIM // GDN_PRE_TC
    qscale = jnp.where(is_q, GDN_DK ** -0.5, 1.0).astype(F32)
    for h in range(GDN_PRE_TC // GDN_DK):
        yh = y[:, :, h * GDN_DK:(h + 1) * GDN_DK]
        inv = lax.rsqrt(jnp.sum(yh * yh, axis=-1, keepdims=True) + L2_EPS)
        o_ref[:, :, h * GDN_DK:(h + 1) * GDN_DK] = jnp.where(is_v, yh, (yh * inv) * qscale).astype(o_ref.dtype)
    nb_ref[...] = x[:, t - (GDN_CONV_W - 1):t, :]


def _gdn_pre(proj, buf, cw):
    b, t, _ = proj.shape
    nj = GDN_CONV_CH // GDN_PRE_TC
    bb = _batch_block(b, t)
    return pl.pallas_call(
        _gdn_pre_body,
        grid=(b // bb, nj),
        in_specs=[pl.BlockSpec((bb, t, GDN_PRE_TC), lambda i, j: (i, 0, j)),
                  pl.BlockSpec((bb, GDN_CONV_W - 1, GDN_PRE_TC), lambda i, j: (i, 0, j)),
                  pl.BlockSpec((GDN_CONV_W, GDN_PRE_TC), lambda i, j: (0, j))],
        out_specs=[pl.BlockSpec((bb, t, GDN_PRE_TC), lambda i, j: (i, 0, j)),
                   pl.BlockSpec((bb, GDN_CONV_W - 1, GDN_PRE_TC), lambda i, j: (i, 0, j))],
        out_shape=[jax.ShapeDtypeStruct((b, t, GDN_CONV_CH), BF16),
                   jax.ShapeDtypeStruct((b, GDN_CONV_W - 1, GDN_CONV_CH), F32)],
        compiler_params=_params("parallel", "parallel"),
        name="gdn_pre",
    )(proj, buf, cw)


def _gdn_gate_body(x_ref, alog_ref, dtb_ref, g_ref, gcum_ref, beta_ref, *, chunk):
    x = x_ref[...]
    z = x + dtb_ref[...]
    softplus = jnp.maximum(z, 0.0) + jnp.log(1.0 + jnp.exp(-jnp.abs(z)))
    g = -jnp.exp(alog_ref[...]) * softplus
    g_ref[...] = g
    row = lax.broadcasted_iota(jnp.int32, x.shape, 1) % chunk
    acc = g
    s = 1
    while s < chunk:
        acc = acc + jnp.where(row >= s, pltpu.roll(acc, s, 1), 0.0)
        s *= 2
    gcum_ref[...] = acc
    beta_ref[...] = _sigmoid(x)


def _gdn_gate(proj, alog_pad, dtb_pad, lane_block):
    b, t, _ = proj.shape
    chunk = min(GDN_CHUNK, t)
    bb = _batch_block(b, t)
    spec = pl.BlockSpec((bb, t, 128), lambda i: (i, 0, 0))
    return pl.pallas_call(
        functools.partial(_gdn_gate_body, chunk=chunk),
        grid=(b // bb,),
        in_specs=[pl.BlockSpec((bb, t, 128), lambda i: (i, 0, lane_block)),
                  pl.BlockSpec((1, 128), lambda i: (0, 0)),
                  pl.BlockSpec((1, 128), lambda i: (0, 0))],
        out_specs=[spec, spec, spec],
        out_shape=[jax.ShapeDtypeStruct((b, t, 128), F32)] * 3,
        compiler_params=_params("parallel"),
        name="gdn_gate",
    )(proj, alog_pad, dtb_pad)


def _unit_lower_inverses(lows, n):
    eye = (lax.broadcasted_iota(jnp.int32, (n, n), 0) == lax.broadcasted_iota(jnp.int32, (n, n), 1)).astype(F32)
    ps = [eye - low for low in lows]
    ms = [_mm(low, low) for low in lows]
    k = 2
    while True:
        ps = [p + _mm(p, m) for p, m in zip(ps, ms)]
        k *= 2
        if k >= n:
            break
        ms = [_mm(m, m) for m in ms]
    return ps


def _gdn_scan_body(q_ref, k_ref, v_ref, z_ref, gc_ref, bc_ref, gr_ref, br_ref, s0_ref, onw_ref, o_ref, s_ref,
                   u_ref, w_ref, qk_ref, qd_ref, kd_ref):
    c = GDN_CHUNK
    hb = GDN_HEADS_PER_STEP
    n_chunks = q_ref.shape[1] // c
    s_ref[...] = s0_ref[...]
    ri = lax.broadcasted_iota(jnp.int32, (c, c), 0)
    ci = lax.broadcasted_iota(jnp.int32, (c, c), 1)
    tri = ri >= ci
    stri = ri > ci
    onw = onw_ref[...]

    def qk_slice(ref, r0, qh):
        return ref[0, pl.ds(r0, c), qh * GDN_DK:(qh + 1) * GDN_DK]

    def v_slice(ref, r0, hh):
        return ref[0, pl.ds(r0, c), hh * GDN_DV:(hh + 1) * GDN_DV]

    cpb = 2 if n_chunks % 2 == 0 else 1

    def prep(nb, carry):
        items = []
        kk, qk, ks, qs = {}, {}, {}, {}
        for ch in range(cpb):
            n = nb * cpb + ch
            r0 = pl.multiple_of(n * c, c)
            gcol = gc_ref[0, 0, pl.ds(r0, c), :]
            bcol = bc_ref[0, 0, pl.ds(r0, c), :]
            grow = gr_ref[0, 0, n]
            brow = br_ref[0, 0, n]
            for qh in range(hb // 2):
                ks[ch, qh] = qk_slice(k_ref, r0, qh)
                qs[ch, qh] = qk_slice(q_ref, r0, qh)
            for hh in range(hb):
                items.append(dict(ch=ch, hh=hh, r0=r0, v=v_slice(v_ref, r0, hh),
                                  gcb=jnp.broadcast_to(gcol[:, hh:hh + 1], (c, GDN_DK)),
                                  bcb=jnp.broadcast_to(bcol[:, hh:hh + 1], (c, c)),
                                  gr=grow[hh:hh + 1, :], br=brow[hh:hh + 1, :]))
        for key in ks:
            kk[key] = _mm_nt(ks[key], ks[key])
        for key in ks:
            qk[key] = _mm_nt(qs[key], ks[key])
        lows = []
        for it in items:
            key = (it["ch"], it["hh"] // 2)
            it["decay"] = jnp.where(tri, jnp.exp(jnp.where(tri, it["gcb"][:, :c] - it["gr"], 0.0)), 0.0)
            lows.append(jnp.where(stri, (kk[key] * it["bcb"]) * it["decay"], 0.0))
        tinvs = _unit_lower_inverses(lows, c)
        us = [_mm(tinv * it["br"], it["v"]) for tinv, it in zip(tinvs, items)]
        ws = [_mm(tinv * (it["br"] * jnp.exp(it["gr"])), ks[it["ch"], it["hh"] // 2])
              for tinv, it in zip(tinvs, items)]
        for it, u, w in zip(items, us, ws):
            key = (it["ch"], it["hh"] // 2)
            hh, r0, gcb = it["hh"], it["r0"], it["gcb"]
            u_ref[hh, pl.ds(r0, c), :] = u
            w_ref[hh, pl.ds(r0, c), :] = w.astype(BF16)
            qk_ref[hh, pl.ds(r0, c), :] = jnp.where(tri, qk[key] * it["decay"], 0.0).astype(BF16)
            qd_ref[hh, pl.ds(r0, c), :] = (qs[key] * jnp.exp(gcb)).astype(BF16)
            kd_ref[hh, pl.ds(r0, c), :] = (ks[key] * jnp.exp(gcb[c - 1:c, :] - gcb)).astype(BF16)
        return carry

    lax.fori_loop(0, n_chunks // cpb, prep, 0)

    def scan(n, carry):
        r0 = pl.multiple_of(n * c, c)
        g_last = gc_ref[0, 0, pl.ds(r0 + (c - 1), 1), :]
        loaded = []
        for hh in range(hb):
            loaded.append((u_ref[hh, pl.ds(r0, c), :], w_ref[hh, pl.ds(r0, c), :], qk_ref[hh, pl.ds(r0, c), :],
                           qd_ref[hh, pl.ds(r0, c), :], kd_ref[hh, pl.ds(r0, c), :], v_slice(z_ref, r0, hh),
                           s_ref[0, hh]))
        ws_s = [_mm(w, s) for (u, w, qkm, qd, kd, z, s) in loaded]
        qd_s = [_mm(qd, s) for (u, w, qkm, qd, kd, z, s) in loaded]
        v_news = [ld[0] - ws for ld, ws in zip(loaded, ws_s)]
        qk_v = [_mm(ld[2], vn) for ld, vn in zip(loaded, v_news)]
        kd_v = [_mm_tn(ld[4], vn) for ld, vn in zip(loaded, v_news)]
        results = []
        for hh, ld in enumerate(loaded):
            o = qd_s[hh] + qk_v[hh]
            s_new = ld[6] * jnp.exp(g_last[:, hh:hh + 1]) + kd_v[hh]
            inv = lax.rsqrt(jnp.mean(o * o, axis=-1, keepdims=True) + RMS_EPS)
            results.append((((o * inv) * onw) * _silu(ld[5].astype(F32)), s_new))
        for hh, (og, s_new) in enumerate(results):
            s_ref[0, hh] = s_new
            o_ref[0, pl.ds(r0, c), hh * GDN_DV:(hh + 1) * GDN_DV] = og.astype(o_ref.dtype)
        return carry

    lax.fori_loop(0, n_chunks, scan, 0)


def _gdn_scan(qkv, proj, gcol, bcol, grow, brow, s0, onw):
    b, t, _ = qkv.shape
    hb = GDN_HEADS_PER_STEP
    hg = GDN_V_HEADS // hb
    qw = hb // 2 * GDN_DK
    vw = hb * GDN_DV
    n = t // GDN_CHUNK
    return pl.pallas_call(
        _gdn_scan_body,
        grid=(b, hg),
        in_specs=[pl.BlockSpec((1, t, qw), lambda i, j: (i, 0, j)),
                  pl.BlockSpec((1, t, qw), lambda i, j: (i, 0, GDN_QK_DIM // qw + j)),
                  pl.BlockSpec((1, t, vw), lambda i, j: (i, 0, 2 * GDN_QK_DIM // vw + j)),
                  pl.BlockSpec((1, t, vw), lambda i, j: (i, 0, GDN_CONV_CH // vw + j)),
                  pl.BlockSpec((1, 1, t, hb), lambda i, j: (i, j, 0, 0)),
                  pl.BlockSpec((1, 1, t, hb), lambda i, j: (i, j, 0, 0)),
                  pl.BlockSpec((1, 1, n, hb, GDN_CHUNK), lambda i, j: (i, j, 0, 0, 0)),
                  pl.BlockSpec((1, 1, n, hb, GDN_CHUNK), lambda i, j: (i, j, 0, 0, 0)),
                  pl.BlockSpec((1, hb, GDN_DK, GDN_DV), lambda i, j: (i, j, 0, 0)),
                  pl.BlockSpec((1, GDN_DV), lambda i, j: (0, 0))],
        out_specs=[pl.BlockSpec((1, t, vw), lambda i, j: (i, 0, j)),
                   pl.BlockSpec((1, hb, GDN_DK, GDN_DV), lambda i, j: (i, j, 0, 0))],
        out_shape=[jax.ShapeDtypeStruct((b, t, GDN_V_DIM), BF16),
                   jax.ShapeDtypeStruct((b, GDN_V_HEADS, GDN_DK, GDN_DV), F32)],
        scratch_shapes=[pltpu.VMEM((hb, t, GDN_DV), F32),
                        pltpu.VMEM((hb, t, GDN_DK), BF16),
                        pltpu.VMEM((hb, t, GDN_CHUNK), BF16),
                        pltpu.VMEM((hb, t, GDN_DK), BF16),
                        pltpu.VMEM((hb, t, GDN_DK), BF16)],
        compiler_params=_params("parallel", "parallel"),
        name="gdn_scan",
    )(qkv, qkv, qkv, proj, gcol, bcol, grow, brow, s0, onw.reshape(1, GDN_DV))


def _gdn_layer(x, s0, conv_buf, norm_w, w_in, conv_w, alog_pad, dtb_pad, out_norm_w, w_out):
    b, t, d = x.shape
    xf = x.reshape(b * t, d)
    w_main, w_tail = w_in
    proj = _rms_matmul(xf, norm_w, w_main, 1024, BF16).reshape(b, t, -1)
    tail = _rms_matmul(xf, norm_w, w_tail, 128).reshape(b, t, -1)
    qkv, new_buf = _gdn_pre(proj, conv_buf, conv_w)
    _, gcum, beta = _gdn_gate(tail, alog_pad, dtb_pad, 0)
    gcum = gcum[:, :, :GDN_V_HEADS]
    beta = beta[:, :, GDN_V_HEADS:2 * GDN_V_HEADS]
    tp = -(-t // GDN_CHUNK) * GDN_CHUNK
    if tp != t:
        pad = [(0, 0), (0, tp - t), (0, 0)]
        qkv = jnp.pad(qkv, pad)
        proj_z = jnp.pad(proj, pad)
        gcum = jnp.pad(gcum, pad, mode="edge")
        beta = jnp.pad(beta, pad)
    else:
        proj_z = proj
    hb = GDN_HEADS_PER_STEP
    hg = GDN_V_HEADS // hb
    n = tp // GDN_CHUNK
    gcol = gcum.reshape(b, tp, hg, hb).transpose(0, 2, 1, 3)
    bcol = beta.reshape(b, tp, hg, hb).transpose(0, 2, 1, 3)
    grow = gcum.reshape(b, n, GDN_CHUNK, hg, hb).transpose(0, 3, 1, 4, 2)
    brow = beta.reshape(b, n, GDN_CHUNK, hg, hb).transpose(0, 3, 1, 4, 2)
    o, s_new = _gdn_scan(qkv, proj_z, gcol, bcol, grow, brow, s0, out_norm_w)
    o = o[:, :t].reshape(b * t, GDN_V_DIM)
    y = _matmul_res(o, w_out, xf).reshape(b, t, d)
    return y, s_new, new_buf


def _page_gather_body(pt_ref, *refs, pps):
    cmp_refs, o_ref = refs[:pps], refs[pps]
    sub = cmp_refs[0].shape[3]
    for s in range(pps):
        o_ref[0, :, s * sub:(s + 1) * sub, :] = cmp_refs[s][0, 0]


def _page_gather(cache_cmp, layer, page_table):
    b, n_pages = page_table.shape
    _, _, n_cg, sub, flat = cache_cmp.shape
    pps = 8 if n_pages % 8 == 0 else 1
    n_steps = n_pages // pps

    def page_map(s):
        return lambda i, p, pt: (layer, pt[i, p * pps + s], 0, 0, 0)

    grid_spec = pltpu.PrefetchScalarGridSpec(
        num_scalar_prefetch=1,
        grid=(b, n_steps),
        in_specs=[pl.BlockSpec((1, 1, n_cg, sub, flat), page_map(s)) for s in range(pps)],
        out_specs=pl.BlockSpec((1, n_cg, pps * sub, flat), lambda i, p, pt: (i, 0, p, 0)),
    )
    return pl.pallas_call(
        functools.partial(_page_gather_body, pps=pps),
        grid_spec=grid_spec,
        out_shape=jax.ShapeDtypeStruct((b, n_cg, n_pages * sub, flat), cache_cmp.dtype),
        compiler_params=_params("parallel", "arbitrary"),
        name="page_gather",
    )(page_table, *([cache_cmp] * pps))


def _compress_body(x_ref, w1_ref, pe_ref, w2_ref, o_ref, *, n_sub):
    hd = CMP_HIDDEN
    acc = jnp.dot(x_ref[0, 0], w1_ref[0], preferred_element_type=F32)
    pe = pe_ref[0]
    pe0 = jnp.broadcast_to(pe[0:1], (8, pe.shape[1]))
    pe1 = jnp.broadcast_to(pe[1:2], (8, pe.shape[1]))
    b0 = jnp.dot(pe0.astype(BF16), w1_ref[0, :, :hd], preferred_element_type=F32)[0:1]
    b1 = jnp.dot(pe1.astype(BF16), w1_ref[0, :, hd:], preferred_element_type=F32)[0:1]
    first = acc[:, :hd] + b0
    second = acc[:, hd:] + b1
    hid = _silu(first + pltpu.roll(second, n_sub - 1, 0))
    o_ref[0, 0] = jnp.dot(hid.astype(BF16), w2_ref[0], preferred_element_type=F32)


def _compress(x, n_sub, w1cat, pe_flat, w2):
    b, _, _, flat = x.shape
    dh = flat // CMP_STRIDE
    return pl.pallas_call(
        functools.partial(_compress_body, n_sub=n_sub),
        grid=(b, 2 * NSA_G),
        in_specs=[pl.BlockSpec((1, 1, n_sub, flat), lambda i, j: (i, j, 0, 0)),
                  pl.BlockSpec((1, flat, 2 * CMP_HIDDEN), lambda i, j: (j // NSA_G, 0, 0)),
                  pl.BlockSpec((1, 2, flat), lambda i, j: (j // NSA_G, 0, 0)),
                  pl.BlockSpec((1, CMP_HIDDEN, dh), lambda i, j: (j // NSA_G, 0, 0))],
        out_specs=pl.BlockSpec((1, 1, n_sub, dh), lambda i, j: (i, j, 0, 0)),
        out_shape=jax.ShapeDtypeStruct((b, 2 * NSA_G, n_sub, dh), F32),
        compiler_params=_params("parallel", "parallel"),
        name="nsa_compress",
    )(x, w1cat, pe_flat, w2)


def _inv_or_zero(l):
    return jnp.where(l > 0.0, 1.0 / jnp.where(l > 0.0, l, 1.0), 0.0)


def _attn_branch(q, slopes, t_pos, k, v, kp, ok_fn, may_be_empty=False):
    r, qb, nk = slopes.shape[0], t_pos.shape[1], k.shape[0]
    dist = t_pos - kp
    ok = ok_fn(dist)
    pen = jnp.where(ok, dist.astype(F32), MASKED_DIST)
    s = _mm_nt(q, k).reshape(r, qb, nk) - slopes * pen
    m = jnp.max(s, axis=-1, keepdims=True)
    p = jnp.exp2(s - m)
    if may_be_empty:
        p = jnp.where(ok, p, 0.0)
    inv = _inv_or_zero(jnp.sum(p, axis=-1, keepdims=True))
    o = _mm(p.reshape(r * qb, nk), v) * inv.reshape(r * qb, 1)
    return o, p, inv


def _pick_blocks(p_sum, t_lane, n_cmp, n_blk, nb_rows):
    nq, nc = p_sum.shape
    nb = -(-n_blk // 8) * 8
    jj = lax.broadcasted_iota(jnp.int32, (nb, nc), 0)
    nn = lax.broadcasted_iota(jnp.int32, (nb, nc), 1)
    overlap_t = ((nn * CMP_STRIDE < (jj + 1) * SEL_BLOCK) & (nn * CMP_STRIDE + CMP_LEN > jj * SEL_BLOCK)
                 & (nn < n_cmp) & (jj < n_blk)).astype(F32)
    imp = lax.dot_general(overlap_t, p_sum, (((1,), (1,)), ((), ())), preferred_element_type=F32,
                          precision=lax.Precision.HIGHEST)
    j = lax.broadcasted_iota(jnp.int32, (nb, nq), 0)
    cur = t_lane // SEL_BLOCK
    imp = jnp.where((j == 0) | (j == cur) | (j == cur - 1), FORCE_SCORE, imp)
    imp = jnp.where(j > cur, -FORCE_SCORE, imp)
    imp = jnp.where(j >= n_blk, NEVER, imp)
    rank = jnp.zeros((nb, nq), jnp.int32)
    for jp in range(n_blk):
        row = imp[jp:jp + 1, :]
        rank = rank + ((row > imp) | ((row == imp) & (j > jp))).astype(jnp.int32)
    picked_t = ((rank < SEL_TOPN) & (j < n_blk)).astype(F32)
    if nb_rows > nb:
        picked_t = jnp.concatenate([picked_t, jnp.zeros((nb_rows - nb, nq), F32)], axis=0)
    return picked_t


def _transpose_01(x_t):
    m = x_t.shape[1]
    eye = (lax.broadcasted_iota(jnp.int32, (m, m), 0) == lax.broadcasted_iota(jnp.int32, (m, m), 1))
    return _mm_nt(eye.astype(BF16), x_t)


def _head_rows(x, n_heads):
    dh = x.shape[1] // n_heads
    return jnp.concatenate([x[:, h * dh:(h + 1) * dh] for h in range(n_heads)], axis=0)


def _nsa_attn_body(q_ref, gate_ref, slope_ref, ck_ref, cv_ref, ks_ref, vs_ref, kw_ref, vw_ref, o_ref, osel_ref,
                   *, qb, p_len, n_cmp, n_blk, w0, sel_classes, kw_len):
    i = pl.program_id(2)
    r = NSA_R
    rows = r * qb
    q = _head_rows(q_ref[0].astype(F32), r)
    q = (q * (NSA_DH ** -0.5 * LOG2E)).astype(BF16)
    slopes = slope_ref[0] * LOG2E
    q0 = p_len + i * qb
    t_pos = q0 + lax.broadcasted_iota(jnp.int32, (1, qb, 1), 1)
    branch = functools.partial(_attn_branch, q, slopes, t_pos)

    tw = kw_ref.shape[2]
    if tw == kw_len:
        start = 0
    else:
        start = pl.multiple_of(jnp.clip(q0 + qb - w0 - kw_len, 0, tw - kw_len), SEL_BLOCK)
    kp_win = w0 + start + lax.broadcasted_iota(jnp.int32, (1, 1, kw_len), 2)
    o_win, _, _ = branch(kw_ref[0, 0, pl.ds(start, kw_len), :], vw_ref[0, 0, pl.ds(start, kw_len), :], kp_win,
                         lambda dist: (dist >= 0) & (dist < WINDOW) & (kp_win >= 0))

    nc = ck_ref.shape[2]
    n_idx = lax.broadcasted_iota(jnp.int32, (1, 1, nc), 2)
    o_cmp, p_cmp, inv_cmp = branch(ck_ref[0, 0], cv_ref[0, 0], n_idx * CMP_STRIDE + (CMP_LEN - 1),
                                   lambda dist: (dist >= 0) & (n_idx < n_cmp), may_be_empty=True)
    p_sum = jnp.sum(p_cmp * inv_cmp, axis=0)
    nb8 = -(-n_blk // 16) * 16
    picked_t = _pick_blocks(p_sum, q0 + lax.broadcasted_iota(jnp.int32, (1, qb), 1), n_cmp, n_blk, nb8)
    picked = _transpose_01(picked_t)

    def sel_branch(nk):
        blk_of_key = lax.broadcasted_iota(jnp.int32, (nb8, nk), 1) // SEL_BLOCK
        expand = (blk_of_key == lax.broadcasted_iota(jnp.int32, (nb8, nk), 0)).astype(BF16)
        key_picked = (_mm(picked, expand) > 0.5).reshape(1, qb, nk)
        o, _, _ = branch(ks_ref[0, 0, 0:nk, :], vs_ref[0, 0, 0:nk, :],
                         lax.broadcasted_iota(jnp.int32, (1, 1, nk), 2),
                         lambda dist: key_picked & (dist >= 0))
        osel_ref[...] = o

    if len(sel_classes) == 1:
        sel_branch(sel_classes[0])
    else:
        need = q0 + qb
        prev = 0
        for nk in sel_classes:
            pl.when((need > prev) & (need <= nk))(functools.partial(sel_branch, nk))
            prev = nk

    gates = _sigmoid(gate_ref[0, 0]).reshape(rows, 3)
    o = gates[:, 0:1] * o_cmp + gates[:, 1:2] * osel_ref[...] + gates[:, 2:3] * o_win
    for h in range(r):
        o_ref[0, :, h * NSA_DH:(h + 1) * NSA_DH] = o[h * qb:(h + 1) * qb].astype(o_ref.dtype)


def _nsa_attn(q, gates_t, slopes, cmp_kv, sel_arr, sel_k0, sel_v0, win_arr, win_k0, win_v0,
              *, p_len, n_cmp, n_blk, w0):
    b, tq, _ = q.shape
    g, r, dh = NSA_G, NSA_R, NSA_DH
    qb = NSA_Q_BLOCK if tq % NSA_Q_BLOCK == 0 else tq
    nc = cmp_kv.shape[2]
    tk = sel_arr.shape[2]
    tw = win_arr.shape[2]
    if tq == qb or tk % NSA_SEL_CLASS != 0:
        sel_classes = (tk,)
    else:
        sel_classes = tuple(range(NSA_SEL_CLASS, tk + 1, NSA_SEL_CLASS))
    kw_len = min(tw, -(-(WINDOW - 1 + qb) // KEY_CHUNK) * KEY_CHUNK)
    body = functools.partial(_nsa_attn_body, qb=qb, p_len=p_len, n_cmp=n_cmp, n_blk=n_blk, w0=w0,
                             sel_classes=sel_classes, kw_len=kw_len)
    return pl.pallas_call(
        body,
        grid=(b, g, tq // qb),
        in_specs=[pl.BlockSpec((1, qb, r * dh), lambda bi, gi, i: (bi, i, gi)),
                  pl.BlockSpec((1, 1, r, qb, 3), lambda bi, gi, i: (bi, gi, 0, i, 0)),
                  pl.BlockSpec((1, r, 1, 1), lambda bi, gi, i: (gi, 0, 0, 0)),
                  pl.BlockSpec((1, 1, nc, dh), lambda bi, gi, i: (bi, gi, 0, 0)),
                  pl.BlockSpec((1, 1, nc, dh), lambda bi, gi, i: (bi, NSA_G + gi, 0, 0)),
                  pl.BlockSpec((1, 1, tk, dh), lambda bi, gi, i: (bi, sel_k0 + gi, 0, 0)),
                  pl.BlockSpec((1, 1, tk, dh), lambda bi, gi, i: (bi, sel_v0 + gi, 0, 0)),
                  pl.BlockSpec((1, 1, tw, dh), lambda bi, gi, i: (bi, win_k0 + gi, 0, 0)),
                  pl.BlockSpec((1, 1, tw, dh), lambda bi, gi, i: (bi, win_v0 + gi, 0, 0))],
        out_specs=pl.BlockSpec((1, qb, r * dh), lambda bi, gi, i: (bi, i, gi)),
        out_shape=jax.ShapeDtypeStruct((b, tq, g * r * dh), BF16),
        scratch_shapes=[pltpu.VMEM((r * qb, dh), F32)],
        compiler_params=_params("parallel", "parallel", "arbitrary"),
        name="nsa_attn",
    )(q, gates_t, slopes, cmp_kv, cmp_kv, sel_arr, sel_arr, win_arr, win_arr)


def _nsa_paged_body(pt_ref, *refs, tq, p_len, n_cmp, n_blk, w0, pps, n_steps):
    k_refs, v_refs = refs[:pps], refs[pps:2 * pps]
    (q_ref, gate_ref, slope_ref, cmp_ref, new_ref, win_ref, o_ref,
     qbd_ref, pen_ref, m_ref, l_ref, acc_ref, ocmp_ref) = refs[2 * pps:]
    p = pl.program_id(1)
    g_n, r = NSA_G, NSA_R
    dh = NSA_DH
    rg = r * tq
    rows = g_n * rg
    page = k_refs[0].shape[5]
    chunk = pps * page
    n_keys = pen_ref.shape[1]
    t_pos = p_len + lax.broadcasted_iota(jnp.int32, (1, tq, 1), 1)

    @pl.when(p == 0)
    def _():
        qbd_ref[...] = jnp.zeros(qbd_ref.shape, qbd_ref.dtype)
        nc = cmp_ref.shape[2]
        n_idx = lax.broadcasted_iota(jnp.int32, (1, 1, nc), 2)
        p_sums = []
        for g in range(g_n):
            q = _head_rows(q_ref[0, :, g * r * dh:(g + 1) * r * dh].astype(F32), r)
            q = (q * (dh ** -0.5 * LOG2E)).astype(BF16)
            qbd_ref[g * rg:(g + 1) * rg, g * dh:(g + 1) * dh] = q
            o_cmp, p_cmp, inv_cmp = _attn_branch(
                q, slope_ref[g] * LOG2E, t_pos, cmp_ref[0, g], cmp_ref[0, g_n + g],
                n_idx * CMP_STRIDE + (CMP_LEN - 1), lambda dist: (dist >= 0) & (n_idx < n_cmp), may_be_empty=True)
            ocmp_ref[g * rg:(g + 1) * rg, :] = o_cmp
            p_sums.append(jnp.sum(p_cmp * inv_cmp, axis=0))
        nbp = -(-n_blk // 128) * 128
        nq = g_n * tq
        t_lane = p_len + lax.broadcasted_iota(jnp.int32, (1, nq), 1) % tq
        picked_t = _pick_blocks(jnp.concatenate(p_sums, axis=0), t_lane, n_cmp, n_blk, nbp)
        picked = _transpose_01(picked_t.astype(BF16))
        picked_rows = jnp.concatenate([picked[g * tq:(g + 1) * tq] for g in range(g_n) for _ in range(r)], axis=0)
        slope_rows = jnp.broadcast_to(slope_ref[...] * LOG2E, (g_n, r, tq, 1)).reshape(rows, 1)
        t_rows = p_len + lax.broadcasted_iota(jnp.int32, (rows, 1), 0) % tq

        def fill(k0, width):
            kp = k0 + lax.broadcasted_iota(jnp.int32, (1, width), 1)
            blk = lax.broadcasted_iota(jnp.int32, (nbp, width), 0)
            expand = ((k0 + lax.broadcasted_iota(jnp.int32, (nbp, width), 1)) // SEL_BLOCK == blk).astype(BF16)
            key_picked = _mm(picked_rows, expand) > 0.5
            dist = t_rows - kp
            return slope_rows * jnp.where(key_picked & (dist >= 0), dist.astype(F32), MASKED_DIST)

        def fill_chunk(c, carry):
            k0 = pl.multiple_of(c * chunk, chunk)
            pen_ref[:, pl.ds(k0, chunk)] = fill(k0, chunk)
            return carry

        lax.fori_loop(0, n_steps, fill_chunk, 0)
        pen_ref[:, n_steps * chunk:n_keys] = fill(n_steps * chunk, n_keys - n_steps * chunk)
        m_ref[...] = jnp.full(m_ref.shape, NEVER, F32)
        l_ref[...] = jnp.zeros(l_ref.shape, F32)
        acc_ref[...] = jnp.zeros(acc_ref.shape, F32)

    def online_update(kts, vts, k0):
        width = len(kts) * page
        qbd = qbd_ref[...]
        s = jnp.concatenate([jnp.dot(qbd, kt.astype(BF16), preferred_element_type=F32) for kt in kts], axis=1)
        s = s - pen_ref[:, pl.ds(k0, width)]
        m_prev = m_ref[...]
        m_new = jnp.maximum(m_prev, jnp.max(s, axis=-1, keepdims=True))
        alpha = jnp.exp2(m_prev - m_new)
        prob = jnp.exp2(s - m_new)
        l_ref[...] = alpha * l_ref[...] + jnp.sum(prob, axis=-1, keepdims=True)
        pv = [_mm_nt(prob[:, i * page:(i + 1) * page], vt) for i, vt in enumerate(vts)]
        acc_ref[...] = alpha * acc_ref[...] + sum(pv[1:], pv[0])
        m_ref[...] = m_new

    @pl.when(p < n_steps)
    def _():
        online_update([ref[0, 0, 0].reshape(g_n * dh, page) for ref in k_refs],
                      [ref[0, 0, 0].reshape(g_n * dh, page) for ref in v_refs], pl.multiple_of(p * chunk, chunk))

    @pl.when(p == n_steps)
    def _():
        online_update([new_ref[0, 0].reshape(g_n * dh, page)], [new_ref[0, 1].reshape(g_n * dh, page)],
                      n_steps * chunk)
        acc = acc_ref[...] * _inv_or_zero(l_ref[...])
        tw = win_ref.shape[2]
        kp_win = w0 + lax.broadcasted_iota(jnp.int32, (1, 1, tw), 2)
        for g in range(g_n):
            q = qbd_ref[g * rg:(g + 1) * rg, g * dh:(g + 1) * dh]
            o_win, _, _ = _attn_branch(q, slope_ref[g] * LOG2E, t_pos, win_ref[0, g], win_ref[0, g_n + g], kp_win,
                                       lambda dist: (dist >= 0) & (dist < WINDOW) & (kp_win >= 0))
            gates = _sigmoid(gate_ref[0, g]).reshape(rg, 3)
            o = (gates[:, 0:1] * ocmp_ref[g * rg:(g + 1) * rg, :]
                 + gates[:, 1:2] * acc[g * rg:(g + 1) * rg, g * dh:(g + 1) * dh] + gates[:, 2:3] * o_win)
            for h in range(r):
                o_ref[0, :, (g * r + h) * dh:(g * r + h + 1) * dh] = o[h * tq:(h + 1) * tq].astype(o_ref.dtype)


def _nsa_paged(q, gates_t, slopes, cmp_kv, cache_nt, layer, page_table, new_t, win_arr,
               *, p_len, n_cmp, n_blk, w0):
    b, tq, _ = q.shape
    g, r, dh = NSA_G, NSA_R, NSA_DH
    n_pages = page_table.shape[1]
    page = cache_nt.shape[5]
    pps = 8 if n_pages % 8 == 0 else (4 if n_pages % 4 == 0 else 1)
    n_steps = n_pages // pps
    nc = cmp_kv.shape[2]
    tw = win_arr.shape[2]
    rows = g * r * tq
    n_keys = (n_pages + 1) * page

    def page_map(c, s):
        return lambda i, p, pt: (layer, pt[i, jnp.minimum(p, n_steps - 1) * pps + s], c, 0, 0, 0)

    const = lambda i, p, pt: (i, 0, 0, 0, 0)
    grid_spec = pltpu.PrefetchScalarGridSpec(
        num_scalar_prefetch=1,
        grid=(b, n_steps + 1),
        in_specs=[pl.BlockSpec((1, 1, 1, g, dh, page), page_map(2, s)) for s in range(pps)]
        + [pl.BlockSpec((1, 1, 1, g, dh, page), page_map(3, s)) for s in range(pps)]
        + [pl.BlockSpec((1, tq, g * r * dh), lambda i, p, pt: (i, 0, 0)),
           pl.BlockSpec((1, g, r, tq, 3), const),
           pl.BlockSpec((g, r, 1, 1), lambda i, p, pt: (0, 0, 0, 0)),
           pl.BlockSpec((1, 2 * g, nc, dh), lambda i, p, pt: (i, 0, 0, 0)),
           pl.BlockSpec((1, 2, g, dh, page), const),
           pl.BlockSpec((1, 2 * g, tw, dh), lambda i, p, pt: (i, 0, 0, 0))],
        out_specs=pl.BlockSpec((1, tq, g * r * dh), lambda i, p, pt: (i, 0, 0)),
        scratch_shapes=[pltpu.VMEM((rows, g * dh), BF16),
                        pltpu.VMEM((rows, n_keys), F32),
                        pltpu.VMEM((rows, 1), F32),
                        pltpu.VMEM((rows, 1), F32),
                        pltpu.VMEM((rows, g * dh), F32),
                        pltpu.VMEM((rows, dh), F32)],
    )
    body = functools.partial(_nsa_paged_body, tq=tq, p_len=p_len, n_cmp=n_cmp, n_blk=n_blk, w0=w0, pps=pps,
                             n_steps=n_steps)
    return pl.pallas_call(
        body,
        grid_spec=grid_spec,
        out_shape=jax.ShapeDtypeStruct((b, tq, g * r * dh), BF16),
        compiler_params=_params("parallel", "arbitrary"),
        name="nsa_paged",
    )(page_table, *([cache_nt] * (2 * pps)), q, gates_t, slopes, cmp_kv, new_t, win_arr)


def _nsa_layer(x, cache_hm, layer, page_table, win_buf, norm_w, w_in, w1cat, pe_flat, w2, w_out, slopes):
    b, t, d = x.shape
    g, r, dh = NSA_G, NSA_R, NSA_DH
    xf = x.reshape(b * t, d)
    w_q, w_kv = w_in
    q = _rms_matmul(xf, norm_w, w_q, NSA_Q_DIM, BF16).reshape(b, t, NSA_Q_DIM)
    proj = _rms_matmul(xf, norm_w, w_kv, 896).reshape(b, t, -1)
    kv = proj[..., :6 * NSA_KV_DIM]
    gates_t = proj[..., 6 * NSA_KV_DIM:6 * NSA_KV_DIM + 3 * NSA_HEADS]
    gates_t = gates_t.reshape(b, t, g, r, 3).transpose(0, 2, 3, 1, 4)
    kv6 = kv.reshape(b, t, 6, g, dh)
    new_rows = kv6[:, :, :4]
    if cache_hm is None:
        p_len = 0
        tk = t
        assert t % CMP_STRIDE == 0
        cmp_x = kv6[:, :, :2].astype(BF16).transpose(0, 2, 3, 1, 4).reshape(b, 2 * g, t // CMP_STRIDE,
                                                                           CMP_STRIDE * dh)
        kv_hm = kv6[:, :, 2:].astype(BF16).transpose(0, 2, 3, 1, 4).reshape(b, 4 * g, t, dh)
        new_win = kv6[:, t - min(WINDOW, t):, 4:]
    else:
        cache_cmp, cache_nt = cache_hm
        n_pages = page_table.shape[1]
        page = cache_nt.shape[5]
        p_len = n_pages * page
        tk = p_len + t
        assert p_len % CMP_STRIDE == 0 and t < CMP_STRIDE and t <= page
        cmp_x = _page_gather(cache_cmp, layer, page_table)
        new_t = jnp.pad(kv6[:, :, 2:4].transpose(0, 2, 3, 4, 1), [(0, 0)] * 4 + [(0, page - t)])
        win_all = jnp.concatenate([win_buf, kv6[:, :, 4:]], axis=1)
        wl = win_all.shape[1]
        wlp = -(-wl // KEY_CHUNK) * KEY_CHUNK
        win_arr = jnp.pad(win_all.astype(BF16), [(0, 0), (0, wlp - wl), (0, 0), (0, 0), (0, 0)])
        win_arr = win_arr.transpose(0, 2, 3, 1, 4).reshape(b, 2 * g, wlp, dh)
        new_win = win_all[:, wl - min(WINDOW, tk):]
    n_sub = tk // CMP_STRIDE
    n_cmp = n_sub - 1
    n_blk = -(-tk // SEL_BLOCK)
    cmp_kv = _compress(cmp_x, n_sub, w1cat, pe_flat, w2)
    if cache_hm is None:
        o = _nsa_attn(q, gates_t, slopes, cmp_kv, kv_hm, 0, g, kv_hm, 2 * g, 3 * g,
                      p_len=0, n_cmp=n_cmp, n_blk=n_blk, w0=0)
    else:
        o = _nsa_paged(q, gates_t, slopes, cmp_kv, cache_nt, layer, page_table, new_t, win_arr,
                       p_len=p_len, n_cmp=n_cmp, n_blk=n_blk, w0=p_len - win_buf.shape[1])
    y = _matmul_res(o.reshape(b * t, NSA_Q_DIM), w_out, xf).reshape(b, t, d)
    return y, new_rows, new_win


def _ffn_layer(x, buf, norm_w, w_up, conv_w, w_down):
    b, t, d = x.shape
    xf = x.reshape(b * t, d)
    h = _rms_matmul(xf, norm_w, w_up, 512, BF16).reshape(b, t, -1)
    act, new_buf = _ffn_act(h, buf, conv_w)
    y = _matmul_res(act.reshape(b * t, D_FF), w_down, xf).reshape(b, t, d)
    return y, new_buf


def _pad_cols(w, n):
    return jnp.pad(w, [(0, 0)] * (w.ndim - 1) + [(0, n - w.shape[-1])])


def _trunk(x, cache_hm, page_table, nsa_win, gdn_state, gdn_conv, ffn_conv, wts):
    depth = wts["ffn_w_up"].shape[0]
    rows_l, win_l, s_l, gconv_l, fconv_l = [], [], [], [], []
    for i in range(depth):
        j = i // 2
        if i % 2 == 0:
            x, s_new, cb = _gdn_layer(x, gdn_state[j], gdn_conv[j], wts["gdn_norm"][j],
                                      (wts["gdn_w_main"][j], wts["gdn_w_tail"][j]),
                                      wts["gdn_conv_w"][j], wts["gdn_a_log"][j], wts["gdn_dt_bias"][j],
                                      wts["gdn_out_norm"][j], wts["gdn_w_out"][j])
            s_l.append(s_new)
            gconv_l.append(cb)
        else:
            x, rows, wb = _nsa_layer(x, cache_hm, j, page_table, None if nsa_win is None else nsa_win[j],
                                     wts["nsa_norm"][j], (wts["nsa_w_q"][j], wts["nsa_w_kv"][j]),
                                     wts["nsa_w1cat"][j],
                                     wts["nsa_pe_flat"][j], wts["nsa_w2"][j], wts["nsa_w_out"][j],
                                     wts["slopes"])
            rows_l.append(rows)
            win_l.append(wb)
        x, fb = _ffn_layer(x, ffn_conv[i], wts["ffn_norm"][i], wts["ffn_w_up"][i], wts["ffn_conv_w"][i],
                           wts["ffn_w_down"][i])
        fconv_l.append(fb)
    b, t, d = x.shape
    y = _rms(x.reshape(b * t, d), wts["final_norm"]).reshape(b, t, d)
    return (y, jnp.stack(rows_l), jnp.stack(win_l), jnp.stack(s_l), jnp.stack(gconv_l), jnp.stack(fconv_l))


def kernel(x_prompt, x_sample, cache_nsa_kv, cache_nsa_win, state_gdn_s, state_gdn_conv, state_ffn_conv,
           page_table, gdn_norm, gdn_w_in, gdn_conv_w, gdn_a_log, gdn_dt_bias, gdn_out_norm, gdn_w_out,
           nsa_norm, nsa_w_in, nsa_cmp_pe, nsa_cmp_w1, nsa_cmp_w2, nsa_w_out,
           ffn_norm, ffn_w_up, ffn_conv_w, ffn_w_down, final_norm):
    n_gdn = gdn_w_in.shape[0]
    n_nsa = nsa_w_in.shape[0]
    depth = ffn_w_up.shape[0]
    bp = x_prompt.shape[0]
    gdn_main = GDN_CONV_CH + GDN_V_DIM
    nsa_cols = -(-(nsa_w_in.shape[2] - NSA_Q_DIM) // 896) * 896
    w1 = nsa_cmp_w1.reshape(n_nsa, 2, 2, CMP_STRIDE * NSA_DH, CMP_HIDDEN)
    w1cat = jnp.concatenate([w1[:, :, 0], w1[:, :, 1]], axis=-1).astype(BF16)
    slopes = 2.0 ** (-8.0 * jnp.arange(1, NSA_HEADS + 1, dtype=F32) / NSA_HEADS)
    wts = {
        "gdn_norm": gdn_norm,
        "gdn_w_main": gdn_w_in[:, :, :gdn_main].astype(BF16),
        "gdn_w_tail": _pad_cols(gdn_w_in[:, :, gdn_main:], 128).astype(BF16),
        "gdn_conv_w": gdn_conv_w,
        "gdn_a_log": _pad_cols(gdn_a_log, 128).reshape(n_gdn, 1, 128),
        "gdn_dt_bias": _pad_cols(gdn_dt_bias, 128).reshape(n_gdn, 1, 128),
        "gdn_out_norm": gdn_out_norm,
        "gdn_w_out": gdn_w_out.astype(BF16),
        "nsa_norm": nsa_norm,
        "nsa_w_q": nsa_w_in[:, :, :NSA_Q_DIM].astype(BF16),
        "nsa_w_kv": _pad_cols(nsa_w_in[:, :, NSA_Q_DIM:], nsa_cols).astype(BF16),
        "nsa_w1cat": w1cat,
        "nsa_pe_flat": nsa_cmp_pe.reshape(n_nsa, 2, 2, CMP_STRIDE * NSA_DH),
        "nsa_w2": nsa_cmp_w2.astype(BF16),
        "nsa_w_out": nsa_w_out.astype(BF16),
        "slopes": slopes.reshape(NSA_G, NSA_R, 1, 1),
        "ffn_norm": ffn_norm,
        "ffn_w_up": ffn_w_up.astype(BF16),
        "ffn_conv_w": ffn_conv_w,
        "ffn_w_down": ffn_w_down.astype(BF16),
        "final_norm": final_norm,
    }
    n_l, pool, page = cache_nsa_kv.shape[:3]
    cache_cmp = cache_nsa_kv[:, :, :, :2].astype(BF16).transpose(0, 1, 3, 4, 2, 5)
    cache_hm = (cache_cmp.reshape(n_l, pool, 2 * NSA_G, page // CMP_STRIDE, CMP_STRIDE * NSA_DH),
                cache_nsa_kv.transpose(0, 1, 3, 4, 5, 2))

    zeros = functools.partial(jnp.zeros, dtype=F32)
    prompt = _trunk(x_prompt, None, None, None,
                    zeros((n_gdn, bp, GDN_V_HEADS, GDN_DK, GDN_DV)),
                    zeros((n_gdn, bp, GDN_CONV_W - 1, GDN_CONV_CH)),
                    zeros((depth, bp, FFN_CONV_W - 1, D_FF)), wts)
    sample = _trunk(x_sample, cache_hm, page_table, cache_nsa_win, state_gdn_s, state_gdn_conv, state_ffn_conv, wts)
    out = []
    for p, s in zip(prompt, sample):
        out.extend([p, s])
    return tuple(out)
```

```python
import functools

import jax
import jax.numpy as jnp
from jax import lax
from jax.experimental import pallas as pl
from jax.experimental.pallas import tpu as pltpu

F32 = jnp.float32
BF16 = jnp.bfloat16

RMS_EPS = 1e-6
L2_EPS = 1e-6
NEG_INF = -1e30
FORCE_SCORE = 1e9
NEVER = -3e38
MASKED_DIST = 1e30
LOG2E = 1.4426950408889634

GDN_QK_HEADS = 8
GDN_V_HEADS = 16
GDN_DK = 128
GDN_DV = 128
GDN_QK_DIM = GDN_QK_HEADS * GDN_DK
GDN_V_DIM = GDN_V_HEADS * GDN_DV
GDN_CONV_CH = 2 * GDN_QK_DIM + GDN_V_DIM
GDN_CONV_W = 4
GDN_CHUNK = 64
GDN_HEADS_PER_STEP = 4
NSA_HEADS = 16
NSA_G = 4
NSA_R = 4
NSA_DH = 64
NSA_Q_DIM = NSA_HEADS * NSA_DH
NSA_KV_DIM = NSA_G * NSA_DH
CMP_STRIDE = 16
CMP_LEN = 32
CMP_HIDDEN = 256
SEL_BLOCK = 64
SEL_TOPN = 8
WINDOW = 512
NSA_Q_BLOCK = 64
NSA_Q_BLOCKS_PER_STEP = 4
KEY_CHUNK = 128
NSA_SEL_CLASS = 512
D_FF = 2816
FFN_CONV_W = 3
FFN_TC = 256
SEQ_ROWS_PER_STEP = 2048

VMEM_LIMIT = 52 * 1024 * 1024


def _params(*sem):
    return pltpu.CompilerParams(dimension_semantics=sem, vmem_limit_bytes=VMEM_LIMIT)


def _mm(a, b):
    return jnp.dot(a.astype(BF16), b.astype(BF16), preferred_element_type=F32)


def _mm_nt(a, b):
    return lax.dot_general(a.astype(BF16), b.astype(BF16), (((1,), (1,)), ((), ())),
                           preferred_element_type=F32)


def _mm_tn(a, b):
    return lax.dot_general(a.astype(BF16), b.astype(BF16), (((0,), (0,)), ((), ())),
                           preferred_element_type=F32)


def _sigmoid(x):
    return 1.0 / (1.0 + jnp.exp(-x))


def _silu(x):
    return x * _sigmoid(x)


def _rms_matmul_body(x_ref, nw_ref, w_ref, o_ref, xn_ref):
    @pl.when(pl.program_id(1) == 0)
    def _():
        x = x_ref[...]
        inv = lax.rsqrt(jnp.mean(x * x, axis=-1, keepdims=True) + RMS_EPS)
        xn_ref[...] = ((x * inv) * nw_ref[...]).astype(BF16)

    o_ref[...] = jnp.dot(xn_ref[...], w_ref[...], preferred_element_type=F32).astype(o_ref.dtype)


def _rms_matmul(x, nw, w, tn, out_dtype=F32):
    m, k = x.shape
    n = w.shape[1]
    tm = min(m, 1024)
    return pl.pallas_call(
        _rms_matmul_body,
        grid=(m // tm, n // tn),
        in_specs=[pl.BlockSpec((tm, k), lambda i, j: (i, 0)),
                  pl.BlockSpec((1, k), lambda i, j: (0, 0)),
                  pl.BlockSpec((k, tn), lambda i, j: (0, j))],
        out_specs=pl.BlockSpec((tm, tn), lambda i, j: (i, j)),
        out_shape=jax.ShapeDtypeStruct((m, n), out_dtype),
        scratch_shapes=[pltpu.VMEM((tm, k), BF16)],
        compiler_params=_params("parallel", "arbitrary"),
        name="rms_matmul",
    )(x, nw.reshape(1, k), w)


def _matmul_res_body(a_ref, w_ref, r_ref, o_ref):
    o_ref[...] = r_ref[...] + jnp.dot(a_ref[...].astype(BF16), w_ref[...], preferred_element_type=F32)


def _matmul_res(a, w, res):
    m, k = a.shape
    n = w.shape[1]
    tm = min(m, 512)
    return pl.pallas_call(
        _matmul_res_body,
        grid=(m // tm,),
        in_specs=[pl.BlockSpec((tm, k), lambda i: (i, 0)),
                  pl.BlockSpec((k, n), lambda i: (0, 0)),
                  pl.BlockSpec((tm, n), lambda i: (i, 0))],
        out_specs=pl.BlockSpec((tm, n), lambda i: (i, 0)),
        out_shape=jax.ShapeDtypeStruct((m, n), F32),
        compiler_params=_params("parallel"),
        name="matmul_res",
    )(a, w, res)


def _rms_body(x_ref, nw_ref, o_ref):
    x = x_ref[...]
    inv = lax.rsqrt(jnp.mean(x * x, axis=-1, keepdims=True) + RMS_EPS)
    o_ref[...] = (x * inv) * nw_ref[...]


def _rms(x, nw):
    m, k = x.shape
    tm = min(m, 1024)
    return pl.pallas_call(
        _rms_body,
        grid=(m // tm,),
        in_specs=[pl.BlockSpec((tm, k), lambda i: (i, 0)), pl.BlockSpec((1, k), lambda i: (0, 0))],
        out_specs=pl.BlockSpec((tm, k), lambda i: (i, 0)),
        out_shape=jax.ShapeDtypeStruct((m, k), F32),
        compiler_params=_params("parallel"),
        name="final_rms",
    )(x, nw.reshape(1, k))


def _shifted(x, prev_rows, shift, row):
    nb = prev_rows.shape[1]
    y = pltpu.roll(x, shift, 1)
    for r in range(shift):
        y = jnp.where(row == r, prev_rows[:, nb - shift + r:nb - shift + r + 1], y)
    return y


def _batch_block(b, t):
    bb = max(1, min(b, SEQ_ROWS_PER_STEP // t))
    while b % bb:
        bb -= 1
    return bb


def _ffn_act_body(a_ref, g_ref, buf_ref, cw_ref, act_ref, nb_ref):
    a = a_ref[...].astype(F32)
    t = a.shape[1]
    buf = buf_ref[...]
    w = cw_ref[...]
    row = lax.broadcasted_iota(jnp.int32, a.shape, 1)
    y = _shifted(a, buf, 2, row) * w[0:1] + _shifted(a, buf, 1, row) * w[1:2] + a * w[2:3]
    act_ref[...] = (_silu(y) * g_ref[...].astype(F32)).astype(act_ref.dtype)
    nb_ref[...] = a[:, t - (FFN_CONV_W - 1):t, :]


def _ffn_act(h, buf, cw):
    b, t, _ = h.shape
    nj = D_FF // FFN_TC
    bb = _batch_block(b, t)
    return pl.pallas_call(
        _ffn_act_body,
        grid=(b // bb, nj),
        in_specs=[pl.BlockSpec((bb, t, FFN_TC), lambda i, j: (i, 0, j)),
                  pl.BlockSpec((bb, t, FFN_TC), lambda i, j: (i, 0, j + nj)),
                  pl.BlockSpec((bb, FFN_CONV_W - 1, FFN_TC), lambda i, j: (i, 0, j)),
                  pl.BlockSpec((FFN_CONV_W, FFN_TC), lambda i, j: (0, j))],
        out_specs=[pl.BlockSpec((bb, t, FFN_TC), lambda i, j: (i, 0, j)),
                   pl.BlockSpec((bb, FFN_CONV_W - 1, FFN_TC), lambda i, j: (i, 0, j))],
        out_shape=[jax.ShapeDtypeStruct((b, t, D_FF), BF16),
                   jax.ShapeDtypeStruct((b, FFN_CONV_W - 1, D_FF), F32)],
        compiler_params=_params("parallel", "parallel"),
        name="ffn_act",
    )(h, h, buf, cw)


GDN_PRE_TC = 512


def _gdn_pre_body(x_ref, buf_ref, cw_ref, o_ref, nb_ref):
    j = pl.program_id(1)
    x = x_ref[...].astype(F32)
    t = x.shape[1]
    buf = buf_ref[...]
    w = cw_ref[...]
    row = lax.broadcasted_iota(jnp.int32, x.shape, 1)
    y = (_shifted(x, buf, 3, row) * w[0:1] + _shifted(x, buf, 2, row) * w[1:2]
         + _shifted(x, buf, 1, row) * w[2:3] + x * w[3:4])
    y = _silu(y)
    is_q = j < GDN_QK_DIM // GDN_PRE_TC
    is_v = j >= 2 * GDN_QK_DIM // GDN_PRE_TC
    qscale = jnp.where(is_q, GDN_DK ** -0.5, 1.0).astype(F32)
    for h in range(GDN_PRE_TC // GDN_DK):
        yh = y[:, :, h * GDN_DK:(h + 1) * GDN_DK]
        inv = lax.rsqrt(jnp.sum(yh * yh, axis=-1, keepdims=True) + L2_EPS)
        o_ref[:, :, h * GDN_DK:(h + 1) * GDN_DK] = jnp.where(is_v, yh, (yh * inv) * qscale).astype(o_ref.dtype)
    nb_ref[...] = x[:, t - (GDN_CONV_W - 1):t, :]


def _gdn_pre(proj, buf, cw):
    b, t, _ = proj.shape
    nj = GDN_CONV_CH // GDN_PRE_TC
    bb = _batch_block(b, t)
    return pl.pallas_call(
        _gdn_pre_body,
        grid=(b // bb, nj),
        in_specs=[pl.BlockSpec((bb, t, GDN_PRE_TC), lambda i, j: (i, 0, j)),
                  pl.BlockSpec((bb, GDN_CONV_W - 1, GDN_PRE_TC), lambda i, j: (i, 0, j)),
                  pl.BlockSpec((GDN_CONV_W, GDN_PRE_TC), lambda i, j: (0, j))],
        out_specs=[pl.BlockSpec((bb, t, GDN_PRE_TC), lambda i, j: (i, 0, j)),
                   pl.BlockSpec((bb, GDN_CONV_W - 1, GDN_PRE_TC), lambda i, j: (i, 0, j))],
        out_shape=[jax.ShapeDtypeStruct((b, t, GDN_CONV_CH), BF16),
                   jax.ShapeDtypeStruct((b, GDN_CONV_W - 1, GDN_CONV_CH), F32)],
        compiler_params=_params("parallel", "parallel"),
        name="gdn_pre",
    )(proj, buf, cw)


def _gdn_gate_body(x_ref, alog_ref, dtb_ref, g_ref, gcum_ref, beta_ref, *, chunk):
    x = x_ref[...]
    z = x + dtb_ref[...]
    softplus = jnp.maximum(z, 0.0) + jnp.log(1.0 + jnp.exp(-jnp.abs(z)))
    g = -jnp.exp(alog_ref[...]) * softplus
    g_ref[...] = g
    row = lax.broadcasted_iota(jnp.int32, x.shape, 1) % chunk
    acc = g
    s = 1
    while s < chunk:
        acc = acc + jnp.where(row >= s, pltpu.roll(acc, s, 1), 0.0)
        s *= 2
    gcum_ref[...] = acc
    beta_ref[...] = _sigmoid(x)


def _gdn_gate(proj, alog_pad, dtb_pad, lane_block):
    b, t, _ = proj.shape
    chunk = min(GDN_CHUNK, t)
    bb = _batch_block(b, t)
    spec = pl.BlockSpec((bb, t, 128), lambda i: (i, 0, 0))
    return pl.pallas_call(
        functools.partial(_gdn_gate_body, chunk=chunk),
        grid=(b // bb,),
        in_specs=[pl.BlockSpec((bb, t, 128), lambda i: (i, 0, lane_block)),
                  pl.BlockSpec((1, 128), lambda i: (0, 0)),
                  pl.BlockSpec((1, 128), lambda i: (0, 0))],
        out_specs=[spec, spec, spec],
        out_shape=[jax.ShapeDtypeStruct((b, t, 128), F32)] * 3,
        compiler_params=_params("parallel"),
        name="gdn_gate",
    )(proj, alog_pad, dtb_pad)


def _unit_lower_inverses(lows, n):
    eye = (lax.broadcasted_iota(jnp.int32, (n, n), 0) == lax.broadcasted_iota(jnp.int32, (n, n), 1)).astype(F32)
    ps = [eye - low for low in lows]
    ms = [_mm(low, low) for low in lows]
    k = 2
    while True:
        ps = [p + _mm(p, m) for p, m in zip(ps, ms)]
        k *= 2
        if k >= n:
            break
        ms = [_mm(m, m) for m in ms]
    return ps


def _gdn_scan_body(q_ref, k_ref, v_ref, z_ref, gc_ref, bc_ref, gr_ref, br_ref, s0_ref, onw_ref, o_ref, s_ref,
                   u_ref, w_ref, qk_ref, qd_ref, kd_ref):
    c = GDN_CHUNK
    hb = GDN_HEADS_PER_STEP
    n_chunks = q_ref.shape[1] // c
    s_ref[...] = s0_ref[...]
    ri = lax.broadcasted_iota(jnp.int32, (c, c), 0)
    ci = lax.broadcasted_iota(jnp.int32, (c, c), 1)
    tri = ri >= ci
    stri = ri > ci
    onw = onw_ref[...]

    def qk_slice(ref, r0, qh):
        return ref[0, pl.ds(r0, c), qh * GDN_DK:(qh + 1) * GDN_DK]

    def v_slice(ref, r0, hh):
        return ref[0, pl.ds(r0, c), hh * GDN_DV:(hh + 1) * GDN_DV]

    cpb = 2 if n_chunks % 2 == 0 else 1

    def prep(nb, carry):
        items = []
        kk, qk, ks, qs = {}, {}, {}, {}
        for ch in range(cpb):
            n = nb * cpb + ch
            r0 = pl.multiple_of(n * c, c)
            gcol = gc_ref[0, 0, pl.ds(r0, c), :]
            bcol = bc_ref[0, 0, pl.ds(r0, c), :]
            grow = gr_ref[0, 0, n]
            brow = br_ref[0, 0, n]
            for qh in range(hb // 2):
                ks[ch, qh] = qk_slice(k_ref, r0, qh)
                qs[ch, qh] = qk_slice(q_ref, r0, qh)
            for hh in range(hb):
                items.append(dict(ch=ch, hh=hh, r0=r0, v=v_slice(v_ref, r0, hh),
                                  gcb=jnp.broadcast_to(gcol[:, hh:hh + 1], (c, GDN_DK)),
                                  bcb=jnp.broadcast_to(bcol[:, hh:hh + 1], (c, c)),
                                  gr=grow[hh:hh + 1, :], br=brow[hh:hh + 1, :]))
        for key in ks:
            kk[key] = _mm_nt(ks[key], ks[key])
        for key in ks:
            qk[key] = _mm_nt(qs[key], ks[key])
        lows = []
        for it in items:
            key = (it["ch"], it["hh"] // 2)
            it["decay"] = jnp.where(tri, jnp.exp(jnp.where(tri, it["gcb"][:, :c] - it["gr"], 0.0)), 0.0)
            lows.append(jnp.where(stri, (kk[key] * it["bcb"]) * it["decay"], 0.0))
        tinvs = _unit_lower_inverses(lows, c)
        us = [_mm(tinv * it["br"], it["v"]) for tinv, it in zip(tinvs, items)]
        ws = [_mm(tinv * (it["br"] * jnp.exp(it["gr"])), ks[it["ch"], it["hh"] // 2])
              for tinv, it in zip(tinvs, items)]
        for it, u, w in zip(items, us, ws):
            key = (it["ch"], it["hh"] // 2)
            hh, r0, gcb = it["hh"], it["r0"], it["gcb"]
            u_ref[hh, pl.ds(r0, c), :] = u
            w_ref[hh, pl.ds(r0, c), :] = w.astype(BF16)
            qk_ref[hh, pl.ds(r0, c), :] = jnp.where(tri, qk[key] * it["decay"], 0.0).astype(BF16)
            qd_ref[hh, pl.ds(r0, c), :] = (qs[key] * jnp.exp(gcb)).astype(BF16)
            kd_ref[hh, pl.ds(r0, c), :] = (ks[key] * jnp.exp(gcb[c - 1:c, :] - gcb)).astype(BF16)
        return carry

    lax.fori_loop(0, n_chunks // cpb, prep, 0)

    def scan(n, carry):
        r0 = pl.multiple_of(n * c, c)
        g_last = gc_ref[0, 0, pl.ds(r0 + (c - 1), 1), :]
        loaded = []
        for hh in range(hb):
            loaded.append((u_ref[hh, pl.ds(r0, c), :], w_ref[hh, pl.ds(r0, c), :], qk_ref[hh, pl.ds(r0, c), :],
                           qd_ref[hh, pl.ds(r0, c), :], kd_ref[hh, pl.ds(r0, c), :], v_slice(z_ref, r0, hh),
                           s_ref[0, hh]))
        ws_s = [_mm(w, s) for (u, w, qkm, qd, kd, z, s) in loaded]
        qd_s = [_mm(qd, s) for (u, w, qkm, qd, kd, z, s) in loaded]
        v_news = [ld[0] - ws for ld, ws in zip(loaded, ws_s)]
        qk_v = [_mm(ld[2], vn) for ld, vn in zip(loaded, v_news)]
        kd_v = [_mm_tn(ld[4], vn) for ld, vn in zip(loaded, v_news)]
        results = []
        for hh, ld in enumerate(loaded):
            o = qd_s[hh] + qk_v[hh]
            s_new = ld[6] * jnp.exp(g_last[:, hh:hh + 1]) + kd_v[hh]
            inv = lax.rsqrt(jnp.mean(o * o, axis=-1, keepdims=True) + RMS_EPS)
            results.append((((o * inv) * onw) * _silu(ld[5].astype(F32)), s_new))
        for hh, (og, s_new) in enumerate(results):
            s_ref[0, hh] = s_new
            o_ref[0, pl.ds(r0, c), hh * GDN_DV:(hh + 1) * GDN_DV] = og.astype(o_ref.dtype)
        return carry

    lax.fori_loop(0, n_chunks, scan, 0)


def _gdn_scan(qkv, proj, gcol, bcol, grow, brow, s0, onw):
    b, t, _ = qkv.shape
    hb = GDN_HEADS_PER_STEP
    hg = GDN_V_HEADS // hb
    qw = hb // 2 * GDN_DK
    vw = hb * GDN_DV
    n = t // GDN_CHUNK
    return pl.pallas_call(
        _gdn_scan_body,
        grid=(b, hg),
        in_specs=[pl.BlockSpec((1, t, qw), lambda i, j: (i, 0, j)),
                  pl.BlockSpec((1, t, qw), lambda i, j: (i, 0, GDN_QK_DIM // qw + j)),
                  pl.BlockSpec((1, t, vw), lambda i, j: (i, 0, 2 * GDN_QK_DIM // vw + j)),
                  pl.BlockSpec((1, t, vw), lambda i, j: (i, 0, GDN_CONV_CH // vw + j)),
                  pl.BlockSpec((1, 1, t, hb), lambda i, j: (i, j, 0, 0)),
                  pl.BlockSpec((1, 1, t, hb), lambda i, j: (i, j, 0, 0)),
                  pl.BlockSpec((1, 1, n, hb, GDN_CHUNK), lambda i, j: (i, j, 0, 0, 0)),
                  pl.BlockSpec((1, 1, n, hb, GDN_CHUNK), lambda i, j: (i, j, 0, 0, 0)),
                  pl.BlockSpec((1, hb, GDN_DK, GDN_DV), lambda i, j: (i, j, 0, 0)),
                  pl.BlockSpec((1, GDN_DV), lambda i, j: (0, 0))],
        out_specs=[pl.BlockSpec((1, t, vw), lambda i, j: (i, 0, j)),
                   pl.BlockSpec((1, hb, GDN_DK, GDN_DV), lambda i, j: (i, j, 0, 0))],
        out_shape=[jax.ShapeDtypeStruct((b, t, GDN_V_DIM), BF16),
                   jax.ShapeDtypeStruct((b, GDN_V_HEADS, GDN_DK, GDN_DV), F32)],
        scratch_shapes=[pltpu.VMEM((hb, t, GDN_DV), F32),
                        pltpu.VMEM((hb, t, GDN_DK), BF16),
                        pltpu.VMEM((hb, t, GDN_CHUNK), BF16),
                        pltpu.VMEM((hb, t, GDN_DK), BF16),
                        pltpu.VMEM((hb, t, GDN_DK), BF16)],
        compiler_params=_params("parallel", "parallel"),
        name="gdn_scan",
    )(qkv, qkv, qkv, proj, gcol, bcol, grow, brow, s0, onw.reshape(1, GDN_DV))


def _gdn_layer(x, s0, conv_buf, norm_w, w_in, conv_w, alog_pad, dtb_pad, out_norm_w, w_out):
    b, t, d = x.shape
    xf = x.reshape(b * t, d)
    w_main, w_tail = w_in
    proj = _rms_matmul(xf, norm_w, w_main, 2048, BF16).reshape(b, t, -1)
    tail = _rms_matmul(xf, norm_w, w_tail, 128).reshape(b, t, -1)
    qkv, new_buf = _gdn_pre(proj, conv_buf, conv_w)
    _, gcum, beta = _gdn_gate(tail, alog_pad, dtb_pad, 0)
    gcum = gcum[:, :, :GDN_V_HEADS]
    beta = beta[:, :, GDN_V_HEADS:2 * GDN_V_HEADS]
    tp = -(-t // GDN_CHUNK) * GDN_CHUNK
    if tp != t:
        pad = [(0, 0), (0, tp - t), (0, 0)]
        qkv = jnp.pad(qkv, pad)
        proj_z = jnp.pad(proj, pad)
        gcum = jnp.pad(gcum, pad, mode="edge")
        beta = jnp.pad(beta, pad)
    else:
        proj_z = proj
    hb = GDN_HEADS_PER_STEP
    hg = GDN_V_HEADS // hb
    n = tp // GDN_CHUNK
    gcol = gcum.reshape(b, tp, hg, hb).transpose(0, 2, 1, 3)
    bcol = beta.reshape(b, tp, hg, hb).transpose(0, 2, 1, 3)
    grow = gcum.reshape(b, n, GDN_CHUNK, hg, hb).transpose(0, 3, 1, 4, 2)
    brow = beta.reshape(b, n, GDN_CHUNK, hg, hb).transpose(0, 3, 1, 4, 2)
    o, s_new = _gdn_scan(qkv, proj_z, gcol, bcol, grow, brow, s0, out_norm_w)
    o = o[:, :t].reshape(b * t, GDN_V_DIM)
    y = _matmul_res(o, w_out, xf).reshape(b, t, d)
    return y, s_new, new_buf


def _page_gather_body(pt_ref, *refs, pps):
    cmp_refs, o_ref = refs[:pps], refs[pps]
    sub = cmp_refs[0].shape[3]
    for s in range(pps):
        o_ref[0, :, s * sub:(s + 1) * sub, :] = cmp_refs[s][0, 0]


def _page_gather(cache_cmp, layer, page_table):
    b, n_pages = page_table.shape
    _, _, n_cg, sub, flat = cache_cmp.shape
    pps = 16 if n_pages % 16 == 0 else (8 if n_pages % 8 == 0 else 1)
    n_steps = n_pages // pps

    def page_map(s):
        return lambda i, p, pt: (layer, pt[i, p * pps + s], 0, 0, 0)

    grid_spec = pltpu.PrefetchScalarGridSpec(
        num_scalar_prefetch=1,
        grid=(b, n_steps),
        in_specs=[pl.BlockSpec((1, 1, n_cg, sub, flat), page_map(s)) for s in range(pps)],
        out_specs=pl.BlockSpec((1, n_cg, pps * sub, flat), lambda i, p, pt: (i, 0, p, 0)),
    )
    return pl.pallas_call(
        functools.partial(_page_gather_body, pps=pps),
        grid_spec=grid_spec,
        out_shape=jax.ShapeDtypeStruct((b, n_cg, n_pages * sub, flat), cache_cmp.dtype),
        compiler_params=_params("parallel", "arbitrary"),
        name="page_gather",
    )(page_table, *([cache_cmp] * pps))


def _compress_body(x_ref, w1_ref, pe_ref, w2_ref, o_ref, bias_ref, *, n_sub):
    hd = CMP_HIDDEN

    @pl.when(pl.program_id(1) == 0)
    def _():
        pe = pe_ref[0]
        pe0 = jnp.broadcast_to(pe[0:1], (8, pe.shape[1]))
        pe1 = jnp.broadcast_to(pe[1:2], (8, pe.shape[1]))
        bias_ref[:, :hd] = jnp.dot(pe0.astype(BF16), w1_ref[0, :, :hd], preferred_element_type=F32)
        bias_ref[:, hd:] = jnp.dot(pe1.astype(BF16), w1_ref[0, :, hd:], preferred_element_type=F32)

    acc = jnp.dot(x_ref[0, 0], w1_ref[0], preferred_element_type=F32)
    first = acc[:, :hd] + bias_ref[0:1, :hd]
    second = acc[:, hd:] + bias_ref[0:1, hd:]
    hid = _silu(first + pltpu.roll(second, n_sub - 1, 0))
    o_ref[0, 0] = jnp.dot(hid.astype(BF16), w2_ref[0], preferred_element_type=F32)


def _compress(x, n_sub, w1cat, pe_flat, w2):
    b, _, _, flat = x.shape
    dh = flat // CMP_STRIDE
    return pl.pallas_call(
        functools.partial(_compress_body, n_sub=n_sub),
        grid=(2 * NSA_G, b),
        in_specs=[pl.BlockSpec((1, 1, n_sub, flat), lambda j, i: (i, j, 0, 0)),
                  pl.BlockSpec((1, flat, 2 * CMP_HIDDEN), lambda j, i: (j // NSA_G, 0, 0)),
                  pl.BlockSpec((1, 2, flat), lambda j, i: (j // NSA_G, 0, 0)),
                  pl.BlockSpec((1, CMP_HIDDEN, dh), lambda j, i: (j // NSA_G, 0, 0))],
        out_specs=pl.BlockSpec((1, 1, n_sub, dh), lambda j, i: (i, j, 0, 0)),
        out_shape=jax.ShapeDtypeStruct((b, 2 * NSA_G, n_sub, dh), F32),
        scratch_shapes=[pltpu.VMEM((8, 2 * CMP_HIDDEN), F32)],
        compiler_params=_params("arbitrary", "arbitrary"),
        name="nsa_compress",
    )(x, w1cat, pe_flat, w2)


def _inv_or_zero(l):
    return jnp.where(l > 0.0, 1.0 / jnp.where(l > 0.0, l, 1.0), 0.0)


def _attn_branch(q, slopes, t_pos, k, v, kp, ok_fn, may_be_empty=False):
    r, qb, nk = slopes.shape[0], t_pos.shape[1], k.shape[0]
    dist = t_pos - kp
    ok = ok_fn(dist)
    pen = jnp.where(ok, dist.astype(F32), MASKED_DIST)
    s = _mm_nt(q, k).reshape(r, qb, nk) - slopes * pen
    m = jnp.max(s, axis=-1, keepdims=True)
    p = jnp.exp2(s - m)
    if may_be_empty:
        p = jnp.where(ok, p, 0.0)
    inv = _inv_or_zero(jnp.sum(p, axis=-1, keepdims=True))
    o = _mm(p.reshape(r * qb, nk), v) * inv.reshape(r * qb, 1)
    return o, p, inv


def _pick_blocks(p_sum, t_lane, n_cmp, n_blk, nb_rows):
    nq, nc = p_sum.shape
    nb = -(-n_blk // 8) * 8
    jj = lax.broadcasted_iota(jnp.int32, (nb, nc), 0)
    nn = lax.broadcasted_iota(jnp.int32, (nb, nc), 1)
    overlap_t = ((nn * CMP_STRIDE < (jj + 1) * SEL_BLOCK) & (nn * CMP_STRIDE + CMP_LEN > jj * SEL_BLOCK)
                 & (nn < n_cmp) & (jj < n_blk)).astype(F32)
    imp = lax.dot_general(overlap_t, p_sum, (((1,), (1,)), ((), ())), preferred_element_type=F32,
                          precision=lax.Precision.HIGHEST)
    j = lax.broadcasted_iota(jnp.int32, (nb, nq), 0)
    cur = t_lane // SEL_BLOCK
    imp = jnp.where((j == 0) | (j == cur) | (j == cur - 1), FORCE_SCORE, imp)
    imp = jnp.where(j > cur, -FORCE_SCORE, imp)
    imp = jnp.where(j >= n_blk, NEVER, imp)
    rank = jnp.zeros((nb, nq), jnp.int32)
    for jp in range(n_blk):
        row = imp[jp:jp + 1, :]
        rank = rank + ((row > imp) | ((row == imp) & (j > jp))).astype(jnp.int32)
    picked_t = ((rank < SEL_TOPN) & (j < n_blk)).astype(F32)
    if nb_rows > nb:
        picked_t = jnp.concatenate([picked_t, jnp.zeros((nb_rows - nb, nq), F32)], axis=0)
    return picked_t


def _transpose_01(x_t):
    m = x_t.shape[1]
    eye = (lax.broadcasted_iota(jnp.int32, (m, m), 0) == lax.broadcasted_iota(jnp.int32, (m, m), 1))
    return _mm_nt(eye.astype(BF16), x_t)


def _head_rows(x, n_heads):
    dh = x.shape[1] // n_heads
    return jnp.concatenate([x[:, h * dh:(h + 1) * dh] for h in range(n_heads)], axis=0)


def _nsa_attn_body(q_ref, gate_ref, slope_ref, ck_ref, cv_ref, ks_ref, vs_ref, kw_ref, vw_ref, o_ref, osel_ref,
                   *, nsb, **kw):
    refs = (q_ref, gate_ref, slope_ref, ck_ref, cv_ref, ks_ref, vs_ref, kw_ref, vw_ref, o_ref, osel_ref)
    if nsb == 1:
        _nsa_attn_block(0, pl.program_id(2), *refs, **kw)
    else:
        def one(sub, carry):
            _nsa_attn_block(sub, pl.program_id(2) * nsb + sub, *refs, **kw)
            return carry

        lax.fori_loop(0, nsb, one, 0)


def _nsa_attn_block(sub, i, q_ref, gate_ref, slope_ref, ck_ref, cv_ref, ks_ref, vs_ref, kw_ref, vw_ref, o_ref,
                    osel_ref, *, qb, p_len, n_cmp, n_blk, w0, sel_classes, kw_len):
    r = NSA_R
    rows = r * qb
    rq = sub * qb if isinstance(sub, int) else pl.multiple_of(sub * qb, qb)
    q = _head_rows(q_ref[0, pl.ds(rq, qb), :].astype(F32), r)
    q = (q * (NSA_DH ** -0.5 * LOG2E)).astype(BF16)
    slopes = slope_ref[0] * LOG2E
    q0 = p_len + i * qb
    t_pos = q0 + lax.broadcasted_iota(jnp.int32, (1, qb, 1), 1)
    branch = functools.partial(_attn_branch, q, slopes, t_pos)

    tw = kw_ref.shape[2]
    if tw == kw_len:
        start = 0
    else:
        start = pl.multiple_of(jnp.clip(q0 + qb - w0 - kw_len, 0, tw - kw_len), SEL_BLOCK)
    kp_win = w0 + start + lax.broadcasted_iota(jnp.int32, (1, 1, kw_len), 2)
    o_win, _, _ = branch(kw_ref[0, 0, pl.ds(start, kw_len), :], vw_ref[0, 0, pl.ds(start, kw_len), :], kp_win,
                         lambda dist: (dist >= 0) & (dist < WINDOW) & (kp_win >= 0))

    nc = ck_ref.shape[2]
    n_idx = lax.broadcasted_iota(jnp.int32, (1, 1, nc), 2)
    o_cmp, p_cmp, inv_cmp = branch(ck_ref[0, 0], cv_ref[0, 0], n_idx * CMP_STRIDE + (CMP_LEN - 1),
                                   lambda dist: (dist >= 0) & (n_idx < n_cmp), may_be_empty=True)
    p_sum = jnp.sum(p_cmp * inv_cmp, axis=0)
    nb8 = -(-n_blk // 16) * 16
    picked_t = _pick_blocks(p_sum, q0 + lax.broadcasted_iota(jnp.int32, (1, qb), 1), n_cmp, n_blk, nb8)
    picked = _transpose_01(picked_t)

    def sel_branch(nk):
        blk_of_key = lax.broadcasted_iota(jnp.int32, (nb8, nk), 1) // SEL_BLOCK
        expand = (blk_of_key == lax.broadcasted_iota(jnp.int32, (nb8, nk), 0)).astype(BF16)
        key_picked = (_mm(picked, expand) > 0.5).reshape(1, qb, nk)
        o, _, _ = branch(ks_ref[0, 0, 0:nk, :], vs_ref[0, 0, 0:nk, :],
                         lax.broadcasted_iota(jnp.int32, (1, 1, nk), 2),
                         lambda dist: key_picked & (dist >= 0))
        osel_ref[...] = o

    if len(sel_classes) == 1:
        sel_branch(sel_classes[0])
    else:
        need = q0 + qb
        prev = 0
        for nk in sel_classes:
            pl.when((need > prev) & (need <= nk))(functools.partial(sel_branch, nk))
            prev = nk

    gates = _sigmoid(gate_ref[0, 0, :, pl.ds(rq, qb), :]).reshape(rows, 3)
    o = gates[:, 0:1] * o_cmp + gates[:, 1:2] * osel_ref[...] + gates[:, 2:3] * o_win
    for h in range(r):
        o_ref[0, pl.ds(rq, qb), h * NSA_DH:(h + 1) * NSA_DH] = o[h * qb:(h + 1) * qb].astype(o_ref.dtype)


def _nsa_attn(q, gates_t, slopes, cmp_kv, sel_arr, sel_k0, sel_v0, win_arr, win_k0, win_v0,
              *, p_len, n_cmp, n_blk, w0):
    b, tq, _ = q.shape
    g, r, dh = NSA_G, NSA_R, NSA_DH
    qb = NSA_Q_BLOCK if tq % NSA_Q_BLOCK == 0 else tq
    nc = cmp_kv.shape[2]
    tk = sel_arr.shape[2]
    tw = win_arr.shape[2]
    if tq == qb or tk % NSA_SEL_CLASS != 0:
        sel_classes = (tk,)
    else:
        sel_classes = tuple(range(NSA_SEL_CLASS, tk + 1, NSA_SEL_CLASS))
    kw_len = min(tw, -(-(WINDOW - 1 + qb) // KEY_CHUNK) * KEY_CHUNK)
    nsb = NSA_Q_BLOCKS_PER_STEP if (tq // qb) % NSA_Q_BLOCKS_PER_STEP == 0 else 1
    body = functools.partial(_nsa_attn_body, nsb=nsb, qb=qb, p_len=p_len, n_cmp=n_cmp, n_blk=n_blk, w0=w0,
                             sel_classes=sel_classes, kw_len=kw_len)
    qs = nsb * qb
    return pl.pallas_call(
        body,
        grid=(b, g, tq // qs),
        in_specs=[pl.BlockSpec((1, qs, r * dh), lambda bi, gi, i: (bi, i, gi)),
                  pl.BlockSpec((1, 1, r, qs, 3), lambda bi, gi, i: (bi, gi, 0, i, 0)),
                  pl.BlockSpec((1, r, 1, 1), lambda bi, gi, i: (gi, 0, 0, 0)),
                  pl.BlockSpec((1, 1, nc, dh), lambda bi, gi, i: (bi, gi, 0, 0)),
                  pl.BlockSpec((1, 1, nc, dh), lambda bi, gi, i: (bi, NSA_G + gi, 0, 0)),
                  pl.BlockSpec((1, 1, tk, dh), lambda bi, gi, i: (bi, sel_k0 + gi, 0, 0)),
                  pl.BlockSpec((1, 1, tk, dh), lambda bi, gi, i: (bi, sel_v0 + gi, 0, 0)),
                  pl.BlockSpec((1, 1, tw, dh), lambda bi, gi, i: (bi, win_k0 + gi, 0, 0)),
                  pl.BlockSpec((1, 1, tw, dh), lambda bi, gi, i: (bi, win_v0 + gi, 0, 0))],
        out_specs=pl.BlockSpec((1, qs, r * dh), lambda bi, gi, i: (bi, i, gi)),
        out_shape=jax.ShapeDtypeStruct((b, tq, g * r * dh), BF16),
        scratch_shapes=[pltpu.VMEM((r * qb, dh), F32)],
        compiler_params=_params("parallel", "parallel", "arbitrary"),
        name="nsa_attn",
    )(q, gates_t, slopes, cmp_kv, cmp_kv, sel_arr, sel_arr, win_arr, win_arr)


def _nsa_paged_body(pt_ref, *refs, tq, p_len, n_cmp, n_blk, w0, pps, n_steps):
    k_refs, v_refs = refs[:pps], refs[pps:2 * pps]
    (q_ref, gate_ref, slope_ref, cmp_ref, new_ref, win_ref, o_ref,
     qbd_ref, pen_ref, m_ref, l_ref, acc_ref, ocmp_ref) = refs[2 * pps:]
    p = pl.program_id(1)
    g_n, r = NSA_G, NSA_R
    dh = NSA_DH
    rg = r * tq
    rows = g_n * rg
    page = k_refs[0].shape[5]
    chunk = pps * page
    n_keys = pen_ref.shape[1]
    t_pos = p_len + lax.broadcasted_iota(jnp.int32, (1, tq, 1), 1)

    @pl.when(p == 0)
    def _():
        qbd_ref[...] = jnp.zeros(qbd_ref.shape, qbd_ref.dtype)
        nc = cmp_ref.shape[2]
        n_idx = lax.broadcasted_iota(jnp.int32, (1, 1, nc), 2)
        p_sums = []
        for g in range(g_n):
            q = _head_rows(q_ref[0, :, g * r * dh:(g + 1) * r * dh].astype(F32), r)
            q = (q * (dh ** -0.5 * LOG2E)).astype(BF16)
            qbd_ref[g * rg:(g + 1) * rg, g * dh:(g + 1) * dh] = q
            o_cmp, p_cmp, inv_cmp = _attn_branch(
                q, slope_ref[g] * LOG2E, t_pos, cmp_ref[0, g], cmp_ref[0, g_n + g],
                n_idx * CMP_STRIDE + (CMP_LEN - 1), lambda dist: (dist >= 0) & (n_idx < n_cmp), may_be_empty=True)
            ocmp_ref[g * rg:(g + 1) * rg, :] = o_cmp
            p_sums.append(jnp.sum(p_cmp * inv_cmp, axis=0))
        nbp = -(-n_blk // 128) * 128
        nq = g_n * tq
        t_lane = p_len + lax.broadcasted_iota(jnp.int32, (1, nq), 1) % tq
        picked_t = _pick_blocks(jnp.concatenate(p_sums, axis=0), t_lane, n_cmp, n_blk, nbp)
        picked = _transpose_01(picked_t.astype(BF16))
        picked_rows = jnp.concatenate([picked[g * tq:(g + 1) * tq] for g in range(g_n) for _ in range(r)], axis=0)
        slope_rows = jnp.broadcast_to(slope_ref[...] * LOG2E, (g_n, r, tq, 1)).reshape(rows, 1)
        t_rows = p_len + lax.broadcasted_iota(jnp.int32, (rows, 1), 0) % tq

        def fill(k0, width):
            kp = k0 + lax.broadcasted_iota(jnp.int32, (1, width), 1)
            blk = lax.broadcasted_iota(jnp.int32, (nbp, width), 0)
            expand = ((k0 + lax.broadcasted_iota(jnp.int32, (nbp, width), 1)) // SEL_BLOCK == blk).astype(BF16)
            key_picked = _mm(picked_rows, expand) > 0.5
            dist = t_rows - kp
            return slope_rows * jnp.where(key_picked & (dist >= 0), dist.astype(F32), MASKED_DIST)

        def fill_chunk(c, carry):
            k0 = pl.multiple_of(c * chunk, chunk)
            pen_ref[:, pl.ds(k0, chunk)] = fill(k0, chunk)
            return carry

        lax.fori_loop(0, n_steps, fill_chunk, 0)
        pen_ref[:, n_steps * chunk:n_keys] = fill(n_steps * chunk, n_keys - n_steps * chunk)
        m_ref[...] = jnp.full(m_ref.shape, NEVER, F32)
        l_ref[...] = jnp.zeros(l_ref.shape, F32)
        acc_ref[...] = jnp.zeros(acc_ref.shape, F32)

    def online_update(kts, vts, k0):
        width = len(kts) * page
        qbd = qbd_ref[...]
        s = jnp.concatenate([jnp.dot(qbd, kt.astype(BF16), preferred_element_type=F32) for kt in kts], axis=1)
        s = s - pen_ref[:, pl.ds(k0, width)]
        m_prev = m_ref[...]
        m_new = jnp.maximum(m_prev, jnp.max(s, axis=-1, keepdims=True))
        alpha = jnp.exp2(m_prev - m_new)
        prob = jnp.exp2(s - m_new)
        l_ref[...] = alpha * l_ref[...] + jnp.sum(prob, axis=-1, keepdims=True)
        pv = [_mm_nt(prob[:, i * page:(i + 1) * page], vt) for i, vt in enumerate(vts)]
        acc_ref[...] = alpha * acc_ref[...] + sum(pv[1:], pv[0])
        m_ref[...] = m_new

    @pl.when(p < n_steps)
    def _():
        online_update([ref[0, 0, 0].reshape(g_n * dh, page) for ref in k_refs],
                      [ref[0, 0, 0].reshape(g_n * dh, page) for ref in v_refs], pl.multiple_of(p * chunk, chunk))

    @pl.when(p == n_steps)
    def _():
        online_update([new_ref[0, 0].reshape(g_n * dh, page)], [new_ref[0, 1].reshape(g_n * dh, page)],
                      n_steps * chunk)
        acc = acc_ref[...] * _inv_or_zero(l_ref[...])
        tw = win_ref.shape[2]
        kp_win = w0 + lax.broadcasted_iota(jnp.int32, (1, 1, tw), 2)
        for g in range(g_n):
            q = qbd_ref[g * rg:(g + 1) * rg, g * dh:(g + 1) * dh]
            o_win, _, _ = _attn_branch(q, slope_ref[g] * LOG2E, t_pos, win_ref[0, g], win_ref[0, g_n + g], kp_win,
                                       lambda dist: (dist >= 0) & (dist < WINDOW) & (kp_win >= 0))
            gates = _sigmoid(gate_ref[0, g]).reshape(rg, 3)
            o = (gates[:, 0:1] * ocmp_ref[g * rg:(g + 1) * rg, :]
                 + gates[:, 1:2] * acc[g * rg:(g + 1) * rg, g * dh:(g + 1) * dh] + gates[:, 2:3] * o_win)
            for h in range(r):
                o_ref[0, :, (g * r + h) * dh:(g * r + h + 1) * dh] = o[h * tq:(h + 1) * tq].astype(o_ref.dtype)


def _nsa_paged(q, gates_t, slopes, cmp_kv, cache_nt, layer, page_table, new_t, win_arr,
               *, p_len, n_cmp, n_blk, w0):
    b, tq, _ = q.shape
    g, r, dh = NSA_G, NSA_R, NSA_DH
    n_pages = page_table.shape[1]
    page = cache_nt.shape[5]
    pps = 8 if n_pages % 8 == 0 else (4 if n_pages % 4 == 0 else 1)
    n_steps = n_pages // pps
    nc = cmp_kv.shape[2]
    tw = win_arr.shape[2]
    rows = g * r * tq
    n_keys = (n_pages + 1) * page

    def page_map(c, s):
        return lambda i, p, pt: (layer, pt[i, jnp.minimum(p, n_steps - 1) * pps + s], c, 0, 0, 0)

    const = lambda i, p, pt: (i, 0, 0, 0, 0)
    grid_spec = pltpu.PrefetchScalarGridSpec(
        num_scalar_prefetch=1,
        grid=(b, n_steps + 1),
        in_specs=[pl.BlockSpec((1, 1, 1, g, dh, page), page_map(2, s)) for s in range(pps)]
        + [pl.BlockSpec((1, 1, 1, g, dh, page), page_map(3, s)) for s in range(pps)]
        + [pl.BlockSpec((1, tq, g * r * dh), lambda i, p, pt: (i, 0, 0)),
           pl.BlockSpec((1, g, r, tq, 3), const),
           pl.BlockSpec((g, r, 1, 1), lambda i, p, pt: (0, 0, 0, 0)),
           pl.BlockSpec((1, 2 * g, nc, dh), lambda i, p, pt: (i, 0, 0, 0)),
           pl.BlockSpec((1, 2, g, dh, page), const),
           pl.BlockSpec((1, 2 * g, tw, dh), lambda i, p, pt: (i, 0, 0, 0))],
        out_specs=pl.BlockSpec((1, tq, g * r * dh), lambda i, p, pt: (i, 0, 0)),
        scratch_shapes=[pltpu.VMEM((rows, g * dh), BF16),
                        pltpu.VMEM((rows, n_keys), F32),
                        pltpu.VMEM((rows, 1), F32),
                        pltpu.VMEM((rows, 1), F32),
                        pltpu.VMEM((rows, g * dh), F32),
                        pltpu.VMEM((rows, dh), F32)],
    )
    body = functools.partial(_nsa_paged_body, tq=tq, p_len=p_len, n_cmp=n_cmp, n_blk=n_blk, w0=w0, pps=pps,
                             n_steps=n_steps)
    return pl.pallas_call(
        body,
        grid_spec=grid_spec,
        out_shape=jax.ShapeDtypeStruct((b, tq, g * r * dh), BF16),
        compiler_params=_params("parallel", "arbitrary"),
        name="nsa_paged",
    )(page_table, *([cache_nt] * (2 * pps)), q, gates_t, slopes, cmp_kv, new_t, win_arr)


def _nsa_layer(x, cache_hm, layer, page_table, win_buf, norm_w, w_in, w1cat, pe_flat, w2, w_out, slopes):
    b, t, d = x.shape
    g, r, dh = NSA_G, NSA_R, NSA_DH
    xf = x.reshape(b * t, d)
    w_q, w_kv = w_in
    q = _rms_matmul(xf, norm_w, w_q, NSA_Q_DIM, BF16).reshape(b, t, NSA_Q_DIM)
    proj = _rms_matmul(xf, norm_w, w_kv, w_kv.shape[1]).reshape(b, t, -1)
    kv = proj[..., :6 * NSA_KV_DIM]
    gates_t = proj[..., 6 * NSA_KV_DIM:6 * NSA_KV_DIM + 3 * NSA_HEADS]
    gates_t = gates_t.reshape(b, t, g, r, 3).transpose(0, 2, 3, 1, 4)
    kv6 = kv.reshape(b, t, 6, g, dh)
    new_rows = kv6[:, :, :4]
    if cache_hm is None:
        p_len = 0
        tk = t
        assert t % CMP_STRIDE == 0
        cmp_x = kv6[:, :, :2].astype(BF16).transpose(0, 2, 3, 1, 4).reshape(b, 2 * g, t // CMP_STRIDE,
                                                                           CMP_STRIDE * dh)
        kv_hm = kv6[:, :, 2:].astype(BF16).transpose(0, 2, 3, 1, 4).reshape(b, 4 * g, t, dh)
        new_win = kv6[:, t - min(WINDOW, t):, 4:]
    else:
        cache_cmp, cache_nt = cache_hm
        n_pages = page_table.shape[1]
        page = cache_nt.shape[5]
        p_len = n_pages * page
        tk = p_len + t
        assert p_len % CMP_STRIDE == 0 and t < CMP_STRIDE and t <= page
        cmp_x = _page_gather(cache_cmp, layer, page_table)
        new_t = jnp.pad(kv6[:, :, 2:4].transpose(0, 2, 3, 4, 1), [(0, 0)] * 4 + [(0, page - t)])
        win_all = jnp.concatenate([win_buf, kv6[:, :, 4:]], axis=1)
        wl = win_all.shape[1]
        wlp = -(-wl // KEY_CHUNK) * KEY_CHUNK
        win_arr = jnp.pad(win_all.astype(BF16), [(0, 0), (0, wlp - wl), (0, 0), (0, 0), (0, 0)])
        win_arr = win_arr.transpose(0, 2, 3, 1, 4).reshape(b, 2 * g, wlp, dh)
        new_win = win_all[:, wl - min(WINDOW, tk):]
    n_sub = tk // CMP_STRIDE
    n_cmp = n_sub - 1
    n_blk = -(-tk // SEL_BLOCK)
    cmp_kv = _compress(cmp_x, n_sub, w1cat, pe_flat, w2)
    if cache_hm is None:
        o = _nsa_attn(q, gates_t, slopes, cmp_kv, kv_hm, 0, g, kv_hm, 2 * g, 3 * g,
                      p_len=0, n_cmp=n_cmp, n_blk=n_blk, w0=0)
    else:
        o = _nsa_paged(q, gates_t, slopes, cmp_kv, cache_nt, layer, page_table, new_t, win_arr,
                       p_len=p_len, n_cmp=n_cmp, n_blk=n_blk, w0=p_len - win_buf.shape[1])
    y = _matmul_res(o.reshape(b * t, NSA_Q_DIM), w_out, xf).reshape(b, t, d)
    return y, new_rows, new_win


def _ffn_layer(x, buf, norm_w, w_up, conv_w, w_down):
    b, t, d = x.shape
    xf = x.reshape(b * t, d)
    h = _rms_matmul(xf, norm_w, w_up, D_FF, BF16).reshape(b, t, -1)
    act, new_buf = _ffn_act(h, buf, conv_w)
    y = _matmul_res(act.reshape(b * t, D_FF), w_down, xf).reshape(b, t, d)
    return y, new_buf


def _pad_cols(w, n):
    return jnp.pad(w, [(0, 0)] * (w.ndim - 1) + [(0, n - w.shape[-1])])


def _trunk(x, cache_hm, page_table, nsa_win, gdn_state, gdn_conv, ffn_conv, wts):
    depth = wts["ffn_w_up"].shape[0]
    rows_l, win_l, s_l, gconv_l, fconv_l = [], [], [], [], []
    for i in range(depth):
        j = i // 2
        if i % 2 == 0:
            x, s_new, cb = _gdn_layer(x, gdn_state[j], gdn_conv[j], wts["gdn_norm"][j],
                                      (wts["gdn_w_main"][j], wts["gdn_w_tail"][j]),
                                      wts["gdn_conv_w"][j], wts["gdn_a_log"][j], wts["gdn_dt_bias"][j],
                                      wts["gdn_out_norm"][j], wts["gdn_w_out"][j])
            s_l.append(s_new)
            gconv_l.append(cb)
        else:
            x, rows, wb = _nsa_layer(x, cache_hm, j, page_table, None if nsa_win is None else nsa_win[j],
                                     wts["nsa_norm"][j], (wts["nsa_w_q"][j], wts["nsa_w_kv"][j]),
                                     wts["nsa_w1cat"][j],
                                     wts["nsa_pe_flat"][j], wts["nsa_w2"][j], wts["nsa_w_out"][j],
                                     wts["slopes"])
            rows_l.append(rows)
            win_l.append(wb)
        x, fb = _ffn_layer(x, ffn_conv[i], wts["ffn_norm"][i], wts["ffn_w_up"][i], wts["ffn_conv_w"][i],
                           wts["ffn_w_down"][i])
        fconv_l.append(fb)
    b, t, d = x.shape
    y = _rms(x.reshape(b * t, d), wts["final_norm"]).reshape(b, t, d)
    return (y, jnp.stack(rows_l), jnp.stack(win_l), jnp.stack(s_l), jnp.stack(gconv_l), jnp.stack(fconv_l))


def kernel(x_prompt, x_sample, cache_nsa_kv, cache_nsa_win, state_gdn_s, state_gdn_conv, state_ffn_conv,
           page_table, gdn_norm, gdn_w_in, gdn_conv_w, gdn_a_log, gdn_dt_bias, gdn_out_norm, gdn_w_out,
           nsa_norm, nsa_w_in, nsa_cmp_pe, nsa_cmp_w1, nsa_cmp_w2, nsa_w_out,
           ffn_norm, ffn_w_up, ffn_conv_w, ffn_w_down, final_norm):
    n_gdn = gdn_w_in.shape[0]
    n_nsa = nsa_w_in.shape[0]
    depth = ffn_w_up.shape[0]
    bp = x_prompt.shape[0]
    gdn_main = GDN_CONV_CH + GDN_V_DIM
    nsa_cols = -(-(nsa_w_in.shape[2] - NSA_Q_DIM) // 896) * 896
    w1 = nsa_cmp_w1.reshape(n_nsa, 2, 2, CMP_STRIDE * NSA_DH, CMP_HIDDEN)
    w1cat = jnp.concatenate([w1[:, :, 0], w1[:, :, 1]], axis=-1).astype(BF16)
    slopes = 2.0 ** (-8.0 * jnp.arange(1, NSA_HEADS + 1, dtype=F32) / NSA_HEADS)
    wts = {
        "gdn_norm": gdn_norm,
        "gdn_w_main": gdn_w_in[:, :, :gdn_main].astype(BF16),
        "gdn_w_tail": _pad_cols(gdn_w_in[:, :, gdn_main:], 128).astype(BF16),
        "gdn_conv_w": gdn_conv_w,
        "gdn_a_log": _pad_cols(gdn_a_log, 128).reshape(n_gdn, 1, 128),
        "gdn_dt_bias": _pad_cols(gdn_dt_bias, 128).reshape(n_gdn, 1, 128),
        "gdn_out_norm": gdn_out_norm,
        "gdn_w_out": gdn_w_out.astype(BF16),
        "nsa_norm": nsa_norm,
        "nsa_w_q": nsa_w_in[:, :, :NSA_Q_DIM].astype(BF16),
        "nsa_w_kv": _pad_cols(nsa_w_in[:, :, NSA_Q_DIM:], nsa_cols).astype(BF16),
        "nsa_w1cat": w1cat,
        "nsa_pe_flat": nsa_cmp_pe.reshape(n_nsa, 2, 2, CMP_STRIDE * NSA_DH),
        "nsa_w2": nsa_cmp_w2.astype(BF16),
        "nsa_w_out": nsa_w_out.astype(BF16),
        "slopes": slopes.reshape(NSA_G, NSA_R, 1, 1),
        "ffn_norm": ffn_norm,
        "ffn_w_up": ffn_w_up.astype(BF16),
        "ffn_conv_w": ffn_conv_w,
        "ffn_w_down": ffn_w_down.astype(BF16),
        "final_norm": final_norm,
    }
    n_l, pool, page = cache_nsa_kv.shape[:3]
    cache_cmp = cache_nsa_kv[:, :, :, :2].astype(BF16).transpose(0, 1, 3, 4, 2, 5)
    cache_hm = (cache_cmp.reshape(n_l, pool, 2 * NSA_G, page // CMP_STRIDE, CMP_STRIDE * NSA_DH),
                cache_nsa_kv.transpose(0, 1, 3, 4, 5, 2))

    zeros = functools.partial(jnp.zeros, dtype=F32)
    prompt = _trunk(x_prompt, None, None, None,
                    zeros((n_gdn, bp, GDN_V_HEADS, GDN_DK, GDN_DV)),
                    zeros((n_gdn, bp, GDN_CONV_W - 1, GDN_CONV_CH)),
                    zeros((depth, bp, FFN_CONV_W - 1, D_FF)), wts)
    sample = _trunk(x_sample, cache_hm, page_table, cache_nsa_win, state_gdn_s, state_gdn_conv, state_ffn_conv, wts)
    out = []
    for p, s in zip(prompt, sample):
        out.extend([p, s])
    return tuple(out)
```

```python
import functools

import jax
import jax.numpy as jnp
from jax import lax
from jax.experimental import pallas as pl
from jax.experimental.pallas import tpu as pltpu

F32 = jnp.float32
BF16 = jnp.bfloat16

RMS_EPS = 1e-6
L2_EPS = 1e-6
NEG_INF = -1e30
FORCE_SCORE = 1e9
NEVER = -3e38
MASKED_DIST = 1e30
LOG2E = 1.4426950408889634

GDN_QK_HEADS = 8
GDN_V_HEADS = 16
GDN_DK = 128
GDN_DV = 128
GDN_QK_DIM = GDN_QK_HEADS * GDN_DK
GDN_V_DIM = GDN_V_HEADS * GDN_DV
GDN_CONV_CH = 2 * GDN_QK_DIM + GDN_V_DIM
GDN_CONV_W = 4
GDN_CHUNK = 64
GDN_HEADS_PER_STEP = 4
GDN_PREP_CHUNKS = 8
NSA_HEADS = 16
NSA_G = 4
NSA_R = 4
NSA_DH = 64
NSA_Q_DIM = NSA_HEADS * NSA_DH
NSA_KV_DIM = NSA_G * NSA_DH
CMP_STRIDE = 16
CMP_LEN = 32
CMP_HIDDEN = 256
SEL_BLOCK = 64
SEL_TOPN = 8
WINDOW = 512
NSA_Q_BLOCK = 64
NSA_Q_BLOCKS_PER_STEP = 4
KEY_CHUNK = 128
NSA_SEL_CLASS = 512
D_FF = 2816
FFN_CONV_W = 3
FFN_TC = 256
SEQ_ROWS_PER_STEP = 2048

VMEM_LIMIT = 52 * 1024 * 1024


def _params(*sem):
    return pltpu.CompilerParams(dimension_semantics=sem, vmem_limit_bytes=VMEM_LIMIT)


def _mm(a, b):
    return jnp.dot(a.astype(BF16), b.astype(BF16), preferred_element_type=F32)


def _mm_nt(a, b):
    return lax.dot_general(a.astype(BF16), b.astype(BF16), (((1,), (1,)), ((), ())),
                           preferred_element_type=F32)


def _mm_tn(a, b):
    return lax.dot_general(a.astype(BF16), b.astype(BF16), (((0,), (0,)), ((), ())),
                           preferred_element_type=F32)


def _sigmoid(x):
    return 1.0 / (1.0 + jnp.exp(-x))


def _silu(x):
    return x * _sigmoid(x)


def _rms_matmul_body(x_ref, nw_ref, w_ref, o_ref, xn_ref):
    @pl.when(pl.program_id(1) == 0)
    def _():
        x = x_ref[...]
        inv = lax.rsqrt(jnp.mean(x * x, axis=-1, keepdims=True) + RMS_EPS)
        xn_ref[...] = ((x * inv) * nw_ref[...]).astype(BF16)

    o_ref[...] = jnp.dot(xn_ref[...], w_ref[...], preferred_element_type=F32).astype(o_ref.dtype)


def _rms_matmul(x, nw, w, tn, out_dtype=F32):
    m, k = x.shape
    n = w.shape[1]
    tm = min(m, 1024)
    return pl.pallas_call(
        _rms_matmul_body,
        grid=(m // tm, n // tn),
        in_specs=[pl.BlockSpec((tm, k), lambda i, j: (i, 0)),
                  pl.BlockSpec((1, k), lambda i, j: (0, 0)),
                  pl.BlockSpec((k, tn), lambda i, j: (0, j))],
        out_specs=pl.BlockSpec((tm, tn), lambda i, j: (i, j)),
        out_shape=jax.ShapeDtypeStruct((m, n), out_dtype),
        scratch_shapes=[pltpu.VMEM((tm, k), BF16)],
        compiler_params=_params("parallel", "arbitrary"),
        name="rms_matmul",
    )(x, nw.reshape(1, k), w)


def _matmul_res_body(a_ref, w_ref, r_ref, o_ref):
    o_ref[...] = r_ref[...] + jnp.dot(a_ref[...].astype(BF16), w_ref[...], preferred_element_type=F32)


def _matmul_res(a, w, res):
    m, k = a.shape
    n = w.shape[1]
    tm = min(m, 1024)
    return pl.pallas_call(
        _matmul_res_body,
        grid=(m // tm,),
        in_specs=[pl.BlockSpec((tm, k), lambda i: (i, 0)),
                  pl.BlockSpec((k, n), lambda i: (0, 0)),
                  pl.BlockSpec((tm, n), lambda i: (i, 0))],
        out_specs=pl.BlockSpec((tm, n), lambda i: (i, 0)),
        out_shape=jax.ShapeDtypeStruct((m, n), F32),
        compiler_params=_params("parallel"),
        name="matmul_res",
    )(a, w, res)


def _rms_body(x_ref, nw_ref, o_ref):
    x = x_ref[...]
    inv = lax.rsqrt(jnp.mean(x * x, axis=-1, keepdims=True) + RMS_EPS)
    o_ref[...] = (x * inv) * nw_ref[...]


def _rms(x, nw):
    m, k = x.shape
    tm = min(m, 1024)
    return pl.pallas_call(
        _rms_body,
        grid=(m // tm,),
        in_specs=[pl.BlockSpec((tm, k), lambda i: (i, 0)), pl.BlockSpec((1, k), lambda i: (0, 0))],
        out_specs=pl.BlockSpec((tm, k), lambda i: (i, 0)),
        out_shape=jax.ShapeDtypeStruct((m, k), F32),
        compiler_params=_params("parallel"),
        name="final_rms",
    )(x, nw.reshape(1, k))


def _shifted(x, prev_rows, shift, row):
    nb = prev_rows.shape[1]
    y = pltpu.roll(x, shift, 1)
    for r in range(shift):
        y = jnp.where(row == r, prev_rows[:, nb - shift + r:nb - shift + r + 1], y)
    return y


def _batch_block(b, t):
    bb = max(1, min(b, SEQ_ROWS_PER_STEP // t))
    while b % bb:
        bb -= 1
    return bb


def _ffn_act_body(a_ref, g_ref, buf_ref, cw_ref, act_ref, nb_ref):
    a = a_ref[...].astype(F32)
    t = a.shape[1]
    buf = buf_ref[...]
    w = cw_ref[...]
    row = lax.broadcasted_iota(jnp.int32, a.shape, 1)
    y = _shifted(a, buf, 2, row) * w[0:1] + _shifted(a, buf, 1, row) * w[1:2] + a * w[2:3]
    act_ref[...] = (_silu(y) * g_ref[...].astype(F32)).astype(act_ref.dtype)
    nb_ref[...] = a[:, t - (FFN_CONV_W - 1):t, :]


def _ffn_act(h, buf, cw):
    b, t, _ = h.shape
    nj = D_FF // FFN_TC
    bb = _batch_block(b, t)
    return pl.pallas_call(
        _ffn_act_body,
        grid=(b // bb, nj),
        in_specs=[pl.BlockSpec((bb, t, FFN_TC), lambda i, j: (i, 0, j)),
                  pl.BlockSpec((bb, t, FFN_TC), lambda i, j: (i, 0, j + nj)),
                  pl.BlockSpec((bb, FFN_CONV_W - 1, FFN_TC), lambda i, j: (i, 0, j)),
                  pl.BlockSpec((FFN_CONV_W, FFN_TC), lambda i, j: (0, j))],
        out_specs=[pl.BlockSpec((bb, t, FFN_TC), lambda i, j: (i, 0, j)),
                   pl.BlockSpec((bb, FFN_CONV_W - 1, FFN_TC), lambda i, j: (i, 0, j))],
        out_shape=[jax.ShapeDtypeStruct((b, t, D_FF), BF16),
                   jax.ShapeDtypeStruct((b, FFN_CONV_W - 1, D_FF), F32)],
        compiler_params=_params("parallel", "parallel"),
        name="ffn_act",
    )(h, h, buf, cw)


GDN_PRE_TC = 512


def _gdn_pre_body(x_ref, buf_ref, cw_ref, o_ref, nb_ref):
    j = pl.program_id(1)
    x = x_ref[...].astype(F32)
    t = x.shape[1]
    buf = buf_ref[...]
    w = cw_ref[...]
    row = lax.broadcasted_iota(jnp.int32, x.shape, 1)
    y = (_shifted(x, buf, 3, row) * w[0:1] + _shifted(x, buf, 2, row) * w[1:2]
         + _shifted(x, buf, 1, row) * w[2:3] + x * w[3:4])
    y = _silu(y)
    is_q = j < GDN_QK_DIM // GDN_PRE_TC
    is_v = j >= 2 * GDN_QK_DIM // GDN_PRE_TC
    qscale = jnp.where(is_q, GDN_DK ** -0.5, 1.0).astype(F32)
    for h in range(GDN_PRE_TC // GDN_DK):
        yh = y[:, :, h * GDN_DK:(h + 1) * GDN_DK]
        inv = lax.rsqrt(jnp.sum(yh * yh, axis=-1, keepdims=True) + L2_EPS)
        o_ref[:, :, h * GDN_DK:(h + 1) * GDN_DK] = jnp.where(is_v, yh, (yh * inv) * qscale).astype(o_ref.dtype)
    nb_ref[...] = x[:, t - (GDN_CONV_W - 1):t, :]


def _gdn_pre(proj, buf, cw):
    b, t, _ = proj.shape
    nj = GDN_CONV_CH // GDN_PRE_TC
    bb = _batch_block(b, t)
    return pl.pallas_call(
        _gdn_pre_body,
        grid=(b // bb, nj),
        in_specs=[pl.BlockSpec((bb, t, GDN_PRE_TC), lambda i, j: (i, 0, j)),
                  pl.BlockSpec((bb, GDN_CONV_W - 1, GDN_PRE_TC), lambda i, j: (i, 0, j)),
                  pl.BlockSpec((GDN_CONV_W, GDN_PRE_TC), lambda i, j: (0, j))],
        out_specs=[pl.BlockSpec((bb, t, GDN_PRE_TC), lambda i, j: (i, 0, j)),
                   pl.BlockSpec((bb, GDN_CONV_W - 1, GDN_PRE_TC), lambda i, j: (i, 0, j))],
        out_shape=[jax.ShapeDtypeStruct((b, t, GDN_CONV_CH), BF16),
                   jax.ShapeDtypeStruct((b, GDN_CONV_W - 1, GDN_CONV_CH), F32)],
        compiler_params=_params("parallel", "parallel"),
        name="gdn_pre",
    )(proj, buf, cw)


def _gdn_gate_body(x_ref, alog_ref, dtb_ref, g_ref, gcum_ref, beta_ref, *, chunk):
    x = x_ref[...]
    z = x + dtb_ref[...]
    softplus = jnp.maximum(z, 0.0) + jnp.log(1.0 + jnp.exp(-jnp.abs(z)))
    g = -jnp.exp(alog_ref[...]) * softplus
    g_ref[...] = g
    row = lax.broadcasted_iota(jnp.int32, x.shape, 1) % chunk
    acc = g
    s = 1
    while s < chunk:
        acc = acc + jnp.where(row >= s, pltpu.roll(acc, s, 1), 0.0)
        s *= 2
    gcum_ref[...] = acc
    beta_ref[...] = _sigmoid(x)


def _gdn_gate(proj, alog_pad, dtb_pad, lane_block):
    b, t, _ = proj.shape
    chunk = min(GDN_CHUNK, t)
    bb = _batch_block(b, t)
    spec = pl.BlockSpec((bb, t, 128), lambda i: (i, 0, 0))
    return pl.pallas_call(
        functools.partial(_gdn_gate_body, chunk=chunk),
        grid=(b // bb,),
        in_specs=[pl.BlockSpec((bb, t, 128), lambda i: (i, 0, lane_block)),
                  pl.BlockSpec((1, 128), lambda i: (0, 0)),
                  pl.BlockSpec((1, 128), lambda i: (0, 0))],
        out_specs=[spec, spec, spec],
        out_shape=[jax.ShapeDtypeStruct((b, t, 128), F32)] * 3,
        compiler_params=_params("parallel"),
        name="gdn_gate",
    )(proj, alog_pad, dtb_pad)


def _unit_lower_inverses(lows, n):
    eye = (lax.broadcasted_iota(jnp.int32, (n, n), 0) == lax.broadcasted_iota(jnp.int32, (n, n), 1)).astype(F32)
    ps = [eye - low for low in lows]
    ms = [_mm(low, low) for low in lows]
    k = 2
    while True:
        ps = [p + _mm(p, m) for p, m in zip(ps, ms)]
        k *= 2
        if k >= n:
            break
        ms = [_mm(m, m) for m in ms]
    return ps


def _gdn_scan_body(q_ref, k_ref, v_ref, z_ref, gc_ref, bc_ref, gr_ref, br_ref, s0_ref, onw_ref, o_ref, s_ref,
                   u_ref, w_ref, qk_ref, qd_ref, kd_ref):
    c = GDN_CHUNK
    hb = GDN_HEADS_PER_STEP
    n_chunks = q_ref.shape[1] // c
    s_ref[...] = s0_ref[...]
    ri = lax.broadcasted_iota(jnp.int32, (c, c), 0)
    ci = lax.broadcasted_iota(jnp.int32, (c, c), 1)
    tri = ri >= ci
    stri = ri > ci
    onw = onw_ref[...]

    def qk_slice(ref, r0, qh):
        return ref[0, pl.ds(r0, c), qh * GDN_DK:(qh + 1) * GDN_DK]

    def v_slice(ref, r0, hh):
        return ref[0, pl.ds(r0, c), hh * GDN_DV:(hh + 1) * GDN_DV]

    cpb = max(c_ for c_ in (GDN_PREP_CHUNKS, 4, 2, 1) if n_chunks % c_ == 0)

    def prep(nb, carry):
        items = []
        kk, qk, ks, qs = {}, {}, {}, {}
        for ch in range(cpb):
            n = nb * cpb + ch
            r0 = pl.multiple_of(n * c, c)
            gcol = gc_ref[0, 0, pl.ds(r0, c), :]
            bcol = bc_ref[0, 0, pl.ds(r0, c), :]
            grow = gr_ref[0, 0, n]
            brow = br_ref[0, 0, n]
            for qh in range(hb // 2):
                ks[ch, qh] = qk_slice(k_ref, r0, qh)
                qs[ch, qh] = qk_slice(q_ref, r0, qh)
            for hh in range(hb):
                items.append(dict(ch=ch, hh=hh, r0=r0, v=v_slice(v_ref, r0, hh),
                                  gcb=jnp.broadcast_to(gcol[:, hh:hh + 1], (c, GDN_DK)),
                                  bcb=jnp.broadcast_to(bcol[:, hh:hh + 1], (c, c)),
                                  gr=grow[hh:hh + 1, :], br=brow[hh:hh + 1, :]))
        for key in ks:
            kk[key] = _mm_nt(ks[key], ks[key])
        for key in ks:
            qk[key] = _mm_nt(qs[key], ks[key])
        lows = []
        for it in items:
            key = (it["ch"], it["hh"] // 2)
            it["decay"] = jnp.where(tri, jnp.exp(jnp.where(tri, it["gcb"][:, :c] - it["gr"], 0.0)), 0.0)
            lows.append(jnp.where(stri, (kk[key] * it["bcb"]) * it["decay"], 0.0))
        tinvs = _unit_lower_inverses(lows, c)
        us = [_mm(tinv * it["br"], it["v"]) for tinv, it in zip(tinvs, items)]
        ws = [_mm(tinv * (it["br"] * jnp.exp(it["gr"])), ks[it["ch"], it["hh"] // 2])
              for tinv, it in zip(tinvs, items)]
        for it, u, w in zip(items, us, ws):
            key = (it["ch"], it["hh"] // 2)
            hh, r0, gcb = it["hh"], it["r0"], it["gcb"]
            u_ref[hh, pl.ds(r0, c), :] = u
            w_ref[hh, pl.ds(r0, c), :] = w.astype(BF16)
            qk_ref[hh, pl.ds(r0, c), :] = jnp.where(tri, qk[key] * it["decay"], 0.0).astype(BF16)
            qd_ref[hh, pl.ds(r0, c), :] = (qs[key] * jnp.exp(gcb)).astype(BF16)
            kd_ref[hh, pl.ds(r0, c), :] = (ks[key] * jnp.exp(gcb[c - 1:c, :] - gcb)).astype(BF16)
        return carry

    lax.fori_loop(0, n_chunks // cpb, prep, 0)

    def scan(n, carry):
        r0 = pl.multiple_of(n * c, c)
        g_last = gc_ref[0, 0, pl.ds(r0 + (c - 1), 1), :]
        loaded = []
        for hh in range(hb):
            loaded.append((u_ref[hh, pl.ds(r0, c), :], w_ref[hh, pl.ds(r0, c), :], qk_ref[hh, pl.ds(r0, c), :],
                           qd_ref[hh, pl.ds(r0, c), :], kd_ref[hh, pl.ds(r0, c), :], v_slice(z_ref, r0, hh),
                           s_ref[0, hh]))
        ws_s = [_mm(w, s) for (u, w, qkm, qd, kd, z, s) in loaded]
        qd_s = [_mm(qd, s) for (u, w, qkm, qd, kd, z, s) in loaded]
        v_news = [ld[0] - ws for ld, ws in zip(loaded, ws_s)]
        qk_v = [_mm(ld[2], vn) for ld, vn in zip(loaded, v_news)]
        kd_v = [_mm_tn(ld[4], vn) for ld, vn in zip(loaded, v_news)]
        results = []
        for hh, ld in enumerate(loaded):
            o = qd_s[hh] + qk_v[hh]
            s_new = ld[6] * jnp.exp(g_last[:, hh:hh + 1]) + kd_v[hh]
            inv = lax.rsqrt(jnp.mean(o * o, axis=-1, keepdims=True) + RMS_EPS)
            results.append((((o * inv) * onw) * _silu(ld[5].astype(F32)), s_new))
        for hh, (og, s_new) in enumerate(results):
            s_ref[0, hh] = s_new
            o_ref[0, pl.ds(r0, c), hh * GDN_DV:(hh + 1) * GDN_DV] = og.astype(o_ref.dtype)
        return carry

    lax.fori_loop(0, n_chunks, scan, 0)


def _gdn_scan(qkv, proj, gcol, bcol, grow, brow, s0, onw):
    b, t, _ = qkv.shape
    hb = GDN_HEADS_PER_STEP
    hg = GDN_V_HEADS // hb
    qw = hb // 2 * GDN_DK
    vw = hb * GDN_DV
    n = t // GDN_CHUNK
    return pl.pallas_call(
        _gdn_scan_body,
        grid=(b, hg),
        in_specs=[pl.BlockSpec((1, t, qw), lambda i, j: (i, 0, j)),
                  pl.BlockSpec((1, t, qw), lambda i, j: (i, 0, GDN_QK_DIM // qw + j)),
                  pl.BlockSpec((1, t, vw), lambda i, j: (i, 0, 2 * GDN_QK_DIM // vw + j)),
                  pl.BlockSpec((1, t, vw), lambda i, j: (i, 0, GDN_CONV_CH // vw + j)),
                  pl.BlockSpec((1, 1, t, hb), lambda i, j: (i, j, 0, 0)),
                  pl.BlockSpec((1, 1, t, hb), lambda i, j: (i, j, 0, 0)),
                  pl.BlockSpec((1, 1, n, hb, GDN_CHUNK), lambda i, j: (i, j, 0, 0, 0)),
                  pl.BlockSpec((1, 1, n, hb, GDN_CHUNK), lambda i, j: (i, j, 0, 0, 0)),
                  pl.BlockSpec((1, hb, GDN_DK, GDN_DV), lambda i, j: (i, j, 0, 0)),
                  pl.BlockSpec((1, GDN_DV), lambda i, j: (0, 0))],
        out_specs=[pl.BlockSpec((1, t, vw), lambda i, j: (i, 0, j)),
                   pl.BlockSpec((1, hb, GDN_DK, GDN_DV), lambda i, j: (i, j, 0, 0))],
        out_shape=[jax.ShapeDtypeStruct((b, t, GDN_V_DIM), BF16),
                   jax.ShapeDtypeStruct((b, GDN_V_HEADS, GDN_DK, GDN_DV), F32)],
        scratch_shapes=[pltpu.VMEM((hb, t, GDN_DV), F32),
                        pltpu.VMEM((hb, t, GDN_DK), BF16),
                        pltpu.VMEM((hb, t, GDN_CHUNK), BF16),
                        pltpu.VMEM((hb, t, GDN_DK), BF16),
                        pltpu.VMEM((hb, t, GDN_DK), BF16)],
        compiler_params=_params("parallel", "parallel"),
        name="gdn_scan",
    )(qkv, qkv, qkv, proj, gcol, bcol, grow, brow, s0, onw.reshape(1, GDN_DV))


def _gdn_layer(x, s0, conv_buf, norm_w, w_in, conv_w, alog_pad, dtb_pad, out_norm_w, w_out):
    b, t, d = x.shape
    xf = x.reshape(b * t, d)
    w_main, w_tail = w_in
    proj = _rms_matmul(xf, norm_w, w_main, 2048, BF16).reshape(b, t, -1)
    tail = _rms_matmul(xf, norm_w, w_tail, 128).reshape(b, t, -1)
    qkv, new_buf = _gdn_pre(proj, conv_buf, conv_w)
    _, gcum, beta = _gdn_gate(tail, alog_pad, dtb_pad, 0)
    gcum = gcum[:, :, :GDN_V_HEADS]
    beta = beta[:, :, GDN_V_HEADS:2 * GDN_V_HEADS]
    tp = -(-t // GDN_CHUNK) * GDN_CHUNK
    if tp != t:
        pad = [(0, 0), (0, tp - t), (0, 0)]
        qkv = jnp.pad(qkv, pad)
        proj_z = jnp.pad(proj, pad)
        gcum = jnp.pad(gcum, pad, mode="edge")
        beta = jnp.pad(beta, pad)
    else:
        proj_z = proj
    hb = GDN_HEADS_PER_STEP
    hg = GDN_V_HEADS // hb
    n = tp // GDN_CHUNK
    gcol = gcum.reshape(b, tp, hg, hb).transpose(0, 2, 1, 3)
    bcol = beta.reshape(b, tp, hg, hb).transpose(0, 2, 1, 3)
    grow = gcum.reshape(b, n, GDN_CHUNK, hg, hb).transpose(0, 3, 1, 4, 2)
    brow = beta.reshape(b, n, GDN_CHUNK, hg, hb).transpose(0, 3, 1, 4, 2)
    o, s_new = _gdn_scan(qkv, proj_z, gcol, bcol, grow, brow, s0, out_norm_w)
    o = o[:, :t].reshape(b * t, GDN_V_DIM)
    y = _matmul_res(o, w_out, xf).reshape(b, t, d)
    return y, s_new, new_buf


def _page_gather_body(pt_ref, *refs, pps):
    cmp_refs, o_ref = refs[:pps], refs[pps]
    sub = cmp_refs[0].shape[3]
    for s in range(pps):
        o_ref[0, :, s * sub:(s + 1) * sub, :] = cmp_refs[s][0, 0]


def _page_gather(cache_cmp, layer, page_table):
    b, n_pages = page_table.shape
    _, _, n_cg, sub, flat = cache_cmp.shape
    pps = 16 if n_pages % 16 == 0 else (8 if n_pages % 8 == 0 else 1)
    n_steps = n_pages // pps

    def page_map(s):
        return lambda i, p, pt: (layer, pt[i, p * pps + s], 0, 0, 0)

    grid_spec = pltpu.PrefetchScalarGridSpec(
        num_scalar_prefetch=1,
        grid=(b, n_steps),
        in_specs=[pl.BlockSpec((1, 1, n_cg, sub, flat), page_map(s)) for s in range(pps)],
        out_specs=pl.BlockSpec((1, n_cg, pps * sub, flat), lambda i, p, pt: (i, 0, p, 0)),
    )
    return pl.pallas_call(
        functools.partial(_page_gather_body, pps=pps),
        grid_spec=grid_spec,
        out_shape=jax.ShapeDtypeStruct((b, n_cg, n_pages * sub, flat), cache_cmp.dtype),
        compiler_params=_params("parallel", "arbitrary"),
        name="page_gather",
    )(page_table, *([cache_cmp] * pps))


def _compress_body(x_ref, w1_ref, pe_ref, w2_ref, o_ref, bias_ref, *, n_sub):
    hd = CMP_HIDDEN

    @pl.when(pl.program_id(1) == 0)
    def _():
        pe = pe_ref[0]
        pe0 = jnp.broadcast_to(pe[0:1], (8, pe.shape[1]))
        pe1 = jnp.broadcast_to(pe[1:2], (8, pe.shape[1]))
        bias_ref[:, :hd] = jnp.dot(pe0.astype(BF16), w1_ref[0, :, :hd], preferred_element_type=F32)
        bias_ref[:, hd:] = jnp.dot(pe1.astype(BF16), w1_ref[0, :, hd:], preferred_element_type=F32)

    acc = jnp.dot(x_ref[0, 0], w1_ref[0], preferred_element_type=F32)
    first = acc[:, :hd] + bias_ref[0:1, :hd]
    second = acc[:, hd:] + bias_ref[0:1, hd:]
    hid = _silu(first + pltpu.roll(second, n_sub - 1, 0))
    o_ref[0, 0] = jnp.dot(hid.astype(BF16), w2_ref[0], preferred_element_type=F32)


def _compress(x, n_sub, w1cat, pe_flat, w2):
    b, _, _, flat = x.shape
    dh = flat // CMP_STRIDE
    return pl.pallas_call(
        functools.partial(_compress_body, n_sub=n_sub),
        grid=(2 * NSA_G, b),
        in_specs=[pl.BlockSpec((1, 1, n_sub, flat), lambda j, i: (i, j, 0, 0)),
                  pl.BlockSpec((1, flat, 2 * CMP_HIDDEN), lambda j, i: (j // NSA_G, 0, 0)),
                  pl.BlockSpec((1, 2, flat), lambda j, i: (j // NSA_G, 0, 0)),
                  pl.BlockSpec((1, CMP_HIDDEN, dh), lambda j, i: (j // NSA_G, 0, 0))],
        out_specs=pl.BlockSpec((1, 1, n_sub, dh), lambda j, i: (i, j, 0, 0)),
        out_shape=jax.ShapeDtypeStruct((b, 2 * NSA_G, n_sub, dh), F32),
        scratch_shapes=[pltpu.VMEM((8, 2 * CMP_HIDDEN), F32)],
        compiler_params=_params("arbitrary", "arbitrary"),
        name="nsa_compress",
    )(x, w1cat, pe_flat, w2)


def _inv_or_zero(l):
    return jnp.where(l > 0.0, 1.0 / jnp.where(l > 0.0, l, 1.0), 0.0)


def _attn_branch(q, slopes, t_pos, k, v, kp, ok_fn, may_be_empty=False):
    r, qb, nk = slopes.shape[0], t_pos.shape[1], k.shape[0]
    dist = t_pos - kp
    ok = ok_fn(dist)
    pen = jnp.where(ok, dist.astype(F32), MASKED_DIST)
    s = _mm_nt(q, k).reshape(r, qb, nk) - slopes * pen
    m = jnp.max(s, axis=-1, keepdims=True)
    p = jnp.exp2(s - m)
    if may_be_empty:
        p = jnp.where(ok, p, 0.0)
    inv = _inv_or_zero(jnp.sum(p, axis=-1, keepdims=True))
    o = _mm(p.reshape(r * qb, nk), v) * inv.reshape(r * qb, 1)
    return o, p, inv


def _pick_blocks(p_sum, t_lane, n_cmp, n_blk, nb_rows):
    nq, nc = p_sum.shape
    nb = -(-n_blk // 8) * 8
    jj = lax.broadcasted_iota(jnp.int32, (nb, nc), 0)
    nn = lax.broadcasted_iota(jnp.int32, (nb, nc), 1)
    overlap_t = ((nn * CMP_STRIDE < (jj + 1) * SEL_BLOCK) & (nn * CMP_STRIDE + CMP_LEN > jj * SEL_BLOCK)
                 & (nn < n_cmp) & (jj < n_blk)).astype(F32)
    imp = lax.dot_general(overlap_t, p_sum, (((1,), (1,)), ((), ())), preferred_element_type=F32,
                          precision=lax.Precision.HIGHEST)
    j = lax.broadcasted_iota(jnp.int32, (nb, nq), 0)
    cur = t_lane // SEL_BLOCK
    imp = jnp.where((j == 0) | (j == cur) | (j == cur - 1), FORCE_SCORE, imp)
    imp = jnp.where(j > cur, -FORCE_SCORE, imp)
    imp = jnp.where(j >= n_blk, NEVER, imp)
    rank = jnp.zeros((nb, nq), jnp.int32)
    for jp in range(n_blk):
        row = imp[jp:jp + 1, :]
        rank = rank + ((row > imp) | ((row == imp) & (j > jp))).astype(jnp.int32)
    picked_t = ((rank < SEL_TOPN) & (j < n_blk)).astype(F32)
    if nb_rows > nb:
        picked_t = jnp.concatenate([picked_t, jnp.zeros((nb_rows - nb, nq), F32)], axis=0)
    return picked_t


def _transpose_01(x_t):
    m = x_t.shape[1]
    eye = (lax.broadcasted_iota(jnp.int32, (m, m), 0) == lax.broadcasted_iota(jnp.int32, (m, m), 1))
    return _mm_nt(eye.astype(BF16), x_t)


def _head_rows(x, n_heads):
    dh = x.shape[1] // n_heads
    return jnp.concatenate([x[:, h * dh:(h + 1) * dh] for h in range(n_heads)], axis=0)


def _nsa_attn_body(q_ref, gate_ref, slope_ref, ck_ref, cv_ref, ks_ref, vs_ref, kw_ref, vw_ref, o_ref, osel_ref,
                   *, nsb, **kw):
    refs = (q_ref, gate_ref, slope_ref, ck_ref, cv_ref, ks_ref, vs_ref, kw_ref, vw_ref, o_ref, osel_ref)
    if nsb == 1:
        _nsa_attn_block(0, pl.program_id(2), *refs, **kw)
    else:
        def one(sub, carry):
            _nsa_attn_block(sub, pl.program_id(2) * nsb + sub, *refs, **kw)
            return carry

        lax.fori_loop(0, nsb, one, 0)


def _nsa_attn_block(sub, i, q_ref, gate_ref, slope_ref, ck_ref, cv_ref, ks_ref, vs_ref, kw_ref, vw_ref, o_ref,
                    osel_ref, *, qb, p_len, n_cmp, n_blk, w0, sel_classes, kw_len):
    r = NSA_R
    rows = r * qb
    rq = sub * qb if isinstance(sub, int) else pl.multiple_of(sub * qb, qb)
    q = _head_rows(q_ref[0, pl.ds(rq, qb), :].astype(F32), r)
    q = (q * (NSA_DH ** -0.5 * LOG2E)).astype(BF16)
    slopes = slope_ref[0] * LOG2E
    q0 = p_len + i * qb
    t_pos = q0 + lax.broadcasted_iota(jnp.int32, (1, qb, 1), 1)
    branch = functools.partial(_attn_branch, q, slopes, t_pos)

    tw = kw_ref.shape[2]
    if tw == kw_len:
        start = 0
    else:
        start = pl.multiple_of(jnp.clip(q0 + qb - w0 - kw_len, 0, tw - kw_len), SEL_BLOCK)
    kp_win = w0 + start + lax.broadcasted_iota(jnp.int32, (1, 1, kw_len), 2)
    o_win, _, _ = branch(kw_ref[0, 0, pl.ds(start, kw_len), :], vw_ref[0, 0, pl.ds(start, kw_len), :], kp_win,
                         lambda dist: (dist >= 0) & (dist < WINDOW) & (kp_win >= 0))

    nc = ck_ref.shape[2]
    n_idx = lax.broadcasted_iota(jnp.int32, (1, 1, nc), 2)
    o_cmp, p_cmp, inv_cmp = branch(ck_ref[0, 0], cv_ref[0, 0], n_idx * CMP_STRIDE + (CMP_LEN - 1),
                                   lambda dist: (dist >= 0) & (n_idx < n_cmp), may_be_empty=True)
    p_sum = jnp.sum(p_cmp * inv_cmp, axis=0)
    nb8 = -(-n_blk // 16) * 16
    picked_t = _pick_blocks(p_sum, q0 + lax.broadcasted_iota(jnp.int32, (1, qb), 1), n_cmp, n_blk, nb8)
    picked = _transpose_01(picked_t)

    def sel_branch(nk):
        blk_of_key = lax.broadcasted_iota(jnp.int32, (nb8, nk), 1) // SEL_BLOCK
        expand = (blk_of_key == lax.broadcasted_iota(jnp.int32, (nb8, nk), 0)).astype(BF16)
        key_picked = (_mm(picked, expand) > 0.5).reshape(1, qb, nk)
        o, _, _ = branch(ks_ref[0, 0, 0:nk, :], vs_ref[0, 0, 0:nk, :],
                         lax.broadcasted_iota(jnp.int32, (1, 1, nk), 2),
                         lambda dist: key_picked & (dist >= 0))
        osel_ref[...] = o

    if len(sel_classes) == 1:
        sel_branch(sel_classes[0])
    else:
        need = q0 + qb
        prev = 0
        for nk in sel_classes:
            pl.when((need > prev) & (need <= nk))(functools.partial(sel_branch, nk))
            prev = nk

    gates = _sigmoid(gate_ref[0, 0, :, pl.ds(rq, qb), :]).reshape(rows, 3)
    o = gates[:, 0:1] * o_cmp + gates[:, 1:2] * osel_ref[...] + gates[:, 2:3] * o_win
    for h in range(r):
        o_ref[0, pl.ds(rq, qb), h * NSA_DH:(h + 1) * NSA_DH] = o[h * qb:(h + 1) * qb].astype(o_ref.dtype)


def _nsa_attn(q, gates_t, slopes, cmp_kv, sel_arr, sel_k0, sel_v0, win_arr, win_k0, win_v0,
              *, p_len, n_cmp, n_blk, w0):
    b, tq, _ = q.shape
    g, r, dh = NSA_G, NSA_R, NSA_DH
    qb = NSA_Q_BLOCK if tq % NSA_Q_BLOCK == 0 else tq
    nc = cmp_kv.shape[2]
    tk = sel_arr.shape[2]
    tw = win_arr.shape[2]
    if tq == qb or tk % NSA_SEL_CLASS != 0:
        sel_classes = (tk,)
    else:
        sel_classes = tuple(range(NSA_SEL_CLASS, tk + 1, NSA_SEL_CLASS))
    kw_len = min(tw, -(-(WINDOW - 1 + qb) // KEY_CHUNK) * KEY_CHUNK)
    nsb = NSA_Q_BLOCKS_PER_STEP if (tq // qb) % NSA_Q_BLOCKS_PER_STEP == 0 else 1
    body = functools.partial(_nsa_attn_body, nsb=nsb, qb=qb, p_len=p_len, n_cmp=n_cmp, n_blk=n_blk, w0=w0,
                             sel_classes=sel_classes, kw_len=kw_len)
    qs = nsb * qb
    return pl.pallas_call(
        body,
        grid=(b, g, tq // qs),
        in_specs=[pl.BlockSpec((1, qs, r * dh), lambda bi, gi, i: (bi, i, gi)),
                  pl.BlockSpec((1, 1, r, qs, 3), lambda bi, gi, i: (bi, gi, 0, i, 0)),
                  pl.BlockSpec((1, r, 1, 1), lambda bi, gi, i: (gi, 0, 0, 0)),
                  pl.BlockSpec((1, 1, nc, dh), lambda bi, gi, i: (bi, gi, 0, 0)),
                  pl.BlockSpec((1, 1, nc, dh), lambda bi, gi, i: (bi, NSA_G + gi, 0, 0)),
                  pl.BlockSpec((1, 1, tk, dh), lambda bi, gi, i: (bi, sel_k0 + gi, 0, 0)),
                  pl.BlockSpec((1, 1, tk, dh), lambda bi, gi, i: (bi, sel_v0 + gi, 0, 0)),
                  pl.BlockSpec((1, 1, tw, dh), lambda bi, gi, i: (bi, win_k0 + gi, 0, 0)),
                  pl.BlockSpec((1, 1, tw, dh), lambda bi, gi, i: (bi, win_v0 + gi, 0, 0))],
        out_specs=pl.BlockSpec((1, qs, r * dh), lambda bi, gi, i: (bi, i, gi)),
        out_shape=jax.ShapeDtypeStruct((b, tq, g * r * dh), BF16),
        scratch_shapes=[pltpu.VMEM((r * qb, dh), F32)],
        compiler_params=_params("parallel", "parallel", "arbitrary"),
        name="nsa_attn",
    )(q, gates_t, slopes, cmp_kv, cmp_kv, sel_arr, sel_arr, win_arr, win_arr)


def _nsa_paged_body(pt_ref, *refs, tq, p_len, n_cmp, n_blk, w0, pps, n_steps):
    k_refs, v_refs = refs[:pps], refs[pps:2 * pps]
    (q_ref, gate_ref, slope_ref, cmp_ref, new_ref, win_ref, o_ref,
     qbd_ref, pen_ref, m_ref, l_ref, acc_ref, ocmp_ref) = refs[2 * pps:]
    p = pl.program_id(1)
    g_n, r = NSA_G, NSA_R
    dh = NSA_DH
    rg = r * tq
    rows = g_n * rg
    page = k_refs[0].shape[5]
    chunk = pps * page
    n_keys = pen_ref.shape[1]
    t_pos = p_len + lax.broadcasted_iota(jnp.int32, (1, tq, 1), 1)

    @pl.when(p == 0)
    def _():
        qbd_ref[...] = jnp.zeros(qbd_ref.shape, qbd_ref.dtype)
        nc = cmp_ref.shape[2]
        n_idx = lax.broadcasted_iota(jnp.int32, (1, 1, nc), 2)
        p_sums = []
        for g in range(g_n):
            q = _head_rows(q_ref[0, :, g * r * dh:(g + 1) * r * dh].astype(F32), r)
            q = (q * (dh ** -0.5 * LOG2E)).astype(BF16)
            qbd_ref[g * rg:(g + 1) * rg, g * dh:(g + 1) * dh] = q
            o_cmp, p_cmp, inv_cmp = _attn_branch(
                q, slope_ref[g] * LOG2E, t_pos, cmp_ref[0, g], cmp_ref[0, g_n + g],
                n_idx * CMP_STRIDE + (CMP_LEN - 1), lambda dist: (dist >= 0) & (n_idx < n_cmp), may_be_empty=True)
            ocmp_ref[g * rg:(g + 1) * rg, :] = o_cmp
            p_sums.append(jnp.sum(p_cmp * inv_cmp, axis=0))
        nbp = -(-n_blk // 128) * 128
        nq = g_n * tq
        t_lane = p_len + lax.broadcasted_iota(jnp.int32, (1, nq), 1) % tq
        picked_t = _pick_blocks(jnp.concatenate(p_sums, axis=0), t_lane, n_cmp, n_blk, nbp)
        picked = _transpose_01(picked_t.astype(BF16))
        picked_rows = jnp.concatenate([picked[g * tq:(g + 1) * tq] for g in range(g_n) for _ in range(r)], axis=0)
        slope_rows = jnp.broadcast_to(slope_ref[...] * LOG2E, (g_n, r, tq, 1)).reshape(rows, 1)
        t_rows = p_len + lax.broadcasted_iota(jnp.int32, (rows, 1), 0) % tq

        def fill(k0, width):
            kp = k0 + lax.broadcasted_iota(jnp.int32, (1, width), 1)
            blk = lax.broadcasted_iota(jnp.int32, (nbp, width), 0)
            expand = ((k0 + lax.broadcasted_iota(jnp.int32, (nbp, width), 1)) // SEL_BLOCK == blk).astype(BF16)
            key_picked = _mm(picked_rows, expand) > 0.5
            dist = t_rows - kp
            return slope_rows * jnp.where(key_picked & (dist >= 0), dist.astype(F32), MASKED_DIST)

        def fill_chunk(c, carry):
            k0 = pl.multiple_of(c * chunk, chunk)
            pen_ref[:, pl.ds(k0, chunk)] = fill(k0, chunk)
            return carry

        lax.fori_loop(0, n_steps, fill_chunk, 0)
        pen_ref[:, n_steps * chunk:n_keys] = fill(n_steps * chunk, n_keys - n_steps * chunk)
        m_ref[...] = jnp.full(m_ref.shape, NEVER, F32)
        l_ref[...] = jnp.zeros(l_ref.shape, F32)
        acc_ref[...] = jnp.zeros(acc_ref.shape, F32)

    def online_update(kts, vts, k0):
        width = len(kts) * page
        qbd = qbd_ref[...]
        s = jnp.concatenate([jnp.dot(qbd, kt.astype(BF16), preferred_element_type=F32) for kt in kts], axis=1)
        s = s - pen_ref[:, pl.ds(k0, width)]
        m_prev = m_ref[...]
        m_new = jnp.maximum(m_prev, jnp.max(s, axis=-1, keepdims=True))
        alpha = jnp.exp2(m_prev - m_new)
        prob = jnp.exp2(s - m_new)
        l_ref[...] = alpha * l_ref[...] + jnp.sum(prob, axis=-1, keepdims=True)
        pv = [_mm_nt(prob[:, i * page:(i + 1) * page], vt) for i, vt in enumerate(vts)]
        acc_ref[...] = alpha * acc_ref[...] + sum(pv[1:], pv[0])
        m_ref[...] = m_new

    @pl.when(p < n_steps)
    def _():
        online_update([ref[0, 0, 0].reshape(g_n * dh, page) for ref in k_refs],
                      [ref[0, 0, 0].reshape(g_n * dh, page) for ref in v_refs], pl.multiple_of(p * chunk, chunk))

    @pl.when(p == n_steps)
    def _():
        online_update([new_ref[0, 0].reshape(g_n * dh, page)], [new_ref[0, 1].reshape(g_n * dh, page)],
                      n_steps * chunk)
        acc = acc_ref[...] * _inv_or_zero(l_ref[...])
        tw = win_ref.shape[2]
        kp_win = w0 + lax.broadcasted_iota(jnp.int32, (1, 1, tw), 2)
        for g in range(g_n):
            q = qbd_ref[g * rg:(g + 1) * rg, g * dh:(g + 1) * dh]
            o_win, _, _ = _attn_branch(q, slope_ref[g] * LOG2E, t_pos, win_ref[0, g], win_ref[0, g_n + g], kp_win,
                                       lambda dist: (dist >= 0) & (dist < WINDOW) & (kp_win >= 0))
            gates = _sigmoid(gate_ref[0, g]).reshape(rg, 3)
            o = (gates[:, 0:1] * ocmp_ref[g * rg:(g + 1) * rg, :]
                 + gates[:, 1:2] * acc[g * rg:(g + 1) * rg, g * dh:(g + 1) * dh] + gates[:, 2:3] * o_win)
            for h in range(r):
                o_ref[0, :, (g * r + h) * dh:(g * r + h + 1) * dh] = o[h * tq:(h + 1) * tq].astype(o_ref.dtype)


def _nsa_paged(q, gates_t, slopes, cmp_kv, cache_nt, layer, page_table, new_t, win_arr,
               *, p_len, n_cmp, n_blk, w0):
    b, tq, _ = q.shape
    g, r, dh = NSA_G, NSA_R, NSA_DH
    n_pages = page_table.shape[1]
    page = cache_nt.shape[5]
    pps = 8 if n_pages % 8 == 0 else (4 if n_pages % 4 == 0 else 1)
    n_steps = n_pages // pps
    nc = cmp_kv.shape[2]
    tw = win_arr.shape[2]
    rows = g * r * tq
    n_keys = (n_pages + 1) * page

    def page_map(c, s):
        return lambda i, p, pt: (layer, pt[i, jnp.minimum(p, n_steps - 1) * pps + s], c, 0, 0, 0)

    const = lambda i, p, pt: (i, 0, 0, 0, 0)
    grid_spec = pltpu.PrefetchScalarGridSpec(
        num_scalar_prefetch=1,
        grid=(b, n_steps + 1),
        in_specs=[pl.BlockSpec((1, 1, 1, g, dh, page), page_map(2, s)) for s in range(pps)]
        + [pl.BlockSpec((1, 1, 1, g, dh, page), page_map(3, s)) for s in range(pps)]
        + [pl.BlockSpec((1, tq, g * r * dh), lambda i, p, pt: (i, 0, 0)),
           pl.BlockSpec((1, g, r, tq, 3), const),
           pl.BlockSpec((g, r, 1, 1), lambda i, p, pt: (0, 0, 0, 0)),
           pl.BlockSpec((1, 2 * g, nc, dh), lambda i, p, pt: (i, 0, 0, 0)),
           pl.BlockSpec((1, 2, g, dh, page), const),
           pl.BlockSpec((1, 2 * g, tw, dh), lambda i, p, pt: (i, 0, 0, 0))],
        out_specs=pl.BlockSpec((1, tq, g * r * dh), lambda i, p, pt: (i, 0, 0)),
        scratch_shapes=[pltpu.VMEM((rows, g * dh), BF16),
                        pltpu.VMEM((rows, n_keys), F32),
                        pltpu.VMEM((rows, 1), F32),
                        pltpu.VMEM((rows, 1), F32),
                        pltpu.VMEM((rows, g * dh), F32),
                        pltpu.VMEM((rows, dh), F32)],
    )
    body = functools.partial(_nsa_paged_body, tq=tq, p_len=p_len, n_cmp=n_cmp, n_blk=n_blk, w0=w0, pps=pps,
                             n_steps=n_steps)
    return pl.pallas_call(
        body,
        grid_spec=grid_spec,
        out_shape=jax.ShapeDtypeStruct((b, tq, g * r * dh), BF16),
        compiler_params=_params("parallel", "arbitrary"),
        name="nsa_paged",
    )(page_table, *([cache_nt] * (2 * pps)), q, gates_t, slopes, cmp_kv, new_t, win_arr)


def _nsa_layer(x, cache_hm, layer, page_table, win_buf, norm_w, w_in, w1cat, pe_flat, w2, w_out, slopes):
    b, t, d = x.shape
    g, r, dh = NSA_G, NSA_R, NSA_DH
    xf = x.reshape(b * t, d)
    w_q, w_kv = w_in
    q = _rms_matmul(xf, norm_w, w_q, NSA_Q_DIM, BF16).reshape(b, t, NSA_Q_DIM)
    proj = _rms_matmul(xf, norm_w, w_kv, w_kv.shape[1]).reshape(b, t, -1)
    kv = proj[..., :6 * NSA_KV_DIM]
    gates_t = proj[..., 6 * NSA_KV_DIM:6 * NSA_KV_DIM + 3 * NSA_HEADS]
    gates_t = gates_t.reshape(b, t, g, r, 3).transpose(0, 2, 3, 1, 4)
    kv6 = kv.reshape(b, t, 6, g, dh)
    new_rows = kv6[:, :, :4]
    if cache_hm is None:
        p_len = 0
        tk = t
        assert t % CMP_STRIDE == 0
        cmp_x = kv6[:, :, :2].astype(BF16).transpose(0, 2, 3, 1, 4).reshape(b, 2 * g, t // CMP_STRIDE,
                                                                           CMP_STRIDE * dh)
        kv_hm = kv6[:, :, 2:].astype(BF16).transpose(0, 2, 3, 1, 4).reshape(b, 4 * g, t, dh)
        new_win = kv6[:, t - min(WINDOW, t):, 4:]
    else:
        cache_cmp, cache_nt = cache_hm
        n_pages = page_table.shape[1]
        page = cache_nt.shape[5]
        p_len = n_pages * page
        tk = p_len + t
        assert p_len % CMP_STRIDE == 0 and t < CMP_STRIDE and t <= page
        cmp_x = _page_gather(cache_cmp, layer, page_table)
        new_t = jnp.pad(kv6[:, :, 2:4].transpose(0, 2, 3, 4, 1), [(0, 0)] * 4 + [(0, page - t)])
        win_all = jnp.concatenate([win_buf, kv6[:, :, 4:]], axis=1)
        wl = win_all.shape[1]
        wlp = -(-wl // KEY_CHUNK) * KEY_CHUNK
        win_arr = jnp.pad(win_all.astype(BF16), [(0, 0), (0, wlp - wl), (0, 0), (0, 0), (0, 0)])
        win_arr = win_arr.transpose(0, 2, 3, 1, 4).reshape(b, 2 * g, wlp, dh)
        new_win = win_all[:, wl - min(WINDOW, tk):]
    n_sub = tk // CMP_STRIDE
    n_cmp = n_sub - 1
    n_blk = -(-tk // SEL_BLOCK)
    cmp_kv = _compress(cmp_x, n_sub, w1cat, pe_flat, w2)
    if cache_hm is None:
        o = _nsa_attn(q, gates_t, slopes, cmp_kv, kv_hm, 0, g, kv_hm, 2 * g, 3 * g,
                      p_len=0, n_cmp=n_cmp, n_blk=n_blk, w0=0)
    else:
        o = _nsa_paged(q, gates_t, slopes, cmp_kv, cache_nt, layer, page_table, new_t, win_arr,
                       p_len=p_len, n_cmp=n_cmp, n_blk=n_blk, w0=p_len - win_buf.shape[1])
    y = _matmul_res(o.reshape(b * t, NSA_Q_DIM), w_out, xf).reshape(b, t, d)
    return y, new_rows, new_win


def _ffn_layer(x, buf, norm_w, w_up, conv_w, w_down):
    b, t, d = x.shape
    xf = x.reshape(b * t, d)
    h = _rms_matmul(xf, norm_w, w_up, D_FF, BF16).reshape(b, t, -1)
    act, new_buf = _ffn_act(h, buf, conv_w)
    y = _matmul_res(act.reshape(b * t, D_FF), w_down, xf).reshape(b, t, d)
    return y, new_buf


def _pad_cols(w, n):
    return jnp.pad(w, [(0, 0)] * (w.ndim - 1) + [(0, n - w.shape[-1])])


def _trunk(x, cache_hm, page_table, nsa_win, gdn_state, gdn_conv, ffn_conv, wts):
    depth = wts["ffn_w_up"].shape[0]
    rows_l, win_l, s_l, gconv_l, fconv_l = [], [], [], [], []
    for i in range(depth):
        j = i // 2
        if i % 2 == 0:
            x, s_new, cb = _gdn_layer(x, gdn_state[j], gdn_conv[j], wts["gdn_norm"][j],
                                      (wts["gdn_w_main"][j], wts["gdn_w_tail"][j]),
                                      wts["gdn_conv_w"][j], wts["gdn_a_log"][j], wts["gdn_dt_bias"][j],
                                      wts["gdn_out_norm"][j], wts["gdn_w_out"][j])
            s_l.append(s_new)
            gconv_l.append(cb)
        else:
            x, rows, wb = _nsa_layer(x, cache_hm, j, page_table, None if nsa_win is None else nsa_win[j],
                                     wts["nsa_norm"][j], (wts["nsa_w_q"][j], wts["nsa_w_kv"][j]),
                                     wts["nsa_w1cat"][j],
                                     wts["nsa_pe_flat"][j], wts["nsa_w2"][j], wts["nsa_w_out"][j],
                                     wts["slopes"])
            rows_l.append(rows)
            win_l.append(wb)
        x, fb = _ffn_layer(x, ffn_conv[i], wts["ffn_norm"][i], wts["ffn_w_up"][i], wts["ffn_conv_w"][i],
                           wts["ffn_w_down"][i])
        fconv_l.append(fb)
    b, t, d = x.shape
    y = _rms(x.reshape(b * t, d), wts["final_norm"]).reshape(b, t, d)
    return (y, jnp.stack(rows_l), jnp.stack(win_l), jnp.stack(s_l), jnp.stack(gconv_l), jnp.stack(fconv_l))


def kernel(x_prompt, x_sample, cache_nsa_kv, cache_nsa_win, state_gdn_s, state_gdn_conv, state_ffn_conv,
           page_table, gdn_norm, gdn_w_in, gdn_conv_w, gdn_a_log, gdn_dt_bias, gdn_out_norm, gdn_w_out,
           nsa_norm, nsa_w_in, nsa_cmp_pe, nsa_cmp_w1, nsa_cmp_w2, nsa_w_out,
           ffn_norm, ffn_w_up, ffn_conv_w, ffn_w_down, final_norm):
    n_gdn = gdn_w_in.shape[0]
    n_nsa = nsa_w_in.shape[0]
    depth = ffn_w_up.shape[0]
    bp = x_prompt.shape[0]
    gdn_main = GDN_CONV_CH + GDN_V_DIM
    nsa_cols = -(-(nsa_w_in.shape[2] - NSA_Q_DIM) // 896) * 896
    w1 = nsa_cmp_w1.reshape(n_nsa, 2, 2, CMP_STRIDE * NSA_DH, CMP_HIDDEN)
    w1cat = jnp.concatenate([w1[:, :, 0], w1[:, :, 1]], axis=-1).astype(BF16)
    slopes = 2.0 ** (-8.0 * jnp.arange(1, NSA_HEADS + 1, dtype=F32) / NSA_HEADS)
    wts = {
        "gdn_norm": gdn_norm,
        "gdn_w_main": gdn_w_in[:, :, :gdn_main].astype(BF16),
        "gdn_w_tail": _pad_cols(gdn_w_in[:, :, gdn_main:], 128).astype(BF16),
        "gdn_conv_w": gdn_conv_w,
        "gdn_a_log": _pad_cols(gdn_a_log, 128).reshape(n_gdn, 1, 128),
        "gdn_dt_bias": _pad_cols(gdn_dt_bias, 128).reshape(n_gdn, 1, 128),
        "gdn_out_norm": gdn_out_norm,
        "gdn_w_out": gdn_w_out.astype(BF16),
        "nsa_norm": nsa_norm,
        "nsa_w_q": nsa_w_in[:, :, :NSA_Q_DIM].astype(BF16),
        "nsa_w_kv": _pad_cols(nsa_w_in[:, :, NSA_Q_DIM:], nsa_cols).astype(BF16),
        "nsa_w1cat": w1cat,
        "nsa_pe_flat": nsa_cmp_pe.reshape(n_nsa, 2, 2, CMP_STRIDE * NSA_DH),
        "nsa_w2": nsa_cmp_w2.astype(BF16),
        "nsa_w_out": nsa_w_out.astype(BF16),
        "slopes": slopes.reshape(NSA_G, NSA_R, 1, 1),
        "ffn_norm": ffn_norm,
        "ffn_w_up": ffn_w_up.astype(BF16),
        "ffn_conv_w": ffn_conv_w,
        "ffn_w_down": ffn_w_down.astype(BF16),
        "final_norm": final_norm,
    }
    n_l, pool, page = cache_nsa_kv.shape[:3]
    cache_cmp = cache_nsa_kv[:, :, :, :2].astype(BF16).transpose(0, 1, 3, 4, 2, 5)
    cache_hm = (cache_cmp.reshape(n_l, pool, 2 * NSA_G, page // CMP_STRIDE, CMP_STRIDE * NSA_DH),
                cache_nsa_kv.transpose(0, 1, 3, 4, 5, 2))

    zeros = functools.partial(jnp.zeros, dtype=F32)
    prompt = _trunk(x_prompt, None, None, None,
                    zeros((n_gdn, bp, GDN_V_HEADS, GDN_DK, GDN_DV)),
                    zeros((n_gdn, bp, GDN_CONV_W - 1, GDN_CONV_CH)),
                    zeros((depth, bp, FFN_CONV_W - 1, D_FF)), wts)
    sample = _trunk(x_sample, cache_hm, page_table, cache_nsa_win, state_gdn_s, state_gdn_conv, state_ffn_conv, wts)
    out = []
    for p, s in zip(prompt, sample):
        out.extend([p, s])
    return tuple(out)
```

```python
import functools

import jax
import jax.numpy as jnp
from jax import lax
from jax.experimental import pallas as pl
from jax.experimental.pallas import tpu as pltpu

F32 = jnp.float32
BF16 = jnp.bfloat16

RMS_EPS = 1e-6
L2_EPS = 1e-6
NEG_INF = -1e30
FORCE_SCORE = 1e9
NEVER = -3e38
MASKED_DIST = 1e30
LOG2E = 1.4426950408889634

GDN_QK_HEADS = 8
GDN_V_HEADS = 16
GDN_DK = 128
GDN_DV = 128
GDN_QK_DIM = GDN_QK_HEADS * GDN_DK
GDN_V_DIM = GDN_V_HEADS * GDN_DV
GDN_CONV_CH = 2 * GDN_QK_DIM + GDN_V_DIM
GDN_CONV_W = 4
GDN_CHUNK = 64
GDN_HEADS_PER_STEP = 4
GDN_PREP_CHUNKS = 8
NSA_HEADS = 16
NSA_G = 4
NSA_R = 4
NSA_DH = 64
NSA_Q_DIM = NSA_HEADS * NSA_DH
NSA_KV_DIM = NSA_G * NSA_DH
CMP_STRIDE = 16
CMP_LEN = 32
CMP_HIDDEN = 256
SEL_BLOCK = 64
SEL_TOPN = 8
WINDOW = 512
NSA_Q_BLOCK = 64
NSA_Q_BLOCKS_PER_STEP = 8
KEY_CHUNK = 128
NSA_SEL_CLASS = 512
D_FF = 2816
FFN_CONV_W = 3
FFN_TC = 256
SEQ_ROWS_PER_STEP = 2048

VMEM_LIMIT = 52 * 1024 * 1024


def _params(*sem):
    return pltpu.CompilerParams(dimension_semantics=sem, vmem_limit_bytes=VMEM_LIMIT)


def _mm(a, b):
    return jnp.dot(a.astype(BF16), b.astype(BF16), preferred_element_type=F32)


def _mm_nt(a, b):
    return lax.dot_general(a.astype(BF16), b.astype(BF16), (((1,), (1,)), ((), ())),
                           preferred_element_type=F32)


def _mm_tn(a, b):
    return lax.dot_general(a.astype(BF16), b.astype(BF16), (((0,), (0,)), ((), ())),
                           preferred_element_type=F32)


def _sigmoid(x):
    return 1.0 / (1.0 + jnp.exp(-x))


def _silu(x):
    return x * _sigmoid(x)


def _rms_matmul_body(x_ref, nw_ref, w_ref, o_ref, xn_ref):
    @pl.when(pl.program_id(1) == 0)
    def _():
        x = x_ref[...]
        inv = lax.rsqrt(jnp.mean(x * x, axis=-1, keepdims=True) + RMS_EPS)
        xn_ref[...] = ((x * inv) * nw_ref[...]).astype(BF16)

    o_ref[...] = jnp.dot(xn_ref[...], w_ref[...], preferred_element_type=F32).astype(o_ref.dtype)


def _rms_matmul(x, nw, w, tn, out_dtype=F32):
    m, k = x.shape
    n = w.shape[1]
    tm = min(m, 1024)
    return pl.pallas_call(
        _rms_matmul_body,
        grid=(m // tm, n // tn),
        in_specs=[pl.BlockSpec((tm, k), lambda i, j: (i, 0)),
                  pl.BlockSpec((1, k), lambda i, j: (0, 0)),
                  pl.BlockSpec((k, tn), lambda i, j: (0, j))],
        out_specs=pl.BlockSpec((tm, tn), lambda i, j: (i, j)),
        out_shape=jax.ShapeDtypeStruct((m, n), out_dtype),
        scratch_shapes=[pltpu.VMEM((tm, k), BF16)],
        compiler_params=_params("parallel", "arbitrary"),
        name="rms_matmul",
    )(x, nw.reshape(1, k), w)


def _matmul_res_body(a_ref, w_ref, r_ref, o_ref):
    o_ref[...] = r_ref[...] + jnp.dot(a_ref[...].astype(BF16), w_ref[...], preferred_element_type=F32)


def _matmul_res(a, w, res):
    m, k = a.shape
    n = w.shape[1]
    tm = min(m, 1024)
    return pl.pallas_call(
        _matmul_res_body,
        grid=(m // tm,),
        in_specs=[pl.BlockSpec((tm, k), lambda i: (i, 0)),
                  pl.BlockSpec((k, n), lambda i: (0, 0)),
                  pl.BlockSpec((tm, n), lambda i: (i, 0))],
        out_specs=pl.BlockSpec((tm, n), lambda i: (i, 0)),
        out_shape=jax.ShapeDtypeStruct((m, n), F32),
        compiler_params=_params("parallel"),
        name="matmul_res",
    )(a, w, res)


def _rms_body(x_ref, nw_ref, o_ref):
    x = x_ref[...]
    inv = lax.rsqrt(jnp.mean(x * x, axis=-1, keepdims=True) + RMS_EPS)
    o_ref[...] = (x * inv) * nw_ref[...]


def _rms(x, nw):
    m, k = x.shape
    tm = min(m, 1024)
    return pl.pallas_call(
        _rms_body,
        grid=(m // tm,),
        in_specs=[pl.BlockSpec((tm, k), lambda i: (i, 0)), pl.BlockSpec((1, k), lambda i: (0, 0))],
        out_specs=pl.BlockSpec((tm, k), lambda i: (i, 0)),
        out_shape=jax.ShapeDtypeStruct((m, k), F32),
        compiler_params=_params("parallel"),
        name="final_rms",
    )(x, nw.reshape(1, k))


def _shifted(x, prev_rows, shift, row):
    nb = prev_rows.shape[1]
    y = pltpu.roll(x, shift, 1)
    for r in range(shift):
        y = jnp.where(row == r, prev_rows[:, nb - shift + r:nb - shift + r + 1], y)
    return y


def _batch_block(b, t):
    bb = max(1, min(b, SEQ_ROWS_PER_STEP // t))
    while b % bb:
        bb -= 1
    return bb


def _ffn_act_body(a_ref, g_ref, buf_ref, cw_ref, act_ref, nb_ref):
    a = a_ref[...].astype(F32)
    t = a.shape[1]
    buf = buf_ref[...]
    w = cw_ref[...]
    row = lax.broadcasted_iota(jnp.int32, a.shape, 1)
    y = _shifted(a, buf, 2, row) * w[0:1] + _shifted(a, buf, 1, row) * w[1:2] + a * w[2:3]
    act_ref[...] = (_silu(y) * g_ref[...].astype(F32)).astype(act_ref.dtype)
    nb_ref[...] = a[:, t - (FFN_CONV_W - 1):t, :]


def _ffn_act(h, buf, cw):
    b, t, _ = h.shape
    nj = D_FF // FFN_TC
    bb = _batch_block(b, t)
    return pl.pallas_call(
        _ffn_act_body,
        grid=(b // bb, nj),
        in_specs=[pl.BlockSpec((bb, t, FFN_TC), lambda i, j: (i, 0, j)),
                  pl.BlockSpec((bb, t, FFN_TC), lambda i, j: (i, 0, j + nj)),
                  pl.BlockSpec((bb, FFN_CONV_W - 1, FFN_TC), lambda i, j: (i, 0, j)),
                  pl.BlockSpec((FFN_CONV_W, FFN_TC), lambda i, j: (0, j))],
        out_specs=[pl.BlockSpec((bb, t, FFN_TC), lambda i, j: (i, 0, j)),
                   pl.BlockSpec((bb, FFN_CONV_W - 1, FFN_TC), lambda i, j: (i, 0, j))],
        out_shape=[jax.ShapeDtypeStruct((b, t, D_FF), BF16),
                   jax.ShapeDtypeStruct((b, FFN_CONV_W - 1, D_FF), F32)],
        compiler_params=_params("parallel", "parallel"),
        name="ffn_act",
    )(h, h, buf, cw)


GDN_PRE_TC = 512


def _gdn_pre_body(x_ref, buf_ref, cw_ref, o_ref, nb_ref):
    j = pl.program_id(1)
    x = x_ref[...].astype(F32)
    t = x.shape[1]
    buf = buf_ref[...]
    w = cw_ref[...]
    row = lax.broadcasted_iota(jnp.int32, x.shape, 1)
    y = (_shifted(x, buf, 3, row) * w[0:1] + _shifted(x, buf, 2, row) * w[1:2]
         + _shifted(x, buf, 1, row) * w[2:3] + x * w[3:4])
    y = _silu(y)
    is_q = j < GDN_QK_DIM // GDN_PRE_TC
    is_v = j >= 2 * GDN_QK_DIM // GDN_PRE_TC
    qscale = jnp.where(is_q, GDN_DK ** -0.5, 1.0).astype(F32)
    for h in range(GDN_PRE_TC // GDN_DK):
        yh = y[:, :, h * GDN_DK:(h + 1) * GDN_DK]
        inv = lax.rsqrt(jnp.sum(yh * yh, axis=-1, keepdims=True) + L2_EPS)
        o_ref[:, :, h * GDN_DK:(h + 1) * GDN_DK] = jnp.where(is_v, yh, (yh * inv) * qscale).astype(o_ref.dtype)
    nb_ref[...] = x[:, t - (GDN_CONV_W - 1):t, :]


def _gdn_pre(proj, buf, cw):
    b, t, _ = proj.shape
    nj = GDN_CONV_CH // GDN_PRE_TC
    bb = _batch_block(b, t)
    return pl.pallas_call(
        _gdn_pre_body,
        grid=(b // bb, nj),
        in_specs=[pl.BlockSpec((bb, t, GDN_PRE_TC), lambda i, j: (i, 0, j)),
                  pl.BlockSpec((bb, GDN_CONV_W - 1, GDN_PRE_TC), lambda i, j: (i, 0, j)),
                  pl.BlockSpec((GDN_CONV_W, GDN_PRE_TC), lambda i, j: (0, j))],
        out_specs=[pl.BlockSpec((bb, t, GDN_PRE_TC), lambda i, j: (i, 0, j)),
                   pl.BlockSpec((bb, GDN_CONV_W - 1, GDN_PRE_TC), lambda i, j: (i, 0, j))],
        out_shape=[jax.ShapeDtypeStruct((b, t, GDN_CONV_CH), BF16),
                   jax.ShapeDtypeStruct((b, GDN_CONV_W - 1, GDN_CONV_CH), F32)],
        compiler_params=_params("parallel", "parallel"),
        name="gdn_pre",
    )(proj, buf, cw)


def _gdn_gate_body(x_ref, alog_ref, dtb_ref, g_ref, gcum_ref, beta_ref, *, chunk):
    x = x_ref[...]
    z = x + dtb_ref[...]
    softplus = jnp.maximum(z, 0.0) + jnp.log(1.0 + jnp.exp(-jnp.abs(z)))
    g = -jnp.exp(alog_ref[...]) * softplus
    g_ref[...] = g
    row = lax.broadcasted_iota(jnp.int32, x.shape, 1) % chunk
    acc = g
    s = 1
    while s < chunk:
        acc = acc + jnp.where(row >= s, pltpu.roll(acc, s, 1), 0.0)
        s *= 2
    gcum_ref[...] = acc
    beta_ref[...] = _sigmoid(x)


def _gdn_gate(proj, alog_pad, dtb_pad, lane_block):
    b, t, _ = proj.shape
    chunk = min(GDN_CHUNK, t)
    bb = _batch_block(b, t)
    spec = pl.BlockSpec((bb, t, 128), lambda i: (i, 0, 0))
    return pl.pallas_call(
        functools.partial(_gdn_gate_body, chunk=chunk),
        grid=(b // bb,),
        in_specs=[pl.BlockSpec((bb, t, 128), lambda i: (i, 0, lane_block)),
                  pl.BlockSpec((1, 128), lambda i: (0, 0)),
                  pl.BlockSpec((1, 128), lambda i: (0, 0))],
        out_specs=[spec, spec, spec],
        out_shape=[jax.ShapeDtypeStruct((b, t, 128), F32)] * 3,
        compiler_params=_params("parallel"),
        name="gdn_gate",
    )(proj, alog_pad, dtb_pad)


def _unit_lower_inverses(lows, n):
    eye = (lax.broadcasted_iota(jnp.int32, (n, n), 0) == lax.broadcasted_iota(jnp.int32, (n, n), 1)).astype(F32)
    ps = [eye - low for low in lows]
    ms = [_mm(low, low) for low in lows]
    k = 2
    while True:
        ps = [p + _mm(p, m) for p, m in zip(ps, ms)]
        k *= 2
        if k >= n:
            break
        ms = [_mm(m, m) for m in ms]
    return ps


def _gdn_scan_body(q_ref, k_ref, v_ref, z_ref, gc_ref, bc_ref, gr_ref, br_ref, s0_ref, onw_ref, o_ref, s_ref,
                   u_ref, w_ref, qk_ref, qd_ref, kd_ref):
    c = GDN_CHUNK
    hb = s0_ref.shape[1]
    n_chunks = q_ref.shape[1] // c
    s_ref[...] = s0_ref[...]
    ri = lax.broadcasted_iota(jnp.int32, (c, c), 0)
    ci = lax.broadcasted_iota(jnp.int32, (c, c), 1)
    tri = ri >= ci
    stri = ri > ci
    onw = onw_ref[...]

    def qk_slice(ref, r0, qh):
        return ref[0, pl.ds(r0, c), qh * GDN_DK:(qh + 1) * GDN_DK]

    def v_slice(ref, r0, hh):
        return ref[0, pl.ds(r0, c), hh * GDN_DV:(hh + 1) * GDN_DV]

    cpb = max(c_ for c_ in (GDN_PREP_CHUNKS, 4, 2, 1) if n_chunks % c_ == 0)

    def prep(nb, carry):
        items = []
        kk, qk, ks, qs = {}, {}, {}, {}
        for ch in range(cpb):
            n = nb * cpb + ch
            r0 = pl.multiple_of(n * c, c)
            gcol = gc_ref[0, 0, pl.ds(r0, c), :]
            bcol = bc_ref[0, 0, pl.ds(r0, c), :]
            grow = gr_ref[0, 0, n]
            brow = br_ref[0, 0, n]
            for qh in range(hb // 2):
                ks[ch, qh] = qk_slice(k_ref, r0, qh)
                qs[ch, qh] = qk_slice(q_ref, r0, qh)
            for hh in range(hb):
                items.append(dict(ch=ch, hh=hh, r0=r0, v=v_slice(v_ref, r0, hh),
                                  gcb=jnp.broadcast_to(gcol[:, hh:hh + 1], (c, GDN_DK)),
                                  bcb=jnp.broadcast_to(bcol[:, hh:hh + 1], (c, c)),
                                  gr=grow[hh:hh + 1, :], br=brow[hh:hh + 1, :]))
        for key in ks:
            kk[key] = _mm_nt(ks[key], ks[key])
        for key in ks:
            qk[key] = _mm_nt(qs[key], ks[key])
        lows = []
        for it in items:
            key = (it["ch"], it["hh"] // 2)
            it["decay"] = jnp.where(tri, jnp.exp(jnp.where(tri, it["gcb"][:, :c] - it["gr"], 0.0)), 0.0)
            lows.append(jnp.where(stri, (kk[key] * it["bcb"]) * it["decay"], 0.0))
        tinvs = _unit_lower_inverses(lows, c)
        us = [_mm(tinv * it["br"], it["v"]) for tinv, it in zip(tinvs, items)]
        ws = [_mm(tinv * (it["br"] * jnp.exp(it["gr"])), ks[it["ch"], it["hh"] // 2])
              for tinv, it in zip(tinvs, items)]
        for it, u, w in zip(items, us, ws):
            key = (it["ch"], it["hh"] // 2)
            hh, r0, gcb = it["hh"], it["r0"], it["gcb"]
            u_ref[hh, pl.ds(r0, c), :] = u
            w_ref[hh, pl.ds(r0, c), :] = w.astype(BF16)
            qk_ref[hh, pl.ds(r0, c), :] = jnp.where(tri, qk[key] * it["decay"], 0.0).astype(BF16)
            qd_ref[hh, pl.ds(r0, c), :] = (qs[key] * jnp.exp(gcb)).astype(BF16)
            kd_ref[hh, pl.ds(r0, c), :] = (ks[key] * jnp.exp(gcb[c - 1:c, :] - gcb)).astype(BF16)
        return carry

    lax.fori_loop(0, n_chunks // cpb, prep, 0)

    def scan(n, carry):
        r0 = pl.multiple_of(n * c, c)
        g_last = gc_ref[0, 0, pl.ds(r0 + (c - 1), 1), :]
        loaded = []
        for hh in range(hb):
            loaded.append((u_ref[hh, pl.ds(r0, c), :], w_ref[hh, pl.ds(r0, c), :], qk_ref[hh, pl.ds(r0, c), :],
                           qd_ref[hh, pl.ds(r0, c), :], kd_ref[hh, pl.ds(r0, c), :], v_slice(z_ref, r0, hh),
                           s_ref[0, hh]))
        ws_s = [_mm(w, s) for (u, w, qkm, qd, kd, z, s) in loaded]
        qd_s = [_mm(qd, s) for (u, w, qkm, qd, kd, z, s) in loaded]
        v_news = [ld[0] - ws for ld, ws in zip(loaded, ws_s)]
        qk_v = [_mm(ld[2], vn) for ld, vn in zip(loaded, v_news)]
        kd_v = [_mm_tn(ld[4], vn) for ld, vn in zip(loaded, v_news)]
        results = []
        for hh, ld in enumerate(loaded):
            o = qd_s[hh] + qk_v[hh]
            s_new = ld[6] * jnp.exp(g_last[:, hh:hh + 1]) + kd_v[hh]
            inv = lax.rsqrt(jnp.mean(o * o, axis=-1, keepdims=True) + RMS_EPS)
            results.append((((o * inv) * onw) * _silu(ld[5].astype(F32)), s_new))
        for hh, (og, s_new) in enumerate(results):
            s_ref[0, hh] = s_new
            o_ref[0, pl.ds(r0, c), hh * GDN_DV:(hh + 1) * GDN_DV] = og.astype(o_ref.dtype)
        return carry

    lax.fori_loop(0, n_chunks, scan, 0)


def _gdn_scan(qkv, proj, gcol, bcol, grow, brow, s0, onw):
    b, t, _ = qkv.shape
    hb = gcol.shape[3]
    hg = GDN_V_HEADS // hb
    qw = hb // 2 * GDN_DK
    vw = hb * GDN_DV
    n = t // GDN_CHUNK
    return pl.pallas_call(
        _gdn_scan_body,
        grid=(b, hg),
        in_specs=[pl.BlockSpec((1, t, qw), lambda i, j: (i, 0, j)),
                  pl.BlockSpec((1, t, qw), lambda i, j: (i, 0, GDN_QK_DIM // qw + j)),
                  pl.BlockSpec((1, t, vw), lambda i, j: (i, 0, 2 * GDN_QK_DIM // vw + j)),
                  pl.BlockSpec((1, t, vw), lambda i, j: (i, 0, GDN_CONV_CH // vw + j)),
                  pl.BlockSpec((1, 1, t, hb), lambda i, j: (i, j, 0, 0)),
                  pl.BlockSpec((1, 1, t, hb), lambda i, j: (i, j, 0, 0)),
                  pl.BlockSpec((1, 1, n, hb, GDN_CHUNK), lambda i, j: (i, j, 0, 0, 0)),
                  pl.BlockSpec((1, 1, n, hb, GDN_CHUNK), lambda i, j: (i, j, 0, 0, 0)),
                  pl.BlockSpec((1, hb, GDN_DK, GDN_DV), lambda i, j: (i, j, 0, 0)),
                  pl.BlockSpec((1, GDN_DV), lambda i, j: (0, 0))],
        out_specs=[pl.BlockSpec((1, t, vw), lambda i, j: (i, 0, j)),
                   pl.BlockSpec((1, hb, GDN_DK, GDN_DV), lambda i, j: (i, j, 0, 0))],
        out_shape=[jax.ShapeDtypeStruct((b, t, GDN_V_DIM), BF16),
                   jax.ShapeDtypeStruct((b, GDN_V_HEADS, GDN_DK, GDN_DV), F32)],
        scratch_shapes=[pltpu.VMEM((hb, t, GDN_DV), F32),
                        pltpu.VMEM((hb, t, GDN_DK), BF16),
                        pltpu.VMEM((hb, t, GDN_CHUNK), BF16),
                        pltpu.VMEM((hb, t, GDN_DK), BF16),
                        pltpu.VMEM((hb, t, GDN_DK), BF16)],
        compiler_params=_params("parallel", "parallel"),
        name="gdn_scan",
    )(qkv, qkv, qkv, proj, gcol, bcol, grow, brow, s0, onw.reshape(1, GDN_DV))


def _gdn_layer(x, s0, conv_buf, norm_w, w_in, conv_w, alog_pad, dtb_pad, out_norm_w, w_out):
    b, t, d = x.shape
    xf = x.reshape(b * t, d)
    w_main, w_tail = w_in
    proj = _rms_matmul(xf, norm_w, w_main, 2048, BF16).reshape(b, t, -1)
    tail = _rms_matmul(xf, norm_w, w_tail, 128).reshape(b, t, -1)
    qkv, new_buf = _gdn_pre(proj, conv_buf, conv_w)
    _, gcum, beta = _gdn_gate(tail, alog_pad, dtb_pad, 0)
    gcum = gcum[:, :, :GDN_V_HEADS]
    beta = beta[:, :, GDN_V_HEADS:2 * GDN_V_HEADS]
    tp = -(-t // GDN_CHUNK) * GDN_CHUNK
    if tp != t:
        pad = [(0, 0), (0, tp - t), (0, 0)]
        qkv = jnp.pad(qkv, pad)
        proj_z = jnp.pad(proj, pad)
        gcum = jnp.pad(gcum, pad, mode="edge")
        beta = jnp.pad(beta, pad)
    else:
        proj_z = proj
    hb = GDN_V_HEADS if tp == GDN_CHUNK else GDN_HEADS_PER_STEP
    hg = GDN_V_HEADS // hb
    n = tp // GDN_CHUNK
    gcol = gcum.reshape(b, tp, hg, hb).transpose(0, 2, 1, 3)
    bcol = beta.reshape(b, tp, hg, hb).transpose(0, 2, 1, 3)
    grow = gcum.reshape(b, n, GDN_CHUNK, hg, hb).transpose(0, 3, 1, 4, 2)
    brow = beta.reshape(b, n, GDN_CHUNK, hg, hb).transpose(0, 3, 1, 4, 2)
    o, s_new = _gdn_scan(qkv, proj_z, gcol, bcol, grow, brow, s0, out_norm_w)
    o = o[:, :t].reshape(b * t, GDN_V_DIM)
    y = _matmul_res(o, w_out, xf).reshape(b, t, d)
    return y, s_new, new_buf


def _page_gather_body(pt_ref, *refs, pps):
    cmp_refs, o_ref = refs[:pps], refs[pps]
    sub = cmp_refs[0].shape[3]
    for s in range(pps):
        o_ref[0, :, s * sub:(s + 1) * sub, :] = cmp_refs[s][0, 0]


def _page_gather(cache_cmp, layer, page_table):
    b, n_pages = page_table.shape
    _, _, n_cg, sub, flat = cache_cmp.shape
    pps = 16 if n_pages % 16 == 0 else (8 if n_pages % 8 == 0 else 1)
    n_steps = n_pages // pps

    def page_map(s):
        return lambda i, p, pt: (layer, pt[i, p * pps + s], 0, 0, 0)

    grid_spec = pltpu.PrefetchScalarGridSpec(
        num_scalar_prefetch=1,
        grid=(b, n_steps),
        in_specs=[pl.BlockSpec((1, 1, n_cg, sub, flat), page_map(s)) for s in range(pps)],
        out_specs=pl.BlockSpec((1, n_cg, pps * sub, flat), lambda i, p, pt: (i, 0, p, 0)),
    )
    return pl.pallas_call(
        functools.partial(_page_gather_body, pps=pps),
        grid_spec=grid_spec,
        out_shape=jax.ShapeDtypeStruct((b, n_cg, n_pages * sub, flat), cache_cmp.dtype),
        compiler_params=_params("parallel", "arbitrary"),
        name="page_gather",
    )(page_table, *([cache_cmp] * pps))


def _compress_body(x_ref, w1_ref, pe_ref, w2_ref, o_ref, bias_ref, *, n_sub):
    hd = CMP_HIDDEN

    @pl.when(pl.program_id(1) == 0)
    def _():
        pe = pe_ref[0]
        pe0 = jnp.broadcast_to(pe[0:1], (8, pe.shape[1]))
        pe1 = jnp.broadcast_to(pe[1:2], (8, pe.shape[1]))
        bias_ref[:, :hd] = jnp.dot(pe0.astype(BF16), w1_ref[0, :, :hd], preferred_element_type=F32)
        bias_ref[:, hd:] = jnp.dot(pe1.astype(BF16), w1_ref[0, :, hd:], preferred_element_type=F32)

    acc = jnp.dot(x_ref[0, 0], w1_ref[0], preferred_element_type=F32)
    first = acc[:, :hd] + bias_ref[0:1, :hd]
    second = acc[:, hd:] + bias_ref[0:1, hd:]
    hid = _silu(first + pltpu.roll(second, n_sub - 1, 0))
    o_ref[0, 0] = jnp.dot(hid.astype(BF16), w2_ref[0], preferred_element_type=F32)


def _compress(x, n_sub, w1cat, pe_flat, w2):
    b, _, _, flat = x.shape
    dh = flat // CMP_STRIDE
    return pl.pallas_call(
        functools.partial(_compress_body, n_sub=n_sub),
        grid=(2 * NSA_G, b),
        in_specs=[pl.BlockSpec((1, 1, n_sub, flat), lambda j, i: (i, j, 0, 0)),
                  pl.BlockSpec((1, flat, 2 * CMP_HIDDEN), lambda j, i: (j // NSA_G, 0, 0)),
                  pl.BlockSpec((1, 2, flat), lambda j, i: (j // NSA_G, 0, 0)),
                  pl.BlockSpec((1, CMP_HIDDEN, dh), lambda j, i: (j // NSA_G, 0, 0))],
        out_specs=pl.BlockSpec((1, 1, n_sub, dh), lambda j, i: (i, j, 0, 0)),
        out_shape=jax.ShapeDtypeStruct((b, 2 * NSA_G, n_sub, dh), F32),
        scratch_shapes=[pltpu.VMEM((8, 2 * CMP_HIDDEN), F32)],
        compiler_params=_params("arbitrary", "arbitrary"),
        name="nsa_compress",
    )(x, w1cat, pe_flat, w2)


def _inv_or_zero(l):
    return jnp.where(l > 0.0, 1.0 / jnp.where(l > 0.0, l, 1.0), 0.0)


def _attn_branch(q, slopes, t_pos, k, v, kp, ok_fn, may_be_empty=False):
    r, qb, nk = slopes.shape[0], t_pos.shape[1], k.shape[0]
    dist = t_pos - kp
    ok = ok_fn(dist)
    pen = jnp.where(ok, dist.astype(F32), MASKED_DIST)
    s = _mm_nt(q, k).reshape(r, qb, nk) - slopes * pen
    m = jnp.max(s, axis=-1, keepdims=True)
    p = jnp.exp2(s - m)
    if may_be_empty:
        p = jnp.where(ok, p, 0.0)
    inv = _inv_or_zero(jnp.sum(p, axis=-1, keepdims=True))
    o = _mm(p.reshape(r * qb, nk), v) * inv.reshape(r * qb, 1)
    return o, p, inv


def _pick_blocks(p_sum, t_lane, n_cmp, n_blk, nb_rows):
    nq, nc = p_sum.shape
    nb = -(-n_blk // 8) * 8
    jj = lax.broadcasted_iota(jnp.int32, (nb, nc), 0)
    nn = lax.broadcasted_iota(jnp.int32, (nb, nc), 1)
    overlap_t = ((nn * CMP_STRIDE < (jj + 1) * SEL_BLOCK) & (nn * CMP_STRIDE + CMP_LEN > jj * SEL_BLOCK)
                 & (nn < n_cmp) & (jj < n_blk)).astype(F32)
    imp = lax.dot_general(overlap_t, p_sum, (((1,), (1,)), ((), ())), preferred_element_type=F32,
                          precision=lax.Precision.HIGHEST)
    j = lax.broadcasted_iota(jnp.int32, (nb, nq), 0)
    cur = t_lane // SEL_BLOCK
    imp = jnp.where((j == 0) | (j == cur) | (j == cur - 1), FORCE_SCORE, imp)
    imp = jnp.where(j > cur, -FORCE_SCORE, imp)
    imp = jnp.where(j >= n_blk, NEVER, imp)
    rank = jnp.zeros((nb, nq), jnp.int32)
    for jp in range(n_blk):
        row = imp[jp:jp + 1, :]
        rank = rank + ((row > imp) | ((row == imp) & (j > jp))).astype(jnp.int32)
    picked_t = ((rank < SEL_TOPN) & (j < n_blk)).astype(F32)
    if nb_rows > nb:
        picked_t = jnp.concatenate([picked_t, jnp.zeros((nb_rows - nb, nq), F32)], axis=0)
    return picked_t


def _transpose_01(x_t):
    m = x_t.shape[1]
    eye = (lax.broadcasted_iota(jnp.int32, (m, m), 0) == lax.broadcasted_iota(jnp.int32, (m, m), 1))
    return _mm_nt(eye.astype(BF16), x_t)


def _head_rows(x, n_heads):
    dh = x.shape[1] // n_heads
    return jnp.concatenate([x[:, h * dh:(h + 1) * dh] for h in range(n_heads)], axis=0)


def _nsa_attn_body(q_ref, gate_ref, slope_ref, ck_ref, cv_ref, ks_ref, vs_ref, kw_ref, vw_ref, o_ref, osel_ref,
                   *, nsb, **kw):
    refs = (q_ref, gate_ref, slope_ref, ck_ref, cv_ref, ks_ref, vs_ref, kw_ref, vw_ref, o_ref, osel_ref)
    if nsb == 1:
        _nsa_attn_block(0, pl.program_id(2), *refs, **kw)
    else:
        def one(sub, carry):
            _nsa_attn_block(sub, pl.program_id(2) * nsb + sub, *refs, **kw)
            return carry

        lax.fori_loop(0, nsb, one, 0)


def _nsa_attn_block(sub, i, q_ref, gate_ref, slope_ref, ck_ref, cv_ref, ks_ref, vs_ref, kw_ref, vw_ref, o_ref,
                    osel_ref, *, qb, p_len, n_cmp, n_blk, w0, sel_classes, kw_len):
    r = NSA_R
    rows = r * qb
    rq = sub * qb if isinstance(sub, int) else pl.multiple_of(sub * qb, qb)
    q = _head_rows(q_ref[0, pl.ds(rq, qb), :].astype(F32), r)
    q = (q * (NSA_DH ** -0.5 * LOG2E)).astype(BF16)
    slopes = slope_ref[0] * LOG2E
    q0 = p_len + i * qb
    t_pos = q0 + lax.broadcasted_iota(jnp.int32, (1, qb, 1), 1)
    branch = functools.partial(_attn_branch, q, slopes, t_pos)

    tw = kw_ref.shape[2]
    if tw == kw_len:
        start = 0
    else:
        start = pl.multiple_of(jnp.clip(q0 + qb - w0 - kw_len, 0, tw - kw_len), SEL_BLOCK)
    kp_win = w0 + start + lax.broadcasted_iota(jnp.int32, (1, 1, kw_len), 2)
    o_win, _, _ = branch(kw_ref[0, 0, pl.ds(start, kw_len), :], vw_ref[0, 0, pl.ds(start, kw_len), :], kp_win,
                         lambda dist: (dist >= 0) & (dist < WINDOW) & (kp_win >= 0))

    nc = ck_ref.shape[2]
    n_idx = lax.broadcasted_iota(jnp.int32, (1, 1, nc), 2)
    o_cmp, p_cmp, inv_cmp = branch(ck_ref[0, 0], cv_ref[0, 0], n_idx * CMP_STRIDE + (CMP_LEN - 1),
                                   lambda dist: (dist >= 0) & (n_idx < n_cmp), may_be_empty=True)
    p_sum = jnp.sum(p_cmp * inv_cmp, axis=0)
    nb8 = -(-n_blk // 16) * 16
    picked_t = _pick_blocks(p_sum, q0 + lax.broadcasted_iota(jnp.int32, (1, qb), 1), n_cmp, n_blk, nb8)
    picked = _transpose_01(picked_t)

    def sel_branch(nk):
        blk_of_key = lax.broadcasted_iota(jnp.int32, (nb8, nk), 1) // SEL_BLOCK
        expand = (blk_of_key == lax.broadcasted_iota(jnp.int32, (nb8, nk), 0)).astype(BF16)
        key_picked = (_mm(picked, expand) > 0.5).reshape(1, qb, nk)
        o, _, _ = branch(ks_ref[0, 0, 0:nk, :], vs_ref[0, 0, 0:nk, :],
                         lax.broadcasted_iota(jnp.int32, (1, 1, nk), 2),
                         lambda dist: key_picked & (dist >= 0))
        osel_ref[...] = o

    if len(sel_classes) == 1:
        sel_branch(sel_classes[0])
    else:
        need = q0 + qb
        prev = 0
        for nk in sel_classes:
            pl.when((need > prev) & (need <= nk))(functools.partial(sel_branch, nk))
            prev = nk

    gates = _sigmoid(gate_ref[0, 0, :, pl.ds(rq, qb), :]).reshape(rows, 3)
    o = gates[:, 0:1] * o_cmp + gates[:, 1:2] * osel_ref[...] + gates[:, 2:3] * o_win
    for h in range(r):
        o_ref[0, pl.ds(rq, qb), h * NSA_DH:(h + 1) * NSA_DH] = o[h * qb:(h + 1) * qb].astype(o_ref.dtype)


def _nsa_attn(q, gates_t, slopes, cmp_kv, sel_arr, sel_k0, sel_v0, win_arr, win_k0, win_v0,
              *, p_len, n_cmp, n_blk, w0):
    b, tq, _ = q.shape
    g, r, dh = NSA_G, NSA_R, NSA_DH
    qb = NSA_Q_BLOCK if tq % NSA_Q_BLOCK == 0 else tq
    nc = cmp_kv.shape[2]
    tk = sel_arr.shape[2]
    tw = win_arr.shape[2]
    if tq == qb or tk % NSA_SEL_CLASS != 0:
        sel_classes = (tk,)
    else:
        sel_classes = tuple(range(NSA_SEL_CLASS, tk + 1, NSA_SEL_CLASS))
    kw_len = min(tw, -(-(WINDOW - 1 + qb) // KEY_CHUNK) * KEY_CHUNK)
    nsb = NSA_Q_BLOCKS_PER_STEP if (tq // qb) % NSA_Q_BLOCKS_PER_STEP == 0 else 1
    body = functools.partial(_nsa_attn_body, nsb=nsb, qb=qb, p_len=p_len, n_cmp=n_cmp, n_blk=n_blk, w0=w0,
                             sel_classes=sel_classes, kw_len=kw_len)
    qs = nsb * qb
    return pl.pallas_call(
        body,
        grid=(b, g, tq // qs),
        in_specs=[pl.BlockSpec((1, qs, r * dh), lambda bi, gi, i: (bi, i, gi)),
                  pl.BlockSpec((1, 1, r, qs, 3), lambda bi, gi, i: (bi, gi, 0, i, 0)),
                  pl.BlockSpec((1, r, 1, 1), lambda bi, gi, i: (gi, 0, 0, 0)),
                  pl.BlockSpec((1, 1, nc, dh), lambda bi, gi, i: (bi, gi, 0, 0)),
                  pl.BlockSpec((1, 1, nc, dh), lambda bi, gi, i: (bi, NSA_G + gi, 0, 0)),
                  pl.BlockSpec((1, 1, tk, dh), lambda bi, gi, i: (bi, sel_k0 + gi, 0, 0)),
                  pl.BlockSpec((1, 1, tk, dh), lambda bi, gi, i: (bi, sel_v0 + gi, 0, 0)),
                  pl.BlockSpec((1, 1, tw, dh), lambda bi, gi, i: (bi, win_k0 + gi, 0, 0)),
                  pl.BlockSpec((1, 1, tw, dh), lambda bi, gi, i: (bi, win_v0 + gi, 0, 0))],
        out_specs=pl.BlockSpec((1, qs, r * dh), lambda bi, gi, i: (bi, i, gi)),
        out_shape=jax.ShapeDtypeStruct((b, tq, g * r * dh), BF16),
        scratch_shapes=[pltpu.VMEM((r * qb, dh), F32)],
        compiler_params=_params("parallel", "parallel", "arbitrary"),
        name="nsa_attn",
    )(q, gates_t, slopes, cmp_kv, cmp_kv, sel_arr, sel_arr, win_arr, win_arr)


def _nsa_paged_body(pt_ref, *refs, tq, p_len, n_cmp, n_blk, w0, pps, n_steps):
    k_refs, v_refs = refs[:pps], refs[pps:2 * pps]
    (q_ref, gate_ref, slope_ref, cmp_ref, new_ref, win_ref, o_ref,
     qbd_ref, pen_ref, m_ref, l_ref, acc_ref, ocmp_ref) = refs[2 * pps:]
    p = pl.program_id(1)
    g_n, r = NSA_G, NSA_R
    dh = NSA_DH
    rg = r * tq
    rows = g_n * rg
    page = k_refs[0].shape[5]
    chunk = pps * page
    n_keys = pen_ref.shape[1]
    t_pos = p_len + lax.broadcasted_iota(jnp.int32, (1, tq, 1), 1)

    @pl.when(p == 0)
    def _():
        qbd_ref[...] = jnp.zeros(qbd_ref.shape, qbd_ref.dtype)
        nc = cmp_ref.shape[2]
        n_idx = lax.broadcasted_iota(jnp.int32, (1, 1, nc), 2)
        p_sums = []
        for g in range(g_n):
            q = _head_rows(q_ref[0, :, g * r * dh:(g + 1) * r * dh].astype(F32), r)
            q = (q * (dh ** -0.5 * LOG2E)).astype(BF16)
            qbd_ref[g * rg:(g + 1) * rg, g * dh:(g + 1) * dh] = q
            o_cmp, p_cmp, inv_cmp = _attn_branch(
                q, slope_ref[g] * LOG2E, t_pos, cmp_ref[0, g], cmp_ref[0, g_n + g],
                n_idx * CMP_STRIDE + (CMP_LEN - 1), lambda dist: (dist >= 0) & (n_idx < n_cmp), may_be_empty=True)
            ocmp_ref[g * rg:(g + 1) * rg, :] = o_cmp
            p_sums.append(jnp.sum(p_cmp * inv_cmp, axis=0))
        nbp = -(-n_blk // 128) * 128
        nq = g_n * tq
        t_lane = p_len + lax.broadcasted_iota(jnp.int32, (1, nq), 1) % tq
        picked_t = _pick_blocks(jnp.concatenate(p_sums, axis=0), t_lane, n_cmp, n_blk, nbp)
        picked = _transpose_01(picked_t.astype(BF16))
        picked_rows = jnp.concatenate([picked[g * tq:(g + 1) * tq] for g in range(g_n) for _ in range(r)], axis=0)
        slope_rows = jnp.broadcast_to(slope_ref[...] * LOG2E, (g_n, r, tq, 1)).reshape(rows, 1)
        t_rows = p_len + lax.broadcasted_iota(jnp.int32, (rows, 1), 0) % tq

        def fill(k0, width):
            kp = k0 + lax.broadcasted_iota(jnp.int32, (1, width), 1)
            blk = lax.broadcasted_iota(jnp.int32, (nbp, width), 0)
            expand = ((k0 + lax.broadcasted_iota(jnp.int32, (nbp, width), 1)) // SEL_BLOCK == blk).astype(BF16)
            key_picked = _mm(picked_rows, expand) > 0.5
            dist = t_rows - kp
            return slope_rows * jnp.where(key_picked & (dist >= 0), dist.astype(F32), MASKED_DIST)

        def fill_chunk(c, carry):
            k0 = pl.multiple_of(c * chunk, chunk)
            pen_ref[:, pl.ds(k0, chunk)] = fill(k0, chunk)
            return carry

        lax.fori_loop(0, n_steps, fill_chunk, 0)
        pen_ref[:, n_steps * chunk:n_keys] = fill(n_steps * chunk, n_keys - n_steps * chunk)
        m_ref[...] = jnp.full(m_ref.shape, NEVER, F32)
        l_ref[...] = jnp.zeros(l_ref.shape, F32)
        acc_ref[...] = jnp.zeros(acc_ref.shape, F32)

    def online_update(kts, vts, k0):
        width = len(kts) * page
        qbd = qbd_ref[...]
        s = jnp.concatenate([jnp.dot(qbd, kt.astype(BF16), preferred_element_type=F32) for kt in kts], axis=1)
        s = s - pen_ref[:, pl.ds(k0, width)]
        m_prev = m_ref[...]
        m_new = jnp.maximum(m_prev, jnp.max(s, axis=-1, keepdims=True))
        alpha = jnp.exp2(m_prev - m_new)
        prob = jnp.exp2(s - m_new)
        l_ref[...] = alpha * l_ref[...] + jnp.sum(prob, axis=-1, keepdims=True)
        pv = [_mm_nt(prob[:, i * page:(i + 1) * page], vt) for i, vt in enumerate(vts)]
        acc_ref[...] = alpha * acc_ref[...] + sum(pv[1:], pv[0])
        m_ref[...] = m_new

    @pl.when(p < n_steps)
    def _():
        online_update([ref[0, 0, 0].reshape(g_n * dh, page) for ref in k_refs],
                      [ref[0, 0, 0].reshape(g_n * dh, page) for ref in v_refs], pl.multiple_of(p * chunk, chunk))

    @pl.when(p == n_steps)
    def _():
        online_update([new_ref[0, 0].reshape(g_n * dh, page)], [new_ref[0, 1].reshape(g_n * dh, page)],
                      n_steps * chunk)
        acc = acc_ref[...] * _inv_or_zero(l_ref[...])
        tw = win_ref.shape[2]
        kp_win = w0 + lax.broadcasted_iota(jnp.int32, (1, 1, tw), 2)
        for g in range(g_n):
            q = qbd_ref[g * rg:(g + 1) * rg, g * dh:(g + 1) * dh]
            o_win, _, _ = _attn_branch(q, slope_ref[g] * LOG2E, t_pos, win_ref[0, g], win_ref[0, g_n + g], kp_win,
                                       lambda dist: (dist >= 0) & (dist < WINDOW) & (kp_win >= 0))
            gates = _sigmoid(gate_ref[0, g]).reshape(rg, 3)
            o = (gates[:, 0:1] * ocmp_ref[g * rg:(g + 1) * rg, :]
                 + gates[:, 1:2] * acc[g * rg:(g + 1) * rg, g * dh:(g + 1) * dh] + gates[:, 2:3] * o_win)
            for h in range(r):
                o_ref[0, :, (g * r + h) * dh:(g * r + h + 1) * dh] = o[h * tq:(h + 1) * tq].astype(o_ref.dtype)


def _nsa_paged(q, gates_t, slopes, cmp_kv, cache_nt, layer, page_table, new_t, win_arr,
               *, p_len, n_cmp, n_blk, w0):
    b, tq, _ = q.shape
    g, r, dh = NSA_G, NSA_R, NSA_DH
    n_pages = page_table.shape[1]
    page = cache_nt.shape[5]
    pps = 8 if n_pages % 8 == 0 else (4 if n_pages % 4 == 0 else 1)
    n_steps = n_pages // pps
    nc = cmp_kv.shape[2]
    tw = win_arr.shape[2]
    rows = g * r * tq
    n_keys = (n_pages + 1) * page

    def page_map(c, s):
        return lambda i, p, pt: (layer, pt[i, jnp.minimum(p, n_steps - 1) * pps + s], c, 0, 0, 0)

    const = lambda i, p, pt: (i, 0, 0, 0, 0)
    grid_spec = pltpu.PrefetchScalarGridSpec(
        num_scalar_prefetch=1,
        grid=(b, n_steps + 1),
        in_specs=[pl.BlockSpec((1, 1, 1, g, dh, page), page_map(2, s)) for s in range(pps)]
        + [pl.BlockSpec((1, 1, 1, g, dh, page), page_map(3, s)) for s in range(pps)]
        + [pl.BlockSpec((1, tq, g * r * dh), lambda i, p, pt: (i, 0, 0)),
           pl.BlockSpec((1, g, r, tq, 3), const),
           pl.BlockSpec((g, r, 1, 1), lambda i, p, pt: (0, 0, 0, 0)),
           pl.BlockSpec((1, 2 * g, nc, dh), lambda i, p, pt: (i, 0, 0, 0)),
           pl.BlockSpec((1, 2, g, dh, page), const),
           pl.BlockSpec((1, 2 * g, tw, dh), lambda i, p, pt: (i, 0, 0, 0))],
        out_specs=pl.BlockSpec((1, tq, g * r * dh), lambda i, p, pt: (i, 0, 0)),
        scratch_shapes=[pltpu.VMEM((rows, g * dh), BF16),
                        pltpu.VMEM((rows, n_keys), F32),
                        pltpu.VMEM((rows, 1), F32),
                        pltpu.VMEM((rows, 1), F32),
                        pltpu.VMEM((rows, g * dh), F32),
                        pltpu.VMEM((rows, dh), F32)],
    )
    body = functools.partial(_nsa_paged_body, tq=tq, p_len=p_len, n_cmp=n_cmp, n_blk=n_blk, w0=w0, pps=pps,
                             n_steps=n_steps)
    return pl.pallas_call(
        body,
        grid_spec=grid_spec,
        out_shape=jax.ShapeDtypeStruct((b, tq, g * r * dh), BF16),
        compiler_params=_params("parallel", "arbitrary"),
        name="nsa_paged",
    )(page_table, *([cache_nt] * (2 * pps)), q, gates_t, slopes, cmp_kv, new_t, win_arr)


def _nsa_layer(x, cache_hm, layer, page_table, win_buf, norm_w, w_in, w1cat, pe_flat, w2, w_out, slopes):
    b, t, d = x.shape
    g, r, dh = NSA_G, NSA_R, NSA_DH
    xf = x.reshape(b * t, d)
    w_q, w_kv = w_in
    q = _rms_matmul(xf, norm_w, w_q, NSA_Q_DIM, BF16).reshape(b, t, NSA_Q_DIM)
    proj = _rms_matmul(xf, norm_w, w_kv, w_kv.shape[1]).reshape(b, t, -1)
    kv = proj[..., :6 * NSA_KV_DIM]
    gates_t = proj[..., 6 * NSA_KV_DIM:6 * NSA_KV_DIM + 3 * NSA_HEADS]
    gates_t = gates_t.reshape(b, t, g, r, 3).transpose(0, 2, 3, 1, 4)
    kv6 = kv.reshape(b, t, 6, g, dh)
    new_rows = kv6[:, :, :4]
    if cache_hm is None:
        p_len = 0
        tk = t
        assert t % CMP_STRIDE == 0
        cmp_x = kv6[:, :, :2].astype(BF16).transpose(0, 2, 3, 1, 4).reshape(b, 2 * g, t // CMP_STRIDE,
                                                                           CMP_STRIDE * dh)
        kv_hm = kv6[:, :, 2:].astype(BF16).transpose(0, 2, 3, 1, 4).reshape(b, 4 * g, t, dh)
        new_win = kv6[:, t - min(WINDOW, t):, 4:]
    else:
        cache_cmp, cache_nt = cache_hm
        n_pages = page_table.shape[1]
        page = cache_nt.shape[5]
        p_len = n_pages * page
        tk = p_len + t
        assert p_len % CMP_STRIDE == 0 and t < CMP_STRIDE and t <= page
        cmp_x = _page_gather(cache_cmp, layer, page_table)
        new_t = jnp.pad(kv6[:, :, 2:4].transpose(0, 2, 3, 4, 1), [(0, 0)] * 4 + [(0, page - t)])
        win_all = jnp.concatenate([win_buf, kv6[:, :, 4:]], axis=1)
        wl = win_all.shape[1]
        wlp = -(-wl // KEY_CHUNK) * KEY_CHUNK
        win_arr = jnp.pad(win_all.astype(BF16), [(0, 0), (0, wlp - wl), (0, 0), (0, 0), (0, 0)])
        win_arr = win_arr.transpose(0, 2, 3, 1, 4).reshape(b, 2 * g, wlp, dh)
        new_win = win_all[:, wl - min(WINDOW, tk):]
    n_sub = tk // CMP_STRIDE
    n_cmp = n_sub - 1
    n_blk = -(-tk // SEL_BLOCK)
    cmp_kv = _compress(cmp_x, n_sub, w1cat, pe_flat, w2)
    if cache_hm is None:
        o = _nsa_attn(q, gates_t, slopes, cmp_kv, kv_hm, 0, g, kv_hm, 2 * g, 3 * g,
                      p_len=0, n_cmp=n_cmp, n_blk=n_blk, w0=0)
    else:
        o = _nsa_paged(q, gates_t, slopes, cmp_kv, cache_nt, layer, page_table, new_t, win_arr,
                       p_len=p_len, n_cmp=n_cmp, n_blk=n_blk, w0=p_len - win_buf.shape[1])
    y = _matmul_res(o.reshape(b * t, NSA_Q_DIM), w_out, xf).reshape(b, t, d)
    return y, new_rows, new_win


def _ffn_layer(x, buf, norm_w, w_up, conv_w, w_down):
    b, t, d = x.shape
    xf = x.reshape(b * t, d)
    h = _rms_matmul(xf, norm_w, w_up, D_FF, BF16).reshape(b, t, -1)
    act, new_buf = _ffn_act(h, buf, conv_w)
    y = _matmul_res(act.reshape(b * t, D_FF), w_down, xf).reshape(b, t, d)
    return y, new_buf


def _pad_cols(w, n):
    return jnp.pad(w, [(0, 0)] * (w.ndim - 1) + [(0, n - w.shape[-1])])


def _trunk(x, cache_hm, page_table, nsa_win, gdn_state, gdn_conv, ffn_conv, wts):
    depth = wts["ffn_w_up"].shape[0]
    rows_l, win_l, s_l, gconv_l, fconv_l = [], [], [], [], []
    for i in range(depth):
        j = i // 2
        if i % 2 == 0:
            x, s_new, cb = _gdn_layer(x, gdn_state[j], gdn_conv[j], wts["gdn_norm"][j],
                                      (wts["gdn_w_main"][j], wts["gdn_w_tail"][j]),
                                      wts["gdn_conv_w"][j], wts["gdn_a_log"][j], wts["gdn_dt_bias"][j],
                                      wts["gdn_out_norm"][j], wts["gdn_w_out"][j])
            s_l.append(s_new)
            gconv_l.append(cb)
        else:
            x, rows, wb = _nsa_layer(x, cache_hm, j, page_table, None if nsa_win is None else nsa_win[j],
                                     wts["nsa_norm"][j], (wts["nsa_w_q"][j], wts["nsa_w_kv"][j]),
                                     wts["nsa_w1cat"][j],
                                     wts["nsa_pe_flat"][j], wts["nsa_w2"][j], wts["nsa_w_out"][j],
                                     wts["slopes"])
            rows_l.append(rows)
            win_l.append(wb)
        x, fb = _ffn_layer(x, ffn_conv[i], wts["ffn_norm"][i], wts["ffn_w_up"][i], wts["ffn_conv_w"][i],
                           wts["ffn_w_down"][i])
        fconv_l.append(fb)
    b, t, d = x.shape
    y = _rms(x.reshape(b * t, d), wts["final_norm"]).reshape(b, t, d)
    return (y, jnp.stack(rows_l), jnp.stack(win_l), jnp.stack(s_l), jnp.stack(gconv_l), jnp.stack(fconv_l))


def kernel(x_prompt, x_sample, cache_nsa_kv, cache_nsa_win, state_gdn_s, state_gdn_conv, state_ffn_conv,
           page_table, gdn_norm, gdn_w_in, gdn_conv_w, gdn_a_log, gdn_dt_bias, gdn_out_norm, gdn_w_out,
           nsa_norm, nsa_w_in, nsa_cmp_pe, nsa_cmp_w1, nsa_cmp_w2, nsa_w_out,
           ffn_norm, ffn_w_up, ffn_conv_w, ffn_w_down, final_norm):
    n_gdn = gdn_w_in.shape[0]
    n_nsa = nsa_w_in.shape[0]
    depth = ffn_w_up.shape[0]
    bp = x_prompt.shape[0]
    gdn_main = GDN_CONV_CH + GDN_V_DIM
    nsa_cols = -(-(nsa_w_in.shape[2] - NSA_Q_DIM) // 896) * 896
    w1 = nsa_cmp_w1.reshape(n_nsa, 2, 2, CMP_STRIDE * NSA_DH, CMP_HIDDEN)
    w1cat = jnp.concatenate([w1[:, :, 0], w1[:, :, 1]], axis=-1).astype(BF16)
    slopes = 2.0 ** (-8.0 * jnp.arange(1, NSA_HEADS + 1, dtype=F32) / NSA_HEADS)
    wts = {
        "gdn_norm": gdn_norm,
        "gdn_w_main": gdn_w_in[:, :, :gdn_main].astype(BF16),
        "gdn_w_tail": _pad_cols(gdn_w_in[:, :, gdn_main:], 128).astype(BF16),
        "gdn_conv_w": gdn_conv_w,
        "gdn_a_log": _pad_cols(gdn_a_log, 128).reshape(n_gdn, 1, 128),
        "gdn_dt_bias": _pad_cols(gdn_dt_bias, 128).reshape(n_gdn, 1, 128),
        "gdn_out_norm": gdn_out_norm,
        "gdn_w_out": gdn_w_out.astype(BF16),
        "nsa_norm": nsa_norm,
        "nsa_w_q": nsa_w_in[:, :, :NSA_Q_DIM].astype(BF16),
        "nsa_w_kv": _pad_cols(nsa_w_in[:, :, NSA_Q_DIM:], nsa_cols).astype(BF16),
        "nsa_w1cat": w1cat,
        "nsa_pe_flat": nsa_cmp_pe.reshape(n_nsa, 2, 2, CMP_STRIDE * NSA_DH),
        "nsa_w2": nsa_cmp_w2.astype(BF16),
        "nsa_w_out": nsa_w_out.astype(BF16),
        "slopes": slopes.reshape(NSA_G, NSA_R, 1, 1),
        "ffn_norm": ffn_norm,
        "ffn_w_up": ffn_w_up.astype(BF16),
        "ffn_conv_w": ffn_conv_w,
        "ffn_w_down": ffn_w_down.astype(BF16),
        "final_norm": final_norm,
    }
    n_l, pool, page = cache_nsa_kv.shape[:3]
    cache_cmp = cache_nsa_kv[:, :, :, :2].astype(BF16).transpose(0, 1, 3, 4, 2, 5)
    cache_hm = (cache_cmp.reshape(n_l, pool, 2 * NSA_G, page // CMP_STRIDE, CMP_STRIDE * NSA_DH),
                cache_nsa_kv.transpose(0, 1, 3, 4, 5, 2))

    zeros = functools.partial(jnp.zeros, dtype=F32)
    prompt = _trunk(x_prompt, None, None, None,
                    zeros((n_gdn, bp, GDN_V_HEADS, GDN_DK, GDN_DV)),
                    zeros((n_gdn, bp, GDN_CONV_W - 1, GDN_CONV_CH)),
                    zeros((depth, bp, FFN_CONV_W - 1, D_FF)), wts)
    sample = _trunk(x_sample, cache_hm, page_table, cache_nsa_win, state_gdn_s, state_gdn_conv, state_ffn_conv, wts)
    out = []
    for p, s in zip(prompt, sample):
        out.extend([p, s])
    return tuple(out)
```

```python
import functools

import jax
import jax.numpy as jnp
from jax import lax
from jax.experimental import pallas as pl
from jax.experimental.pallas import tpu as pltpu

F32 = jnp.float32
BF16 = jnp.bfloat16

RMS_EPS = 1e-6
L2_EPS = 1e-6
NEG_INF = -1e30
FORCE_SCORE = 1e9
NEVER = -3e38
MASKED_DIST = 1e30
LOG2E = 1.4426950408889634

GDN_QK_HEADS = 8
GDN_V_HEADS = 16
GDN_DK = 128
GDN_DV = 128
GDN_QK_DIM = GDN_QK_HEADS * GDN_DK
GDN_V_DIM = GDN_V_HEADS * GDN_DV
GDN_CONV_CH = 2 * GDN_QK_DIM + GDN_V_DIM
GDN_CONV_W = 4
GDN_CHUNK = 64
GDN_HEADS_PER_STEP = 4
GDN_PREP_CHUNKS = 8
NSA_HEADS = 16
NSA_G = 4
NSA_R = 4
NSA_DH = 64
NSA_Q_DIM = NSA_HEADS * NSA_DH
NSA_KV_DIM = NSA_G * NSA_DH
CMP_STRIDE = 16
CMP_LEN = 32
CMP_HIDDEN = 256
SEL_BLOCK = 64
SEL_TOPN = 8
WINDOW = 512
NSA_Q_BLOCK = 64
NSA_Q_BLOCKS_PER_STEP = 8
KEY_CHUNK = 128
NSA_SEL_CLASS = 512
D_FF = 2816
FFN_CONV_W = 3
FFN_TC = 256
SEQ_ROWS_PER_STEP = 2048
CONV_HEAD_ROWS = 16
HM_ROWS = 512

VMEM_LIMIT = 52 * 1024 * 1024


def _params(*sem):
    return pltpu.CompilerParams(dimension_semantics=sem, vmem_limit_bytes=VMEM_LIMIT)


def _mm(a, b):
    return jnp.dot(a.astype(BF16), b.astype(BF16), preferred_element_type=F32)


def _mm_nt(a, b):
    return lax.dot_general(a.astype(BF16), b.astype(BF16), (((1,), (1,)), ((), ())),
                           preferred_element_type=F32)


def _mm_tn(a, b):
    return lax.dot_general(a.astype(BF16), b.astype(BF16), (((0,), (0,)), ((), ())),
                           preferred_element_type=F32)


def _sigmoid(x):
    return 1.0 / (1.0 + jnp.exp(-x))


def _silu(x):
    return x * _sigmoid(x)


def _rms_matmul_body(x_ref, nw_ref, w_ref, o_ref, xn_ref):
    @pl.when(pl.program_id(1) == 0)
    def _():
        x = x_ref[...]
        inv = lax.rsqrt(jnp.mean(x * x, axis=-1, keepdims=True) + RMS_EPS)
        xn_ref[...] = ((x * inv) * nw_ref[...]).astype(BF16)

    o_ref[...] = jnp.dot(xn_ref[...], w_ref[...], preferred_element_type=F32).astype(o_ref.dtype)


def _rms_matmul(x, nw, w, tn, out_dtype=F32):
    m, k = x.shape
    n = w.shape[1]
    tm = min(m, 1024)
    return pl.pallas_call(
        _rms_matmul_body,
        grid=(m // tm, n // tn),
        in_specs=[pl.BlockSpec((tm, k), lambda i, j: (i, 0)),
                  pl.BlockSpec((1, k), lambda i, j: (0, 0)),
                  pl.BlockSpec((k, tn), lambda i, j: (0, j))],
        out_specs=pl.BlockSpec((tm, tn), lambda i, j: (i, j)),
        out_shape=jax.ShapeDtypeStruct((m, n), out_dtype),
        scratch_shapes=[pltpu.VMEM((tm, k), BF16)],
        compiler_params=_params("parallel", "arbitrary"),
        name="rms_matmul",
    )(x, nw.reshape(1, k), w)


def _rms_matmul_hm_body(x_ref, nw_ref, w_ref, o_ref, hm_ref, *, col0):
    x = x_ref[...]
    inv = lax.rsqrt(jnp.mean(x * x, axis=-1, keepdims=True) + RMS_EPS)
    xn = ((x * inv) * nw_ref[...]).astype(BF16)
    acc = jnp.dot(xn, w_ref[...], preferred_element_type=F32)
    o_ref[...] = acc
    dh = hm_ref.shape[3]
    for cg in range(hm_ref.shape[1]):
        hm_ref[0, cg] = acc[:, col0 + cg * dh:col0 + (cg + 1) * dh].astype(hm_ref.dtype)


def _rms_matmul_hm(x, nw, w, b, t, col0, n_cg, dh):
    m, k = x.shape
    n = w.shape[1]
    tm = HM_ROWS
    tps = t // tm
    return pl.pallas_call(
        functools.partial(_rms_matmul_hm_body, col0=col0),
        grid=(m // tm,),
        in_specs=[pl.BlockSpec((tm, k), lambda i: (i, 0)),
                  pl.BlockSpec((1, k), lambda i: (0, 0)),
                  pl.BlockSpec((k, n), lambda i: (0, 0))],
        out_specs=[pl.BlockSpec((tm, n), lambda i: (i, 0)),
                   pl.BlockSpec((1, n_cg, tm, dh), lambda i: (i // tps, 0, i % tps, 0))],
        out_shape=[jax.ShapeDtypeStruct((m, n), F32),
                   jax.ShapeDtypeStruct((b, n_cg, t, dh), BF16)],
        compiler_params=_params("parallel"),
        name="rms_matmul_hm",
    )(x, nw.reshape(1, k), w)


def _matmul_res_body(a_ref, w_ref, r_ref, o_ref):
    o_ref[...] = r_ref[...] + jnp.dot(a_ref[...].astype(BF16), w_ref[...], preferred_element_type=F32)


def _matmul_res(a, w, res):
    m, k = a.shape
    n = w.shape[1]
    tm = min(m, 1024)
    return pl.pallas_call(
        _matmul_res_body,
        grid=(m // tm,),
        in_specs=[pl.BlockSpec((tm, k), lambda i: (i, 0)),
                  pl.BlockSpec((k, n), lambda i: (0, 0)),
                  pl.BlockSpec((tm, n), lambda i: (i, 0))],
        out_specs=pl.BlockSpec((tm, n), lambda i: (i, 0)),
        out_shape=jax.ShapeDtypeStruct((m, n), F32),
        compiler_params=_params("parallel"),
        name="matmul_res",
    )(a, w, res)


def _rms_body(x_ref, nw_ref, o_ref):
    x = x_ref[...]
    inv = lax.rsqrt(jnp.mean(x * x, axis=-1, keepdims=True) + RMS_EPS)
    o_ref[...] = (x * inv) * nw_ref[...]


def _rms(x, nw):
    m, k = x.shape
    tm = min(m, 1024)
    return pl.pallas_call(
        _rms_body,
        grid=(m // tm,),
        in_specs=[pl.BlockSpec((tm, k), lambda i: (i, 0)), pl.BlockSpec((1, k), lambda i: (0, 0))],
        out_specs=pl.BlockSpec((tm, k), lambda i: (i, 0)),
        out_shape=jax.ShapeDtypeStruct((m, k), F32),
        compiler_params=_params("parallel"),
        name="final_rms",
    )(x, nw.reshape(1, k))


def _shifted(x, prev_rows, shift, row):
    nb = prev_rows.shape[1]
    y = pltpu.roll(x, shift, 1)
    for r in range(shift):
        y = jnp.where(row == r, prev_rows[:, nb - shift + r:nb - shift + r + 1], y)
    return y


def _batch_block(b, t):
    bb = max(1, min(b, SEQ_ROWS_PER_STEP // t))
    while b % bb:
        bb -= 1
    return bb


def _causal_conv(x, buf, w):
    width = w.shape[0]
    t = x.shape[1]

    def conv(xs, exact):
        row = lax.broadcasted_iota(jnp.int32, xs.shape, 1)
        y = None
        for i in range(width):
            shift = width - 1 - i
            if shift == 0:
                term = xs
            elif exact:
                term = _shifted(xs, buf, shift, row)
            else:
                term = pltpu.roll(xs, shift, 1)
            term = term * w[i:i + 1]
            y = term if y is None else y + term
        return y

    if t <= CONV_HEAD_ROWS:
        return conv(x, True), None
    return conv(x, False), conv(x[:, :CONV_HEAD_ROWS], True)


def _ffn_act_body(a_ref, g_ref, buf_ref, cw_ref, act_ref, nb_ref):
    a = a_ref[...].astype(F32)
    g = g_ref[...].astype(F32)
    t = a.shape[1]
    y, y_head = _causal_conv(a, buf_ref[...], cw_ref[...])
    act_ref[...] = (_silu(y) * g).astype(act_ref.dtype)
    if y_head is not None:
        act_ref[:, :CONV_HEAD_ROWS, :] = (_silu(y_head) * g[:, :CONV_HEAD_ROWS]).astype(act_ref.dtype)
    nb_ref[...] = a[:, t - (FFN_CONV_W - 1):t, :]


def _ffn_act(h, buf, cw):
    b, t, _ = h.shape
    nj = D_FF // FFN_TC
    bb = _batch_block(b, t)
    return pl.pallas_call(
        _ffn_act_body,
        grid=(b // bb, nj),
        in_specs=[pl.BlockSpec((bb, t, FFN_TC), lambda i, j: (i, 0, j)),
                  pl.BlockSpec((bb, t, FFN_TC), lambda i, j: (i, 0, j + nj)),
                  pl.BlockSpec((bb, FFN_CONV_W - 1, FFN_TC), lambda i, j: (i, 0, j)),
                  pl.BlockSpec((FFN_CONV_W, FFN_TC), lambda i, j: (0, j))],
        out_specs=[pl.BlockSpec((bb, t, FFN_TC), lambda i, j: (i, 0, j)),
                   pl.BlockSpec((bb, FFN_CONV_W - 1, FFN_TC), lambda i, j: (i, 0, j))],
        out_shape=[jax.ShapeDtypeStruct((b, t, D_FF), BF16),
                   jax.ShapeDtypeStruct((b, FFN_CONV_W - 1, D_FF), F32)],
        compiler_params=_params("parallel", "parallel"),
        name="ffn_act",
    )(h, h, buf, cw)


GDN_PRE_TC = 512


def _gdn_pre_body(x_ref, buf_ref, cw_ref, o_ref, nb_ref):
    j = pl.program_id(1)
    x = x_ref[...].astype(F32)
    t = x.shape[1]
    is_q = j < GDN_QK_DIM // GDN_PRE_TC
    is_v = j >= 2 * GDN_QK_DIM // GDN_PRE_TC
    qscale = jnp.where(is_q, GDN_DK ** -0.5, 1.0).astype(F32)

    def finish(y, n_rows):
        y = _silu(y)
        for h in range(GDN_PRE_TC // GDN_DK):
            yh = y[:, :, h * GDN_DK:(h + 1) * GDN_DK]
            inv = lax.rsqrt(jnp.sum(yh * yh, axis=-1, keepdims=True) + L2_EPS)
            o_ref[:, :n_rows, h * GDN_DK:(h + 1) * GDN_DK] = jnp.where(
                is_v, yh, (yh * inv) * qscale).astype(o_ref.dtype)

    y, y_head = _causal_conv(x, buf_ref[...], cw_ref[...])
    finish(y, t)
    if y_head is not None:
        finish(y_head, CONV_HEAD_ROWS)
    nb_ref[...] = x[:, t - (GDN_CONV_W - 1):t, :]


def _gdn_pre(proj, buf, cw):
    b, t, _ = proj.shape
    nj = GDN_CONV_CH // GDN_PRE_TC
    bb = _batch_block(b, t)
    return pl.pallas_call(
        _gdn_pre_body,
        grid=(b // bb, nj),
        in_specs=[pl.BlockSpec((bb, t, GDN_PRE_TC), lambda i, j: (i, 0, j)),
                  pl.BlockSpec((bb, GDN_CONV_W - 1, GDN_PRE_TC), lambda i, j: (i, 0, j)),
                  pl.BlockSpec((GDN_CONV_W, GDN_PRE_TC), lambda i, j: (0, j))],
        out_specs=[pl.BlockSpec((bb, t, GDN_PRE_TC), lambda i, j: (i, 0, j)),
                   pl.BlockSpec((bb, GDN_CONV_W - 1, GDN_PRE_TC), lambda i, j: (i, 0, j))],
        out_shape=[jax.ShapeDtypeStruct((b, t, GDN_CONV_CH), BF16),
                   jax.ShapeDtypeStruct((b, GDN_CONV_W - 1, GDN_CONV_CH), F32)],
        compiler_params=_params("parallel", "parallel"),
        name="gdn_pre",
    )(proj, buf, cw)


def _gdn_gate_body(x_ref, alog_ref, dtb_ref, g_ref, gcum_ref, beta_ref, *, chunk):
    x = x_ref[...]
    z = x + dtb_ref[...]
    softplus = jnp.maximum(z, 0.0) + jnp.log(1.0 + jnp.exp(-jnp.abs(z)))
    g = -jnp.exp(alog_ref[...]) * softplus
    g_ref[...] = g
    row = lax.broadcasted_iota(jnp.int32, x.shape, 1) % chunk
    acc = g
    s = 1
    while s < chunk:
        acc = acc + jnp.where(row >= s, pltpu.roll(acc, s, 1), 0.0)
        s *= 2
    gcum_ref[...] = acc
    beta_ref[...] = _sigmoid(x)


def _gdn_gate(proj, alog_pad, dtb_pad, lane_block):
    b, t, _ = proj.shape
    chunk = min(GDN_CHUNK, t)
    bb = _batch_block(b, t)
    spec = pl.BlockSpec((bb, t, 128), lambda i: (i, 0, 0))
    return pl.pallas_call(
        functools.partial(_gdn_gate_body, chunk=chunk),
        grid=(b // bb,),
        in_specs=[pl.BlockSpec((bb, t, 128), lambda i: (i, 0, lane_block)),
                  pl.BlockSpec((1, 128), lambda i: (0, 0)),
                  pl.BlockSpec((1, 128), lambda i: (0, 0))],
        out_specs=[spec, spec, spec],
        out_shape=[jax.ShapeDtypeStruct((b, t, 128), F32)] * 3,
        compiler_params=_params("parallel"),
        name="gdn_gate",
    )(proj, alog_pad, dtb_pad)


def _unit_lower_inverses(lows, n):
    eye = (lax.broadcasted_iota(jnp.int32, (n, n), 0) == lax.broadcasted_iota(jnp.int32, (n, n), 1)).astype(F32)
    ps = [eye - low for low in lows]
    ms = [_mm(low, low) for low in lows]
    k = 2
    while True:
        ps = [p + _mm(p, m) for p, m in zip(ps, ms)]
        k *= 2
        if k >= n:
            break
        ms = [_mm(m, m) for m in ms]
    return ps


def _gdn_scan_body(q_ref, k_ref, v_ref, z_ref, gc_ref, bc_ref, gr_ref, br_ref, s0_ref, onw_ref, o_ref, s_ref,
                   u_ref, w_ref, qk_ref, qd_ref, kd_ref):
    c = GDN_CHUNK
    hb = s0_ref.shape[1]
    n_chunks = q_ref.shape[1] // c
    s_ref[...] = s0_ref[...]
    ri = lax.broadcasted_iota(jnp.int32, (c, c), 0)
    ci = lax.broadcasted_iota(jnp.int32, (c, c), 1)
    tri = ri >= ci
    stri = ri > ci
    onw = onw_ref[...]

    def qk_slice(ref, r0, qh):
        return ref[0, pl.ds(r0, c), qh * GDN_DK:(qh + 1) * GDN_DK]

    def v_slice(ref, r0, hh):
        return ref[0, pl.ds(r0, c), hh * GDN_DV:(hh + 1) * GDN_DV]

    cpb = max(c_ for c_ in (GDN_PREP_CHUNKS, 4, 2, 1) if n_chunks % c_ == 0)

    def prep(nb, carry):
        items = []
        kk, qk, ks, qs = {}, {}, {}, {}
        for ch in range(cpb):
            n = nb * cpb + ch
            r0 = pl.multiple_of(n * c, c)
            gcol = gc_ref[0, 0, pl.ds(r0, c), :]
            bcol = bc_ref[0, 0, pl.ds(r0, c), :]
            grow = gr_ref[0, 0, n]
            brow = br_ref[0, 0, n]
            for qh in range(hb // 2):
                ks[ch, qh] = qk_slice(k_ref, r0, qh)
                qs[ch, qh] = qk_slice(q_ref, r0, qh)
            for hh in range(hb):
                items.append(dict(ch=ch, hh=hh, r0=r0, v=v_slice(v_ref, r0, hh),
                                  gcb=jnp.broadcast_to(gcol[:, hh:hh + 1], (c, GDN_DK)),
                                  bcb=jnp.broadcast_to(bcol[:, hh:hh + 1], (c, c)),
                                  gr=grow[hh:hh + 1, :], br=brow[hh:hh + 1, :]))
        for key in ks:
            kk[key] = _mm_nt(ks[key], ks[key])
        for key in ks:
            qk[key] = _mm_nt(qs[key], ks[key])
        lows = []
        for it in items:
            key = (it["ch"], it["hh"] // 2)
            it["decay"] = jnp.where(tri, jnp.exp(jnp.where(tri, it["gcb"][:, :c] - it["gr"], 0.0)), 0.0)
            lows.append(jnp.where(stri, (kk[key] * it["bcb"]) * it["decay"], 0.0))
        tinvs = _unit_lower_inverses(lows, c)
        us = [_mm(tinv * it["br"], it["v"]) for tinv, it in zip(tinvs, items)]
        ws = [_mm(tinv * (it["br"] * jnp.exp(it["gr"])), ks[it["ch"], it["hh"] // 2])
              for tinv, it in zip(tinvs, items)]
        for it, u, w in zip(items, us, ws):
            key = (it["ch"], it["hh"] // 2)
            hh, r0, gcb = it["hh"], it["r0"], it["gcb"]
            u_ref[hh, pl.ds(r0, c), :] = u
            w_ref[hh, pl.ds(r0, c), :] = w.astype(BF16)
            qk_ref[hh, pl.ds(r0, c), :] = jnp.where(tri, qk[key] * it["decay"], 0.0).astype(BF16)
            qd_ref[hh, pl.ds(r0, c), :] = (qs[key] * jnp.exp(gcb)).astype(BF16)
            kd_ref[hh, pl.ds(r0, c), :] = (ks[key] * jnp.exp(gcb[c - 1:c, :] - gcb)).astype(BF16)
        return carry

    lax.fori_loop(0, n_chunks // cpb, prep, 0)

    def scan(n, carry):
        r0 = pl.multiple_of(n * c, c)
        g_last = gc_ref[0, 0, pl.ds(r0 + (c - 1), 1), :]
        loaded = []
        for hh in range(hb):
            loaded.append((u_ref[hh, pl.ds(r0, c), :], w_ref[hh, pl.ds(r0, c), :], qk_ref[hh, pl.ds(r0, c), :],
                           qd_ref[hh, pl.ds(r0, c), :], kd_ref[hh, pl.ds(r0, c), :], v_slice(z_ref, r0, hh),
                           s_ref[0, hh]))
        ws_s = [_mm(w, s) for (u, w, qkm, qd, kd, z, s) in loaded]
        qd_s = [_mm(qd, s) for (u, w, qkm, qd, kd, z, s) in loaded]
        v_news = [ld[0] - ws for ld, ws in zip(loaded, ws_s)]
        qk_v = [_mm(ld[2], vn) for ld, vn in zip(loaded, v_news)]
        kd_v = [_mm_tn(ld[4], vn) for ld, vn in zip(loaded, v_news)]
        results = []
        for hh, ld in enumerate(loaded):
            o = qd_s[hh] + qk_v[hh]
            s_new = ld[6] * jnp.exp(g_last[:, hh:hh + 1]) + kd_v[hh]
            inv = lax.rsqrt(jnp.mean(o * o, axis=-1, keepdims=True) + RMS_EPS)
            results.append((((o * inv) * onw) * _silu(ld[5].astype(F32)), s_new))
        for hh, (og, s_new) in enumerate(results):
            s_ref[0, hh] = s_new
            o_ref[0, pl.ds(r0, c), hh * GDN_DV:(hh + 1) * GDN_DV] = og.astype(o_ref.dtype)
        return carry

    lax.fori_loop(0, n_chunks, scan, 0)


def _gdn_scan(qkv, proj, gcol, bcol, grow, brow, s0, layer, onw):
    b, t, _ = qkv.shape
    hb = gcol.shape[3]
    hg = GDN_V_HEADS // hb
    qw = hb // 2 * GDN_DK
    vw = hb * GDN_DV
    n = t // GDN_CHUNK
    return pl.pallas_call(
        _gdn_scan_body,
        grid=(b, hg),
        in_specs=[pl.BlockSpec((1, t, qw), lambda i, j: (i, 0, j)),
                  pl.BlockSpec((1, t, qw), lambda i, j: (i, 0, GDN_QK_DIM // qw + j)),
                  pl.BlockSpec((1, t, vw), lambda i, j: (i, 0, 2 * GDN_QK_DIM // vw + j)),
                  pl.BlockSpec((1, t, vw), lambda i, j: (i, 0, GDN_CONV_CH // vw + j)),
                  pl.BlockSpec((1, 1, t, hb), lambda i, j: (i, j, 0, 0)),
                  pl.BlockSpec((1, 1, t, hb), lambda i, j: (i, j, 0, 0)),
                  pl.BlockSpec((1, 1, n, hb, GDN_CHUNK), lambda i, j: (i, j, 0, 0, 0)),
                  pl.BlockSpec((1, 1, n, hb, GDN_CHUNK), lambda i, j: (i, j, 0, 0, 0)),
                  pl.BlockSpec((None, 1, hb, GDN_DK, GDN_DV), lambda i, j: (layer, i, j, 0, 0)),
                  pl.BlockSpec((1, GDN_DV), lambda i, j: (0, 0))],
        out_specs=[pl.BlockSpec((1, t, vw), lambda i, j: (i, 0, j)),
                   pl.BlockSpec((1, hb, GDN_DK, GDN_DV), lambda i, j: (i, j, 0, 0))],
        out_shape=[jax.ShapeDtypeStruct((b, t, GDN_V_DIM), BF16),
                   jax.ShapeDtypeStruct((b, GDN_V_HEADS, GDN_DK, GDN_DV), F32)],
        scratch_shapes=[pltpu.VMEM((hb, t, GDN_DV), F32),
                        pltpu.VMEM((hb, t, GDN_DK), BF16),
                        pltpu.VMEM((hb, t, GDN_CHUNK), BF16),
                        pltpu.VMEM((hb, t, GDN_DK), BF16),
                        pltpu.VMEM((hb, t, GDN_DK), BF16)],
        compiler_params=_params("parallel", "parallel"),
        name="gdn_scan",
    )(qkv, qkv, qkv, proj, gcol, bcol, grow, brow, s0, onw.reshape(1, GDN_DV))


def _gdn_layer(x, s0, layer, conv_buf, norm_w, w_in, conv_w, alog_pad, dtb_pad, out_norm_w, w_out):
    b, t, d = x.shape
    xf = x.reshape(b * t, d)
    w_main, w_tail = w_in
    proj = _rms_matmul(xf, norm_w, w_main, 2048, BF16).reshape(b, t, -1)
    tail = _rms_matmul(xf, norm_w, w_tail, 128).reshape(b, t, -1)
    qkv, new_buf = _gdn_pre(proj, conv_buf, conv_w)
    _, gcum, beta = _gdn_gate(tail, alog_pad, dtb_pad, 0)
    gcum = gcum[:, :, :GDN_V_HEADS]
    beta = beta[:, :, GDN_V_HEADS:2 * GDN_V_HEADS]
    tp = -(-t // GDN_CHUNK) * GDN_CHUNK
    if tp != t:
        pad = [(0, 0), (0, tp - t), (0, 0)]
        qkv = jnp.pad(qkv, pad)
        proj_z = jnp.pad(proj, pad)
        gcum = jnp.pad(gcum, pad, mode="edge")
        beta = jnp.pad(beta, pad)
    else:
        proj_z = proj
    hb = GDN_V_HEADS if tp == GDN_CHUNK else GDN_HEADS_PER_STEP
    hg = GDN_V_HEADS // hb
    n = tp // GDN_CHUNK
    gcol = gcum.reshape(b, tp, hg, hb).transpose(0, 2, 1, 3)
    bcol = beta.reshape(b, tp, hg, hb).transpose(0, 2, 1, 3)
    grow = gcum.reshape(b, n, GDN_CHUNK, hg, hb).transpose(0, 3, 1, 4, 2)
    brow = beta.reshape(b, n, GDN_CHUNK, hg, hb).transpose(0, 3, 1, 4, 2)
    o, s_new = _gdn_scan(qkv, proj_z, gcol, bcol, grow, brow, s0, layer, out_norm_w)
    o = o[:, :t].reshape(b * t, GDN_V_DIM)
    y = _matmul_res(o, w_out, xf).reshape(b, t, d)
    return y, s_new, new_buf


def _page_gather_body(pt_ref, *refs, pps):
    cmp_refs, o_ref = refs[:pps], refs[pps]
    sub = cmp_refs[0].shape[3]
    for s in range(pps):
        o_ref[0, :, s * sub:(s + 1) * sub, :] = cmp_refs[s][0, 0]


def _page_gather(cache_cmp, layer, page_table):
    b, n_pages = page_table.shape
    _, _, n_cg, sub, flat = cache_cmp.shape
    pps = 16 if n_pages % 16 == 0 else (8 if n_pages % 8 == 0 else 1)
    n_steps = n_pages // pps

    def page_map(s):
        return lambda i, p, pt: (layer, pt[i, p * pps + s], 0, 0, 0)

    grid_spec = pltpu.PrefetchScalarGridSpec(
        num_scalar_prefetch=1,
        grid=(b, n_steps),
        in_specs=[pl.BlockSpec((1, 1, n_cg, sub, flat), page_map(s)) for s in range(pps)],
        out_specs=pl.BlockSpec((1, n_cg, pps * sub, flat), lambda i, p, pt: (i, 0, p, 0)),
    )
    return pl.pallas_call(
        functools.partial(_page_gather_body, pps=pps),
        grid_spec=grid_spec,
        out_shape=jax.ShapeDtypeStruct((b, n_cg, n_pages * sub, flat), cache_cmp.dtype),
        compiler_params=_params("parallel", "arbitrary"),
        name="page_gather",
    )(page_table, *([cache_cmp] * pps))


def _compress_body(x_ref, w1_ref, pe_ref, w2_ref, o_ref, bias_ref, *, n_sub):
    hd = CMP_HIDDEN

    @pl.when(pl.program_id(1) == 0)
    def _():
        pe = pe_ref[0]
        pe0 = jnp.broadcast_to(pe[0:1], (8, pe.shape[1]))
        pe1 = jnp.broadcast_to(pe[1:2], (8, pe.shape[1]))
        bias_ref[:, :hd] = jnp.dot(pe0.astype(BF16), w1_ref[0, :, :hd], preferred_element_type=F32)
        bias_ref[:, hd:] = jnp.dot(pe1.astype(BF16), w1_ref[0, :, hd:], preferred_element_type=F32)

    acc = jnp.dot(x_ref[0, 0], w1_ref[0], preferred_element_type=F32)
    first = acc[:, :hd] + bias_ref[0:1, :hd]
    second = acc[:, hd:] + bias_ref[0:1, hd:]
    hid = _silu(first + pltpu.roll(second, n_sub - 1, 0))
    o_ref[0, 0] = jnp.dot(hid.astype(BF16), w2_ref[0], preferred_element_type=F32)


def _compress(x, n_sub, w1cat, pe_flat, w2):
    b, _, _, flat = x.shape
    dh = flat // CMP_STRIDE
    return pl.pallas_call(
        functools.partial(_compress_body, n_sub=n_sub),
        grid=(2 * NSA_G, b),
        in_specs=[pl.BlockSpec((1, 1, n_sub, flat), lambda j, i: (i, j, 0, 0)),
                  pl.BlockSpec((1, flat, 2 * CMP_HIDDEN), lambda j, i: (j // NSA_G, 0, 0)),
                  pl.BlockSpec((1, 2, flat), lambda j, i: (j // NSA_G, 0, 0)),
                  pl.BlockSpec((1, CMP_HIDDEN, dh), lambda j, i: (j // NSA_G, 0, 0))],
        out_specs=pl.BlockSpec((1, 1, n_sub, dh), lambda j, i: (i, j, 0, 0)),
        out_shape=jax.ShapeDtypeStruct((b, 2 * NSA_G, n_sub, dh), F32),
        scratch_shapes=[pltpu.VMEM((8, 2 * CMP_HIDDEN), F32)],
        compiler_params=_params("arbitrary", "arbitrary"),
        name="nsa_compress",
    )(x, w1cat, pe_flat, w2)


def _inv_or_zero(l):
    return jnp.where(l > 0.0, 1.0 / jnp.where(l > 0.0, l, 1.0), 0.0)


def _attn_branch(q, slopes, t_pos, k, v, kp, ok_fn, may_be_empty=False):
    r, qb, nk = slopes.shape[0], t_pos.shape[1], k.shape[0]
    dist = t_pos - kp
    ok = ok_fn(dist)
    pen = jnp.where(ok, dist.astype(F32), MASKED_DIST)
    s = _mm_nt(q, k).reshape(r, qb, nk) - slopes * pen
    m = jnp.max(s, axis=-1, keepdims=True)
    p = jnp.exp2(s - m)
    if may_be_empty:
        p = jnp.where(ok, p, 0.0)
    inv = _inv_or_zero(jnp.sum(p, axis=-1, keepdims=True))
    o = _mm(p.reshape(r * qb, nk), v) * inv.reshape(r * qb, 1)
    return o, p, inv


def _pick_blocks(p_sum, t_lane, n_cmp, n_blk, nb_rows):
    nq, nc = p_sum.shape
    nb = -(-n_blk // 8) * 8
    jj = lax.broadcasted_iota(jnp.int32, (nb, nc), 0)
    nn = lax.broadcasted_iota(jnp.int32, (nb, nc), 1)
    overlap_t = ((nn * CMP_STRIDE < (jj + 1) * SEL_BLOCK) & (nn * CMP_STRIDE + CMP_LEN > jj * SEL_BLOCK)
                 & (nn < n_cmp) & (jj < n_blk)).astype(F32)
    imp = lax.dot_general(overlap_t, p_sum, (((1,), (1,)), ((), ())), preferred_element_type=F32,
                          precision=lax.Precision.HIGHEST)
    j = lax.broadcasted_iota(jnp.int32, (nb, nq), 0)
    cur = t_lane // SEL_BLOCK
    imp = jnp.where((j == 0) | (j == cur) | (j == cur - 1), FORCE_SCORE, imp)
    imp = jnp.where(j > cur, -FORCE_SCORE, imp)
    imp = jnp.where(j >= n_blk, NEVER, imp)
    rank = jnp.zeros((nb, nq), jnp.int32)
    for jp in range(n_blk):
        row = imp[jp:jp + 1, :]
        rank = rank + ((row > imp) | ((row == imp) & (j > jp))).astype(jnp.int32)
    picked_t = ((rank < SEL_TOPN) & (j < n_blk)).astype(F32)
    if nb_rows > nb:
        picked_t = jnp.concatenate([picked_t, jnp.zeros((nb_rows - nb, nq), F32)], axis=0)
    return picked_t


def _transpose_01(x_t):
    m = x_t.shape[1]
    eye = (lax.broadcasted_iota(jnp.int32, (m, m), 0) == lax.broadcasted_iota(jnp.int32, (m, m), 1))
    return _mm_nt(eye.astype(BF16), x_t)


def _head_rows(x, n_heads):
    dh = x.shape[1] // n_heads
    return jnp.concatenate([x[:, h * dh:(h + 1) * dh] for h in range(n_heads)], axis=0)


def _nsa_attn_body(q_ref, gate_ref, slope_ref, ck_ref, cv_ref, ks_ref, vs_ref, kw_ref, vw_ref, o_ref, osel_ref,
                   *, nsb, **kw):
    refs = (q_ref, gate_ref, slope_ref, ck_ref, cv_ref, ks_ref, vs_ref, kw_ref, vw_ref, o_ref, osel_ref)
    if nsb == 1:
        _nsa_attn_block(0, pl.program_id(2), *refs, **kw)
    else:
        def one(sub, carry):
            _nsa_attn_block(sub, pl.program_id(2) * nsb + sub, *refs, **kw)
            return carry

        lax.fori_loop(0, nsb, one, 0)


def _nsa_attn_block(sub, i, q_ref, gate_ref, slope_ref, ck_ref, cv_ref, ks_ref, vs_ref, kw_ref, vw_ref, o_ref,
                    osel_ref, *, qb, p_len, n_cmp, n_blk, w0, sel_classes, kw_len):
    r = NSA_R
    rows = r * qb
    rq = sub * qb if isinstance(sub, int) else pl.multiple_of(sub * qb, qb)
    q = _head_rows(q_ref[0, pl.ds(rq, qb), :].astype(F32), r)
    q = (q * (NSA_DH ** -0.5 * LOG2E)).astype(BF16)
    slopes = slope_ref[0] * LOG2E
    q0 = p_len + i * qb
    t_pos = q0 + lax.broadcasted_iota(jnp.int32, (1, qb, 1), 1)
    branch = functools.partial(_attn_branch, q, slopes, t_pos)

    tw = kw_ref.shape[2]
    if tw == kw_len:
        start = 0
    else:
        start = pl.multiple_of(jnp.clip(q0 + qb - w0 - kw_len, 0, tw - kw_len), SEL_BLOCK)
    kp_win = w0 + start + lax.broadcasted_iota(jnp.int32, (1, 1, kw_len), 2)
    o_win, _, _ = branch(kw_ref[0, 0, pl.ds(start, kw_len), :], vw_ref[0, 0, pl.ds(start, kw_len), :], kp_win,
                         lambda dist: (dist >= 0) & (dist < WINDOW) & (kp_win >= 0))

    nc = ck_ref.shape[2]
    n_idx = lax.broadcasted_iota(jnp.int32, (1, 1, nc), 2)
    o_cmp, p_cmp, inv_cmp = branch(ck_ref[0, 0], cv_ref[0, 0], n_idx * CMP_STRIDE + (CMP_LEN - 1),
                                   lambda dist: (dist >= 0) & (n_idx < n_cmp), may_be_empty=True)
    p_sum = jnp.sum(p_cmp * inv_cmp, axis=0)
    nb8 = -(-n_blk // 16) * 16
    picked_t = _pick_blocks(p_sum, q0 + lax.broadcasted_iota(jnp.int32, (1, qb), 1), n_cmp, n_blk, nb8)
    picked = _transpose_01(picked_t)

    def sel_branch(nk):
        blk_of_key = lax.broadcasted_iota(jnp.int32, (nb8, nk), 1) // SEL_BLOCK
        expand = (blk_of_key == lax.broadcasted_iota(jnp.int32, (nb8, nk), 0)).astype(BF16)
        key_picked = (_mm(picked, expand) > 0.5).reshape(1, qb, nk)
        o, _, _ = branch(ks_ref[0, 0, 0:nk, :], vs_ref[0, 0, 0:nk, :],
                         lax.broadcasted_iota(jnp.int32, (1, 1, nk), 2),
                         lambda dist: key_picked & (dist >= 0))
        osel_ref[...] = o

    if len(sel_classes) == 1:
        sel_branch(sel_classes[0])
    else:
        need = q0 + qb
        prev = 0
        for nk in sel_classes:
            pl.when((need > prev) & (need <= nk))(functools.partial(sel_branch, nk))
            prev = nk

    gates = _sigmoid(gate_ref[0, 0, :, pl.ds(rq, qb), :]).reshape(rows, 3)
    o = gates[:, 0:1] * o_cmp + gates[:, 1:2] * osel_ref[...] + gates[:, 2:3] * o_win
    for h in range(r):
        o_ref[0, pl.ds(rq, qb), h * NSA_DH:(h + 1) * NSA_DH] = o[h * qb:(h + 1) * qb].astype(o_ref.dtype)


def _nsa_attn(q, gates_t, slopes, cmp_kv, sel_arr, sel_k0, sel_v0, win_arr, win_k0, win_v0,
              *, p_len, n_cmp, n_blk, w0):
    b, tq, _ = q.shape
    g, r, dh = NSA_G, NSA_R, NSA_DH
    qb = NSA_Q_BLOCK if tq % NSA_Q_BLOCK == 0 else tq
    nc = cmp_kv.shape[2]
    tk = sel_arr.shape[2]
    tw = win_arr.shape[2]
    if tq == qb or tk % NSA_SEL_CLASS != 0:
        sel_classes = (tk,)
    else:
        sel_classes = tuple(range(NSA_SEL_CLASS, tk + 1, NSA_SEL_CLASS))
    kw_len = min(tw, -(-(WINDOW - 1 + qb) // KEY_CHUNK) * KEY_CHUNK)
    nsb = NSA_Q_BLOCKS_PER_STEP if (tq // qb) % NSA_Q_BLOCKS_PER_STEP == 0 else 1
    body = functools.partial(_nsa_attn_body, nsb=nsb, qb=qb, p_len=p_len, n_cmp=n_cmp, n_blk=n_blk, w0=w0,
                             sel_classes=sel_classes, kw_len=kw_len)
    qs = nsb * qb
    return pl.pallas_call(
        body,
        grid=(b, g, tq // qs),
        in_specs=[pl.BlockSpec((1, qs, r * dh), lambda bi, gi, i: (bi, i, gi)),
                  pl.BlockSpec((1, 1, r, qs, 3), lambda bi, gi, i: (bi, gi, 0, i, 0)),
                  pl.BlockSpec((1, r, 1, 1), lambda bi, gi, i: (gi, 0, 0, 0)),
                  pl.BlockSpec((1, 1, nc, dh), lambda bi, gi, i: (bi, gi, 0, 0)),
                  pl.BlockSpec((1, 1, nc, dh), lambda bi, gi, i: (bi, NSA_G + gi, 0, 0)),
                  pl.BlockSpec((1, 1, tk, dh), lambda bi, gi, i: (bi, sel_k0 + gi, 0, 0)),
                  pl.BlockSpec((1, 1, tk, dh), lambda bi, gi, i: (bi, sel_v0 + gi, 0, 0)),
                  pl.BlockSpec((1, 1, tw, dh), lambda bi, gi, i: (bi, win_k0 + gi, 0, 0)),
                  pl.BlockSpec((1, 1, tw, dh), lambda bi, gi, i: (bi, win_v0 + gi, 0, 0))],
        out_specs=pl.BlockSpec((1, qs, r * dh), lambda bi, gi, i: (bi, i, gi)),
        out_shape=jax.ShapeDtypeStruct((b, tq, g * r * dh), BF16),
        scratch_shapes=[pltpu.VMEM((r * qb, dh), F32)],
        compiler_params=_params("parallel", "parallel", "arbitrary"),
        name="nsa_attn",
    )(q, gates_t, slopes, cmp_kv, cmp_kv, sel_arr, sel_arr, win_arr, win_arr)


def _nsa_paged_body(pt_ref, *refs, tq, p_len, n_cmp, n_blk, w0, pps, n_steps):
    k_refs, v_refs = refs[:pps], refs[pps:2 * pps]
    (q_ref, gate_ref, slope_ref, cmp_ref, new_ref, win_ref, o_ref,
     qbd_ref, pen_ref, m_ref, l_ref, acc_ref, ocmp_ref) = refs[2 * pps:]
    p = pl.program_id(1)
    g_n, r = NSA_G, NSA_R
    dh = NSA_DH
    rg = r * tq
    rows = g_n * rg
    page = k_refs[0].shape[5]
    chunk = pps * page
    n_keys = pen_ref.shape[1]
    t_pos = p_len + lax.broadcasted_iota(jnp.int32, (1, tq, 1), 1)

    @pl.when(p == 0)
    def _():
        qbd_ref[...] = jnp.zeros(qbd_ref.shape, qbd_ref.dtype)
        nc = cmp_ref.shape[2]
        n_idx = lax.broadcasted_iota(jnp.int32, (1, 1, nc), 2)
        p_sums = []
        for g in range(g_n):
            q = _head_rows(q_ref[0, :, g * r * dh:(g + 1) * r * dh].astype(F32), r)
            q = (q * (dh ** -0.5 * LOG2E)).astype(BF16)
            qbd_ref[g * rg:(g + 1) * rg, g * dh:(g + 1) * dh] = q
            o_cmp, p_cmp, inv_cmp = _attn_branch(
                q, slope_ref[g] * LOG2E, t_pos, cmp_ref[0, g], cmp_ref[0, g_n + g],
                n_idx * CMP_STRIDE + (CMP_LEN - 1), lambda dist: (dist >= 0) & (n_idx < n_cmp), may_be_empty=True)
            ocmp_ref[g * rg:(g + 1) * rg, :] = o_cmp
            p_sums.append(jnp.sum(p_cmp * inv_cmp, axis=0))
        nbp = -(-n_blk // 128) * 128
        nq = g_n * tq
        t_lane = p_len + lax.broadcasted_iota(jnp.int32, (1, nq), 1) % tq
        picked_t = _pick_blocks(jnp.concatenate(p_sums, axis=0), t_lane, n_cmp, n_blk, nbp)
        picked = _transpose_01(picked_t.astype(BF16))
        picked_rows = jnp.concatenate([picked[g * tq:(g + 1) * tq] for g in range(g_n) for _ in range(r)], axis=0)
        slope_rows = jnp.broadcast_to(slope_ref[...] * LOG2E, (g_n, r, tq, 1)).reshape(rows, 1)
        t_rows = p_len + lax.broadcasted_iota(jnp.int32, (rows, 1), 0) % tq

        def fill(k0, width):
            kp = k0 + lax.broadcasted_iota(jnp.int32, (1, width), 1)
            blk = lax.broadcasted_iota(jnp.int32, (nbp, width), 0)
            expand = ((k0 + lax.broadcasted_iota(jnp.int32, (nbp, width), 1)) // SEL_BLOCK == blk).astype(BF16)
            key_picked = _mm(picked_rows, expand) > 0.5
            dist = t_rows - kp
            return slope_rows * jnp.where(key_picked & (dist >= 0), dist.astype(F32), MASKED_DIST)

        def fill_chunk(c, carry):
            k0 = pl.multiple_of(c * chunk, chunk)
            pen_ref[:, pl.ds(k0, chunk)] = fill(k0, chunk)
            return carry

        lax.fori_loop(0, n_steps, fill_chunk, 0)
        pen_ref[:, n_steps * chunk:n_keys] = fill(n_steps * chunk, n_keys - n_steps * chunk)
        m_ref[...] = jnp.full(m_ref.shape, NEVER, F32)
        l_ref[...] = jnp.zeros(l_ref.shape, F32)
        acc_ref[...] = jnp.zeros(acc_ref.shape, F32)

    def online_update(kts, vts, k0):
        width = len(kts) * page
        qbd = qbd_ref[...]
        s = jnp.concatenate([jnp.dot(qbd, kt.astype(BF16), preferred_element_type=F32) for kt in kts], axis=1)
        s = s - pen_ref[:, pl.ds(k0, width)]
        m_prev = m_ref[...]
        m_new = jnp.maximum(m_prev, jnp.max(s, axis=-1, keepdims=True))
        alpha = jnp.exp2(m_prev - m_new)
        prob = jnp.exp2(s - m_new)
        l_ref[...] = alpha * l_ref[...] + jnp.sum(prob, axis=-1, keepdims=True)
        pv = [_mm_nt(prob[:, i * page:(i + 1) * page], vt) for i, vt in enumerate(vts)]
        acc_ref[...] = alpha * acc_ref[...] + sum(pv[1:], pv[0])
        m_ref[...] = m_new

    @pl.when(p < n_steps)
    def _():
        online_update([ref[0, 0, 0].reshape(g_n * dh, page) for ref in k_refs],
                      [ref[0, 0, 0].reshape(g_n * dh, page) for ref in v_refs], pl.multiple_of(p * chunk, chunk))

    @pl.when(p == n_steps)
    def _():
        online_update([new_ref[0, 0].reshape(g_n * dh, page)], [new_ref[0, 1].reshape(g_n * dh, page)],
                      n_steps * chunk)
        acc = acc_ref[...] * _inv_or_zero(l_ref[...])
        tw = win_ref.shape[2]
        kp_win = w0 + lax.broadcasted_iota(jnp.int32, (1, 1, tw), 2)
        for g in range(g_n):
            q = qbd_ref[g * rg:(g + 1) * rg, g * dh:(g + 1) * dh]
            o_win, _, _ = _attn_branch(q, slope_ref[g] * LOG2E, t_pos, win_ref[0, g], win_ref[0, g_n + g], kp_win,
                                       lambda dist: (dist >= 0) & (dist < WINDOW) & (kp_win >= 0))
            gates = _sigmoid(gate_ref[0, g]).reshape(rg, 3)
            o = (gates[:, 0:1] * ocmp_ref[g * rg:(g + 1) * rg, :]
                 + gates[:, 1:2] * acc[g * rg:(g + 1) * rg, g * dh:(g + 1) * dh] + gates[:, 2:3] * o_win)
            for h in range(r):
                o_ref[0, :, (g * r + h) * dh:(g * r + h + 1) * dh] = o[h * tq:(h + 1) * tq].astype(o_ref.dtype)


def _nsa_paged(q, gates_t, slopes, cmp_kv, cache_nt, layer, page_table, new_t, win_arr,
               *, p_len, n_cmp, n_blk, w0):
    b, tq, _ = q.shape
    g, r, dh = NSA_G, NSA_R, NSA_DH
    n_pages = page_table.shape[1]
    page = cache_nt.shape[5]
    pps = 8 if n_pages % 8 == 0 else (4 if n_pages % 4 == 0 else 1)
    n_steps = n_pages // pps
    nc = cmp_kv.shape[2]
    tw = win_arr.shape[2]
    rows = g * r * tq
    n_keys = (n_pages + 1) * page

    def page_map(c, s):
        return lambda i, p, pt: (layer, pt[i, jnp.minimum(p, n_steps - 1) * pps + s], c, 0, 0, 0)

    const = lambda i, p, pt: (i, 0, 0, 0, 0)
    grid_spec = pltpu.PrefetchScalarGridSpec(
        num_scalar_prefetch=1,
        grid=(b, n_steps + 1),
        in_specs=[pl.BlockSpec((1, 1, 1, g, dh, page), page_map(2, s)) for s in range(pps)]
        + [pl.BlockSpec((1, 1, 1, g, dh, page), page_map(3, s)) for s in range(pps)]
        + [pl.BlockSpec((1, tq, g * r * dh), lambda i, p, pt: (i, 0, 0)),
           pl.BlockSpec((1, g, r, tq, 3), const),
           pl.BlockSpec((g, r, 1, 1), lambda i, p, pt: (0, 0, 0, 0)),
           pl.BlockSpec((1, 2 * g, nc, dh), lambda i, p, pt: (i, 0, 0, 0)),
           pl.BlockSpec((1, 2, g, dh, page), const),
           pl.BlockSpec((1, 2 * g, tw, dh), lambda i, p, pt: (i, 0, 0, 0))],
        out_specs=pl.BlockSpec((1, tq, g * r * dh), lambda i, p, pt: (i, 0, 0)),
        scratch_shapes=[pltpu.VMEM((rows, g * dh), BF16),
                        pltpu.VMEM((rows, n_keys), F32),
                        pltpu.VMEM((rows, 1), F32),
                        pltpu.VMEM((rows, 1), F32),
                        pltpu.VMEM((rows, g * dh), F32),
                        pltpu.VMEM((rows, dh), F32)],
    )
    body = functools.partial(_nsa_paged_body, tq=tq, p_len=p_len, n_cmp=n_cmp, n_blk=n_blk, w0=w0, pps=pps,
                             n_steps=n_steps)
    return pl.pallas_call(
        body,
        grid_spec=grid_spec,
        out_shape=jax.ShapeDtypeStruct((b, tq, g * r * dh), BF16),
        compiler_params=_params("parallel", "arbitrary"),
        name="nsa_paged",
    )(page_table, *([cache_nt] * (2 * pps)), q, gates_t, slopes, cmp_kv, new_t, win_arr)


def _nsa_layer(x, cache_hm, layer, page_table, win_buf, norm_w, w_in, w1cat, pe_flat, w2, w_out, slopes):
    b, t, d = x.shape
    g, r, dh = NSA_G, NSA_R, NSA_DH
    xf = x.reshape(b * t, d)
    w_q, w_kv = w_in
    q = _rms_matmul(xf, norm_w, w_q, NSA_Q_DIM, BF16).reshape(b, t, NSA_Q_DIM)
    if cache_hm is None and t % HM_ROWS == 0:
        proj, kv_hm = _rms_matmul_hm(xf, norm_w, w_kv, b, t, 2 * NSA_KV_DIM, 4 * g, dh)
        proj = proj.reshape(b, t, -1)
    else:
        proj = _rms_matmul(xf, norm_w, w_kv, w_kv.shape[1]).reshape(b, t, -1)
        kv_hm = None
    kv = proj[..., :6 * NSA_KV_DIM]
    gates_t = proj[..., 6 * NSA_KV_DIM:6 * NSA_KV_DIM + 3 * NSA_HEADS]
    gates_t = gates_t.reshape(b, t, g, r, 3).transpose(0, 2, 3, 1, 4)
    kv6 = kv.reshape(b, t, 6, g, dh)
    new_rows = kv6[:, :, :4]
    if cache_hm is None:
        p_len = 0
        tk = t
        assert t % CMP_STRIDE == 0
        cmp_x = kv6[:, :, :2].astype(BF16).transpose(0, 2, 3, 1, 4).reshape(b, 2 * g, t // CMP_STRIDE,
                                                                           CMP_STRIDE * dh)
        if kv_hm is None:
            kv_hm = kv6[:, :, 2:].astype(BF16).transpose(0, 2, 3, 1, 4).reshape(b, 4 * g, t, dh)
        new_win = kv6[:, t - min(WINDOW, t):, 4:]
    else:
        cache_cmp, cache_nt = cache_hm
        n_pages = page_table.shape[1]
        page = cache_nt.shape[5]
        p_len = n_pages * page
        tk = p_len + t
        assert p_len % CMP_STRIDE == 0 and t < CMP_STRIDE and t <= page
        cmp_x = _page_gather(cache_cmp, layer, page_table)
        new_t = jnp.pad(kv6[:, :, 2:4].transpose(0, 2, 3, 4, 1), [(0, 0)] * 4 + [(0, page - t)])
        win_all = jnp.concatenate([win_buf, kv6[:, :, 4:]], axis=1)
        wl = win_all.shape[1]
        wlp = -(-wl // KEY_CHUNK) * KEY_CHUNK
        win_arr = jnp.pad(win_all.astype(BF16), [(0, 0), (0, wlp - wl), (0, 0), (0, 0), (0, 0)])
        win_arr = win_arr.transpose(0, 2, 3, 1, 4).reshape(b, 2 * g, wlp, dh)
        new_win = win_all[:, wl - min(WINDOW, tk):]
    n_sub = tk // CMP_STRIDE
    n_cmp = n_sub - 1
    n_blk = -(-tk // SEL_BLOCK)
    cmp_kv = _compress(cmp_x, n_sub, w1cat, pe_flat, w2)
    if cache_hm is None:
        o = _nsa_attn(q, gates_t, slopes, cmp_kv, kv_hm, 0, g, kv_hm, 2 * g, 3 * g,
                      p_len=0, n_cmp=n_cmp, n_blk=n_blk, w0=0)
    else:
        o = _nsa_paged(q, gates_t, slopes, cmp_kv, cache_nt, layer, page_table, new_t, win_arr,
                       p_len=p_len, n_cmp=n_cmp, n_blk=n_blk, w0=p_len - win_buf.shape[1])
    y = _matmul_res(o.reshape(b * t, NSA_Q_DIM), w_out, xf).reshape(b, t, d)
    return y, new_rows, new_win


def _ffn_layer(x, buf, norm_w, w_up, conv_w, w_down):
    b, t, d = x.shape
    xf = x.reshape(b * t, d)
    h = _rms_matmul(xf, norm_w, w_up, D_FF, BF16).reshape(b, t, -1)
    act, new_buf = _ffn_act(h, buf, conv_w)
    y = _matmul_res(act.reshape(b * t, D_FF), w_down, xf).reshape(b, t, d)
    return y, new_buf


def _pad_cols(w, n):
    return jnp.pad(w, [(0, 0)] * (w.ndim - 1) + [(0, n - w.shape[-1])])


def _trunk(x, cache_hm, page_table, nsa_win, gdn_state, gdn_conv, ffn_conv, wts):
    depth = wts["ffn_w_up"].shape[0]
    rows_l, win_l, s_l, gconv_l, fconv_l = [], [], [], [], []
    for i in range(depth):
        j = i // 2
        if i % 2 == 0:
            x, s_new, cb = _gdn_layer(x, gdn_state, j, gdn_conv[j], wts["gdn_norm"][j],
                                      (wts["gdn_w_main"][j], wts["gdn_w_tail"][j]),
                                      wts["gdn_conv_w"][j], wts["gdn_a_log"][j], wts["gdn_dt_bias"][j],
                                      wts["gdn_out_norm"][j], wts["gdn_w_out"][j])
            s_l.append(s_new)
            gconv_l.append(cb)
        else:
            x, rows, wb = _nsa_layer(x, cache_hm, j, page_table, None if nsa_win is None else nsa_win[j],
                                     wts["nsa_norm"][j], (wts["nsa_w_q"][j], wts["nsa_w_kv"][j]),
                                     wts["nsa_w1cat"][j],
                                     wts["nsa_pe_flat"][j], wts["nsa_w2"][j], wts["nsa_w_out"][j],
                                     wts["slopes"])
            rows_l.append(rows)
            win_l.append(wb)
        x, fb = _ffn_layer(x, ffn_conv[i], wts["ffn_norm"][i], wts["ffn_w_up"][i], wts["ffn_conv_w"][i],
                           wts["ffn_w_down"][i])
        fconv_l.append(fb)
    b, t, d = x.shape
    y = _rms(x.reshape(b * t, d), wts["final_norm"]).reshape(b, t, d)
    return (y, jnp.stack(rows_l), jnp.stack(win_l), jnp.stack(s_l), jnp.stack(gconv_l), jnp.stack(fconv_l))


def kernel(x_prompt, x_sample, cache_nsa_kv, cache_nsa_win, state_gdn_s, state_gdn_conv, state_ffn_conv,
           page_table, gdn_norm, gdn_w_in, gdn_conv_w, gdn_a_log, gdn_dt_bias, gdn_out_norm, gdn_w_out,
           nsa_norm, nsa_w_in, nsa_cmp_pe, nsa_cmp_w1, nsa_cmp_w2, nsa_w_out,
           ffn_norm, ffn_w_up, ffn_conv_w, ffn_w_down, final_norm):
    n_gdn = gdn_w_in.shape[0]
    n_nsa = nsa_w_in.shape[0]
    depth = ffn_w_up.shape[0]
    bp = x_prompt.shape[0]
    gdn_main = GDN_CONV_CH + GDN_V_DIM
    nsa_cols = -(-(nsa_w_in.shape[2] - NSA_Q_DIM) // 896) * 896
    w1 = nsa_cmp_w1.reshape(n_nsa, 2, 2, CMP_STRIDE * NSA_DH, CMP_HIDDEN)
    w1cat = jnp.concatenate([w1[:, :, 0], w1[:, :, 1]], axis=-1).astype(BF16)
    slopes = 2.0 ** (-8.0 * jnp.arange(1, NSA_HEADS + 1, dtype=F32) / NSA_HEADS)
    wts = {
        "gdn_norm": gdn_norm,
        "gdn_w_main": gdn_w_in[:, :, :gdn_main].astype(BF16),
        "gdn_w_tail": _pad_cols(gdn_w_in[:, :, gdn_main:], 128).astype(BF16),
        "gdn_conv_w": gdn_conv_w,
        "gdn_a_log": _pad_cols(gdn_a_log, 128).reshape(n_gdn, 1, 128),
        "gdn_dt_bias": _pad_cols(gdn_dt_bias, 128).reshape(n_gdn, 1, 128),
        "gdn_out_norm": gdn_out_norm,
        "gdn_w_out": gdn_w_out.astype(BF16),
        "nsa_norm": nsa_norm,
        "nsa_w_q": nsa_w_in[:, :, :NSA_Q_DIM].astype(BF16),
        "nsa_w_kv": _pad_cols(nsa_w_in[:, :, NSA_Q_DIM:], nsa_cols).astype(BF16),
        "nsa_w1cat": w1cat,
        "nsa_pe_flat": nsa_cmp_pe.reshape(n_nsa, 2, 2, CMP_STRIDE * NSA_DH),
        "nsa_w2": nsa_cmp_w2.astype(BF16),
        "nsa_w_out": nsa_w_out.astype(BF16),
        "slopes": slopes.reshape(NSA_G, NSA_R, 1, 1),
        "ffn_norm": ffn_norm,
        "ffn_w_up": ffn_w_up.astype(BF16),
        "ffn_conv_w": ffn_conv_w,
        "ffn_w_down": ffn_w_down.astype(BF16),
        "final_norm": final_norm,
    }
    n_l, pool, page = cache_nsa_kv.shape[:3]
    cache_cmp = cache_nsa_kv[:, :, :, :2].astype(BF16).transpose(0, 1, 3, 4, 2, 5)
    cache_hm = (cache_cmp.reshape(n_l, pool, 2 * NSA_G, page // CMP_STRIDE, CMP_STRIDE * NSA_DH),
                cache_nsa_kv.transpose(0, 1, 3, 4, 5, 2))

    zeros = functools.partial(jnp.zeros, dtype=F32)
    prompt = _trunk(x_prompt, None, None, None,
                    zeros((n_gdn, bp, GDN_V_HEADS, GDN_DK, GDN_DV)),
                    zeros((n_gdn, bp, GDN_CONV_W - 1, GDN_CONV_CH)),
                    zeros((depth, bp, FFN_CONV_W - 1, D_FF)), wts)
    sample = _trunk(x_sample, cache_hm, page_table, cache_nsa_win, state_gdn_s, state_gdn_conv, state_ffn_conv, wts)
    out = []
    for p, s in zip(prompt, sample):
        out.extend([p, s])
    return tuple(out)
```

```python
import functools

import jax
import jax.numpy as jnp
from jax import lax
from jax.experimental import pallas as pl
from jax.experimental.pallas import tpu as pltpu

F32 = jnp.float32
BF16 = jnp.bfloat16

RMS_EPS = 1e-6
L2_EPS = 1e-6
NEG_INF = -1e30
FORCE_SCORE = 1e9
NEVER = -3e38
MASKED_DIST = 1e30
LOG2E = 1.4426950408889634

GDN_QK_HEADS = 8
GDN_V_HEADS = 16
GDN_DK = 128
GDN_DV = 128
GDN_QK_DIM = GDN_QK_HEADS * GDN_DK
GDN_V_DIM = GDN_V_HEADS * GDN_DV
GDN_CONV_CH = 2 * GDN_QK_DIM + GDN_V_DIM
GDN_CONV_W = 4
GDN_CHUNK = 64
GDN_HEADS_PER_STEP = 4
GDN_PREP_CHUNKS = 8
NSA_HEADS = 16
NSA_G = 4
NSA_R = 4
NSA_DH = 64
NSA_Q_DIM = NSA_HEADS * NSA_DH
NSA_KV_DIM = NSA_G * NSA_DH
CMP_STRIDE = 16
CMP_LEN = 32
CMP_HIDDEN = 256
SEL_BLOCK = 64
SEL_TOPN = 8
WINDOW = 512
NSA_Q_BLOCK = 64
NSA_Q_BLOCKS_PER_STEP = 8
KEY_CHUNK = 128
NSA_SEL_CLASS = 512
D_FF = 2816
FFN_CONV_W = 3
FFN_TC = 256
SEQ_ROWS_PER_STEP = 2048
CONV_HEAD_ROWS = 16
HM_ROWS = 512

VMEM_LIMIT = 52 * 1024 * 1024


def _params(*sem):
    return pltpu.CompilerParams(dimension_semantics=sem, vmem_limit_bytes=VMEM_LIMIT)


def _mm(a, b):
    return jnp.dot(a.astype(BF16), b.astype(BF16), preferred_element_type=F32)


def _mm_nt(a, b):
    return lax.dot_general(a.astype(BF16), b.astype(BF16), (((1,), (1,)), ((), ())),
                           preferred_element_type=F32)


def _mm_tn(a, b):
    return lax.dot_general(a.astype(BF16), b.astype(BF16), (((0,), (0,)), ((), ())),
                           preferred_element_type=F32)


def _sigmoid(x):
    return 0.5 * jnp.tanh(0.5 * x) + 0.5


def _silu(x):
    h = 0.5 * x
    return h * jnp.tanh(h) + h


def _rms_matmul_body(x_ref, nw_ref, w_ref, o_ref, xn_ref):
    @pl.when(pl.program_id(1) == 0)
    def _():
        x = x_ref[...]
        inv = lax.rsqrt(jnp.mean(x * x, axis=-1, keepdims=True) + RMS_EPS)
        xn_ref[...] = ((x * inv) * nw_ref[...]).astype(BF16)

    o_ref[...] = jnp.dot(xn_ref[...], w_ref[...], preferred_element_type=F32).astype(o_ref.dtype)


def _rms_matmul(x, nw, w, tn, out_dtype=F32):
    m, k = x.shape
    n = w.shape[1]
    tm = min(m, 1024)
    return pl.pallas_call(
        _rms_matmul_body,
        grid=(m // tm, n // tn),
        in_specs=[pl.BlockSpec((tm, k), lambda i, j: (i, 0)),
                  pl.BlockSpec((1, k), lambda i, j: (0, 0)),
                  pl.BlockSpec((k, tn), lambda i, j: (0, j))],
        out_specs=pl.BlockSpec((tm, tn), lambda i, j: (i, j)),
        out_shape=jax.ShapeDtypeStruct((m, n), out_dtype),
        scratch_shapes=[pltpu.VMEM((tm, k), BF16)],
        compiler_params=_params("parallel", "arbitrary"),
        name="rms_matmul",
    )(x, nw.reshape(1, k), w)


def _rms_matmul_hm_body(x_ref, nw_ref, w_ref, o_ref, hm_ref, *, col0):
    x = x_ref[...]
    inv = lax.rsqrt(jnp.mean(x * x, axis=-1, keepdims=True) + RMS_EPS)
    xn = ((x * inv) * nw_ref[...]).astype(BF16)
    acc = jnp.dot(xn, w_ref[...], preferred_element_type=F32)
    o_ref[...] = acc
    dh = hm_ref.shape[3]
    for cg in range(hm_ref.shape[1]):
        hm_ref[0, cg] = acc[:, col0 + cg * dh:col0 + (cg + 1) * dh].astype(hm_ref.dtype)


def _rms_matmul_hm(x, nw, w, b, t, col0, n_cg, dh):
    m, k = x.shape
    n = w.shape[1]
    tm = HM_ROWS
    tps = t // tm
    return pl.pallas_call(
        functools.partial(_rms_matmul_hm_body, col0=col0),
        grid=(m // tm,),
        in_specs=[pl.BlockSpec((tm, k), lambda i: (i, 0)),
                  pl.BlockSpec((1, k), lambda i: (0, 0)),
                  pl.BlockSpec((k, n), lambda i: (0, 0))],
        out_specs=[pl.BlockSpec((tm, n), lambda i: (i, 0)),
                   pl.BlockSpec((1, n_cg, tm, dh), lambda i: (i // tps, 0, i % tps, 0))],
        out_shape=[jax.ShapeDtypeStruct((m, n), F32),
                   jax.ShapeDtypeStruct((b, n_cg, t, dh), BF16)],
        compiler_params=_params("parallel"),
        name="rms_matmul_hm",
    )(x, nw.reshape(1, k), w)


def _matmul_res_body(a_ref, w_ref, r_ref, o_ref):
    o_ref[...] = r_ref[...] + jnp.dot(a_ref[...].astype(BF16), w_ref[...], preferred_element_type=F32)


def _matmul_res(a, w, res):
    m, k = a.shape
    n = w.shape[1]
    tm = min(m, 1024)
    return pl.pallas_call(
        _matmul_res_body,
        grid=(m // tm,),
        in_specs=[pl.BlockSpec((tm, k), lambda i: (i, 0)),
                  pl.BlockSpec((k, n), lambda i: (0, 0)),
                  pl.BlockSpec((tm, n), lambda i: (i, 0))],
        out_specs=pl.BlockSpec((tm, n), lambda i: (i, 0)),
        out_shape=jax.ShapeDtypeStruct((m, n), F32),
        compiler_params=_params("parallel"),
        name="matmul_res",
    )(a, w, res)


def _rms_body(x_ref, nw_ref, o_ref):
    x = x_ref[...]
    inv = lax.rsqrt(jnp.mean(x * x, axis=-1, keepdims=True) + RMS_EPS)
    o_ref[...] = (x * inv) * nw_ref[...]


def _rms(x, nw):
    m, k = x.shape
    tm = min(m, 1024)
    return pl.pallas_call(
        _rms_body,
        grid=(m // tm,),
        in_specs=[pl.BlockSpec((tm, k), lambda i: (i, 0)), pl.BlockSpec((1, k), lambda i: (0, 0))],
        out_specs=pl.BlockSpec((tm, k), lambda i: (i, 0)),
        out_shape=jax.ShapeDtypeStruct((m, k), F32),
        compiler_params=_params("parallel"),
        name="final_rms",
    )(x, nw.reshape(1, k))


def _shifted(x, prev_rows, shift, row):
    nb = prev_rows.shape[1]
    y = pltpu.roll(x, shift, 1)
    for r in range(shift):
        y = jnp.where(row == r, prev_rows[:, nb - shift + r:nb - shift + r + 1], y)
    return y


def _batch_block(b, t):
    bb = max(1, min(b, SEQ_ROWS_PER_STEP // t))
    while b % bb:
        bb -= 1
    return bb


def _causal_conv(x, buf, w):
    width = w.shape[0]
    t = x.shape[1]

    def conv(xs, exact):
        row = lax.broadcasted_iota(jnp.int32, xs.shape, 1)
        y = None
        for i in range(width):
            shift = width - 1 - i
            if shift == 0:
                term = xs
            elif exact:
                term = _shifted(xs, buf, shift, row)
            else:
                term = pltpu.roll(xs, shift, 1)
            term = term * w[i:i + 1]
            y = term if y is None else y + term
        return y

    if t <= CONV_HEAD_ROWS:
        return conv(x, True), None
    return conv(x, False), conv(x[:, :CONV_HEAD_ROWS], True)


def _ffn_act_body(a_ref, g_ref, buf_ref, cw_ref, act_ref, nb_ref):
    a = a_ref[...].astype(F32)
    g = g_ref[...].astype(F32)
    t = a.shape[1]
    y, y_head = _causal_conv(a, buf_ref[...], cw_ref[...])
    act_ref[...] = (_silu(y) * g).astype(act_ref.dtype)
    if y_head is not None:
        act_ref[:, :CONV_HEAD_ROWS, :] = (_silu(y_head) * g[:, :CONV_HEAD_ROWS]).astype(act_ref.dtype)
    nb_ref[...] = a[:, t - (FFN_CONV_W - 1):t, :]


def _ffn_act(h, buf, cw):
    b, t, _ = h.shape
    nj = D_FF // FFN_TC
    bb = _batch_block(b, t)
    return pl.pallas_call(
        _ffn_act_body,
        grid=(b // bb, nj),
        in_specs=[pl.BlockSpec((bb, t, FFN_TC), lambda i, j: (i, 0, j)),
                  pl.BlockSpec((bb, t, FFN_TC), lambda i, j: (i, 0, j + nj)),
                  pl.BlockSpec((bb, FFN_CONV_W - 1, FFN_TC), lambda i, j: (i, 0, j)),
                  pl.BlockSpec((FFN_CONV_W, FFN_TC), lambda i, j: (0, j))],
        out_specs=[pl.BlockSpec((bb, t, FFN_TC), lambda i, j: (i, 0, j)),
                   pl.BlockSpec((bb, FFN_CONV_W - 1, FFN_TC), lambda i, j: (i, 0, j))],
        out_shape=[jax.ShapeDtypeStruct((b, t, D_FF), BF16),
                   jax.ShapeDtypeStruct((b, FFN_CONV_W - 1, D_FF), F32)],
        compiler_params=_params("parallel", "parallel"),
        name="ffn_act",
    )(h, h, buf, cw)


GDN_PRE_TC = 512


def _gdn_pre_body(x_ref, buf_ref, cw_ref, o_ref, nb_ref):
    j = pl.program_id(1)
    x = x_ref[...].astype(F32)
    t = x.shape[1]
    is_q = j < GDN_QK_DIM // GDN_PRE_TC
    is_v = j >= 2 * GDN_QK_DIM // GDN_PRE_TC
    qscale = jnp.where(is_q, GDN_DK ** -0.5, 1.0).astype(F32)

    def finish(y, n_rows):
        y = _silu(y)

        @pl.when(is_v)
        def _():
            o_ref[:, :n_rows, :] = y.astype(o_ref.dtype)

        @pl.when(jnp.logical_not(is_v))
        def _():
            for h in range(GDN_PRE_TC // GDN_DK):
                yh = y[:, :, h * GDN_DK:(h + 1) * GDN_DK]
                inv = lax.rsqrt(jnp.sum(yh * yh, axis=-1, keepdims=True) + L2_EPS)
                o_ref[:, :n_rows, h * GDN_DK:(h + 1) * GDN_DK] = (yh * (inv * qscale)).astype(o_ref.dtype)

    y, y_head = _causal_conv(x, buf_ref[...], cw_ref[...])
    finish(y, t)
    if y_head is not None:
        finish(y_head, CONV_HEAD_ROWS)
    nb_ref[...] = x[:, t - (GDN_CONV_W - 1):t, :]


def _gdn_pre(proj, buf, cw):
    b, t, _ = proj.shape
    nj = GDN_CONV_CH // GDN_PRE_TC
    bb = _batch_block(b, t)
    return pl.pallas_call(
        _gdn_pre_body,
        grid=(b // bb, nj),
        in_specs=[pl.BlockSpec((bb, t, GDN_PRE_TC), lambda i, j: (i, 0, j)),
                  pl.BlockSpec((bb, GDN_CONV_W - 1, GDN_PRE_TC), lambda i, j: (i, 0, j)),
                  pl.BlockSpec((GDN_CONV_W, GDN_PRE_TC), lambda i, j: (0, j))],
        out_specs=[pl.BlockSpec((bb, t, GDN_PRE_TC), lambda i, j: (i, 0, j)),
                   pl.BlockSpec((bb, GDN_CONV_W - 1, GDN_PRE_TC), lambda i, j: (i, 0, j))],
        out_shape=[jax.ShapeDtypeStruct((b, t, GDN_CONV_CH), BF16),
                   jax.ShapeDtypeStruct((b, GDN_CONV_W - 1, GDN_CONV_CH), F32)],
        compiler_params=_params("parallel", "parallel"),
        name="gdn_pre",
    )(proj, buf, cw)


def _gdn_gate_body(x_ref, alog_ref, dtb_ref, g_ref, gcum_ref, beta_ref, *, chunk):
    x = x_ref[...]
    z = x + dtb_ref[...]
    softplus = jnp.maximum(z, 0.0) + jnp.log(1.0 + jnp.exp(-jnp.abs(z)))
    g = -jnp.exp(alog_ref[...]) * softplus
    g_ref[...] = g
    row = lax.broadcasted_iota(jnp.int32, x.shape, 1) % chunk
    acc = g
    s = 1
    while s < chunk:
        acc = acc + jnp.where(row >= s, pltpu.roll(acc, s, 1), 0.0)
        s *= 2
    gcum_ref[...] = acc
    beta_ref[...] = _sigmoid(x)


def _gdn_gate(proj, alog_pad, dtb_pad, lane_block):
    b, t, _ = proj.shape
    chunk = min(GDN_CHUNK, t)
    bb = _batch_block(b, t)
    spec = pl.BlockSpec((bb, t, 128), lambda i: (i, 0, 0))
    return pl.pallas_call(
        functools.partial(_gdn_gate_body, chunk=chunk),
        grid=(b // bb,),
        in_specs=[pl.BlockSpec((bb, t, 128), lambda i: (i, 0, lane_block)),
                  pl.BlockSpec((1, 128), lambda i: (0, 0)),
                  pl.BlockSpec((1, 128), lambda i: (0, 0))],
        out_specs=[spec, spec, spec],
        out_shape=[jax.ShapeDtypeStruct((b, t, 128), F32)] * 3,
        compiler_params=_params("parallel"),
        name="gdn_gate",
    )(proj, alog_pad, dtb_pad)


def _unit_lower_inverses(lows, n):
    eye = (lax.broadcasted_iota(jnp.int32, (n, n), 0) == lax.broadcasted_iota(jnp.int32, (n, n), 1)).astype(F32)
    ps = [eye - low for low in lows]
    ms = [_mm(low, low) for low in lows]
    k = 2
    while True:
        ps = [p + _mm(p, m) for p, m in zip(ps, ms)]
        k *= 2
        if k >= n:
            break
        ms = [_mm(m, m) for m in ms]
    return ps


def _gdn_scan_body(q_ref, k_ref, v_ref, z_ref, gc_ref, bc_ref, gr_ref, br_ref, s0_ref, onw_ref, o_ref, s_ref,
                   u_ref, w_ref, qk_ref, qd_ref, kd_ref):
    c = GDN_CHUNK
    hb = s0_ref.shape[1]
    n_chunks = q_ref.shape[1] // c
    s_ref[...] = s0_ref[...]
    ri = lax.broadcasted_iota(jnp.int32, (c, c), 0)
    ci = lax.broadcasted_iota(jnp.int32, (c, c), 1)
    tri = ri >= ci
    stri = ri > ci
    onw = onw_ref[...]

    def qk_slice(ref, r0, qh):
        return ref[0, pl.ds(r0, c), qh * GDN_DK:(qh + 1) * GDN_DK]

    def v_slice(ref, r0, hh):
        return ref[0, pl.ds(r0, c), hh * GDN_DV:(hh + 1) * GDN_DV]

    cpb = max(c_ for c_ in (GDN_PREP_CHUNKS, 4, 2, 1) if n_chunks % c_ == 0)

    def prep(nb, carry):
        items = []
        kk, qk, ks, qs = {}, {}, {}, {}
        for ch in range(cpb):
            n = nb * cpb + ch
            r0 = pl.multiple_of(n * c, c)
            gcol = gc_ref[0, 0, pl.ds(r0, c), :]
            bcol = bc_ref[0, 0, pl.ds(r0, c), :]
            grow = gr_ref[0, 0, n]
            brow = br_ref[0, 0, n]
            for qh in range(hb // 2):
                ks[ch, qh] = qk_slice(k_ref, r0, qh)
                qs[ch, qh] = qk_slice(q_ref, r0, qh)
            for hh in range(hb):
                items.append(dict(ch=ch, hh=hh, r0=r0, v=v_slice(v_ref, r0, hh),
                                  gcb=jnp.broadcast_to(gcol[:, hh:hh + 1], (c, GDN_DK)),
                                  bcb=jnp.broadcast_to(bcol[:, hh:hh + 1], (c, c)),
                                  gr=grow[hh:hh + 1, :], br=brow[hh:hh + 1, :]))
        for key in ks:
            kk[key] = _mm_nt(ks[key], ks[key])
        for key in ks:
            qk[key] = _mm_nt(qs[key], ks[key])
        lows = []
        for it in items:
            key = (it["ch"], it["hh"] // 2)
            it["decay"] = jnp.where(tri, jnp.exp(jnp.where(tri, it["gcb"][:, :c] - it["gr"], 0.0)), 0.0)
            lows.append(jnp.where(stri, (kk[key] * it["bcb"]) * it["decay"], 0.0))
        tinvs = _unit_lower_inverses(lows, c)
        us = [_mm(tinv * it["br"], it["v"]) for tinv, it in zip(tinvs, items)]
        ws = [_mm(tinv * (it["br"] * jnp.exp(it["gr"])), ks[it["ch"], it["hh"] // 2])
              for tinv, it in zip(tinvs, items)]
        for it, u, w in zip(items, us, ws):
            key = (it["ch"], it["hh"] // 2)
            hh, r0, gcb = it["hh"], it["r0"], it["gcb"]
            u_ref[hh, pl.ds(r0, c), :] = u
            w_ref[hh, pl.ds(r0, c), :] = w.astype(BF16)
            qk_ref[hh, pl.ds(r0, c), :] = jnp.where(tri, qk[key] * it["decay"], 0.0).astype(BF16)
            qd_ref[hh, pl.ds(r0, c), :] = (qs[key] * jnp.exp(gcb)).astype(BF16)
            kd_ref[hh, pl.ds(r0, c), :] = (ks[key] * jnp.exp(gcb[c - 1:c, :] - gcb)).astype(BF16)
        return carry

    lax.fori_loop(0, n_chunks // cpb, prep, 0)

    def scan(n, carry):
        r0 = pl.multiple_of(n * c, c)
        g_last = gc_ref[0, 0, pl.ds(r0 + (c - 1), 1), :]
        loaded = []
        for hh in range(hb):
            loaded.append((u_ref[hh, pl.ds(r0, c), :], w_ref[hh, pl.ds(r0, c), :], qk_ref[hh, pl.ds(r0, c), :],
                           qd_ref[hh, pl.ds(r0, c), :], kd_ref[hh, pl.ds(r0, c), :], v_slice(z_ref, r0, hh),
                           s_ref[0, hh]))
        ws_s = [_mm(w, s) for (u, w, qkm, qd, kd, z, s) in loaded]
        qd_s = [_mm(qd, s) for (u, w, qkm, qd, kd, z, s) in loaded]
        v_news = [ld[0] - ws for ld, ws in zip(loaded, ws_s)]
        qk_v = [_mm(ld[2], vn) for ld, vn in zip(loaded, v_news)]
        kd_v = [_mm_tn(ld[4], vn) for ld, vn in zip(loaded, v_news)]
        results = []
        for hh, ld in enumerate(loaded):
            o = qd_s[hh] + qk_v[hh]
            s_new = ld[6] * jnp.exp(g_last[:, hh:hh + 1]) + kd_v[hh]
            inv = lax.rsqrt(jnp.mean(o * o, axis=-1, keepdims=True) + RMS_EPS)
            results.append((((o * inv) * onw) * _silu(ld[5].astype(F32)), s_new))
        for hh, (og, s_new) in enumerate(results):
            s_ref[0, hh] = s_new
            o_ref[0, pl.ds(r0, c), hh * GDN_DV:(hh + 1) * GDN_DV] = og.astype(o_ref.dtype)
        return carry

    lax.fori_loop(0, n_chunks, scan, 0)


def _gdn_scan(qkv, proj, gcol, bcol, grow, brow, s0, layer, onw):
    b, t, _ = qkv.shape
    hb = gcol.shape[3]
    hg = GDN_V_HEADS // hb
    qw = hb // 2 * GDN_DK
    vw = hb * GDN_DV
    n = t // GDN_CHUNK
    return pl.pallas_call(
        _gdn_scan_body,
        grid=(b, hg),
        in_specs=[pl.BlockSpec((1, t, qw), lambda i, j: (i, 0, j)),
                  pl.BlockSpec((1, t, qw), lambda i, j: (i, 0, GDN_QK_DIM // qw + j)),
                  pl.BlockSpec((1, t, vw), lambda i, j: (i, 0, 2 * GDN_QK_DIM // vw + j)),
                  pl.BlockSpec((1, t, vw), lambda i, j: (i, 0, GDN_CONV_CH // vw + j)),
                  pl.BlockSpec((1, 1, t, hb), lambda i, j: (i, j, 0, 0)),
                  pl.BlockSpec((1, 1, t, hb), lambda i, j: (i, j, 0, 0)),
                  pl.BlockSpec((1, 1, n, hb, GDN_CHUNK), lambda i, j: (i, j, 0, 0, 0)),
                  pl.BlockSpec((1, 1, n, hb, GDN_CHUNK), lambda i, j: (i, j, 0, 0, 0)),
                  pl.BlockSpec((None, 1, hb, GDN_DK, GDN_DV), lambda i, j: (layer, i, j, 0, 0)),
                  pl.BlockSpec((1, GDN_DV), lambda i, j: (0, 0))],
        out_specs=[pl.BlockSpec((1, t, vw), lambda i, j: (i, 0, j)),
                   pl.BlockSpec((1, hb, GDN_DK, GDN_DV), lambda i, j: (i, j, 0, 0))],
        out_shape=[jax.ShapeDtypeStruct((b, t, GDN_V_DIM), BF16),
                   jax.ShapeDtypeStruct((b, GDN_V_HEADS, GDN_DK, GDN_DV), F32)],
        scratch_shapes=[pltpu.VMEM((hb, t, GDN_DV), F32),
                        pltpu.VMEM((hb, t, GDN_DK), BF16),
                        pltpu.VMEM((hb, t, GDN_CHUNK), BF16),
                        pltpu.VMEM((hb, t, GDN_DK), BF16),
                        pltpu.VMEM((hb, t, GDN_DK), BF16)],
        compiler_params=_params("parallel", "parallel"),
        name="gdn_scan",
    )(qkv, qkv, qkv, proj, gcol, bcol, grow, brow, s0, onw.reshape(1, GDN_DV))


def _gdn_layer(x, s0, layer, conv_buf, norm_w, w_in, conv_w, alog_pad, dtb_pad, out_norm_w, w_out):
    b, t, d = x.shape
    xf = x.reshape(b * t, d)
    w_main, w_tail = w_in
    proj = _rms_matmul(xf, norm_w, w_main, 2048, BF16).reshape(b, t, -1)
    tail = _rms_matmul(xf, norm_w, w_tail, 128).reshape(b, t, -1)
    qkv, new_buf = _gdn_pre(proj, conv_buf, conv_w)
    _, gcum, beta = _gdn_gate(tail, alog_pad, dtb_pad, 0)
    gcum = gcum[:, :, :GDN_V_HEADS]
    beta = beta[:, :, GDN_V_HEADS:2 * GDN_V_HEADS]
    tp = -(-t // GDN_CHUNK) * GDN_CHUNK
    if tp != t:
        pad = [(0, 0), (0, tp - t), (0, 0)]
        qkv = jnp.pad(qkv, pad)
        proj_z = jnp.pad(proj, pad)
        gcum = jnp.pad(gcum, pad, mode="edge")
        beta = jnp.pad(beta, pad)
    else:
        proj_z = proj
    hb = GDN_V_HEADS if tp == GDN_CHUNK else GDN_HEADS_PER_STEP
    hg = GDN_V_HEADS // hb
    n = tp // GDN_CHUNK
    gcol = gcum.reshape(b, tp, hg, hb).transpose(0, 2, 1, 3)
    bcol = beta.reshape(b, tp, hg, hb).transpose(0, 2, 1, 3)
    grow = gcum.reshape(b, n, GDN_CHUNK, hg, hb).transpose(0, 3, 1, 4, 2)
    brow = beta.reshape(b, n, GDN_CHUNK, hg, hb).transpose(0, 3, 1, 4, 2)
    o, s_new = _gdn_scan(qkv, proj_z, gcol, bcol, grow, brow, s0, layer, out_norm_w)
    o = o[:, :t].reshape(b * t, GDN_V_DIM)
    y = _matmul_res(o, w_out, xf).reshape(b, t, d)
    return y, s_new, new_buf


def _page_gather_body(pt_ref, *refs, pps):
    cmp_refs, o_ref = refs[:pps], refs[pps]
    sub = cmp_refs[0].shape[3]
    for s in range(pps):
        o_ref[0, :, s * sub:(s + 1) * sub, :] = cmp_refs[s][0, 0]


def _page_gather(cache_cmp, layer, page_table):
    b, n_pages = page_table.shape
    _, _, n_cg, sub, flat = cache_cmp.shape
    pps = 16 if n_pages % 16 == 0 else (8 if n_pages % 8 == 0 else 1)
    n_steps = n_pages // pps

    def page_map(s):
        return lambda i, p, pt: (layer, pt[i, p * pps + s], 0, 0, 0)

    grid_spec = pltpu.PrefetchScalarGridSpec(
        num_scalar_prefetch=1,
        grid=(b, n_steps),
        in_specs=[pl.BlockSpec((1, 1, n_cg, sub, flat), page_map(s)) for s in range(pps)],
        out_specs=pl.BlockSpec((1, n_cg, pps * sub, flat), lambda i, p, pt: (i, 0, p, 0)),
    )
    return pl.pallas_call(
        functools.partial(_page_gather_body, pps=pps),
        grid_spec=grid_spec,
        out_shape=jax.ShapeDtypeStruct((b, n_cg, n_pages * sub, flat), cache_cmp.dtype),
        compiler_params=_params("parallel", "arbitrary"),
        name="page_gather",
    )(page_table, *([cache_cmp] * pps))


def _compress_body(x_ref, w1_ref, pe_ref, w2_ref, o_ref, bias_ref, *, n_sub):
    hd = CMP_HIDDEN

    @pl.when(pl.program_id(1) == 0)
    def _():
        pe = pe_ref[0]
        pe0 = jnp.broadcast_to(pe[0:1], (8, pe.shape[1]))
        pe1 = jnp.broadcast_to(pe[1:2], (8, pe.shape[1]))
        bias_ref[:, :hd] = jnp.dot(pe0.astype(BF16), w1_ref[0, :, :hd], preferred_element_type=F32)
        bias_ref[:, hd:] = jnp.dot(pe1.astype(BF16), w1_ref[0, :, hd:], preferred_element_type=F32)

    acc = jnp.dot(x_ref[0, 0], w1_ref[0], preferred_element_type=F32)
    first = acc[:, :hd] + bias_ref[0:1, :hd]
    second = acc[:, hd:] + bias_ref[0:1, hd:]
    hid = _silu(first + pltpu.roll(second, n_sub - 1, 0))
    o_ref[0, 0] = jnp.dot(hid.astype(BF16), w2_ref[0], preferred_element_type=F32)


def _compress(x, n_sub, w1cat, pe_flat, w2):
    b, _, _, flat = x.shape
    dh = flat // CMP_STRIDE
    return pl.pallas_call(
        functools.partial(_compress_body, n_sub=n_sub),
        grid=(2 * NSA_G, b),
        in_specs=[pl.BlockSpec((1, 1, n_sub, flat), lambda j, i: (i, j, 0, 0)),
                  pl.BlockSpec((1, flat, 2 * CMP_HIDDEN), lambda j, i: (j // NSA_G, 0, 0)),
                  pl.BlockSpec((1, 2, flat), lambda j, i: (j // NSA_G, 0, 0)),
                  pl.BlockSpec((1, CMP_HIDDEN, dh), lambda j, i: (j // NSA_G, 0, 0))],
        out_specs=pl.BlockSpec((1, 1, n_sub, dh), lambda j, i: (i, j, 0, 0)),
        out_shape=jax.ShapeDtypeStruct((b, 2 * NSA_G, n_sub, dh), F32),
        scratch_shapes=[pltpu.VMEM((8, 2 * CMP_HIDDEN), F32)],
        compiler_params=_params("arbitrary", "arbitrary"),
        name="nsa_compress",
    )(x, w1cat, pe_flat, w2)


def _inv_or_zero(l):
    return jnp.where(l > 0.0, 1.0 / jnp.where(l > 0.0, l, 1.0), 0.0)


def _attn_branch(q, slopes, t_pos, k, v, kp, ok_fn, may_be_empty=False):
    r, qb, nk = slopes.shape[0], t_pos.shape[1], k.shape[0]
    dist = t_pos - kp
    ok = ok_fn(dist)
    pen = jnp.where(ok, dist.astype(F32), MASKED_DIST)
    s = _mm_nt(q, k).reshape(r, qb, nk) - slopes * pen
    m = jnp.max(s, axis=-1, keepdims=True)
    p = jnp.exp2(s - m)
    if may_be_empty:
        p = jnp.where(ok, p, 0.0)
    inv = _inv_or_zero(jnp.sum(p, axis=-1, keepdims=True))
    o = _mm(p.reshape(r * qb, nk), v) * inv.reshape(r * qb, 1)
    return o, p, inv


def _pick_blocks(p_sum, t_lane, n_cmp, n_blk, nb_rows):
    nq, nc = p_sum.shape
    nb = -(-n_blk // 8) * 8
    jj = lax.broadcasted_iota(jnp.int32, (nb, nc), 0)
    nn = lax.broadcasted_iota(jnp.int32, (nb, nc), 1)
    overlap_t = ((nn * CMP_STRIDE < (jj + 1) * SEL_BLOCK) & (nn * CMP_STRIDE + CMP_LEN > jj * SEL_BLOCK)
                 & (nn < n_cmp) & (jj < n_blk)).astype(F32)
    imp = lax.dot_general(overlap_t, p_sum, (((1,), (1,)), ((), ())), preferred_element_type=F32,
                          precision=lax.Precision.HIGHEST)
    j = lax.broadcasted_iota(jnp.int32, (nb, nq), 0)
    cur = t_lane // SEL_BLOCK
    imp = jnp.where((j == 0) | (j == cur) | (j == cur - 1), FORCE_SCORE, imp)
    imp = jnp.where(j > cur, -FORCE_SCORE, imp)
    imp = jnp.where(j >= n_blk, NEVER, imp)
    rank = jnp.zeros((nb, nq), jnp.int32)
    for jp in range(n_blk):
        row = imp[jp:jp + 1, :]
        rank = rank + ((row > imp) | ((row == imp) & (j > jp))).astype(jnp.int32)
    picked_t = ((rank < SEL_TOPN) & (j < n_blk)).astype(F32)
    if nb_rows > nb:
        picked_t = jnp.concatenate([picked_t, jnp.zeros((nb_rows - nb, nq), F32)], axis=0)
    return picked_t


def _transpose_01(x_t):
    m = x_t.shape[1]
    eye = (lax.broadcasted_iota(jnp.int32, (m, m), 0) == lax.broadcasted_iota(jnp.int32, (m, m), 1))
    return _mm_nt(eye.astype(BF16), x_t)


def _head_rows(x, n_heads):
    dh = x.shape[1] // n_heads
    return jnp.concatenate([x[:, h * dh:(h + 1) * dh] for h in range(n_heads)], axis=0)


def _nsa_attn_body(q_ref, gate_ref, slope_ref, ck_ref, cv_ref, ks_ref, vs_ref, kw_ref, vw_ref, o_ref, osel_ref,
                   *, nsb, **kw):
    refs = (q_ref, gate_ref, slope_ref, ck_ref, cv_ref, ks_ref, vs_ref, kw_ref, vw_ref, o_ref, osel_ref)
    if nsb == 1:
        _nsa_attn_block(0, pl.program_id(2), *refs, **kw)
    else:
        def one(sub, carry):
            _nsa_attn_block(sub, pl.program_id(2) * nsb + sub, *refs, **kw)
            return carry

        lax.fori_loop(0, nsb, one, 0)


def _nsa_attn_block(sub, i, q_ref, gate_ref, slope_ref, ck_ref, cv_ref, ks_ref, vs_ref, kw_ref, vw_ref, o_ref,
                    osel_ref, *, qb, p_len, n_cmp, n_blk, w0, sel_classes, kw_len):
    r = NSA_R
    rows = r * qb
    rq = sub * qb if isinstance(sub, int) else pl.multiple_of(sub * qb, qb)
    q = _head_rows(q_ref[0, pl.ds(rq, qb), :].astype(F32), r)
    q = (q * (NSA_DH ** -0.5 * LOG2E)).astype(BF16)
    slopes = slope_ref[0] * LOG2E
    q0 = p_len + i * qb
    t_pos = q0 + lax.broadcasted_iota(jnp.int32, (1, qb, 1), 1)
    branch = functools.partial(_attn_branch, q, slopes, t_pos)

    tw = kw_ref.shape[2]
    if tw == kw_len:
        start = 0
    else:
        start = pl.multiple_of(jnp.clip(q0 + qb - w0 - kw_len, 0, tw - kw_len), SEL_BLOCK)
    kp_win = w0 + start + lax.broadcasted_iota(jnp.int32, (1, 1, kw_len), 2)
    o_win, _, _ = branch(kw_ref[0, 0, pl.ds(start, kw_len), :], vw_ref[0, 0, pl.ds(start, kw_len), :], kp_win,
                         lambda dist: (dist >= 0) & (dist < WINDOW) & (kp_win >= 0))

    nc = ck_ref.shape[2]
    n_idx = lax.broadcasted_iota(jnp.int32, (1, 1, nc), 2)
    o_cmp, p_cmp, inv_cmp = branch(ck_ref[0, 0], cv_ref[0, 0], n_idx * CMP_STRIDE + (CMP_LEN - 1),
                                   lambda dist: (dist >= 0) & (n_idx < n_cmp), may_be_empty=True)
    p_sum = jnp.sum(p_cmp * inv_cmp, axis=0)
    nb8 = -(-n_blk // 16) * 16
    picked_t = _pick_blocks(p_sum, q0 + lax.broadcasted_iota(jnp.int32, (1, qb), 1), n_cmp, n_blk, nb8)
    picked = _transpose_01(picked_t)

    def sel_branch(nk):
        blk_of_key = lax.broadcasted_iota(jnp.int32, (nb8, nk), 1) // SEL_BLOCK
        expand = (blk_of_key == lax.broadcasted_iota(jnp.int32, (nb8, nk), 0)).astype(BF16)
        key_picked = (_mm(picked, expand) > 0.5).reshape(1, qb, nk)
        o, _, _ = branch(ks_ref[0, 0, 0:nk, :], vs_ref[0, 0, 0:nk, :],
                         lax.broadcasted_iota(jnp.int32, (1, 1, nk), 2),
                         lambda dist: key_picked & (dist >= 0))
        osel_ref[...] = o

    if len(sel_classes) == 1:
        sel_branch(sel_classes[0])
    else:
        need = q0 + qb
        prev = 0
        for nk in sel_classes:
            pl.when((need > prev) & (need <= nk))(functools.partial(sel_branch, nk))
            prev = nk

    gates = _sigmoid(gate_ref[0, 0, :, pl.ds(rq, qb), :]).reshape(rows, 3)
    o = gates[:, 0:1] * o_cmp + gates[:, 1:2] * osel_ref[...] + gates[:, 2:3] * o_win
    for h in range(r):
        o_ref[0, pl.ds(rq, qb), h * NSA_DH:(h + 1) * NSA_DH] = o[h * qb:(h + 1) * qb].astype(o_ref.dtype)


def _nsa_attn(q, gates_t, slopes, cmp_kv, sel_arr, sel_k0, sel_v0, win_arr, win_k0, win_v0,
              *, p_len, n_cmp, n_blk, w0):
    b, tq, _ = q.shape
    g, r, dh = NSA_G, NSA_R, NSA_DH
    qb = NSA_Q_BLOCK if tq % NSA_Q_BLOCK == 0 else tq
    nc = cmp_kv.shape[2]
    tk = sel_arr.shape[2]
    tw = win_arr.shape[2]
    if tq == qb or tk % NSA_SEL_CLASS != 0:
        sel_classes = (tk,)
    else:
        sel_classes = tuple(range(NSA_SEL_CLASS, tk + 1, NSA_SEL_CLASS))
    kw_len = min(tw, -(-(WINDOW - 1 + qb) // KEY_CHUNK) * KEY_CHUNK)
    nsb = NSA_Q_BLOCKS_PER_STEP if (tq // qb) % NSA_Q_BLOCKS_PER_STEP == 0 else 1
    body = functools.partial(_nsa_attn_body, nsb=nsb, qb=qb, p_len=p_len, n_cmp=n_cmp, n_blk=n_blk, w0=w0,
                             sel_classes=sel_classes, kw_len=kw_len)
    qs = nsb * qb
    return pl.pallas_call(
        body,
        grid=(b, g, tq // qs),
        in_specs=[pl.BlockSpec((1, qs, r * dh), lambda bi, gi, i: (bi, i, gi)),
                  pl.BlockSpec((1, 1, r, qs, 3), lambda bi, gi, i: (bi, gi, 0, i, 0)),
                  pl.BlockSpec((1, r, 1, 1), lambda bi, gi, i: (gi, 0, 0, 0)),
                  pl.BlockSpec((1, 1, nc, dh), lambda bi, gi, i: (bi, gi, 0, 0)),
                  pl.BlockSpec((1, 1, nc, dh), lambda bi, gi, i: (bi, NSA_G + gi, 0, 0)),
                  pl.BlockSpec((1, 1, tk, dh), lambda bi, gi, i: (bi, sel_k0 + gi, 0, 0)),
                  pl.BlockSpec((1, 1, tk, dh), lambda bi, gi, i: (bi, sel_v0 + gi, 0, 0)),
                  pl.BlockSpec((1, 1, tw, dh), lambda bi, gi, i: (bi, win_k0 + gi, 0, 0)),
                  pl.BlockSpec((1, 1, tw, dh), lambda bi, gi, i: (bi, win_v0 + gi, 0, 0))],
        out_specs=pl.BlockSpec((1, qs, r * dh), lambda bi, gi, i: (bi, i, gi)),
        out_shape=jax.ShapeDtypeStruct((b, tq, g * r * dh), BF16),
        scratch_shapes=[pltpu.VMEM((r * qb, dh), F32)],
        compiler_params=_params("parallel", "parallel", "arbitrary"),
        name="nsa_attn",
    )(q, gates_t, slopes, cmp_kv, cmp_kv, sel_arr, sel_arr, win_arr, win_arr)


def _nsa_paged_body(pt_ref, *refs, tq, p_len, n_cmp, n_blk, w0, pps, n_steps):
    k_refs, v_refs = refs[:pps], refs[pps:2 * pps]
    (q_ref, gate_ref, slope_ref, cmp_ref, new_ref, win_ref, o_ref,
     qbd_ref, pen_ref, m_ref, l_ref, acc_ref, ocmp_ref) = refs[2 * pps:]
    p = pl.program_id(1)
    g_n, r = NSA_G, NSA_R
    dh = NSA_DH
    rg = r * tq
    rows = g_n * rg
    page = k_refs[0].shape[5]
    chunk = pps * page
    n_keys = pen_ref.shape[1]
    t_pos = p_len + lax.broadcasted_iota(jnp.int32, (1, tq, 1), 1)

    @pl.when(p == 0)
    def _():
        qbd_ref[...] = jnp.zeros(qbd_ref.shape, qbd_ref.dtype)
        nc = cmp_ref.shape[2]
        n_idx = lax.broadcasted_iota(jnp.int32, (1, 1, nc), 2)
        p_sums = []
        for g in range(g_n):
            q = _head_rows(q_ref[0, :, g * r * dh:(g + 1) * r * dh].astype(F32), r)
            q = (q * (dh ** -0.5 * LOG2E)).astype(BF16)
            qbd_ref[g * rg:(g + 1) * rg, g * dh:(g + 1) * dh] = q
            o_cmp, p_cmp, inv_cmp = _attn_branch(
                q, slope_ref[g] * LOG2E, t_pos, cmp_ref[0, g], cmp_ref[0, g_n + g],
                n_idx * CMP_STRIDE + (CMP_LEN - 1), lambda dist: (dist >= 0) & (n_idx < n_cmp), may_be_empty=True)
            ocmp_ref[g * rg:(g + 1) * rg, :] = o_cmp
            p_sums.append(jnp.sum(p_cmp * inv_cmp, axis=0))
        nbp = -(-n_blk // 128) * 128
        nq = g_n * tq
        t_lane = p_len + lax.broadcasted_iota(jnp.int32, (1, nq), 1) % tq
        picked_t = _pick_blocks(jnp.concatenate(p_sums, axis=0), t_lane, n_cmp, n_blk, nbp)
        picked = _transpose_01(picked_t.astype(BF16))
        picked_rows = jnp.concatenate([picked[g * tq:(g + 1) * tq] for g in range(g_n) for _ in range(r)], axis=0)
        slope_rows = jnp.broadcast_to(slope_ref[...] * LOG2E, (g_n, r, tq, 1)).reshape(rows, 1)
        t_rows = p_len + lax.broadcasted_iota(jnp.int32, (rows, 1), 0) % tq

        def fill(k0, width):
            kp = k0 + lax.broadcasted_iota(jnp.int32, (1, width), 1)
            blk = lax.broadcasted_iota(jnp.int32, (nbp, width), 0)
            expand = ((k0 + lax.broadcasted_iota(jnp.int32, (nbp, width), 1)) // SEL_BLOCK == blk).astype(BF16)
            key_picked = _mm(picked_rows, expand) > 0.5
            dist = t_rows - kp
            return slope_rows * jnp.where(key_picked & (dist >= 0), dist.astype(F32), MASKED_DIST)

        def fill_chunk(c, carry):
            k0 = pl.multiple_of(c * chunk, chunk)
            pen_ref[:, pl.ds(k0, chunk)] = fill(k0, chunk)
            return carry

        lax.fori_loop(0, n_steps, fill_chunk, 0)
        pen_ref[:, n_steps * chunk:n_keys] = fill(n_steps * chunk, n_keys - n_steps * chunk)
        m_ref[...] = jnp.full(m_ref.shape, NEVER, F32)
        l_ref[...] = jnp.zeros(l_ref.shape, F32)
        acc_ref[...] = jnp.zeros(acc_ref.shape, F32)

    def online_update(kts, vts, k0):
        width = len(kts) * page
        qbd = qbd_ref[...]
        s = jnp.concatenate([jnp.dot(qbd, kt.astype(BF16), preferred_element_type=F32) for kt in kts], axis=1)
        s = s - pen_ref[:, pl.ds(k0, width)]
        m_prev = m_ref[...]
        m_new = jnp.maximum(m_prev, jnp.max(s, axis=-1, keepdims=True))
        alpha = jnp.exp2(m_prev - m_new)
        prob = jnp.exp2(s - m_new)
        l_ref[...] = alpha * l_ref[...] + jnp.sum(prob, axis=-1, keepdims=True)
        pv = [_mm_nt(prob[:, i * page:(i + 1) * page], vt) for i, vt in enumerate(vts)]
        acc_ref[...] = alpha * acc_ref[...] + sum(pv[1:], pv[0])
        m_ref[...] = m_new

    @pl.when(p < n_steps)
    def _():
        online_update([ref[0, 0, 0].reshape(g_n * dh, page) for ref in k_refs],
                      [ref[0, 0, 0].reshape(g_n * dh, page) for ref in v_refs], pl.multiple_of(p * chunk, chunk))

    @pl.when(p == n_steps)
    def _():
        online_update([new_ref[0, 0].reshape(g_n * dh, page)], [new_ref[0, 1].reshape(g_n * dh, page)],
                      n_steps * chunk)
        acc = acc_ref[...] * _inv_or_zero(l_ref[...])
        tw = win_ref.shape[2]
        kp_win = w0 + lax.broadcasted_iota(jnp.int32, (1, 1, tw), 2)
        for g in range(g_n):
            q = qbd_ref[g * rg:(g + 1) * rg, g * dh:(g + 1) * dh]
            o_win, _, _ = _attn_branch(q, slope_ref[g] * LOG2E, t_pos, win_ref[0, g], win_ref[0, g_n + g], kp_win,
                                       lambda dist: (dist >= 0) & (dist < WINDOW) & (kp_win >= 0))
            gates = _sigmoid(gate_ref[0, g]).reshape(rg, 3)
            o = (gates[:, 0:1] * ocmp_ref[g * rg:(g + 1) * rg, :]
                 + gates[:, 1:2] * acc[g * rg:(g + 1) * rg, g * dh:(g + 1) * dh] + gates[:, 2:3] * o_win)
            for h in range(r):
                o_ref[0, :, (g * r + h) * dh:(g * r + h + 1) * dh] = o[h * tq:(h + 1) * tq].astype(o_ref.dtype)


def _nsa_paged(q, gates_t, slopes, cmp_kv, cache_nt, layer, page_table, new_t, win_arr,
               *, p_len, n_cmp, n_blk, w0):
    b, tq, _ = q.shape
    g, r, dh = NSA_G, NSA_R, NSA_DH
    n_pages = page_table.shape[1]
    page = cache_nt.shape[5]
    pps = 8 if n_pages % 8 == 0 else (4 if n_pages % 4 == 0 else 1)
    n_steps = n_pages // pps
    nc = cmp_kv.shape[2]
    tw = win_arr.shape[2]
    rows = g * r * tq
    n_keys = (n_pages + 1) * page

    def page_map(c, s):
        return lambda i, p, pt: (layer, pt[i, jnp.minimum(p, n_steps - 1) * pps + s], c, 0, 0, 0)

    const = lambda i, p, pt: (i, 0, 0, 0, 0)
    grid_spec = pltpu.PrefetchScalarGridSpec(
        num_scalar_prefetch=1,
        grid=(b, n_steps + 1),
        in_specs=[pl.BlockSpec((1, 1, 1, g, dh, page), page_map(2, s)) for s in range(pps)]
        + [pl.BlockSpec((1, 1, 1, g, dh, page), page_map(3, s)) for s in range(pps)]
        + [pl.BlockSpec((1, tq, g * r * dh), lambda i, p, pt: (i, 0, 0)),
           pl.BlockSpec((1, g, r, tq, 3), const),
           pl.BlockSpec((g, r, 1, 1), lambda i, p, pt: (0, 0, 0, 0)),
           pl.BlockSpec((1, 2 * g, nc, dh), lambda i, p, pt: (i, 0, 0, 0)),
           pl.BlockSpec((1, 2, g, dh, page), const),
           pl.BlockSpec((1, 2 * g, tw, dh), lambda i, p, pt: (i, 0, 0, 0))],
        out_specs=pl.BlockSpec((1, tq, g * r * dh), lambda i, p, pt: (i, 0, 0)),
        scratch_shapes=[pltpu.VMEM((rows, g * dh), BF16),
                        pltpu.VMEM((rows, n_keys), F32),
                        pltpu.VMEM((rows, 1), F32),
                        pltpu.VMEM((rows, 1), F32),
                        pltpu.VMEM((rows, g * dh), F32),
                        pltpu.VMEM((rows, dh), F32)],
    )
    body = functools.partial(_nsa_paged_body, tq=tq, p_len=p_len, n_cmp=n_cmp, n_blk=n_blk, w0=w0, pps=pps,
                             n_steps=n_steps)
    return pl.pallas_call(
        body,
        grid_spec=grid_spec,
        out_shape=jax.ShapeDtypeStruct((b, tq, g * r * dh), BF16),
        compiler_params=_params("parallel", "arbitrary"),
        name="nsa_paged",
    )(page_table, *([cache_nt] * (2 * pps)), q, gates_t, slopes, cmp_kv, new_t, win_arr)


def _nsa_layer(x, cache_hm, layer, page_table, win_buf, norm_w, w_in, w1cat, pe_flat, w2, w_out, slopes):
    b, t, d = x.shape
    g, r, dh = NSA_G, NSA_R, NSA_DH
    xf = x.reshape(b * t, d)
    w_q, w_kv = w_in
    q = _rms_matmul(xf, norm_w, w_q, NSA_Q_DIM, BF16).reshape(b, t, NSA_Q_DIM)
    if cache_hm is None and t % HM_ROWS == 0:
        proj, kv_hm = _rms_matmul_hm(xf, norm_w, w_kv, b, t, 2 * NSA_KV_DIM, 4 * g, dh)
        proj = proj.reshape(b, t, -1)
    else:
        proj = _rms_matmul(xf, norm_w, w_kv, w_kv.shape[1]).reshape(b, t, -1)
        kv_hm = None
    kv = proj[..., :6 * NSA_KV_DIM]
    gates_t = proj[..., 6 * NSA_KV_DIM:6 * NSA_KV_DIM + 3 * NSA_HEADS]
    gates_t = gates_t.reshape(b, t, g, r, 3).transpose(0, 2, 3, 1, 4)
    kv6 = kv.reshape(b, t, 6, g, dh)
    new_rows = kv6[:, :, :4]
    if cache_hm is None:
        p_len = 0
        tk = t
        assert t % CMP_STRIDE == 0
        cmp_x = kv6[:, :, :2].astype(BF16).transpose(0, 2, 3, 1, 4).reshape(b, 2 * g, t // CMP_STRIDE,
                                                                           CMP_STRIDE * dh)
        if kv_hm is None:
            kv_hm = kv6[:, :, 2:].astype(BF16).transpose(0, 2, 3, 1, 4).reshape(b, 4 * g, t, dh)
        new_win = kv6[:, t - min(WINDOW, t):, 4:]
    else:
        cache_cmp, cache_nt = cache_hm
        n_pages = page_table.shape[1]
        page = cache_nt.shape[5]
        p_len = n_pages * page
        tk = p_len + t
        assert p_len % CMP_STRIDE == 0 and t < CMP_STRIDE and t <= page
        cmp_x = _page_gather(cache_cmp, layer, page_table)
        new_t = jnp.pad(kv6[:, :, 2:4].transpose(0, 2, 3, 4, 1), [(0, 0)] * 4 + [(0, page - t)])
        win_all = jnp.concatenate([win_buf, kv6[:, :, 4:]], axis=1)
        wl = win_all.shape[1]
        wlp = -(-wl // KEY_CHUNK) * KEY_CHUNK
        win_arr = jnp.pad(win_all.astype(BF16), [(0, 0), (0, wlp - wl), (0, 0), (0, 0), (0, 0)])
        win_arr = win_arr.transpose(0, 2, 3, 1, 4).reshape(b, 2 * g, wlp, dh)
        new_win = win_all[:, wl - min(WINDOW, tk):]
    n_sub = tk // CMP_STRIDE
    n_cmp = n_sub - 1
    n_blk = -(-tk // SEL_BLOCK)
    cmp_kv = _compress(cmp_x, n_sub, w1cat, pe_flat, w2)
    if cache_hm is None:
        o = _nsa_attn(q, gates_t, slopes, cmp_kv, kv_hm, 0, g, kv_hm, 2 * g, 3 * g,
                      p_len=0, n_cmp=n_cmp, n_blk=n_blk, w0=0)
    else:
        o = _nsa_paged(q, gates_t, slopes, cmp_kv, cache_nt, layer, page_table, new_t, win_arr,
                       p_len=p_len, n_cmp=n_cmp, n_blk=n_blk, w0=p_len - win_buf.shape[1])
    y = _matmul_res(o.reshape(b * t, NSA_Q_DIM), w_out, xf).reshape(b, t, d)
    return y, new_rows, new_win


def _ffn_layer(x, buf, norm_w, w_up, conv_w, w_down):
    b, t, d = x.shape
    xf = x.reshape(b * t, d)
    h = _rms_matmul(xf, norm_w, w_up, D_FF, BF16).reshape(b, t, -1)
    act, new_buf = _ffn_act(h, buf, conv_w)
    y = _matmul_res(act.reshape(b * t, D_FF), w_down, xf).reshape(b, t, d)
    return y, new_buf


def _pad_cols(w, n):
    return jnp.pad(w, [(0, 0)] * (w.ndim - 1) + [(0, n - w.shape[-1])])


def _trunk(x, cache_hm, page_table, nsa_win, gdn_state, gdn_conv, ffn_conv, wts):
    depth = wts["ffn_w_up"].shape[0]
    rows_l, win_l, s_l, gconv_l, fconv_l = [], [], [], [], []
    for i in range(depth):
        j = i // 2
        if i % 2 == 0:
            x, s_new, cb = _gdn_layer(x, gdn_state, j, gdn_conv[j], wts["gdn_norm"][j],
                                      (wts["gdn_w_main"][j], wts["gdn_w_tail"][j]),
                                      wts["gdn_conv_w"][j], wts["gdn_a_log"][j], wts["gdn_dt_bias"][j],
                                      wts["gdn_out_norm"][j], wts["gdn_w_out"][j])
            s_l.append(s_new)
            gconv_l.append(cb)
        else:
            x, rows, wb = _nsa_layer(x, cache_hm, j, page_table, None if nsa_win is None else nsa_win[j],
                                     wts["nsa_norm"][j], (wts["nsa_w_q"][j], wts["nsa_w_kv"][j]),
                                     wts["nsa_w1cat"][j],
                                     wts["nsa_pe_flat"][j], wts["nsa_w2"][j], wts["nsa_w_out"][j],
                                     wts["slopes"])
            rows_l.append(rows)
            win_l.append(wb)
        x, fb = _ffn_layer(x, ffn_conv[i], wts["ffn_norm"][i], wts["ffn_w_up"][i], wts["ffn_conv_w"][i],
                           wts["ffn_w_down"][i])
        fconv_l.append(fb)
    b, t, d = x.shape
    y = _rms(x.reshape(b * t, d), wts["final_norm"]).reshape(b, t, d)
    return (y, jnp.stack(rows_l), jnp.stack(win_l), jnp.stack(s_l), jnp.stack(gconv_l), jnp.stack(fconv_l))


def kernel(x_prompt, x_sample, cache_nsa_kv, cache_nsa_win, state_gdn_s, state_gdn_conv, state_ffn_conv,
           page_table, gdn_norm, gdn_w_in, gdn_conv_w, gdn_a_log, gdn_dt_bias, gdn_out_norm, gdn_w_out,
           nsa_norm, nsa_w_in, nsa_cmp_pe, nsa_cmp_w1, nsa_cmp_w2, nsa_w_out,
           ffn_norm, ffn_w_up, ffn_conv_w, ffn_w_down, final_norm):
    n_gdn = gdn_w_in.shape[0]
    n_nsa = nsa_w_in.shape[0]
    depth = ffn_w_up.shape[0]
    bp = x_prompt.shape[0]
    gdn_main = GDN_CONV_CH + GDN_V_DIM
    nsa_cols = -(-(nsa_w_in.shape[2] - NSA_Q_DIM) // 896) * 896
    w1 = nsa_cmp_w1.reshape(n_nsa, 2, 2, CMP_STRIDE * NSA_DH, CMP_HIDDEN)
    w1cat = jnp.concatenate([w1[:, :, 0], w1[:, :, 1]], axis=-1).astype(BF16)
    slopes = 2.0 ** (-8.0 * jnp.arange(1, NSA_HEADS + 1, dtype=F32) / NSA_HEADS)
    wts = {
        "gdn_norm": gdn_norm,
        "gdn_w_main": gdn_w_in[:, :, :gdn_main].astype(BF16),
        "gdn_w_tail": _pad_cols(gdn_w_in[:, :, gdn_main:], 128).astype(BF16),
        "gdn_conv_w": gdn_conv_w,
        "gdn_a_log": _pad_cols(gdn_a_log, 128).reshape(n_gdn, 1, 128),
        "gdn_dt_bias": _pad_cols(gdn_dt_bias, 128).reshape(n_gdn, 1, 128),
        "gdn_out_norm": gdn_out_norm,
        "gdn_w_out": gdn_w_out.astype(BF16),
        "nsa_norm": nsa_norm,
        "nsa_w_q": nsa_w_in[:, :, :NSA_Q_DIM].astype(BF16),
        "nsa_w_kv": _pad_cols(nsa_w_in[:, :, NSA_Q_DIM:], nsa_cols).astype(BF16),
        "nsa_w1cat": w1cat,
        "nsa_pe_flat": nsa_cmp_pe.reshape(n_nsa, 2, 2, CMP_STRIDE * NSA_DH),
        "nsa_w2": nsa_cmp_w2.astype(BF16),
        "nsa_w_out": nsa_w_out.astype(BF16),
        "slopes": slopes.reshape(NSA_G, NSA_R, 1, 1),
        "ffn_norm": ffn_norm,
        "ffn_w_up": ffn_w_up.astype(BF16),
        "ffn_conv_w": ffn_conv_w,
        "ffn_w_down": ffn_w_down.astype(BF16),
        "final_norm": final_norm,
    }
    n_l, pool, page = cache_nsa_kv.shape[:3]
    cache_cmp = cache_nsa_kv[:, :, :, :2].astype(BF16).transpose(0, 1, 3, 4, 2, 5)
    cache_hm = (cache_cmp.reshape(n_l, pool, 2 * NSA_G, page // CMP_STRIDE, CMP_STRIDE * NSA_DH),
                cache_nsa_kv.transpose(0, 1, 3, 4, 5, 2))

    zeros = functools.partial(jnp.zeros, dtype=F32)
    prompt = _trunk(x_prompt, None, None, None,
                    zeros((n_gdn, bp, GDN_V_HEADS, GDN_DK, GDN_DV)),
                    zeros((n_gdn, bp, GDN_CONV_W - 1, GDN_CONV_CH)),
                    zeros((depth, bp, FFN_CONV_W - 1, D_FF)), wts)
    sample = _trunk(x_sample, cache_hm, page_table, cache_nsa_win, state_gdn_s, state_gdn_conv, state_ffn_conv, wts)
    out = []
    for p, s in zip(prompt, sample):
        out.extend([p, s])
    return tuple(out)
```

```python
import functools

import jax
import jax.numpy as jnp
from jax import lax
from jax.experimental import pallas as pl
from jax.experimental.pallas import tpu as pltpu

F32 = jnp.float32
BF16 = jnp.bfloat16

RMS_EPS = 1e-6
L2_EPS = 1e-6
NEG_INF = -1e30
FORCE_SCORE = 1e9
NEVER = -3e38
MASKED_DIST = 1e30
LOG2E = 1.4426950408889634

GDN_QK_HEADS = 8
GDN_V_HEADS = 16
GDN_DK = 128
GDN_DV = 128
GDN_QK_DIM = GDN_QK_HEADS * GDN_DK
GDN_V_DIM = GDN_V_HEADS * GDN_DV
GDN_CONV_CH = 2 * GDN_QK_DIM + GDN_V_DIM
GDN_CONV_W = 4
GDN_CHUNK = 64
GDN_HEADS_PER_STEP = 4
GDN_PREP_CHUNKS = 8
NSA_HEADS = 16
NSA_G = 4
NSA_R = 4
NSA_DH = 64
NSA_Q_DIM = NSA_HEADS * NSA_DH
NSA_KV_DIM = NSA_G * NSA_DH
CMP_STRIDE = 16
CMP_LEN = 32
CMP_HIDDEN = 256
SEL_BLOCK = 64
SEL_TOPN = 8
WINDOW = 512
NSA_Q_BLOCK = 64
NSA_Q_BLOCKS_PER_STEP = 8
KEY_CHUNK = 128
NSA_SEL_CLASS = 512
D_FF = 2816
FFN_CONV_W = 3
FFN_TC = 256
SEQ_ROWS_PER_STEP = 2048
CONV_HEAD_ROWS = 16
HM_ROWS = 512

VMEM_LIMIT = 52 * 1024 * 1024


def _params(*sem):
    return pltpu.CompilerParams(dimension_semantics=sem, vmem_limit_bytes=VMEM_LIMIT)


def _mm(a, b):
    return jnp.dot(a.astype(BF16), b.astype(BF16), preferred_element_type=F32)


def _mm_nt(a, b):
    return lax.dot_general(a.astype(BF16), b.astype(BF16), (((1,), (1,)), ((), ())),
                           preferred_element_type=F32)


def _mm_tn(a, b):
    return lax.dot_general(a.astype(BF16), b.astype(BF16), (((0,), (0,)), ((), ())),
                           preferred_element_type=F32)


def _sigmoid(x):
    return 0.5 * jnp.tanh(0.5 * x) + 0.5


def _silu(x):
    h = 0.5 * x
    return h * jnp.tanh(h) + h


def _rms_matmul_body(x_ref, nw_ref, w_ref, o_ref, xn_ref):
    @pl.when(pl.program_id(1) == 0)
    def _():
        x = x_ref[...]
        inv = lax.rsqrt(jnp.mean(x * x, axis=-1, keepdims=True) + RMS_EPS)
        xn_ref[...] = ((x * inv) * nw_ref[...]).astype(BF16)

    o_ref[...] = jnp.dot(xn_ref[...], w_ref[...], preferred_element_type=F32).astype(o_ref.dtype)


def _rms_matmul(x, nw, w, tn, out_dtype=F32):
    m, k = x.shape
    n = w.shape[1]
    tm = min(m, 1024)
    return pl.pallas_call(
        _rms_matmul_body,
        grid=(m // tm, n // tn),
        in_specs=[pl.BlockSpec((tm, k), lambda i, j: (i, 0)),
                  pl.BlockSpec((1, k), lambda i, j: (0, 0)),
                  pl.BlockSpec((k, tn), lambda i, j: (0, j))],
        out_specs=pl.BlockSpec((tm, tn), lambda i, j: (i, j)),
        out_shape=jax.ShapeDtypeStruct((m, n), out_dtype),
        scratch_shapes=[pltpu.VMEM((tm, k), BF16)],
        compiler_params=_params("parallel", "arbitrary"),
        name="rms_matmul",
    )(x, nw.reshape(1, k), w)


def _rms_matmul_hm_body(x_ref, nw_ref, w_ref, o_ref, hm_ref, *, col0):
    x = x_ref[...]
    inv = lax.rsqrt(jnp.mean(x * x, axis=-1, keepdims=True) + RMS_EPS)
    xn = ((x * inv) * nw_ref[...]).astype(BF16)
    acc = jnp.dot(xn, w_ref[...], preferred_element_type=F32)
    o_ref[...] = acc
    dh = hm_ref.shape[3]
    for cg in range(hm_ref.shape[1]):
        hm_ref[0, cg] = acc[:, col0 + cg * dh:col0 + (cg + 1) * dh].astype(hm_ref.dtype)


def _rms_matmul_hm(x, nw, w, b, t, col0, n_cg, dh):
    m, k = x.shape
    n = w.shape[1]
    tm = HM_ROWS
    tps = t // tm
    return pl.pallas_call(
        functools.partial(_rms_matmul_hm_body, col0=col0),
        grid=(m // tm,),
        in_specs=[pl.BlockSpec((tm, k), lambda i: (i, 0)),
                  pl.BlockSpec((1, k), lambda i: (0, 0)),
                  pl.BlockSpec((k, n), lambda i: (0, 0))],
        out_specs=[pl.BlockSpec((tm, n), lambda i: (i, 0)),
                   pl.BlockSpec((1, n_cg, tm, dh), lambda i: (i // tps, 0, i % tps, 0))],
        out_shape=[jax.ShapeDtypeStruct((m, n), F32),
                   jax.ShapeDtypeStruct((b, n_cg, t, dh), BF16)],
        compiler_params=_params("parallel"),
        name="rms_matmul_hm",
    )(x, nw.reshape(1, k), w)


def _matmul_res_body(a_ref, w_ref, r_ref, o_ref):
    o_ref[...] = r_ref[...] + jnp.dot(a_ref[...].astype(BF16), w_ref[...], preferred_element_type=F32)


def _matmul_res(a, w, res):
    m, k = a.shape
    n = w.shape[1]
    tm = min(m, 1024)
    return pl.pallas_call(
        _matmul_res_body,
        grid=(m // tm,),
        in_specs=[pl.BlockSpec((tm, k), lambda i: (i, 0)),
                  pl.BlockSpec((k, n), lambda i: (0, 0)),
                  pl.BlockSpec((tm, n), lambda i: (i, 0))],
        out_specs=pl.BlockSpec((tm, n), lambda i: (i, 0)),
        out_shape=jax.ShapeDtypeStruct((m, n), F32),
        compiler_params=_params("parallel"),
        name="matmul_res",
    )(a, w, res)


def _rms_body(x_ref, nw_ref, o_ref):
    x = x_ref[...]
    inv = lax.rsqrt(jnp.mean(x * x, axis=-1, keepdims=True) + RMS_EPS)
    o_ref[...] = (x * inv) * nw_ref[...]


def _rms(x, nw):
    m, k = x.shape
    tm = min(m, 1024)
    return pl.pallas_call(
        _rms_body,
        grid=(m // tm,),
        in_specs=[pl.BlockSpec((tm, k), lambda i: (i, 0)), pl.BlockSpec((1, k), lambda i: (0, 0))],
        out_specs=pl.BlockSpec((tm, k), lambda i: (i, 0)),
        out_shape=jax.ShapeDtypeStruct((m, k), F32),
        compiler_params=_params("parallel"),
        name="final_rms",
    )(x, nw.reshape(1, k))


def _shifted(x, prev_rows, shift, row):
    nb = prev_rows.shape[1]
    y = pltpu.roll(x, shift, 1)
    for r in range(shift):
        y = jnp.where(row == r, prev_rows[:, nb - shift + r:nb - shift + r + 1], y)
    return y


def _batch_block(b, t):
    bb = max(1, min(b, SEQ_ROWS_PER_STEP // t))
    while b % bb:
        bb -= 1
    return bb


def _causal_conv(x, buf, w):
    width = w.shape[0]
    t = x.shape[1]

    def conv(xs, exact):
        row = lax.broadcasted_iota(jnp.int32, xs.shape, 1)
        y = None
        for i in range(width):
            shift = width - 1 - i
            if shift == 0:
                term = xs
            elif exact:
                term = _shifted(xs, buf, shift, row)
            else:
                term = pltpu.roll(xs, shift, 1)
            term = term * w[i:i + 1]
            y = term if y is None else y + term
        return y

    if t <= CONV_HEAD_ROWS:
        return conv(x, True), None
    return conv(x, False), conv(x[:, :CONV_HEAD_ROWS], True)


def _ffn_act_body(a_ref, g_ref, buf_ref, cw_ref, act_ref, nb_ref):
    a = a_ref[...].astype(F32)
    g = g_ref[...].astype(F32)
    t = a.shape[1]
    y, y_head = _causal_conv(a, buf_ref[...], cw_ref[...])
    act_ref[...] = (_silu(y) * g).astype(act_ref.dtype)
    if y_head is not None:
        act_ref[:, :CONV_HEAD_ROWS, :] = (_silu(y_head) * g[:, :CONV_HEAD_ROWS]).astype(act_ref.dtype)
    nb_ref[...] = a[:, t - (FFN_CONV_W - 1):t, :]


def _ffn_act(h, buf, cw):
    b, t, _ = h.shape
    nj = D_FF // FFN_TC
    bb = _batch_block(b, t)
    return pl.pallas_call(
        _ffn_act_body,
        grid=(b // bb, nj),
        in_specs=[pl.BlockSpec((bb, t, FFN_TC), lambda i, j: (i, 0, j)),
                  pl.BlockSpec((bb, t, FFN_TC), lambda i, j: (i, 0, j + nj)),
                  pl.BlockSpec((bb, FFN_CONV_W - 1, FFN_TC), lambda i, j: (i, 0, j)),
                  pl.BlockSpec((FFN_CONV_W, FFN_TC), lambda i, j: (0, j))],
        out_specs=[pl.BlockSpec((bb, t, FFN_TC), lambda i, j: (i, 0, j)),
                   pl.BlockSpec((bb, FFN_CONV_W - 1, FFN_TC), lambda i, j: (i, 0, j))],
        out_shape=[jax.ShapeDtypeStruct((b, t, D_FF), BF16),
                   jax.ShapeDtypeStruct((b, FFN_CONV_W - 1, D_FF), F32)],
        compiler_params=_params("parallel", "parallel"),
        name="ffn_act",
    )(h, h, buf, cw)


GDN_PRE_TC = 512


def _gdn_pre_body(x_ref, buf_ref, cw_ref, o_ref, nb_ref):
    j = pl.program_id(1)
    x = x_ref[...].astype(F32)
    t = x.shape[1]
    is_q = j < GDN_QK_DIM // GDN_PRE_TC
    is_v = j >= 2 * GDN_QK_DIM // GDN_PRE_TC
    qscale = jnp.where(is_q, GDN_DK ** -0.5, 1.0).astype(F32)

    def finish(y, n_rows):
        y = _silu(y)
        for h in range(GDN_PRE_TC // GDN_DK):
            yh = y[:, :, h * GDN_DK:(h + 1) * GDN_DK]
            inv = lax.rsqrt(jnp.sum(yh * yh, axis=-1, keepdims=True) + L2_EPS)
            o_ref[:, :n_rows, h * GDN_DK:(h + 1) * GDN_DK] = jnp.where(
                is_v, yh, yh * (inv * qscale)).astype(o_ref.dtype)

    y, y_head = _causal_conv(x, buf_ref[...], cw_ref[...])
    finish(y, t)
    if y_head is not None:
        finish(y_head, CONV_HEAD_ROWS)
    nb_ref[...] = x[:, t - (GDN_CONV_W - 1):t, :]


def _gdn_pre(proj, buf, cw):
    b, t, _ = proj.shape
    nj = GDN_CONV_CH // GDN_PRE_TC
    bb = _batch_block(b, t)
    return pl.pallas_call(
        _gdn_pre_body,
        grid=(b // bb, nj),
        in_specs=[pl.BlockSpec((bb, t, GDN_PRE_TC), lambda i, j: (i, 0, j)),
                  pl.BlockSpec((bb, GDN_CONV_W - 1, GDN_PRE_TC), lambda i, j: (i, 0, j)),
                  pl.BlockSpec((GDN_CONV_W, GDN_PRE_TC), lambda i, j: (0, j))],
        out_specs=[pl.BlockSpec((bb, t, GDN_PRE_TC), lambda i, j: (i, 0, j)),
                   pl.BlockSpec((bb, GDN_CONV_W - 1, GDN_PRE_TC), lambda i, j: (i, 0, j))],
        out_shape=[jax.ShapeDtypeStruct((b, t, GDN_CONV_CH), BF16),
                   jax.ShapeDtypeStruct((b, GDN_CONV_W - 1, GDN_CONV_CH), F32)],
        compiler_params=_params("parallel", "parallel"),
        name="gdn_pre",
    )(proj, buf, cw)


def _gdn_gate_body(x_ref, alog_ref, dtb_ref, g_ref, gcum_ref, beta_ref, *, chunk):
    x = x_ref[...]
    z = x + dtb_ref[...]
    softplus = jnp.maximum(z, 0.0) + jnp.log(1.0 + jnp.exp(-jnp.abs(z)))
    g = -jnp.exp(alog_ref[...]) * softplus
    g_ref[...] = g
    row = lax.broadcasted_iota(jnp.int32, x.shape, 1) % chunk
    acc = g
    s = 1
    while s < chunk:
        acc = acc + jnp.where(row >= s, pltpu.roll(acc, s, 1), 0.0)
        s *= 2
    gcum_ref[...] = acc
    beta_ref[...] = _sigmoid(x)


def _gdn_gate(proj, alog_pad, dtb_pad, lane_block):
    b, t, _ = proj.shape
    chunk = min(GDN_CHUNK, t)
    bb = _batch_block(b, t)
    spec = pl.BlockSpec((bb, t, 128), lambda i: (i, 0, 0))
    return pl.pallas_call(
        functools.partial(_gdn_gate_body, chunk=chunk),
        grid=(b // bb,),
        in_specs=[pl.BlockSpec((bb, t, 128), lambda i: (i, 0, lane_block)),
                  pl.BlockSpec((1, 128), lambda i: (0, 0)),
                  pl.BlockSpec((1, 128), lambda i: (0, 0))],
        out_specs=[spec, spec, spec],
        out_shape=[jax.ShapeDtypeStruct((b, t, 128), F32)] * 3,
        compiler_params=_params("parallel"),
        name="gdn_gate",
    )(proj, alog_pad, dtb_pad)


def _unit_lower_inverses(lows, n):
    eye = (lax.broadcasted_iota(jnp.int32, (n, n), 0) == lax.broadcasted_iota(jnp.int32, (n, n), 1)).astype(F32)
    ps = [eye - low for low in lows]
    ms = [_mm(low, low) for low in lows]
    k = 2
    while True:
        ps = [p + _mm(p, m) for p, m in zip(ps, ms)]
        k *= 2
        if k >= n:
            break
        ms = [_mm(m, m) for m in ms]
    return ps


def _gdn_scan_body(q_ref, k_ref, v_ref, z_ref, gc_ref, bc_ref, gr_ref, br_ref, s0_ref, onw_ref, o_ref, s_ref,
                   u_ref, w_ref, qk_ref, qd_ref, kd_ref):
    c = GDN_CHUNK
    hb = s0_ref.shape[1]
    n_chunks = q_ref.shape[1] // c
    s_ref[...] = s0_ref[...]
    ri = lax.broadcasted_iota(jnp.int32, (c, c), 0)
    ci = lax.broadcasted_iota(jnp.int32, (c, c), 1)
    tri = ri >= ci
    stri = ri > ci
    onw = onw_ref[...]

    def qk_slice(ref, r0, qh):
        return ref[0, pl.ds(r0, c), qh * GDN_DK:(qh + 1) * GDN_DK]

    def v_slice(ref, r0, hh):
        return ref[0, pl.ds(r0, c), hh * GDN_DV:(hh + 1) * GDN_DV]

    cpb = max(c_ for c_ in (GDN_PREP_CHUNKS, 4, 2, 1) if n_chunks % c_ == 0)

    def prep(nb, carry):
        items = []
        kk, qk, ks, qs = {}, {}, {}, {}
        for ch in range(cpb):
            n = nb * cpb + ch
            r0 = pl.multiple_of(n * c, c)
            gcol = gc_ref[0, 0, pl.ds(r0, c), :]
            bcol = bc_ref[0, 0, pl.ds(r0, c), :]
            grow = gr_ref[0, 0, n]
            brow = br_ref[0, 0, n]
            for qh in range(hb // 2):
                ks[ch, qh] = qk_slice(k_ref, r0, qh)
                qs[ch, qh] = qk_slice(q_ref, r0, qh)
            for hh in range(hb):
                items.append(dict(ch=ch, hh=hh, r0=r0, v=v_slice(v_ref, r0, hh),
                                  gcb=jnp.broadcast_to(gcol[:, hh:hh + 1], (c, GDN_DK)),
                                  bcb=jnp.broadcast_to(bcol[:, hh:hh + 1], (c, c)),
                                  gr=grow[hh:hh + 1, :], br=brow[hh:hh + 1, :]))
        for key in ks:
            kk[key] = _mm_nt(ks[key], ks[key])
        for key in ks:
            qk[key] = _mm_nt(qs[key], ks[key])
        lows = []
        for it in items:
            key = (it["ch"], it["hh"] // 2)
            it["decay"] = jnp.where(tri, jnp.exp(jnp.where(tri, it["gcb"][:, :c] - it["gr"], 0.0)), 0.0)
            lows.append(jnp.where(stri, (kk[key] * it["bcb"]) * it["decay"], 0.0))
        tinvs = _unit_lower_inverses(lows, c)
        us = [_mm(tinv * it["br"], it["v"]) for tinv, it in zip(tinvs, items)]
        ws = [_mm(tinv * (it["br"] * jnp.exp(it["gr"])), ks[it["ch"], it["hh"] // 2])
              for tinv, it in zip(tinvs, items)]
        for it, u, w in zip(items, us, ws):
            key = (it["ch"], it["hh"] // 2)
            hh, r0, gcb = it["hh"], it["r0"], it["gcb"]
            u_ref[hh, pl.ds(r0, c), :] = u
            w_ref[hh, pl.ds(r0, c), :] = w.astype(BF16)
            qk_ref[hh, pl.ds(r0, c), :] = jnp.where(tri, qk[key] * it["decay"], 0.0).astype(BF16)
            qd_ref[hh, pl.ds(r0, c), :] = (qs[key] * jnp.exp(gcb)).astype(BF16)
            kd_ref[hh, pl.ds(r0, c), :] = (ks[key] * jnp.exp(gcb[c - 1:c, :] - gcb)).astype(BF16)
        return carry

    lax.fori_loop(0, n_chunks // cpb, prep, 0)

    def scan(n, carry):
        r0 = pl.multiple_of(n * c, c)
        g_last = gc_ref[0, 0, pl.ds(r0 + (c - 1), 1), :]
        loaded = []
        for hh in range(hb):
            loaded.append((u_ref[hh, pl.ds(r0, c), :], w_ref[hh, pl.ds(r0, c), :], qk_ref[hh, pl.ds(r0, c), :],
                           qd_ref[hh, pl.ds(r0, c), :], kd_ref[hh, pl.ds(r0, c), :], v_slice(z_ref, r0, hh),
                           s_ref[0, hh]))
        ws_s = [_mm(w, s) for (u, w, qkm, qd, kd, z, s) in loaded]
        qd_s = [_mm(qd, s) for (u, w, qkm, qd, kd, z, s) in loaded]
        v_news = [ld[0] - ws for ld, ws in zip(loaded, ws_s)]
        qk_v = [_mm(ld[2], vn) for ld, vn in zip(loaded, v_news)]
        kd_v = [_mm_tn(ld[4], vn) for ld, vn in zip(loaded, v_news)]
        results = []
        for hh, ld in enumerate(loaded):
            o = qd_s[hh] + qk_v[hh]
            s_new = ld[6] * jnp.exp(g_last[:, hh:hh + 1]) + kd_v[hh]
            inv = lax.rsqrt(jnp.mean(o * o, axis=-1, keepdims=True) + RMS_EPS)
            results.append((((o * inv) * onw) * _silu(ld[5].astype(F32)), s_new))
        for hh, (og, s_new) in enumerate(results):
            s_ref[0, hh] = s_new
            o_ref[0, pl.ds(r0, c), hh * GDN_DV:(hh + 1) * GDN_DV] = og.astype(o_ref.dtype)
        return carry

    lax.fori_loop(0, n_chunks, scan, 0)


def _gdn_scan(qkv, proj, gcol, bcol, grow, brow, s0, layer, onw):
    b, t, _ = qkv.shape
    hb = gcol.shape[3]
    hg = GDN_V_HEADS // hb
    qw = hb // 2 * GDN_DK
    vw = hb * GDN_DV
    n = t // GDN_CHUNK
    return pl.pallas_call(
        _gdn_scan_body,
        grid=(b, hg),
        in_specs=[pl.BlockSpec((1, t, qw), lambda i, j: (i, 0, j)),
                  pl.BlockSpec((1, t, qw), lambda i, j: (i, 0, GDN_QK_DIM // qw + j)),
                  pl.BlockSpec((1, t, vw), lambda i, j: (i, 0, 2 * GDN_QK_DIM // vw + j)),
                  pl.BlockSpec((1, t, vw), lambda i, j: (i, 0, GDN_CONV_CH // vw + j)),
                  pl.BlockSpec((1, 1, t, hb), lambda i, j: (i, j, 0, 0)),
                  pl.BlockSpec((1, 1, t, hb), lambda i, j: (i, j, 0, 0)),
                  pl.BlockSpec((1, 1, n, hb, GDN_CHUNK), lambda i, j: (i, j, 0, 0, 0)),
                  pl.BlockSpec((1, 1, n, hb, GDN_CHUNK), lambda i, j: (i, j, 0, 0, 0)),
                  pl.BlockSpec((None, 1, hb, GDN_DK, GDN_DV), lambda i, j: (layer, i, j, 0, 0)),
                  pl.BlockSpec((1, GDN_DV), lambda i, j: (0, 0))],
        out_specs=[pl.BlockSpec((1, t, vw), lambda i, j: (i, 0, j)),
                   pl.BlockSpec((1, hb, GDN_DK, GDN_DV), lambda i, j: (i, j, 0, 0))],
        out_shape=[jax.ShapeDtypeStruct((b, t, GDN_V_DIM), BF16),
                   jax.ShapeDtypeStruct((b, GDN_V_HEADS, GDN_DK, GDN_DV), F32)],
        scratch_shapes=[pltpu.VMEM((hb, t, GDN_DV), F32),
                        pltpu.VMEM((hb, t, GDN_DK), BF16),
                        pltpu.VMEM((hb, t, GDN_CHUNK), BF16),
                        pltpu.VMEM((hb, t, GDN_DK), BF16),
                        pltpu.VMEM((hb, t, GDN_DK), BF16)],
        compiler_params=_params("parallel", "parallel"),
        name="gdn_scan",
    )(qkv, qkv, qkv, proj, gcol, bcol, grow, brow, s0, onw.reshape(1, GDN_DV))


def _gdn_layer(x, s0, layer, conv_buf, norm_w, w_in, conv_w, alog_pad, dtb_pad, out_norm_w, w_out):
    b, t, d = x.shape
    xf = x.reshape(b * t, d)
    w_main, w_tail = w_in
    proj = _rms_matmul(xf, norm_w, w_main, 2048, BF16).reshape(b, t, -1)
    tail = _rms_matmul(xf, norm_w, w_tail, 128).reshape(b, t, -1)
    qkv, new_buf = _gdn_pre(proj, conv_buf, conv_w)
    _, gcum, beta = _gdn_gate(tail, alog_pad, dtb_pad, 0)
    gcum = gcum[:, :, :GDN_V_HEADS]
    beta = beta[:, :, GDN_V_HEADS:2 * GDN_V_HEADS]
    tp = -(-t // GDN_CHUNK) * GDN_CHUNK
    if tp != t:
        pad = [(0, 0), (0, tp - t), (0, 0)]
        qkv = jnp.pad(qkv, pad)
        proj_z = jnp.pad(proj, pad)
        gcum = jnp.pad(gcum, pad, mode="edge")
        beta = jnp.pad(beta, pad)
    else:
        proj_z = proj
    hb = GDN_V_HEADS if tp == GDN_CHUNK else GDN_HEADS_PER_STEP
    hg = GDN_V_HEADS // hb
    n = tp // GDN_CHUNK
    gcol = gcum.reshape(b, tp, hg, hb).transpose(0, 2, 1, 3)
    bcol = beta.reshape(b, tp, hg, hb).transpose(0, 2, 1, 3)
    grow = gcum.reshape(b, n, GDN_CHUNK, hg, hb).transpose(0, 3, 1, 4, 2)
    brow = beta.reshape(b, n, GDN_CHUNK, hg, hb).transpose(0, 3, 1, 4, 2)
    o, s_new = _gdn_scan(qkv, proj_z, gcol, bcol, grow, brow, s0, layer, out_norm_w)
    o = o[:, :t].reshape(b * t, GDN_V_DIM)
    y = _matmul_res(o, w_out, xf).reshape(b, t, d)
    return y, s_new, new_buf


def _page_gather_body(pt_ref, *refs, pps):
    cmp_refs, o_ref = refs[:pps], refs[pps]
    sub = cmp_refs[0].shape[3]
    for s in range(pps):
        o_ref[0, :, s * sub:(s + 1) * sub, :] = cmp_refs[s][0, 0]


def _page_gather(cache_cmp, layer, page_table):
    b, n_pages = page_table.shape
    _, _, n_cg, sub, flat = cache_cmp.shape
    pps = 16 if n_pages % 16 == 0 else (8 if n_pages % 8 == 0 else 1)
    n_steps = n_pages // pps

    def page_map(s):
        return lambda i, p, pt: (layer, pt[i, p * pps + s], 0, 0, 0)

    grid_spec = pltpu.PrefetchScalarGridSpec(
        num_scalar_prefetch=1,
        grid=(b, n_steps),
        in_specs=[pl.BlockSpec((1, 1, n_cg, sub, flat), page_map(s)) for s in range(pps)],
        out_specs=pl.BlockSpec((1, n_cg, pps * sub, flat), lambda i, p, pt: (i, 0, p, 0)),
    )
    return pl.pallas_call(
        functools.partial(_page_gather_body, pps=pps),
        grid_spec=grid_spec,
        out_shape=jax.ShapeDtypeStruct((b, n_cg, n_pages * sub, flat), cache_cmp.dtype),
        compiler_params=_params("parallel", "arbitrary"),
        name="page_gather",
    )(page_table, *([cache_cmp] * pps))


def _compress_body(x_ref, w1_ref, pe_ref, w2_ref, o_ref, bias_ref, *, n_sub):
    hd = CMP_HIDDEN

    @pl.when(pl.program_id(1) == 0)
    def _():
        pe = pe_ref[0]
        pe0 = jnp.broadcast_to(pe[0:1], (8, pe.shape[1]))
        pe1 = jnp.broadcast_to(pe[1:2], (8, pe.shape[1]))
        bias_ref[:, :hd] = jnp.dot(pe0.astype(BF16), w1_ref[0, :, :hd], preferred_element_type=F32)
        bias_ref[:, hd:] = jnp.dot(pe1.astype(BF16), w1_ref[0, :, hd:], preferred_element_type=F32)

    acc = jnp.dot(x_ref[0, 0], w1_ref[0], preferred_element_type=F32)
    first = acc[:, :hd] + bias_ref[0:1, :hd]
    second = acc[:, hd:] + bias_ref[0:1, hd:]
    hid = _silu(first + pltpu.roll(second, n_sub - 1, 0))
    o_ref[0, 0] = jnp.dot(hid.astype(BF16), w2_ref[0], preferred_element_type=F32)


def _compress(x, n_sub, w1cat, pe_flat, w2):
    b, _, _, flat = x.shape
    dh = flat // CMP_STRIDE
    return pl.pallas_call(
        functools.partial(_compress_body, n_sub=n_sub),
        grid=(2 * NSA_G, b),
        in_specs=[pl.BlockSpec((1, 1, n_sub, flat), lambda j, i: (i, j, 0, 0)),
                  pl.BlockSpec((1, flat, 2 * CMP_HIDDEN), lambda j, i: (j // NSA_G, 0, 0)),
                  pl.BlockSpec((1, 2, flat), lambda j, i: (j // NSA_G, 0, 0)),
                  pl.BlockSpec((1, CMP_HIDDEN, dh), lambda j, i: (j // NSA_G, 0, 0))],
        out_specs=pl.BlockSpec((1, 1, n_sub, dh), lambda j, i: (i, j, 0, 0)),
        out_shape=jax.ShapeDtypeStruct((b, 2 * NSA_G, n_sub, dh), F32),
        scratch_shapes=[pltpu.VMEM((8, 2 * CMP_HIDDEN), F32)],
        compiler_params=_params("arbitrary", "arbitrary"),
        name="nsa_compress",
    )(x, w1cat, pe_flat, w2)


def _inv_or_zero(l):
    return jnp.where(l > 0.0, 1.0 / jnp.where(l > 0.0, l, 1.0), 0.0)


def _attn_branch(q, slopes, t_pos, k, v, kp, ok_fn, may_be_empty=False):
    r, qb, nk = slopes.shape[0], t_pos.shape[1], k.shape[0]
    dist = t_pos - kp
    ok = ok_fn(dist)
    pen = jnp.where(ok, dist.astype(F32), MASKED_DIST)
    s = _mm_nt(q, k).reshape(r, qb, nk) - slopes * pen
    m = jnp.max(s, axis=-1, keepdims=True)
    p = jnp.exp2(s - m)
    if may_be_empty:
        p = jnp.where(ok, p, 0.0)
    inv = _inv_or_zero(jnp.sum(p, axis=-1, keepdims=True))
    o = _mm(p.reshape(r * qb, nk), v) * inv.reshape(r * qb, 1)
    return o, p, inv


def _pick_blocks(p_sum, t_lane, n_cmp, n_blk, nb_rows):
    nq, nc = p_sum.shape
    nb = -(-n_blk // 8) * 8
    jj = lax.broadcasted_iota(jnp.int32, (nb, nc), 0)
    nn = lax.broadcasted_iota(jnp.int32, (nb, nc), 1)
    overlap_t = ((nn * CMP_STRIDE < (jj + 1) * SEL_BLOCK) & (nn * CMP_STRIDE + CMP_LEN > jj * SEL_BLOCK)
                 & (nn < n_cmp) & (jj < n_blk)).astype(F32)
    imp = lax.dot_general(overlap_t, p_sum, (((1,), (1,)), ((), ())), preferred_element_type=F32,
                          precision=lax.Precision.HIGHEST)
    j = lax.broadcasted_iota(jnp.int32, (nb, nq), 0)
    cur = t_lane // SEL_BLOCK
    imp = jnp.where((j == 0) | (j == cur) | (j == cur - 1), FORCE_SCORE, imp)
    imp = jnp.where(j > cur, -FORCE_SCORE, imp)
    imp = jnp.where(j >= n_blk, NEVER, imp)
    rank = jnp.zeros((nb, nq), jnp.int32)
    for jp in range(n_blk):
        row = imp[jp:jp + 1, :]
        rank = rank + ((row > imp) | ((row == imp) & (j > jp))).astype(jnp.int32)
    picked_t = ((rank < SEL_TOPN) & (j < n_blk)).astype(F32)
    if nb_rows > nb:
        picked_t = jnp.concatenate([picked_t, jnp.zeros((nb_rows - nb, nq), F32)], axis=0)
    return picked_t


def _transpose_01(x_t):
    m = x_t.shape[1]
    eye = (lax.broadcasted_iota(jnp.int32, (m, m), 0) == lax.broadcasted_iota(jnp.int32, (m, m), 1))
    return _mm_nt(eye.astype(BF16), x_t)


def _head_rows(x, n_heads):
    dh = x.shape[1] // n_heads
    return jnp.concatenate([x[:, h * dh:(h + 1) * dh] for h in range(n_heads)], axis=0)


def _nsa_attn_body(q_ref, gate_ref, slope_ref, ck_ref, cv_ref, ks_ref, vs_ref, kw_ref, vw_ref, o_ref, osel_ref,
                   *, nsb, **kw):
    refs = (q_ref, gate_ref, slope_ref, ck_ref, cv_ref, ks_ref, vs_ref, kw_ref, vw_ref, o_ref, osel_ref)
    if nsb == 1:
        _nsa_attn_block(0, pl.program_id(2), *refs, **kw)
    else:
        def one(sub, carry):
            _nsa_attn_block(sub, pl.program_id(2) * nsb + sub, *refs, **kw)
            return carry

        lax.fori_loop(0, nsb, one, 0)


def _nsa_attn_block(sub, i, q_ref, gate_ref, slope_ref, ck_ref, cv_ref, ks_ref, vs_ref, kw_ref, vw_ref, o_ref,
                    osel_ref, *, qb, p_len, n_cmp, n_blk, w0, sel_classes, kw_len):
    r = NSA_R
    rows = r * qb
    rq = sub * qb if isinstance(sub, int) else pl.multiple_of(sub * qb, qb)
    q = _head_rows(q_ref[0, pl.ds(rq, qb), :].astype(F32), r)
    q = (q * (NSA_DH ** -0.5 * LOG2E)).astype(BF16)
    slopes = slope_ref[0] * LOG2E
    q0 = p_len + i * qb
    t_pos = q0 + lax.broadcasted_iota(jnp.int32, (1, qb, 1), 1)
    branch = functools.partial(_attn_branch, q, slopes, t_pos)

    tw = kw_ref.shape[2]
    if tw == kw_len:
        start = 0
    else:
        start = pl.multiple_of(jnp.clip(q0 + qb - w0 - kw_len, 0, tw - kw_len), SEL_BLOCK)
    kp_win = w0 + start + lax.broadcasted_iota(jnp.int32, (1, 1, kw_len), 2)
    o_win, _, _ = branch(kw_ref[0, 0, pl.ds(start, kw_len), :], vw_ref[0, 0, pl.ds(start, kw_len), :], kp_win,
                         lambda dist: (dist >= 0) & (dist < WINDOW) & (kp_win >= 0))

    nc = ck_ref.shape[2]
    n_idx = lax.broadcasted_iota(jnp.int32, (1, 1, nc), 2)
    o_cmp, p_cmp, inv_cmp = branch(ck_ref[0, 0], cv_ref[0, 0], n_idx * CMP_STRIDE + (CMP_LEN - 1),
                                   lambda dist: (dist >= 0) & (n_idx < n_cmp), may_be_empty=True)
    p_sum = jnp.sum(p_cmp * inv_cmp, axis=0)
    nb8 = -(-n_blk // 16) * 16
    picked_t = _pick_blocks(p_sum, q0 + lax.broadcasted_iota(jnp.int32, (1, qb), 1), n_cmp, n_blk, nb8)
    picked = _transpose_01(picked_t)

    def sel_branch(nk):
        blk_of_key = lax.broadcasted_iota(jnp.int32, (nb8, nk), 1) // SEL_BLOCK
        expand = (blk_of_key == lax.broadcasted_iota(jnp.int32, (nb8, nk), 0)).astype(BF16)
        key_picked = (_mm(picked, expand) > 0.5).reshape(1, qb, nk)
        o, _, _ = branch(ks_ref[0, 0, 0:nk, :], vs_ref[0, 0, 0:nk, :],
                         lax.broadcasted_iota(jnp.int32, (1, 1, nk), 2),
                         lambda dist: key_picked & (dist >= 0))
        osel_ref[...] = o

    if len(sel_classes) == 1:
        sel_branch(sel_classes[0])
    else:
        need = q0 + qb
        prev = 0
        for nk in sel_classes:
            pl.when((need > prev) & (need <= nk))(functools.partial(sel_branch, nk))
            prev = nk

    gates = _sigmoid(gate_ref[0, 0, :, pl.ds(rq, qb), :]).reshape(rows, 3)
    o = gates[:, 0:1] * o_cmp + gates[:, 1:2] * osel_ref[...] + gates[:, 2:3] * o_win
    for h in range(r):
        o_ref[0, pl.ds(rq, qb), h * NSA_DH:(h + 1) * NSA_DH] = o[h * qb:(h + 1) * qb].astype(o_ref.dtype)


def _nsa_attn(q, gates_t, slopes, cmp_kv, sel_arr, sel_k0, sel_v0, win_arr, win_k0, win_v0,
              *, p_len, n_cmp, n_blk, w0):
    b, tq, _ = q.shape
    g, r, dh = NSA_G, NSA_R, NSA_DH
    qb = NSA_Q_BLOCK if tq % NSA_Q_BLOCK == 0 else tq
    nc = cmp_kv.shape[2]
    tk = sel_arr.shape[2]
    tw = win_arr.shape[2]
    if tq == qb or tk % NSA_SEL_CLASS != 0:
        sel_classes = (tk,)
    else:
        sel_classes = tuple(range(NSA_SEL_CLASS, tk + 1, NSA_SEL_CLASS))
    kw_len = min(tw, -(-(WINDOW - 1 + qb) // KEY_CHUNK) * KEY_CHUNK)
    nsb = NSA_Q_BLOCKS_PER_STEP if (tq // qb) % NSA_Q_BLOCKS_PER_STEP == 0 else 1
    body = functools.partial(_nsa_attn_body, nsb=nsb, qb=qb, p_len=p_len, n_cmp=n_cmp, n_blk=n_blk, w0=w0,
                             sel_classes=sel_classes, kw_len=kw_len)
    qs = nsb * qb
    return pl.pallas_call(
        body,
        grid=(b, g, tq // qs),
        in_specs=[pl.BlockSpec((1, qs, r * dh), lambda bi, gi, i: (bi, i, gi)),
                  pl.BlockSpec((1, 1, r, qs, 3), lambda bi, gi, i: (bi, gi, 0, i, 0)),
                  pl.BlockSpec((1, r, 1, 1), lambda bi, gi, i: (gi, 0, 0, 0)),
                  pl.BlockSpec((1, 1, nc, dh), lambda bi, gi, i: (bi, gi, 0, 0)),
                  pl.BlockSpec((1, 1, nc, dh), lambda bi, gi, i: (bi, NSA_G + gi, 0, 0)),
                  pl.BlockSpec((1, 1, tk, dh), lambda bi, gi, i: (bi, sel_k0 + gi, 0, 0)),
                  pl.BlockSpec((1, 1, tk, dh), lambda bi, gi, i: (bi, sel_v0 + gi, 0, 0)),
                  pl.BlockSpec((1, 1, tw, dh), lambda bi, gi, i: (bi, win_k0 + gi, 0, 0)),
                  pl.BlockSpec((1, 1, tw, dh), lambda bi, gi, i: (bi, win_v0 + gi, 0, 0))],
        out_specs=pl.BlockSpec((1, qs, r * dh), lambda bi, gi, i: (bi, i, gi)),
        out_shape=jax.ShapeDtypeStruct((b, tq, g * r * dh), BF16),
        scratch_shapes=[pltpu.VMEM((r * qb, dh), F32)],
        compiler_params=_params("parallel", "parallel", "arbitrary"),
        name="nsa_attn",
    )(q, gates_t, slopes, cmp_kv, cmp_kv, sel_arr, sel_arr, win_arr, win_arr)


def _nsa_paged_body(pt_ref, *refs, tq, p_len, n_cmp, n_blk, w0, pps, n_steps):
    k_refs, v_refs = refs[:pps], refs[pps:2 * pps]
    (q_ref, gate_ref, slope_ref, cmp_ref, new_ref, win_ref, o_ref,
     qbd_ref, pen_ref, m_ref, l_ref, acc_ref, ocmp_ref) = refs[2 * pps:]
    p = pl.program_id(1)
    g_n, r = NSA_G, NSA_R
    dh = NSA_DH
    rg = r * tq
    rows = g_n * rg
    page = k_refs[0].shape[5]
    chunk = pps * page
    n_keys = pen_ref.shape[1]
    t_pos = p_len + lax.broadcasted_iota(jnp.int32, (1, tq, 1), 1)

    @pl.when(p == 0)
    def _():
        qbd_ref[...] = jnp.zeros(qbd_ref.shape, qbd_ref.dtype)
        nc = cmp_ref.shape[2]
        n_idx = lax.broadcasted_iota(jnp.int32, (1, 1, nc), 2)
        p_sums = []
        for g in range(g_n):
            q = _head_rows(q_ref[0, :, g * r * dh:(g + 1) * r * dh].astype(F32), r)
            q = (q * (dh ** -0.5 * LOG2E)).astype(BF16)
            qbd_ref[g * rg:(g + 1) * rg, g * dh:(g + 1) * dh] = q
            o_cmp, p_cmp, inv_cmp = _attn_branch(
                q, slope_ref[g] * LOG2E, t_pos, cmp_ref[0, g], cmp_ref[0, g_n + g],
                n_idx * CMP_STRIDE + (CMP_LEN - 1), lambda dist: (dist >= 0) & (n_idx < n_cmp), may_be_empty=True)
            ocmp_ref[g * rg:(g + 1) * rg, :] = o_cmp
            p_sums.append(jnp.sum(p_cmp * inv_cmp, axis=0))
        nbp = -(-n_blk // 128) * 128
        nq = g_n * tq
        t_lane = p_len + lax.broadcasted_iota(jnp.int32, (1, nq), 1) % tq
        picked_t = _pick_blocks(jnp.concatenate(p_sums, axis=0), t_lane, n_cmp, n_blk, nbp)
        picked = _transpose_01(picked_t.astype(BF16))
        picked_rows = jnp.concatenate([picked[g * tq:(g + 1) * tq] for g in range(g_n) for _ in range(r)], axis=0)
        slope_rows = jnp.broadcast_to(slope_ref[...] * LOG2E, (g_n, r, tq, 1)).reshape(rows, 1)
        t_rows = p_len + lax.broadcasted_iota(jnp.int32, (rows, 1), 0) % tq

        def fill(k0, width):
            kp = k0 + lax.broadcasted_iota(jnp.int32, (1, width), 1)
            blk = lax.broadcasted_iota(jnp.int32, (nbp, width), 0)
            expand = ((k0 + lax.broadcasted_iota(jnp.int32, (nbp, width), 1)) // SEL_BLOCK == blk).astype(BF16)
            key_picked = _mm(picked_rows, expand) > 0.5
            dist = t_rows - kp
            return slope_rows * jnp.where(key_picked & (dist >= 0), dist.astype(F32), MASKED_DIST)

        def fill_chunk(c, carry):
            k0 = pl.multiple_of(c * chunk, chunk)
            pen_ref[:, pl.ds(k0, chunk)] = fill(k0, chunk)
            return carry

        lax.fori_loop(0, n_steps, fill_chunk, 0)
        pen_ref[:, n_steps * chunk:n_keys] = fill(n_steps * chunk, n_keys - n_steps * chunk)
        m_ref[...] = jnp.full(m_ref.shape, NEVER, F32)
        l_ref[...] = jnp.zeros(l_ref.shape, F32)
        acc_ref[...] = jnp.zeros(acc_ref.shape, F32)

    def online_update(kts, vts, k0):
        width = len(kts) * page
        qbd = qbd_ref[...]
        s = jnp.concatenate([jnp.dot(qbd, kt.astype(BF16), preferred_element_type=F32) for kt in kts], axis=1)
        s = s - pen_ref[:, pl.ds(k0, width)]
        m_prev = m_ref[...]
        m_new = jnp.maximum(m_prev, jnp.max(s, axis=-1, keepdims=True))
        alpha = jnp.exp2(m_prev - m_new)
        prob = jnp.exp2(s - m_new)
        l_ref[...] = alpha * l_ref[...] + jnp.sum(prob, axis=-1, keepdims=True)
        pv = [_mm_nt(prob[:, i * page:(i + 1) * page], vt) for i, vt in enumerate(vts)]
        acc_ref[...] = alpha * acc_ref[...] + sum(pv[1:], pv[0])
        m_ref[...] = m_new

    @pl.when(p < n_steps)
    def _():
        online_update([ref[0, 0, 0].reshape(g_n * dh, page) for ref in k_refs],
                      [ref[0, 0, 0].reshape(g_n * dh, page) for ref in v_refs], pl.multiple_of(p * chunk, chunk))

    @pl.when(p == n_steps)
    def _():
        online_update([new_ref[0, 0].reshape(g_n * dh, page)], [new_ref[0, 1].reshape(g_n * dh, page)],
                      n_steps * chunk)
        acc = acc_ref[...] * _inv_or_zero(l_ref[...])
        tw = win_ref.shape[2]
        kp_win = w0 + lax.broadcasted_iota(jnp.int32, (1, 1, tw), 2)
        for g in range(g_n):
            q = qbd_ref[g * rg:(g + 1) * rg, g * dh:(g + 1) * dh]
            o_win, _, _ = _attn_branch(q, slope_ref[g] * LOG2E, t_pos, win_ref[0, g], win_ref[0, g_n + g], kp_win,
                                       lambda dist: (dist >= 0) & (dist < WINDOW) & (kp_win >= 0))
            gates = _sigmoid(gate_ref[0, g]).reshape(rg, 3)
            o = (gates[:, 0:1] * ocmp_ref[g * rg:(g + 1) * rg, :]
                 + gates[:, 1:2] * acc[g * rg:(g + 1) * rg, g * dh:(g + 1) * dh] + gates[:, 2:3] * o_win)
            for h in range(r):
                o_ref[0, :, (g * r + h) * dh:(g * r + h + 1) * dh] = o[h * tq:(h + 1) * tq].astype(o_ref.dtype)


def _nsa_paged(q, gates_t, slopes, cmp_kv, cache_nt, layer, page_table, new_t, win_arr,
               *, p_len, n_cmp, n_blk, w0):
    b, tq, _ = q.shape
    g, r, dh = NSA_G, NSA_R, NSA_DH
    n_pages = page_table.shape[1]
    page = cache_nt.shape[5]
    pps = 8 if n_pages % 8 == 0 else (4 if n_pages % 4 == 0 else 1)
    n_steps = n_pages // pps
    nc = cmp_kv.shape[2]
    tw = win_arr.shape[2]
    rows = g * r * tq
    n_keys = (n_pages + 1) * page

    def page_map(c, s):
        return lambda i, p, pt: (layer, pt[i, jnp.minimum(p, n_steps - 1) * pps + s], c, 0, 0, 0)

    const = lambda i, p, pt: (i, 0, 0, 0, 0)
    grid_spec = pltpu.PrefetchScalarGridSpec(
        num_scalar_prefetch=1,
        grid=(b, n_steps + 1),
        in_specs=[pl.BlockSpec((1, 1, 1, g, dh, page), page_map(2, s)) for s in range(pps)]
        + [pl.BlockSpec((1, 1, 1, g, dh, page), page_map(3, s)) for s in range(pps)]
        + [pl.BlockSpec((1, tq, g * r * dh), lambda i, p, pt: (i, 0, 0)),
           pl.BlockSpec((1, g, r, tq, 3), const),
           pl.BlockSpec((g, r, 1, 1), lambda i, p, pt: (0, 0, 0, 0)),
           pl.BlockSpec((1, 2 * g, nc, dh), lambda i, p, pt: (i, 0, 0, 0)),
           pl.BlockSpec((1, 2, g, dh, page), const),
           pl.BlockSpec((1, 2 * g, tw, dh), lambda i, p, pt: (i, 0, 0, 0))],
        out_specs=pl.BlockSpec((1, tq, g * r * dh), lambda i, p, pt: (i, 0, 0)),
        scratch_shapes=[pltpu.VMEM((rows, g * dh), BF16),
                        pltpu.VMEM((rows, n_keys), F32),
                        pltpu.VMEM((rows, 1), F32),
                        pltpu.VMEM((rows, 1), F32),
                        pltpu.VMEM((rows, g * dh), F32),
                        pltpu.VMEM((rows, dh), F32)],
    )
    body = functools.partial(_nsa_paged_body, tq=tq, p_len=p_len, n_cmp=n_cmp, n_blk=n_blk, w0=w0, pps=pps,
                             n_steps=n_steps)
    return pl.pallas_call(
        body,
        grid_spec=grid_spec,
        out_shape=jax.ShapeDtypeStruct((b, tq, g * r * dh), BF16),
        compiler_params=_params("parallel", "arbitrary"),
        name="nsa_paged",
    )(page_table, *([cache_nt] * (2 * pps)), q, gates_t, slopes, cmp_kv, new_t, win_arr)


def _nsa_layer(x, cache_hm, layer, page_table, win_buf, norm_w, w_in, w1cat, pe_flat, w2, w_out, slopes):
    b, t, d = x.shape
    g, r, dh = NSA_G, NSA_R, NSA_DH
    xf = x.reshape(b * t, d)
    w_q, w_kv = w_in
    q = _rms_matmul(xf, norm_w, w_q, NSA_Q_DIM, BF16).reshape(b, t, NSA_Q_DIM)
    if cache_hm is None and t % HM_ROWS == 0:
        proj, kv_hm = _rms_matmul_hm(xf, norm_w, w_kv, b, t, 2 * NSA_KV_DIM, 4 * g, dh)
        proj = proj.reshape(b, t, -1)
    else:
        proj = _rms_matmul(xf, norm_w, w_kv, w_kv.shape[1]).reshape(b, t, -1)
        kv_hm = None
    kv = proj[..., :6 * NSA_KV_DIM]
    gates_t = proj[..., 6 * NSA_KV_DIM:6 * NSA_KV_DIM + 3 * NSA_HEADS]
    gates_t = gates_t.reshape(b, t, g, r, 3).transpose(0, 2, 3, 1, 4)
    kv6 = kv.reshape(b, t, 6, g, dh)
    new_rows = kv6[:, :, :4]
    if cache_hm is None:
        p_len = 0
        tk = t
        assert t % CMP_STRIDE == 0
        cmp_x = kv6[:, :, :2].astype(BF16).transpose(0, 2, 3, 1, 4).reshape(b, 2 * g, t // CMP_STRIDE,
                                                                           CMP_STRIDE * dh)
        if kv_hm is None:
            kv_hm = kv6[:, :, 2:].astype(BF16).transpose(0, 2, 3, 1, 4).reshape(b, 4 * g, t, dh)
        new_win = kv6[:, t - min(WINDOW, t):, 4:]
    else:
        cache_cmp, cache_nt = cache_hm
        n_pages = page_table.shape[1]
        page = cache_nt.shape[5]
        p_len = n_pages * page
        tk = p_len + t
        assert p_len % CMP_STRIDE == 0 and t < CMP_STRIDE and t <= page
        cmp_x = _page_gather(cache_cmp, layer, page_table)
        new_t = jnp.pad(kv6[:, :, 2:4].transpose(0, 2, 3, 4, 1), [(0, 0)] * 4 + [(0, page - t)])
        win_all = jnp.concatenate([win_buf, kv6[:, :, 4:]], axis=1)
        wl = win_all.shape[1]
        wlp = -(-wl // KEY_CHUNK) * KEY_CHUNK
        win_arr = jnp.pad(win_all.astype(BF16), [(0, 0), (0, wlp - wl), (0, 0), (0, 0), (0, 0)])
        win_arr = win_arr.transpose(0, 2, 3, 1, 4).reshape(b, 2 * g, wlp, dh)
        new_win = win_all[:, wl - min(WINDOW, tk):]
    n_sub = tk // CMP_STRIDE
    n_cmp = n_sub - 1
    n_blk = -(-tk // SEL_BLOCK)
    cmp_kv = _compress(cmp_x, n_sub, w1cat, pe_flat, w2)
    if cache_hm is None:
        o = _nsa_attn(q, gates_t, slopes, cmp_kv, kv_hm, 0, g, kv_hm, 2 * g, 3 * g,
                      p_len=0, n_cmp=n_cmp, n_blk=n_blk, w0=0)
    else:
        o = _nsa_paged(q, gates_t, slopes, cmp_kv, cache_nt, layer, page_table, new_t, win_arr,
                       p_len=p_len, n_cmp=n_cmp, n_blk=n_blk, w0=p_len - win_buf.shape[1])
    y = _matmul_res(o.reshape(b * t, NSA_Q_DIM), w_out, xf).reshape(b, t, d)
    return y, new_rows, new_win


def _ffn_layer(x, buf, norm_w, w_up, conv_w, w_down):
    b, t, d = x.shape
    xf = x.reshape(b * t, d)
    h = _rms_matmul(xf, norm_w, w_up, D_FF, BF16).reshape(b, t, -1)
    act, new_buf = _ffn_act(h, buf, conv_w)
    y = _matmul_res(act.reshape(b * t, D_FF), w_down, xf).reshape(b, t, d)
    return y, new_buf


def _pad_cols(w, n):
    return jnp.pad(w, [(0, 0)] * (w.ndim - 1) + [(0, n - w.shape[-1])])


def _trunk(x, cache_hm, page_table, nsa_win, gdn_state, gdn_conv, ffn_conv, wts):
    depth = wts["ffn_w_up"].shape[0]
    rows_l, win_l, s_l, gconv_l, fconv_l = [], [], [], [], []
    for i in range(depth):
        j = i // 2
        if i % 2 == 0:
            x, s_new, cb = _gdn_layer(x, gdn_state, j, gdn_conv[j], wts["gdn_norm"][j],
                                      (wts["gdn_w_main"][j], wts["gdn_w_tail"][j]),
                                      wts["gdn_conv_w"][j], wts["gdn_a_log"][j], wts["gdn_dt_bias"][j],
                                      wts["gdn_out_norm"][j], wts["gdn_w_out"][j])
            s_l.append(s_new)
            gconv_l.append(cb)
        else:
            x, rows, wb = _nsa_layer(x, cache_hm, j, page_table, None if nsa_win is None else nsa_win[j],
                                     wts["nsa_norm"][j], (wts["nsa_w_q"][j], wts["nsa_w_kv"][j]),
                                     wts["nsa_w1cat"][j],
                                     wts["nsa_pe_flat"][j], wts["nsa_w2"][j], wts["nsa_w_out"][j],
                                     wts["slopes"])
            rows_l.append(rows)
            win_l.append(wb)
        x, fb = _ffn_layer(x, ffn_conv[i], wts["ffn_norm"][i], wts["ffn_w_up"][i], wts["ffn_conv_w"][i],
                           wts["ffn_w_down"][i])
        fconv_l.append(fb)
    b, t, d = x.shape
    y = _rms(x.reshape(b * t, d), wts["final_norm"]).reshape(b, t, d)
    return (y, jnp.stack(rows_l), jnp.stack(win_l), jnp.stack(s_l), jnp.stack(gconv_l), jnp.stack(fconv_l))


def kernel(x_prompt, x_sample, cache_nsa_kv, cache_nsa_win, state_gdn_s, state_gdn_conv, state_ffn_conv,
           page_table, gdn_norm, gdn_w_in, gdn_conv_w, gdn_a_log, gdn_dt_bias, gdn_out_norm, gdn_w_out,
           nsa_norm, nsa_w_in, nsa_cmp_pe, nsa_cmp_w1, nsa_cmp_w2, nsa_w_out,
           ffn_norm, ffn_w_up, ffn_conv_w, ffn_w_down, final_norm):
    n_gdn = gdn_w_in.shape[0]
    n_nsa = nsa_w_in.shape[0]
    depth = ffn_w_up.shape[0]
    bp = x_prompt.shape[0]
    gdn_main = GDN_CONV_CH + GDN_V_DIM
    nsa_cols = -(-(nsa_w_in.shape[2] - NSA_Q_DIM) // 896) * 896
    w1 = nsa_cmp_w1.reshape(n_nsa, 2, 2, CMP_STRIDE * NSA_DH, CMP_HIDDEN)
    w1cat = jnp.concatenate([w1[:, :, 0], w1[:, :, 1]], axis=-1).astype(BF16)
    slopes = 2.0 ** (-8.0 * jnp.arange(1, NSA_HEADS + 1, dtype=F32) / NSA_HEADS)
    wts = {
        "gdn_norm": gdn_norm,
        "gdn_w_main": gdn_w_in[:, :, :gdn_main].astype(BF16),
        "gdn_w_tail": _pad_cols(gdn_w_in[:, :, gdn_main:], 128).astype(BF16),
        "gdn_conv_w": gdn_conv_w,
        "gdn_a_log": _pad_cols(gdn_a_log, 128).reshape(n_gdn, 1, 128),
        "gdn_dt_bias": _pad_cols(gdn_dt_bias, 128).reshape(n_gdn, 1, 128),
        "gdn_out_norm": gdn_out_norm,
        "gdn_w_out": gdn_w_out.astype(BF16),
        "nsa_norm": nsa_norm,
        "nsa_w_q": nsa_w_in[:, :, :NSA_Q_DIM].astype(BF16),
        "nsa_w_kv": _pad_cols(nsa_w_in[:, :, NSA_Q_DIM:], nsa_cols).astype(BF16),
        "nsa_w1cat": w1cat,
        "nsa_pe_flat": nsa_cmp_pe.reshape(n_nsa, 2, 2, CMP_STRIDE * NSA_DH),
        "nsa_w2": nsa_cmp_w2.astype(BF16),
        "nsa_w_out": nsa_w_out.astype(BF16),
        "slopes": slopes.reshape(NSA_G, NSA_R, 1, 1),
        "ffn_norm": ffn_norm,
        "ffn_w_up": ffn_w_up.astype(BF16),
        "ffn_conv_w": ffn_conv_w,
        "ffn_w_down": ffn_w_down.astype(BF16),
        "final_norm": final_norm,
    }
    n_l, pool, page = cache_nsa_kv.shape[:3]
    cache_cmp = cache_nsa_kv[:, :, :, :2].astype(BF16).transpose(0, 1, 3, 4, 2, 5)
    cache_hm = (cache_cmp.reshape(n_l, pool, 2 * NSA_G, page // CMP_STRIDE, CMP_STRIDE * NSA_DH),
                cache_nsa_kv.transpose(0, 1, 3, 4, 5, 2))

    zeros = functools.partial(jnp.zeros, dtype=F32)
    prompt = _trunk(x_prompt, None, None, None,
                    zeros((n_gdn, bp, GDN_V_HEADS, GDN_DK, GDN_DV)),
                    zeros((n_gdn, bp, GDN_CONV_W - 1, GDN_CONV_CH)),
                    zeros((depth, bp, FFN_CONV_W - 1, D_FF)), wts)
    sample = _trunk(x_sample, cache_hm, page_table, cache_nsa_win, state_gdn_s, state_gdn_conv, state_ffn_conv, wts)
    out = []
    for p, s in zip(prompt, sample):
        out.extend([p, s])
    return tuple(out)
```

```python
import functools

import jax
import jax.numpy as jnp
from jax import lax
from jax.experimental import pallas as pl
from jax.experimental.pallas import tpu as pltpu

F32 = jnp.float32
BF16 = jnp.bfloat16

RMS_EPS = 1e-6
L2_EPS = 1e-6
NEG_INF = -1e30
FORCE_SCORE = 1e9
NEVER = -3e38
MASKED_DIST = 1e30
LOG2E = 1.4426950408889634

GDN_QK_HEADS = 8
GDN_V_HEADS = 16
GDN_DK = 128
GDN_DV = 128
GDN_QK_DIM = GDN_QK_HEADS * GDN_DK
GDN_V_DIM = GDN_V_HEADS * GDN_DV
GDN_CONV_CH = 2 * GDN_QK_DIM + GDN_V_DIM
GDN_CONV_W = 4
GDN_CHUNK = 64
GDN_HEADS_PER_STEP = 4
GDN_PREP_CHUNKS = 8
NSA_HEADS = 16
NSA_G = 4
NSA_R = 4
NSA_DH = 64
NSA_Q_DIM = NSA_HEADS * NSA_DH
NSA_KV_DIM = NSA_G * NSA_DH
CMP_STRIDE = 16
CMP_LEN = 32
CMP_HIDDEN = 256
SEL_BLOCK = 64
SEL_TOPN = 8
WINDOW = 512
NSA_Q_BLOCK = 64
NSA_Q_BLOCKS_PER_STEP = 8
KEY_CHUNK = 128
NSA_SEL_CLASS = 512
D_FF = 2816
FFN_CONV_W = 3
FFN_TC = 256
SEQ_ROWS_PER_STEP = 2048
CONV_HEAD_ROWS = 16
HM_ROWS = 512

VMEM_LIMIT = 52 * 1024 * 1024


def _params(*sem):
    return pltpu.CompilerParams(dimension_semantics=sem, vmem_limit_bytes=VMEM_LIMIT)


def _mm(a, b):
    return jnp.dot(a.astype(BF16), b.astype(BF16), preferred_element_type=F32)


def _mm_nt(a, b):
    return lax.dot_general(a.astype(BF16), b.astype(BF16), (((1,), (1,)), ((), ())),
                           preferred_element_type=F32)


def _mm_tn(a, b):
    return lax.dot_general(a.astype(BF16), b.astype(BF16), (((0,), (0,)), ((), ())),
                           preferred_element_type=F32)


def _sigmoid(x):
    return 0.5 * jnp.tanh(0.5 * x) + 0.5


def _silu(x):
    h = 0.5 * x
    return h * jnp.tanh(h) + h


def _rms_matmul_body(x_ref, nw_ref, w_ref, o_ref, xn_ref):
    @pl.when(pl.program_id(1) == 0)
    def _():
        x = x_ref[...]
        inv = lax.rsqrt(jnp.mean(x * x, axis=-1, keepdims=True) + RMS_EPS)
        xn_ref[...] = ((x * inv) * nw_ref[...]).astype(BF16)

    o_ref[...] = jnp.dot(xn_ref[...], w_ref[...], preferred_element_type=F32).astype(o_ref.dtype)


def _rms_matmul(x, nw, w, tn, out_dtype=F32):
    m, k = x.shape
    n = w.shape[1]
    tm = min(m, 1024)
    return pl.pallas_call(
        _rms_matmul_body,
        grid=(m // tm, n // tn),
        in_specs=[pl.BlockSpec((tm, k), lambda i, j: (i, 0)),
                  pl.BlockSpec((1, k), lambda i, j: (0, 0)),
                  pl.BlockSpec((k, tn), lambda i, j: (0, j))],
        out_specs=pl.BlockSpec((tm, tn), lambda i, j: (i, j)),
        out_shape=jax.ShapeDtypeStruct((m, n), out_dtype),
        scratch_shapes=[pltpu.VMEM((tm, k), BF16)],
        compiler_params=_params("parallel", "arbitrary"),
        name="rms_matmul",
    )(x, nw.reshape(1, k), w)


def _rms_matmul_hm_body(x_ref, nw_ref, w_ref, o_ref, hm_ref, *, col0):
    x = x_ref[...]
    inv = lax.rsqrt(jnp.mean(x * x, axis=-1, keepdims=True) + RMS_EPS)
    xn = ((x * inv) * nw_ref[...]).astype(BF16)
    acc = jnp.dot(xn, w_ref[...], preferred_element_type=F32)
    o_ref[...] = acc
    dh = hm_ref.shape[3]
    for cg in range(hm_ref.shape[1]):
        hm_ref[0, cg] = acc[:, col0 + cg * dh:col0 + (cg + 1) * dh].astype(hm_ref.dtype)


def _rms_matmul_hm(x, nw, w, b, t, col0, n_cg, dh):
    m, k = x.shape
    n = w.shape[1]
    tm = HM_ROWS
    tps = t // tm
    return pl.pallas_call(
        functools.partial(_rms_matmul_hm_body, col0=col0),
        grid=(m // tm,),
        in_specs=[pl.BlockSpec((tm, k), lambda i: (i, 0)),
                  pl.BlockSpec((1, k), lambda i: (0, 0)),
                  pl.BlockSpec((k, n), lambda i: (0, 0))],
        out_specs=[pl.BlockSpec((tm, n), lambda i: (i, 0)),
                   pl.BlockSpec((1, n_cg, tm, dh), lambda i: (i // tps, 0, i % tps, 0))],
        out_shape=[jax.ShapeDtypeStruct((m, n), F32),
                   jax.ShapeDtypeStruct((b, n_cg, t, dh), BF16)],
        compiler_params=_params("parallel"),
        name="rms_matmul_hm",
    )(x, nw.reshape(1, k), w)


def _matmul_res_body(a_ref, w_ref, r_ref, o_ref):
    o_ref[...] = r_ref[...] + jnp.dot(a_ref[...].astype(BF16), w_ref[...], preferred_element_type=F32)


def _matmul_res(a, w, res):
    m, k = a.shape
    n = w.shape[1]
    tm = min(m, 1024)
    return pl.pallas_call(
        _matmul_res_body,
        grid=(m // tm,),
        in_specs=[pl.BlockSpec((tm, k), lambda i: (i, 0)),
                  pl.BlockSpec((k, n), lambda i: (0, 0)),
                  pl.BlockSpec((tm, n), lambda i: (i, 0))],
        out_specs=pl.BlockSpec((tm, n), lambda i: (i, 0)),
        out_shape=jax.ShapeDtypeStruct((m, n), F32),
        compiler_params=_params("parallel"),
        name="matmul_res",
    )(a, w, res)


def _rms_body(x_ref, nw_ref, o_ref):
    x = x_ref[...]
    inv = lax.rsqrt(jnp.mean(x * x, axis=-1, keepdims=True) + RMS_EPS)
    o_ref[...] = (x * inv) * nw_ref[...]


def _rms(x, nw):
    m, k = x.shape
    tm = min(m, 1024)
    return pl.pallas_call(
        _rms_body,
        grid=(m // tm,),
        in_specs=[pl.BlockSpec((tm, k), lambda i: (i, 0)), pl.BlockSpec((1, k), lambda i: (0, 0))],
        out_specs=pl.BlockSpec((tm, k), lambda i: (i, 0)),
        out_shape=jax.ShapeDtypeStruct((m, k), F32),
        compiler_params=_params("parallel"),
        name="final_rms",
    )(x, nw.reshape(1, k))


def _shifted(x, prev_rows, shift, row):
    nb = prev_rows.shape[1]
    y = pltpu.roll(x, shift, 1)
    for r in range(shift):
        y = jnp.where(row == r, prev_rows[:, nb - shift + r:nb - shift + r + 1], y)
    return y


def _batch_block(b, t):
    bb = max(1, min(b, SEQ_ROWS_PER_STEP // t))
    while b % bb:
        bb -= 1
    return bb


def _causal_conv(x, buf, w):
    width = w.shape[0]
    t = x.shape[1]

    def conv(xs, exact):
        row = lax.broadcasted_iota(jnp.int32, xs.shape, 1)
        y = None
        for i in range(width):
            shift = width - 1 - i
            if shift == 0:
                term = xs
            elif exact:
                term = _shifted(xs, buf, shift, row)
            else:
                term = pltpu.roll(xs, shift, 1)
            term = term * w[i:i + 1]
            y = term if y is None else y + term
        return y

    if t <= CONV_HEAD_ROWS:
        return conv(x, True), None
    return conv(x, False), conv(x[:, :CONV_HEAD_ROWS], True)


def _ffn_act_body(a_ref, g_ref, buf_ref, cw_ref, act_ref, nb_ref):
    a = a_ref[...].astype(F32)
    g = g_ref[...].astype(F32)
    t = a.shape[1]
    y, y_head = _causal_conv(a, buf_ref[...], cw_ref[...])
    act_ref[...] = (_silu(y) * g).astype(act_ref.dtype)
    if y_head is not None:
        act_ref[:, :CONV_HEAD_ROWS, :] = (_silu(y_head) * g[:, :CONV_HEAD_ROWS]).astype(act_ref.dtype)
    nb_ref[...] = a[:, t - (FFN_CONV_W - 1):t, :]


def _ffn_act(h, buf, cw):
    b, t, _ = h.shape
    nj = D_FF // FFN_TC
    bb = _batch_block(b, t)
    return pl.pallas_call(
        _ffn_act_body,
        grid=(b // bb, nj),
        in_specs=[pl.BlockSpec((bb, t, FFN_TC), lambda i, j: (i, 0, j)),
                  pl.BlockSpec((bb, t, FFN_TC), lambda i, j: (i, 0, j + nj)),
                  pl.BlockSpec((bb, FFN_CONV_W - 1, FFN_TC), lambda i, j: (i, 0, j)),
                  pl.BlockSpec((FFN_CONV_W, FFN_TC), lambda i, j: (0, j))],
        out_specs=[pl.BlockSpec((bb, t, FFN_TC), lambda i, j: (i, 0, j)),
                   pl.BlockSpec((bb, FFN_CONV_W - 1, FFN_TC), lambda i, j: (i, 0, j))],
        out_shape=[jax.ShapeDtypeStruct((b, t, D_FF), BF16),
                   jax.ShapeDtypeStruct((b, FFN_CONV_W - 1, D_FF), F32)],
        compiler_params=_params("parallel", "parallel"),
        name="ffn_act",
    )(h, h, buf, cw)


GDN_PRE_TC = 512


def _gdn_pre_body(x_ref, buf_ref, cw_ref, o_ref, nb_ref):
    j = pl.program_id(1)
    x = x_ref[...].astype(F32)
    t = x.shape[1]
    is_q = j < GDN_QK_DIM // GDN_PRE_TC
    is_v = j >= 2 * GDN_QK_DIM // GDN_PRE_TC
    qscale = jnp.where(is_q, GDN_DK ** -0.5, 1.0).astype(F32)

    def finish(y, n_rows):
        y = _silu(y)
        for h in range(GDN_PRE_TC // GDN_DK):
            yh = y[:, :, h * GDN_DK:(h + 1) * GDN_DK]
            inv = lax.rsqrt(jnp.sum(yh * yh, axis=-1, keepdims=True) + L2_EPS)
            o_ref[:, :n_rows, h * GDN_DK:(h + 1) * GDN_DK] = jnp.where(
                is_v, yh, yh * (inv * qscale)).astype(o_ref.dtype)

    y, y_head = _causal_conv(x, buf_ref[...], cw_ref[...])
    finish(y, t)
    if y_head is not None:
        finish(y_head, CONV_HEAD_ROWS)
    nb_ref[...] = x[:, t - (GDN_CONV_W - 1):t, :]


def _gdn_pre(proj, buf, cw):
    b, t, _ = proj.shape
    nj = GDN_CONV_CH // GDN_PRE_TC
    bb = _batch_block(b, t)
    return pl.pallas_call(
        _gdn_pre_body,
        grid=(b // bb, nj),
        in_specs=[pl.BlockSpec((bb, t, GDN_PRE_TC), lambda i, j: (i, 0, j)),
                  pl.BlockSpec((bb, GDN_CONV_W - 1, GDN_PRE_TC), lambda i, j: (i, 0, j)),
                  pl.BlockSpec((GDN_CONV_W, GDN_PRE_TC), lambda i, j: (0, j))],
        out_specs=[pl.BlockSpec((bb, t, GDN_PRE_TC), lambda i, j: (i, 0, j)),
                   pl.BlockSpec((bb, GDN_CONV_W - 1, GDN_PRE_TC), lambda i, j: (i, 0, j))],
        out_shape=[jax.ShapeDtypeStruct((b, t, GDN_CONV_CH), BF16),
                   jax.ShapeDtypeStruct((b, GDN_CONV_W - 1, GDN_CONV_CH), F32)],
        compiler_params=_params("parallel", "parallel"),
        name="gdn_pre",
    )(proj, buf, cw)


def _gdn_gate_body(x_ref, alog_ref, dtb_ref, g_ref, gcum_ref, beta_ref, *, chunk):
    x = x_ref[...]
    z = x + dtb_ref[...]
    softplus = jnp.maximum(z, 0.0) + jnp.log(1.0 + jnp.exp(-jnp.abs(z)))
    g = -jnp.exp(alog_ref[...]) * softplus
    g_ref[...] = g
    row = lax.broadcasted_iota(jnp.int32, x.shape, 1) % chunk
    acc = g
    s = 1
    while s < chunk:
        acc = acc + jnp.where(row >= s, pltpu.roll(acc, s, 1), 0.0)
        s *= 2
    gcum_ref[...] = acc
    beta_ref[...] = _sigmoid(x)


def _gdn_gate(proj, alog_pad, dtb_pad, lane_block):
    b, t, _ = proj.shape
    chunk = min(GDN_CHUNK, t)
    bb = _batch_block(b, t)
    spec = pl.BlockSpec((bb, t, 128), lambda i: (i, 0, 0))
    return pl.pallas_call(
        functools.partial(_gdn_gate_body, chunk=chunk),
        grid=(b // bb,),
        in_specs=[pl.BlockSpec((bb, t, 128), lambda i: (i, 0, lane_block)),
                  pl.BlockSpec((1, 128), lambda i: (0, 0)),
                  pl.BlockSpec((1, 128), lambda i: (0, 0))],
        out_specs=[spec, spec, spec],
        out_shape=[jax.ShapeDtypeStruct((b, t, 128), F32)] * 3,
        compiler_params=_params("parallel"),
        name="gdn_gate",
    )(proj, alog_pad, dtb_pad)


def _unit_lower_inverses(lows, n):
    eye = (lax.broadcasted_iota(jnp.int32, (n, n), 0) == lax.broadcasted_iota(jnp.int32, (n, n), 1)).astype(F32)
    ps = [eye - low for low in lows]
    ms = [_mm(low, low) for low in lows]
    k = 2
    while True:
        ps = [p + _mm(p, m) for p, m in zip(ps, ms)]
        k *= 2
        if k >= n:
            break
        ms = [_mm(m, m) for m in ms]
    return ps


def _gdn_scan_body(q_ref, k_ref, v_ref, z_ref, gc_ref, bc_ref, gr_ref, br_ref, s0_ref, onw_ref, o_ref, s_ref,
                   u_ref, w_ref, qk_ref, qd_ref, kd_ref):
    c = GDN_CHUNK
    hb = s0_ref.shape[1]
    n_chunks = q_ref.shape[1] // c
    s_ref[...] = s0_ref[...]
    ri = lax.broadcasted_iota(jnp.int32, (c, c), 0)
    ci = lax.broadcasted_iota(jnp.int32, (c, c), 1)
    tri = ri >= ci
    stri = ri > ci
    onw = onw_ref[...]

    def qk_slice(ref, r0, qh):
        return ref[0, pl.ds(r0, c), qh * GDN_DK:(qh + 1) * GDN_DK]

    def v_slice(ref, r0, hh):
        return ref[0, pl.ds(r0, c), hh * GDN_DV:(hh + 1) * GDN_DV]

    cpb = max(c_ for c_ in (GDN_PREP_CHUNKS, 4, 2, 1) if n_chunks % c_ == 0)

    def prep(nb, carry):
        items = []
        kk, qk, ks, qs = {}, {}, {}, {}
        for ch in range(cpb):
            n = nb * cpb + ch
            r0 = pl.multiple_of(n * c, c)
            gcol = gc_ref[0, 0, pl.ds(r0, c), :]
            bcol = bc_ref[0, 0, pl.ds(r0, c), :]
            grow = gr_ref[0, 0, n]
            brow = br_ref[0, 0, n]
            for qh in range(hb // 2):
                ks[ch, qh] = qk_slice(k_ref, r0, qh)
                qs[ch, qh] = qk_slice(q_ref, r0, qh)
            for hh in range(hb):
                items.append(dict(ch=ch, hh=hh, r0=r0, v=v_slice(v_ref, r0, hh),
                                  gcb=jnp.broadcast_to(gcol[:, hh:hh + 1], (c, GDN_DK)),
                                  bcb=jnp.broadcast_to(bcol[:, hh:hh + 1], (c, c)),
                                  gr=grow[hh:hh + 1, :], br=brow[hh:hh + 1, :]))
        for key in ks:
            kk[key] = _mm_nt(ks[key], ks[key])
        for key in ks:
            qk[key] = _mm_nt(qs[key], ks[key])
        lows = []
        for it in items:
            key = (it["ch"], it["hh"] // 2)
            it["decay"] = jnp.where(tri, jnp.exp(jnp.where(tri, it["gcb"][:, :c] - it["gr"], 0.0)), 0.0)
            lows.append(jnp.where(stri, (kk[key] * it["bcb"]) * it["decay"], 0.0))
        tinvs = _unit_lower_inverses(lows, c)
        us = [_mm(tinv * it["br"], it["v"]) for tinv, it in zip(tinvs, items)]
        ws = [_mm(tinv * (it["br"] * jnp.exp(it["gr"])), ks[it["ch"], it["hh"] // 2])
              for tinv, it in zip(tinvs, items)]
        for it, u, w in zip(items, us, ws):
            key = (it["ch"], it["hh"] // 2)
            hh, r0, gcb = it["hh"], it["r0"], it["gcb"]
            u_ref[hh, pl.ds(r0, c), :] = u
            w_ref[hh, pl.ds(r0, c), :] = w.astype(BF16)
            qk_ref[hh, pl.ds(r0, c), :] = jnp.where(tri, qk[key] * it["decay"], 0.0).astype(BF16)
            qd_ref[hh, pl.ds(r0, c), :] = (qs[key] * jnp.exp(gcb)).astype(BF16)
            kd_ref[hh, pl.ds(r0, c), :] = (ks[key] * jnp.exp(gcb[c - 1:c, :] - gcb)).astype(BF16)
        return carry

    lax.fori_loop(0, n_chunks // cpb, prep, 0)

    def scan(n, carry):
        r0 = pl.multiple_of(n * c, c)
        g_last = gc_ref[0, 0, pl.ds(r0 + (c - 1), 1), :]
        loaded = []
        for hh in range(hb):
            loaded.append((u_ref[hh, pl.ds(r0, c), :], w_ref[hh, pl.ds(r0, c), :], qk_ref[hh, pl.ds(r0, c), :],
                           qd_ref[hh, pl.ds(r0, c), :], kd_ref[hh, pl.ds(r0, c), :], v_slice(z_ref, r0, hh),
                           s_ref[0, hh]))
        ws_s = [_mm(w, s) for (u, w, qkm, qd, kd, z, s) in loaded]
        qd_s = [_mm(qd, s) for (u, w, qkm, qd, kd, z, s) in loaded]
        v_news = [ld[0] - ws for ld, ws in zip(loaded, ws_s)]
        qk_v = [_mm(ld[2], vn) for ld, vn in zip(loaded, v_news)]
        kd_v = [_mm_tn(ld[4], vn) for ld, vn in zip(loaded, v_news)]
        results = []
        for hh, ld in enumerate(loaded):
            o = qd_s[hh] + qk_v[hh]
            s_new = ld[6] * jnp.exp(g_last[:, hh:hh + 1]) + kd_v[hh]
            inv = lax.rsqrt(jnp.mean(o * o, axis=-1, keepdims=True) + RMS_EPS)
            results.append((((o * inv) * onw) * _silu(ld[5].astype(F32)), s_new))
        for hh, (og, s_new) in enumerate(results):
            s_ref[0, hh] = s_new
            o_ref[0, pl.ds(r0, c), hh * GDN_DV:(hh + 1) * GDN_DV] = og.astype(o_ref.dtype)
        return carry

    lax.fori_loop(0, n_chunks, scan, 0)


def _gdn_scan(qkv, proj, gcol, bcol, grow, brow, s0, layer, onw):
    b, t, _ = qkv.shape
    hb = gcol.shape[3]
    hg = GDN_V_HEADS // hb
    qw = hb // 2 * GDN_DK
    vw = hb * GDN_DV
    n = t // GDN_CHUNK
    return pl.pallas_call(
        _gdn_scan_body,
        grid=(b, hg),
        in_specs=[pl.BlockSpec((1, t, qw), lambda i, j: (i, 0, j)),
                  pl.BlockSpec((1, t, qw), lambda i, j: (i, 0, GDN_QK_DIM // qw + j)),
                  pl.BlockSpec((1, t, vw), lambda i, j: (i, 0, 2 * GDN_QK_DIM // vw + j)),
                  pl.BlockSpec((1, t, vw), lambda i, j: (i, 0, GDN_CONV_CH // vw + j)),
                  pl.BlockSpec((1, 1, t, hb), lambda i, j: (i, j, 0, 0)),
                  pl.BlockSpec((1, 1, t, hb), lambda i, j: (i, j, 0, 0)),
                  pl.BlockSpec((1, 1, n, hb, GDN_CHUNK), lambda i, j: (i, j, 0, 0, 0)),
                  pl.BlockSpec((1, 1, n, hb, GDN_CHUNK), lambda i, j: (i, j, 0, 0, 0)),
                  pl.BlockSpec((None, 1, hb, GDN_DK, GDN_DV), lambda i, j: (layer, i, j, 0, 0)),
                  pl.BlockSpec((1, GDN_DV), lambda i, j: (0, 0))],
        out_specs=[pl.BlockSpec((1, t, vw), lambda i, j: (i, 0, j)),
                   pl.BlockSpec((1, hb, GDN_DK, GDN_DV), lambda i, j: (i, j, 0, 0))],
        out_shape=[jax.ShapeDtypeStruct((b, t, GDN_V_DIM), BF16),
                   jax.ShapeDtypeStruct((b, GDN_V_HEADS, GDN_DK, GDN_DV), F32)],
        scratch_shapes=[pltpu.VMEM((hb, t, GDN_DV), F32),
                        pltpu.VMEM((hb, t, GDN_DK), BF16),
                        pltpu.VMEM((hb, t, GDN_CHUNK), BF16),
                        pltpu.VMEM((hb, t, GDN_DK), BF16),
                        pltpu.VMEM((hb, t, GDN_DK), BF16)],
        compiler_params=_params("parallel", "parallel"),
        name="gdn_scan",
    )(qkv, qkv, qkv, proj, gcol, bcol, grow, brow, s0, onw.reshape(1, GDN_DV))


def _gdn_layer(x, s0, layer, conv_buf, norm_w, w_in, conv_w, alog_pad, dtb_pad, out_norm_w, w_out):
    b, t, d = x.shape
    xf = x.reshape(b * t, d)
    w_main, w_tail = w_in
    proj = _rms_matmul(xf, norm_w, w_main, 2048, BF16).reshape(b, t, -1)
    tail = _rms_matmul(xf, norm_w, w_tail, 128).reshape(b, t, -1)
    qkv, new_buf = _gdn_pre(proj, conv_buf, conv_w)
    _, gcum, beta = _gdn_gate(tail, alog_pad, dtb_pad, 0)
    gcum = gcum[:, :, :GDN_V_HEADS]
    beta = beta[:, :, GDN_V_HEADS:2 * GDN_V_HEADS]
    tp = -(-t // GDN_CHUNK) * GDN_CHUNK
    if tp != t:
        pad = [(0, 0), (0, tp - t), (0, 0)]
        qkv = jnp.pad(qkv, pad)
        proj_z = jnp.pad(proj, pad)
        gcum = jnp.pad(gcum, pad, mode="edge")
        beta = jnp.pad(beta, pad)
    else:
        proj_z = proj
    hb = GDN_V_HEADS if tp == GDN_CHUNK else GDN_HEADS_PER_STEP
    hg = GDN_V_HEADS // hb
    n = tp // GDN_CHUNK
    gcol = gcum.reshape(b, tp, hg, hb).transpose(0, 2, 1, 3)
    bcol = beta.reshape(b, tp, hg, hb).transpose(0, 2, 1, 3)
    grow = gcum.reshape(b, n, GDN_CHUNK, hg, hb).transpose(0, 3, 1, 4, 2)
    brow = beta.reshape(b, n, GDN_CHUNK, hg, hb).transpose(0, 3, 1, 4, 2)
    o, s_new = _gdn_scan(qkv, proj_z, gcol, bcol, grow, brow, s0, layer, out_norm_w)
    o = o[:, :t].reshape(b * t, GDN_V_DIM)
    y = _matmul_res(o, w_out, xf).reshape(b, t, d)
    return y, s_new, new_buf


def _page_gather_body(pt_ref, *refs, pps):
    cmp_refs, o_ref = refs[:pps], refs[pps]
    sub = cmp_refs[0].shape[3]
    for s in range(pps):
        o_ref[0, :, s * sub:(s + 1) * sub, :] = cmp_refs[s][0, 0]


def _page_gather(cache_cmp, layer, page_table):
    b, n_pages = page_table.shape
    _, _, n_cg, sub, flat = cache_cmp.shape
    pps = 16 if n_pages % 16 == 0 else (8 if n_pages % 8 == 0 else 1)
    n_steps = n_pages // pps

    def page_map(s):
        return lambda i, p, pt: (layer, pt[i, p * pps + s], 0, 0, 0)

    grid_spec = pltpu.PrefetchScalarGridSpec(
        num_scalar_prefetch=1,
        grid=(b, n_steps),
        in_specs=[pl.BlockSpec((1, 1, n_cg, sub, flat), page_map(s)) for s in range(pps)],
        out_specs=pl.BlockSpec((1, n_cg, pps * sub, flat), lambda i, p, pt: (i, 0, p, 0)),
    )
    return pl.pallas_call(
        functools.partial(_page_gather_body, pps=pps),
        grid_spec=grid_spec,
        out_shape=jax.ShapeDtypeStruct((b, n_cg, n_pages * sub, flat), cache_cmp.dtype),
        compiler_params=_params("parallel", "arbitrary"),
        name="page_gather",
    )(page_table, *([cache_cmp] * pps))


def _compress_body(x_ref, w1_ref, pe_ref, w2_ref, o_ref, bias_ref, *, n_sub):
    hd = CMP_HIDDEN

    @pl.when(pl.program_id(1) == 0)
    def _():
        pe = pe_ref[0]
        pe0 = jnp.broadcast_to(pe[0:1], (8, pe.shape[1]))
        pe1 = jnp.broadcast_to(pe[1:2], (8, pe.shape[1]))
        bias_ref[:, :hd] = jnp.dot(pe0.astype(BF16), w1_ref[0, :, :hd], preferred_element_type=F32)
        bias_ref[:, hd:] = jnp.dot(pe1.astype(BF16), w1_ref[0, :, hd:], preferred_element_type=F32)

    acc = jnp.dot(x_ref[0, 0], w1_ref[0], preferred_element_type=F32)
    first = acc[:, :hd] + bias_ref[0:1, :hd]
    second = acc[:, hd:] + bias_ref[0:1, hd:]
    hid = _silu(first + pltpu.roll(second, n_sub - 1, 0))
    o_ref[0, 0] = jnp.dot(hid.astype(BF16), w2_ref[0], preferred_element_type=F32)


def _compress(x, n_sub, w1cat, pe_flat, w2):
    b, _, _, flat = x.shape
    dh = flat // CMP_STRIDE
    return pl.pallas_call(
        functools.partial(_compress_body, n_sub=n_sub),
        grid=(2 * NSA_G, b),
        in_specs=[pl.BlockSpec((1, 1, n_sub, flat), lambda j, i: (i, j, 0, 0)),
                  pl.BlockSpec((1, flat, 2 * CMP_HIDDEN), lambda j, i: (j // NSA_G, 0, 0)),
                  pl.BlockSpec((1, 2, flat), lambda j, i: (j // NSA_G, 0, 0)),
                  pl.BlockSpec((1, CMP_HIDDEN, dh), lambda j, i: (j // NSA_G, 0, 0))],
        out_specs=pl.BlockSpec((1, 1, n_sub, dh), lambda j, i: (i, j, 0, 0)),
        out_shape=jax.ShapeDtypeStruct((b, 2 * NSA_G, n_sub, dh), F32),
        scratch_shapes=[pltpu.VMEM((8, 2 * CMP_HIDDEN), F32)],
        compiler_params=_params("arbitrary", "arbitrary"),
        name="nsa_compress",
    )(x, w1cat, pe_flat, w2)


def _inv_or_zero(l):
    return jnp.where(l > 0.0, 1.0 / jnp.where(l > 0.0, l, 1.0), 0.0)


def _attn_branch(q, slopes, t_pos, k, v, kp, ok_fn, may_be_empty=False):
    r, qb, nk = slopes.shape[0], t_pos.shape[1], k.shape[0]
    dist = t_pos - kp
    ok = ok_fn(dist)
    pen = jnp.where(ok, dist.astype(F32), MASKED_DIST)
    s = _mm_nt(q, k).reshape(r, qb, nk) - slopes * pen
    m = jnp.max(s, axis=-1, keepdims=True)
    p = jnp.exp2(s - m)
    if may_be_empty:
        p = jnp.where(ok, p, 0.0)
    inv = _inv_or_zero(jnp.sum(p, axis=-1, keepdims=True))
    o = _mm(p.reshape(r * qb, nk), v) * inv.reshape(r * qb, 1)
    return o, p, inv


def _pick_blocks(p_sum, t_lane, n_cmp, n_blk, nb_rows):
    nq, nc = p_sum.shape
    nb = -(-n_blk // 8) * 8
    jj = lax.broadcasted_iota(jnp.int32, (nb, nc), 0)
    nn = lax.broadcasted_iota(jnp.int32, (nb, nc), 1)
    overlap_t = ((nn * CMP_STRIDE < (jj + 1) * SEL_BLOCK) & (nn * CMP_STRIDE + CMP_LEN > jj * SEL_BLOCK)
                 & (nn < n_cmp) & (jj < n_blk)).astype(F32)
    imp = lax.dot_general(overlap_t, p_sum, (((1,), (1,)), ((), ())), preferred_element_type=F32,
                          precision=lax.Precision.HIGHEST)
    j = lax.broadcasted_iota(jnp.int32, (nb, nq), 0)
    cur = t_lane // SEL_BLOCK
    imp = jnp.where((j == 0) | (j == cur) | (j == cur - 1), FORCE_SCORE, imp)
    imp = jnp.where(j > cur, -FORCE_SCORE, imp)
    imp = jnp.where(j >= n_blk, NEVER, imp)
    rank = jnp.zeros((nb, nq), jnp.int32)
    for jp in range(n_blk):
        row = imp[jp:jp + 1, :]
        rank = rank + ((row > imp) | ((row == imp) & (j > jp))).astype(jnp.int32)
    picked_t = ((rank < SEL_TOPN) & (j < n_blk)).astype(F32)
    if nb_rows > nb:
        picked_t = jnp.concatenate([picked_t, jnp.zeros((nb_rows - nb, nq), F32)], axis=0)
    return picked_t


def _transpose_01(x_t):
    m = x_t.shape[1]
    eye = (lax.broadcasted_iota(jnp.int32, (m, m), 0) == lax.broadcasted_iota(jnp.int32, (m, m), 1))
    return _mm_nt(eye.astype(BF16), x_t)


def _head_rows(x, n_heads):
    dh = x.shape[1] // n_heads
    return jnp.concatenate([x[:, h * dh:(h + 1) * dh] for h in range(n_heads)], axis=0)


def _nsa_attn_body(q_ref, gate_ref, slope_ref, ck_ref, cv_ref, ks_ref, vs_ref, kw_ref, vw_ref, o_ref, osel_ref,
                   *, nsb, **kw):
    refs = (q_ref, gate_ref, slope_ref, ck_ref, cv_ref, ks_ref, vs_ref, kw_ref, vw_ref, o_ref, osel_ref)
    if nsb == 1:
        _nsa_attn_block(0, pl.program_id(2), *refs, **kw)
    else:
        def one(sub, carry):
            _nsa_attn_block(sub, pl.program_id(2) * nsb + sub, *refs, **kw)
            return carry

        lax.fori_loop(0, nsb, one, 0)


def _nsa_attn_block(sub, i, q_ref, gate_ref, slope_ref, ck_ref, cv_ref, ks_ref, vs_ref, kw_ref, vw_ref, o_ref,
                    osel_ref, *, qb, p_len, n_cmp, n_blk, w0, sel_classes, kw_len):
    r = NSA_R
    rows = r * qb
    rq = sub * qb if isinstance(sub, int) else pl.multiple_of(sub * qb, qb)
    q = _head_rows(q_ref[0, pl.ds(rq, qb), :].astype(F32), r)
    q = (q * (NSA_DH ** -0.5 * LOG2E)).astype(BF16)
    slopes = slope_ref[0] * LOG2E
    q0 = p_len + i * qb
    t_pos = q0 + lax.broadcasted_iota(jnp.int32, (1, qb, 1), 1)
    branch = functools.partial(_attn_branch, q, slopes, t_pos)

    tw = kw_ref.shape[2]
    if tw == kw_len:
        start = 0
    else:
        start = pl.multiple_of(jnp.clip(q0 + qb - w0 - kw_len, 0, tw - kw_len), SEL_BLOCK)
    kp_win = w0 + start + lax.broadcasted_iota(jnp.int32, (1, 1, kw_len), 2)
    o_win, _, _ = branch(kw_ref[0, 0, pl.ds(start, kw_len), :], vw_ref[0, 0, pl.ds(start, kw_len), :], kp_win,
                         lambda dist: (dist >= 0) & (dist < WINDOW) & (kp_win >= 0))

    nc = ck_ref.shape[2]
    n_idx = lax.broadcasted_iota(jnp.int32, (1, 1, nc), 2)
    o_cmp, p_cmp, inv_cmp = branch(ck_ref[0, 0], cv_ref[0, 0], n_idx * CMP_STRIDE + (CMP_LEN - 1),
                                   lambda dist: (dist >= 0) & (n_idx < n_cmp), may_be_empty=True)
    p_sum = jnp.sum(p_cmp * inv_cmp, axis=0)
    nb8 = -(-n_blk // 16) * 16
    picked_t = _pick_blocks(p_sum, q0 + lax.broadcasted_iota(jnp.int32, (1, qb), 1), n_cmp, n_blk, nb8)
    picked = _transpose_01(picked_t)

    def sel_branch(nk):
        blk_of_key = lax.broadcasted_iota(jnp.int32, (nb8, nk), 1) // SEL_BLOCK
        expand = (blk_of_key == lax.broadcasted_iota(jnp.int32, (nb8, nk), 0)).astype(BF16)
        key_picked = (_mm(picked, expand) > 0.5).reshape(1, qb, nk)
        o, _, _ = branch(ks_ref[0, 0, 0:nk, :], vs_ref[0, 0, 0:nk, :],
                         lax.broadcasted_iota(jnp.int32, (1, 1, nk), 2),
                         lambda dist: key_picked & (dist >= 0))
        osel_ref[...] = o

    if len(sel_classes) == 1:
        sel_branch(sel_classes[0])
    else:
        need = q0 + qb
        prev = 0
        for nk in sel_classes:
            pl.when((need > prev) & (need <= nk))(functools.partial(sel_branch, nk))
            prev = nk

    gates = _sigmoid(gate_ref[0, 0, :, pl.ds(rq, qb), :]).reshape(rows, 3)
    o = gates[:, 0:1] * o_cmp + gates[:, 1:2] * osel_ref[...] + gates[:, 2:3] * o_win
    for h in range(r):
        o_ref[0, pl.ds(rq, qb), h * NSA_DH:(h + 1) * NSA_DH] = o[h * qb:(h + 1) * qb].astype(o_ref.dtype)


def _nsa_attn(q, gates_t, slopes, cmp_kv, sel_arr, sel_k0, sel_v0, win_arr, win_k0, win_v0,
              *, p_len, n_cmp, n_blk, w0):
    b, tq, _ = q.shape
    g, r, dh = NSA_G, NSA_R, NSA_DH
    qb = NSA_Q_BLOCK if tq % NSA_Q_BLOCK == 0 else tq
    nc = cmp_kv.shape[2]
    tk = sel_arr.shape[2]
    tw = win_arr.shape[2]
    if tq == qb or tk % NSA_SEL_CLASS != 0:
        sel_classes = (tk,)
    else:
        sel_classes = tuple(range(NSA_SEL_CLASS, tk + 1, NSA_SEL_CLASS))
    kw_len = min(tw, -(-(WINDOW - 1 + qb) // KEY_CHUNK) * KEY_CHUNK)
    nsb = NSA_Q_BLOCKS_PER_STEP if (tq // qb) % NSA_Q_BLOCKS_PER_STEP == 0 else 1
    body = functools.partial(_nsa_attn_body, nsb=nsb, qb=qb, p_len=p_len, n_cmp=n_cmp, n_blk=n_blk, w0=w0,
                             sel_classes=sel_classes, kw_len=kw_len)
    qs = nsb * qb
    return pl.pallas_call(
        body,
        grid=(b, g, tq // qs),
        in_specs=[pl.BlockSpec((1, qs, r * dh), lambda bi, gi, i: (bi, i, gi)),
                  pl.BlockSpec((1, 1, r, qs, 3), lambda bi, gi, i: (bi, gi, 0, i, 0)),
                  pl.BlockSpec((1, r, 1, 1), lambda bi, gi, i: (gi, 0, 0, 0)),
                  pl.BlockSpec((1, 1, nc, dh), lambda bi, gi, i: (bi, gi, 0, 0)),
                  pl.BlockSpec((1, 1, nc, dh), lambda bi, gi, i: (bi, NSA_G + gi, 0, 0)),
                  pl.BlockSpec((1, 1, tk, dh), lambda bi, gi, i: (bi, sel_k0 + gi, 0, 0)),
                  pl.BlockSpec((1, 1, tk, dh), lambda bi, gi, i: (bi, sel_v0 + gi, 0, 0)),
                  pl.BlockSpec((1, 1, tw, dh), lambda bi, gi, i: (bi, win_k0 + gi, 0, 0)),
                  pl.BlockSpec((1, 1, tw, dh), lambda bi, gi, i: (bi, win_v0 + gi, 0, 0))],
        out_specs=pl.BlockSpec((1, qs, r * dh), lambda bi, gi, i: (bi, i, gi)),
        out_shape=jax.ShapeDtypeStruct((b, tq, g * r * dh), BF16),
        scratch_shapes=[pltpu.VMEM((r * qb, dh), F32)],
        compiler_params=_params("parallel", "parallel", "arbitrary"),
        name="nsa_attn",
    )(q, gates_t, slopes, cmp_kv, cmp_kv, sel_arr, sel_arr, win_arr, win_arr)


def _nsa_paged_body(pt_ref, *refs, tq, p_len, n_cmp, n_blk, w0, pps, n_steps):
    k_refs, v_refs = refs[:pps], refs[pps:2 * pps]
    (q_ref, gate_ref, slope_ref, cmp_ref, new_ref, win_ref, o_ref,
     qbd_ref, pen_ref, m_ref, l_ref, acc_ref, ocmp_ref) = refs[2 * pps:]
    p = pl.program_id(1)
    g_n, r = NSA_G, NSA_R
    dh = NSA_DH
    rg = r * tq
    rows = g_n * rg
    page = k_refs[0].shape[5]
    chunk = pps * page
    n_keys = pen_ref.shape[1]
    t_pos = p_len + lax.broadcasted_iota(jnp.int32, (1, tq, 1), 1)

    @pl.when(p == 0)
    def _():
        qbd_ref[...] = jnp.zeros(qbd_ref.shape, qbd_ref.dtype)
        nc = cmp_ref.shape[2]
        n_idx = lax.broadcasted_iota(jnp.int32, (1, 1, nc), 2)
        p_sums = []
        for g in range(g_n):
            q = _head_rows(q_ref[0, :, g * r * dh:(g + 1) * r * dh].astype(F32), r)
            q = (q * (dh ** -0.5 * LOG2E)).astype(BF16)
            qbd_ref[g * rg:(g + 1) * rg, g * dh:(g + 1) * dh] = q
            o_cmp, p_cmp, inv_cmp = _attn_branch(
                q, slope_ref[g] * LOG2E, t_pos, cmp_ref[0, g], cmp_ref[0, g_n + g],
                n_idx * CMP_STRIDE + (CMP_LEN - 1), lambda dist: (dist >= 0) & (n_idx < n_cmp), may_be_empty=True)
            ocmp_ref[g * rg:(g + 1) * rg, :] = o_cmp
            p_sums.append(jnp.sum(p_cmp * inv_cmp, axis=0))
        nbp = -(-n_blk // 128) * 128
        nq = g_n * tq
        t_lane = p_len + lax.broadcasted_iota(jnp.int32, (1, nq), 1) % tq
        picked_t = _pick_blocks(jnp.concatenate(p_sums, axis=0), t_lane, n_cmp, n_blk, nbp)
        picked = _transpose_01(picked_t.astype(BF16))
        picked_rows = jnp.concatenate([picked[g * tq:(g + 1) * tq] for g in range(g_n) for _ in range(r)], axis=0)
        slope_rows = jnp.broadcast_to(slope_ref[...] * LOG2E, (g_n, r, tq, 1)).reshape(rows, 1)
        t_rows = p_len + lax.broadcasted_iota(jnp.int32, (rows, 1), 0) % tq

        def fill(k0, width):
            kp = k0 + lax.broadcasted_iota(jnp.int32, (1, width), 1)
            blk = lax.broadcasted_iota(jnp.int32, (nbp, width), 0)
            expand = ((k0 + lax.broadcasted_iota(jnp.int32, (nbp, width), 1)) // SEL_BLOCK == blk).astype(BF16)
            key_picked = _mm(picked_rows, expand) > 0.5
            dist = t_rows - kp
            return slope_rows * jnp.where(key_picked & (dist >= 0), dist.astype(F32), MASKED_DIST)

        def fill_chunk(c, carry):
            k0 = pl.multiple_of(c * chunk, chunk)
            pen_ref[:, pl.ds(k0, chunk)] = fill(k0, chunk)
            return carry

        lax.fori_loop(0, n_steps, fill_chunk, 0)
        pen_ref[:, n_steps * chunk:n_keys] = fill(n_steps * chunk, n_keys - n_steps * chunk)
        m_ref[...] = jnp.full(m_ref.shape, NEVER, F32)
        l_ref[...] = jnp.zeros(l_ref.shape, F32)
        acc_ref[...] = jnp.zeros(acc_ref.shape, F32)

    def online_update(kts, vts, k0):
        width = len(kts) * page
        qbd = qbd_ref[...]
        s = jnp.concatenate([jnp.dot(qbd, kt.astype(BF16), preferred_element_type=F32) for kt in kts], axis=1)
        s = s - pen_ref[:, pl.ds(k0, width)]
        m_prev = m_ref[...]
        m_new = jnp.maximum(m_prev, jnp.max(s, axis=-1, keepdims=True))
        alpha = jnp.exp2(m_prev - m_new)
        prob = jnp.exp2(s - m_new)
        l_ref[...] = alpha * l_ref[...] + jnp.sum(prob, axis=-1, keepdims=True)
        pv = [_mm_nt(prob[:, i * page:(i + 1) * page], vt) for i, vt in enumerate(vts)]
        acc_ref[...] = alpha * acc_ref[...] + sum(pv[1:], pv[0])
        m_ref[...] = m_new

    @pl.when(p < n_steps)
    def _():
        online_update([ref[0, 0, 0].reshape(g_n * dh, page) for ref in k_refs],
                      [ref[0, 0, 0].reshape(g_n * dh, page) for ref in v_refs], pl.multiple_of(p * chunk, chunk))

    @pl.when(p == n_steps)
    def _():
        online_update([new_ref[0, 0].reshape(g_n * dh, page)], [new_ref[0, 1].reshape(g_n * dh, page)],
                      n_steps * chunk)
        acc = acc_ref[...] * _inv_or_zero(l_ref[...])
        tw = win_ref.shape[2]
        kp_win = w0 + lax.broadcasted_iota(jnp.int32, (1, 1, tw), 2)
        for g in range(g_n):
            q = qbd_ref[g * rg:(g + 1) * rg, g * dh:(g + 1) * dh]
            o_win, _, _ = _attn_branch(q, slope_ref[g] * LOG2E, t_pos, win_ref[0, g], win_ref[0, g_n + g], kp_win,
                                       lambda dist: (dist >= 0) & (dist < WINDOW) & (kp_win >= 0))
            gates = _sigmoid(gate_ref[0, g]).reshape(rg, 3)
            o = (gates[:, 0:1] * ocmp_ref[g * rg:(g + 1) * rg, :]
                 + gates[:, 1:2] * acc[g * rg:(g + 1) * rg, g * dh:(g + 1) * dh] + gates[:, 2:3] * o_win)
            for h in range(r):
                o_ref[0, :, (g * r + h) * dh:(g * r + h + 1) * dh] = o[h * tq:(h + 1) * tq].astype(o_ref.dtype)


def _nsa_paged(q, gates_t, slopes, cmp_kv, cache_nt, layer, page_table, new_t, win_arr,
               *, p_len, n_cmp, n_blk, w0):
    b, tq, _ = q.shape
    g, r, dh = NSA_G, NSA_R, NSA_DH
    n_pages = page_table.shape[1]
    page = cache_nt.shape[5]
    pps = max(c_ for c_ in (16, 8, 4, 1) if n_pages % c_ == 0)
    n_steps = n_pages // pps
    nc = cmp_kv.shape[2]
    tw = win_arr.shape[2]
    rows = g * r * tq
    n_keys = (n_pages + 1) * page

    def page_map(c, s):
        return lambda i, p, pt: (layer, pt[i, jnp.minimum(p, n_steps - 1) * pps + s], c, 0, 0, 0)

    const = lambda i, p, pt: (i, 0, 0, 0, 0)
    grid_spec = pltpu.PrefetchScalarGridSpec(
        num_scalar_prefetch=1,
        grid=(b, n_steps + 1),
        in_specs=[pl.BlockSpec((1, 1, 1, g, dh, page), page_map(2, s)) for s in range(pps)]
        + [pl.BlockSpec((1, 1, 1, g, dh, page), page_map(3, s)) for s in range(pps)]
        + [pl.BlockSpec((1, tq, g * r * dh), lambda i, p, pt: (i, 0, 0)),
           pl.BlockSpec((1, g, r, tq, 3), const),
           pl.BlockSpec((g, r, 1, 1), lambda i, p, pt: (0, 0, 0, 0)),
           pl.BlockSpec((1, 2 * g, nc, dh), lambda i, p, pt: (i, 0, 0, 0)),
           pl.BlockSpec((1, 2, g, dh, page), const),
           pl.BlockSpec((1, 2 * g, tw, dh), lambda i, p, pt: (i, 0, 0, 0))],
        out_specs=pl.BlockSpec((1, tq, g * r * dh), lambda i, p, pt: (i, 0, 0)),
        scratch_shapes=[pltpu.VMEM((rows, g * dh), BF16),
                        pltpu.VMEM((rows, n_keys), F32),
                        pltpu.VMEM((rows, 1), F32),
                        pltpu.VMEM((rows, 1), F32),
                        pltpu.VMEM((rows, g * dh), F32),
                        pltpu.VMEM((rows, dh), F32)],
    )
    body = functools.partial(_nsa_paged_body, tq=tq, p_len=p_len, n_cmp=n_cmp, n_blk=n_blk, w0=w0, pps=pps,
                             n_steps=n_steps)
    return pl.pallas_call(
        body,
        grid_spec=grid_spec,
        out_shape=jax.ShapeDtypeStruct((b, tq, g * r * dh), BF16),
        compiler_params=_params("parallel", "arbitrary"),
        name="nsa_paged",
    )(page_table, *([cache_nt] * (2 * pps)), q, gates_t, slopes, cmp_kv, new_t, win_arr)


def _nsa_layer(x, cache_hm, layer, page_table, win_buf, norm_w, w_in, w1cat, pe_flat, w2, w_out, slopes):
    b, t, d = x.shape
    g, r, dh = NSA_G, NSA_R, NSA_DH
    xf = x.reshape(b * t, d)
    w_q, w_kv = w_in
    q = _rms_matmul(xf, norm_w, w_q, NSA_Q_DIM, BF16).reshape(b, t, NSA_Q_DIM)
    if cache_hm is None and t % HM_ROWS == 0:
        proj, kv_hm = _rms_matmul_hm(xf, norm_w, w_kv, b, t, 2 * NSA_KV_DIM, 4 * g, dh)
        proj = proj.reshape(b, t, -1)
    else:
        proj = _rms_matmul(xf, norm_w, w_kv, w_kv.shape[1]).reshape(b, t, -1)
        kv_hm = None
    kv = proj[..., :6 * NSA_KV_DIM]
    gates_t = proj[..., 6 * NSA_KV_DIM:6 * NSA_KV_DIM + 3 * NSA_HEADS]
    gates_t = gates_t.reshape(b, t, g, r, 3).transpose(0, 2, 3, 1, 4)
    kv6 = kv.reshape(b, t, 6, g, dh)
    new_rows = kv6[:, :, :4]
    if cache_hm is None:
        p_len = 0
        tk = t
        assert t % CMP_STRIDE == 0
        cmp_x = kv6[:, :, :2].astype(BF16).transpose(0, 2, 3, 1, 4).reshape(b, 2 * g, t // CMP_STRIDE,
                                                                           CMP_STRIDE * dh)
        if kv_hm is None:
            kv_hm = kv6[:, :, 2:].astype(BF16).transpose(0, 2, 3, 1, 4).reshape(b, 4 * g, t, dh)
        new_win = kv6[:, t - min(WINDOW, t):, 4:]
    else:
        cache_cmp, cache_nt = cache_hm
        n_pages = page_table.shape[1]
        page = cache_nt.shape[5]
        p_len = n_pages * page
        tk = p_len + t
        assert p_len % CMP_STRIDE == 0 and t < CMP_STRIDE and t <= page
        cmp_x = _page_gather(cache_cmp, layer, page_table)
        new_t = jnp.pad(kv6[:, :, 2:4].transpose(0, 2, 3, 4, 1), [(0, 0)] * 4 + [(0, page - t)])
        win_all = jnp.concatenate([win_buf, kv6[:, :, 4:]], axis=1)
        wl = win_all.shape[1]
        wlp = -(-wl // KEY_CHUNK) * KEY_CHUNK
        win_arr = jnp.pad(win_all.astype(BF16), [(0, 0), (0, wlp - wl), (0, 0), (0, 0), (0, 0)])
        win_arr = win_arr.transpose(0, 2, 3, 1, 4).reshape(b, 2 * g, wlp, dh)
        new_win = win_all[:, wl - min(WINDOW, tk):]
    n_sub = tk // CMP_STRIDE
    n_cmp = n_sub - 1
    n_blk = -(-tk // SEL_BLOCK)
    cmp_kv = _compress(cmp_x, n_sub, w1cat, pe_flat, w2)
    if cache_hm is None:
        o = _nsa_attn(q, gates_t, slopes, cmp_kv, kv_hm, 0, g, kv_hm, 2 * g, 3 * g,
                      p_len=0, n_cmp=n_cmp, n_blk=n_blk, w0=0)
    else:
        o = _nsa_paged(q, gates_t, slopes, cmp_kv, cache_nt, layer, page_table, new_t, win_arr,
                       p_len=p_len, n_cmp=n_cmp, n_blk=n_blk, w0=p_len - win_buf.shape[1])
    y = _matmul_res(o.reshape(b * t, NSA_Q_DIM), w_out, xf).reshape(b, t, d)
    return y, new_rows, new_win


def _ffn_layer(x, buf, norm_w, w_up, conv_w, w_down):
    b, t, d = x.shape
    xf = x.reshape(b * t, d)
    h = _rms_matmul(xf, norm_w, w_up, D_FF, BF16).reshape(b, t, -1)
    act, new_buf = _ffn_act(h, buf, conv_w)
    y = _matmul_res(act.reshape(b * t, D_FF), w_down, xf).reshape(b, t, d)
    return y, new_buf


def _pad_cols(w, n):
    return jnp.pad(w, [(0, 0)] * (w.ndim - 1) + [(0, n - w.shape[-1])])


def _trunk(x, cache_hm, page_table, nsa_win, gdn_state, gdn_conv, ffn_conv, wts):
    depth = wts["ffn_w_up"].shape[0]
    rows_l, win_l, s_l, gconv_l, fconv_l = [], [], [], [], []
    for i in range(depth):
        j = i // 2
        if i % 2 == 0:
            x, s_new, cb = _gdn_layer(x, gdn_state, j, gdn_conv[j], wts["gdn_norm"][j],
                                      (wts["gdn_w_main"][j], wts["gdn_w_tail"][j]),
                                      wts["gdn_conv_w"][j], wts["gdn_a_log"][j], wts["gdn_dt_bias"][j],
                                      wts["gdn_out_norm"][j], wts["gdn_w_out"][j])
            s_l.append(s_new)
            gconv_l.append(cb)
        else:
            x, rows, wb = _nsa_layer(x, cache_hm, j, page_table, None if nsa_win is None else nsa_win[j],
                                     wts["nsa_norm"][j], (wts["nsa_w_q"][j], wts["nsa_w_kv"][j]),
                                     wts["nsa_w1cat"][j],
                                     wts["nsa_pe_flat"][j], wts["nsa_w2"][j], wts["nsa_w_out"][j],
                                     wts["slopes"])
            rows_l.append(rows)
            win_l.append(wb)
        x, fb = _ffn_layer(x, ffn_conv[i], wts["ffn_norm"][i], wts["ffn_w_up"][i], wts["ffn_conv_w"][i],
                           wts["ffn_w_down"][i])
        fconv_l.append(fb)
    b, t, d = x.shape
    y = _rms(x.reshape(b * t, d), wts["final_norm"]).reshape(b, t, d)
    return (y, jnp.stack(rows_l), jnp.stack(win_l), jnp.stack(s_l), jnp.stack(gconv_l), jnp.stack(fconv_l))


def kernel(x_prompt, x_sample, cache_nsa_kv, cache_nsa_win, state_gdn_s, state_gdn_conv, state_ffn_conv,
           page_table, gdn_norm, gdn_w_in, gdn_conv_w, gdn_a_log, gdn_dt_bias, gdn_out_norm, gdn_w_out,
           nsa_norm, nsa_w_in, nsa_cmp_pe, nsa_cmp_w1, nsa_cmp_w2, nsa_w_out,
           ffn_norm, ffn_w_up, ffn_conv_w, ffn_w_down, final_norm):
    n_gdn = gdn_w_in.shape[0]
    n_nsa = nsa_w_in.shape[0]
    depth = ffn_w_up.shape[0]
    bp = x_prompt.shape[0]
    gdn_main = GDN_CONV_CH + GDN_V_DIM
    nsa_cols = -(-(nsa_w_in.shape[2] - NSA_Q_DIM) // 896) * 896
    w1 = nsa_cmp_w1.reshape(n_nsa, 2, 2, CMP_STRIDE * NSA_DH, CMP_HIDDEN)
    w1cat = jnp.concatenate([w1[:, :, 0], w1[:, :, 1]], axis=-1).astype(BF16)
    slopes = 2.0 ** (-8.0 * jnp.arange(1, NSA_HEADS + 1, dtype=F32) / NSA_HEADS)
    wts = {
        "gdn_norm": gdn_norm,
        "gdn_w_main": gdn_w_in[:, :, :gdn_main].astype(BF16),
        "gdn_w_tail": _pad_cols(gdn_w_in[:, :, gdn_main:], 128).astype(BF16),
        "gdn_conv_w": gdn_conv_w,
        "gdn_a_log": _pad_cols(gdn_a_log, 128).reshape(n_gdn, 1, 128),
        "gdn_dt_bias": _pad_cols(gdn_dt_bias, 128).reshape(n_gdn, 1, 128),
        "gdn_out_norm": gdn_out_norm,
        "gdn_w_out": gdn_w_out.astype(BF16),
        "nsa_norm": nsa_norm,
        "nsa_w_q": nsa_w_in[:, :, :NSA_Q_DIM].astype(BF16),
        "nsa_w_kv": _pad_cols(nsa_w_in[:, :, NSA_Q_DIM:], nsa_cols).astype(BF16),
        "nsa_w1cat": w1cat,
        "nsa_pe_flat": nsa_cmp_pe.reshape(n_nsa, 2, 2, CMP_STRIDE * NSA_DH),
        "nsa_w2": nsa_cmp_w2.astype(BF16),
        "nsa_w_out": nsa_w_out.astype(BF16),
        "slopes": slopes.reshape(NSA_G, NSA_R, 1, 1),
        "ffn_norm": ffn_norm,
        "ffn_w_up": ffn_w_up.astype(BF16),
        "ffn_conv_w": ffn_conv_w,
        "ffn_w_down": ffn_w_down.astype(BF16),
        "final_norm": final_norm,
    }
    n_l, pool, page = cache_nsa_kv.shape[:3]
    cache_cmp = cache_nsa_kv[:, :, :, :2].astype(BF16).transpose(0, 1, 3, 4, 2, 5)
    cache_hm = (cache_cmp.reshape(n_l, pool, 2 * NSA_G, page // CMP_STRIDE, CMP_STRIDE * NSA_DH),
                cache_nsa_kv.transpose(0, 1, 3, 4, 5, 2))

    zeros = functools.partial(jnp.zeros, dtype=F32)
    prompt = _trunk(x_prompt, None, None, None,
                    zeros((n_gdn, bp, GDN_V_HEADS, GDN_DK, GDN_DV)),
                    zeros((n_gdn, bp, GDN_CONV_W - 1, GDN_CONV_CH)),
                    zeros((depth, bp, FFN_CONV_W - 1, D_FF)), wts)
    sample = _trunk(x_sample, cache_hm, page_table, cache_nsa_win, state_gdn_s, state_gdn_conv, state_ffn_conv, wts)
    out = []
    for p, s in zip(prompt, sample):
        out.extend([p, s])
    return tuple(out)
```

```python
import functools

import jax
import jax.numpy as jnp
from jax import lax
from jax.experimental import pallas as pl
from jax.experimental.pallas import tpu as pltpu

F32 = jnp.float32
BF16 = jnp.bfloat16

RMS_EPS = 1e-6
L2_EPS = 1e-6
NEG_INF = -1e30
FORCE_SCORE = 1e9
NEVER = -3e38
MASKED_DIST = 1e30
LOG2E = 1.4426950408889634

GDN_QK_HEADS = 8
GDN_V_HEADS = 16
GDN_DK = 128
GDN_DV = 128
GDN_QK_DIM = GDN_QK_HEADS * GDN_DK
GDN_V_DIM = GDN_V_HEADS * GDN_DV
GDN_CONV_CH = 2 * GDN_QK_DIM + GDN_V_DIM
GDN_CONV_W = 4
GDN_CHUNK = 64
GDN_HEADS_PER_STEP = 4
GDN_PREP_CHUNKS = 8
NSA_HEADS = 16
NSA_G = 4
NSA_R = 4
NSA_DH = 64
NSA_Q_DIM = NSA_HEADS * NSA_DH
NSA_KV_DIM = NSA_G * NSA_DH
CMP_STRIDE = 16
CMP_LEN = 32
CMP_HIDDEN = 256
SEL_BLOCK = 64
SEL_TOPN = 8
WINDOW = 512
NSA_Q_BLOCK = 64
NSA_Q_BLOCKS_PER_STEP = 8
KEY_CHUNK = 128
NSA_SEL_CLASS = 256
D_FF = 2816
FFN_CONV_W = 3
FFN_TC = 256
SEQ_ROWS_PER_STEP = 2048
CONV_HEAD_ROWS = 16
HM_ROWS = 512

VMEM_LIMIT = 52 * 1024 * 1024


def _params(*sem):
    return pltpu.CompilerParams(dimension_semantics=sem, vmem_limit_bytes=VMEM_LIMIT)


def _mm(a, b):
    return jnp.dot(a.astype(BF16), b.astype(BF16), preferred_element_type=F32)


def _mm_nt(a, b):
    return lax.dot_general(a.astype(BF16), b.astype(BF16), (((1,), (1,)), ((), ())),
                           preferred_element_type=F32)


def _mm_tn(a, b):
    return lax.dot_general(a.astype(BF16), b.astype(BF16), (((0,), (0,)), ((), ())),
                           preferred_element_type=F32)


def _sigmoid(x):
    return 0.5 * jnp.tanh(0.5 * x) + 0.5


def _silu(x):
    h = 0.5 * x
    return h * jnp.tanh(h) + h


def _rms_matmul_body(x_ref, nw_ref, w_ref, o_ref, xn_ref):
    @pl.when(pl.program_id(1) == 0)
    def _():
        x = x_ref[...]
        inv = lax.rsqrt(jnp.mean(x * x, axis=-1, keepdims=True) + RMS_EPS)
        xn_ref[...] = ((x * inv) * nw_ref[...]).astype(BF16)

    o_ref[...] = jnp.dot(xn_ref[...], w_ref[...], preferred_element_type=F32).astype(o_ref.dtype)


def _rms_matmul(x, nw, w, tn, out_dtype=F32):
    m, k = x.shape
    n = w.shape[1]
    tm = min(m, 1024)
    return pl.pallas_call(
        _rms_matmul_body,
        grid=(m // tm, n // tn),
        in_specs=[pl.BlockSpec((tm, k), lambda i, j: (i, 0)),
                  pl.BlockSpec((1, k), lambda i, j: (0, 0)),
                  pl.BlockSpec((k, tn), lambda i, j: (0, j))],
        out_specs=pl.BlockSpec((tm, tn), lambda i, j: (i, j)),
        out_shape=jax.ShapeDtypeStruct((m, n), out_dtype),
        scratch_shapes=[pltpu.VMEM((tm, k), BF16)],
        compiler_params=_params("parallel", "arbitrary"),
        name="rms_matmul",
    )(x, nw.reshape(1, k), w)


def _rms_matmul_hm_body(x_ref, nw_ref, w_ref, o_ref, hm_ref, *, col0):
    x = x_ref[...]
    inv = lax.rsqrt(jnp.mean(x * x, axis=-1, keepdims=True) + RMS_EPS)
    xn = ((x * inv) * nw_ref[...]).astype(BF16)
    acc = jnp.dot(xn, w_ref[...], preferred_element_type=F32)
    o_ref[...] = acc
    dh = hm_ref.shape[3]
    for cg in range(hm_ref.shape[1]):
        hm_ref[0, cg] = acc[:, col0 + cg * dh:col0 + (cg + 1) * dh].astype(hm_ref.dtype)


def _rms_matmul_hm(x, nw, w, b, t, col0, n_cg, dh):
    m, k = x.shape
    n = w.shape[1]
    tm = HM_ROWS
    tps = t // tm
    return pl.pallas_call(
        functools.partial(_rms_matmul_hm_body, col0=col0),
        grid=(m // tm,),
        in_specs=[pl.BlockSpec((tm, k), lambda i: (i, 0)),
                  pl.BlockSpec((1, k), lambda i: (0, 0)),
                  pl.BlockSpec((k, n), lambda i: (0, 0))],
        out_specs=[pl.BlockSpec((tm, n), lambda i: (i, 0)),
                   pl.BlockSpec((1, n_cg, tm, dh), lambda i: (i // tps, 0, i % tps, 0))],
        out_shape=[jax.ShapeDtypeStruct((m, n), F32),
                   jax.ShapeDtypeStruct((b, n_cg, t, dh), BF16)],
        compiler_params=_params("parallel"),
        name="rms_matmul_hm",
    )(x, nw.reshape(1, k), w)


def _matmul_res_body(a_ref, w_ref, r_ref, o_ref):
    o_ref[...] = r_ref[...] + jnp.dot(a_ref[...].astype(BF16), w_ref[...], preferred_element_type=F32)


def _matmul_res(a, w, res):
    m, k = a.shape
    n = w.shape[1]
    tm = min(m, 1024)
    return pl.pallas_call(
        _matmul_res_body,
        grid=(m // tm,),
        in_specs=[pl.BlockSpec((tm, k), lambda i: (i, 0)),
                  pl.BlockSpec((k, n), lambda i: (0, 0)),
                  pl.BlockSpec((tm, n), lambda i: (i, 0))],
        out_specs=pl.BlockSpec((tm, n), lambda i: (i, 0)),
        out_shape=jax.ShapeDtypeStruct((m, n), F32),
        compiler_params=_params("parallel"),
        name="matmul_res",
    )(a, w, res)


def _rms_body(x_ref, nw_ref, o_ref):
    x = x_ref[...]
    inv = lax.rsqrt(jnp.mean(x * x, axis=-1, keepdims=True) + RMS_EPS)
    o_ref[...] = (x * inv) * nw_ref[...]


def _rms(x, nw):
    m, k = x.shape
    tm = min(m, 1024)
    return pl.pallas_call(
        _rms_body,
        grid=(m // tm,),
        in_specs=[pl.BlockSpec((tm, k), lambda i: (i, 0)), pl.BlockSpec((1, k), lambda i: (0, 0))],
        out_specs=pl.BlockSpec((tm, k), lambda i: (i, 0)),
        out_shape=jax.ShapeDtypeStruct((m, k), F32),
        compiler_params=_params("parallel"),
        name="final_rms",
    )(x, nw.reshape(1, k))


def _shifted(x, prev_rows, shift, row):
    nb = prev_rows.shape[1]
    y = pltpu.roll(x, shift, 1)
    for r in range(shift):
        y = jnp.where(row == r, prev_rows[:, nb - shift + r:nb - shift + r + 1], y)
    return y


def _batch_block(b, t):
    bb = max(1, min(b, SEQ_ROWS_PER_STEP // t))
    while b % bb:
        bb -= 1
    return bb


def _causal_conv(x, buf, w):
    width = w.shape[0]
    t = x.shape[1]

    def conv(xs, exact):
        row = lax.broadcasted_iota(jnp.int32, xs.shape, 1)
        y = None
        for i in range(width):
            shift = width - 1 - i
            if shift == 0:
                term = xs
            elif exact:
                term = _shifted(xs, buf, shift, row)
            else:
                term = pltpu.roll(xs, shift, 1)
            term = term * w[i:i + 1]
            y = term if y is None else y + term
        return y

    if t <= CONV_HEAD_ROWS:
        return conv(x, True), None
    return conv(x, False), conv(x[:, :CONV_HEAD_ROWS], True)


def _ffn_act_body(a_ref, g_ref, buf_ref, cw_ref, act_ref, nb_ref):
    a = a_ref[...].astype(F32)
    g = g_ref[...].astype(F32)
    t = a.shape[1]
    y, y_head = _causal_conv(a, buf_ref[...], cw_ref[...])
    act_ref[...] = (_silu(y) * g).astype(act_ref.dtype)
    if y_head is not None:
        act_ref[:, :CONV_HEAD_ROWS, :] = (_silu(y_head) * g[:, :CONV_HEAD_ROWS]).astype(act_ref.dtype)
    nb_ref[...] = a[:, t - (FFN_CONV_W - 1):t, :]


def _ffn_act(h, buf, cw):
    b, t, _ = h.shape
    nj = D_FF // FFN_TC
    bb = _batch_block(b, t)
    return pl.pallas_call(
        _ffn_act_body,
        grid=(b // bb, nj),
        in_specs=[pl.BlockSpec((bb, t, FFN_TC), lambda i, j: (i, 0, j)),
                  pl.BlockSpec((bb, t, FFN_TC), lambda i, j: (i, 0, j + nj)),
                  pl.BlockSpec((bb, FFN_CONV_W - 1, FFN_TC), lambda i, j: (i, 0, j)),
                  pl.BlockSpec((FFN_CONV_W, FFN_TC), lambda i, j: (0, j))],
        out_specs=[pl.BlockSpec((bb, t, FFN_TC), lambda i, j: (i, 0, j)),
                   pl.BlockSpec((bb, FFN_CONV_W - 1, FFN_TC), lambda i, j: (i, 0, j))],
        out_shape=[jax.ShapeDtypeStruct((b, t, D_FF), BF16),
                   jax.ShapeDtypeStruct((b, FFN_CONV_W - 1, D_FF), F32)],
        compiler_params=_params("parallel", "parallel"),
        name="ffn_act",
    )(h, h, buf, cw)


GDN_PRE_TC = 512


def _gdn_pre_body(x_ref, buf_ref, cw_ref, o_ref, nb_ref):
    j = pl.program_id(1)
    x = x_ref[...].astype(F32)
    t = x.shape[1]
    is_q = j < GDN_QK_DIM // GDN_PRE_TC
    is_v = j >= 2 * GDN_QK_DIM // GDN_PRE_TC
    qscale = jnp.where(is_q, GDN_DK ** -0.5, 1.0).astype(F32)

    def finish(y, n_rows):
        y = _silu(y)
        for h in range(GDN_PRE_TC // GDN_DK):
            yh = y[:, :, h * GDN_DK:(h + 1) * GDN_DK]
            inv = lax.rsqrt(jnp.sum(yh * yh, axis=-1, keepdims=True) + L2_EPS)
            o_ref[:, :n_rows, h * GDN_DK:(h + 1) * GDN_DK] = jnp.where(
                is_v, yh, yh * (inv * qscale)).astype(o_ref.dtype)

    y, y_head = _causal_conv(x, buf_ref[...], cw_ref[...])
    finish(y, t)
    if y_head is not None:
        finish(y_head, CONV_HEAD_ROWS)
    nb_ref[...] = x[:, t - (GDN_CONV_W - 1):t, :]


def _gdn_pre(proj, buf, cw):
    b, t, _ = proj.shape
    nj = GDN_CONV_CH // GDN_PRE_TC
    bb = _batch_block(b, t)
    return pl.pallas_call(
        _gdn_pre_body,
        grid=(b // bb, nj),
        in_specs=[pl.BlockSpec((bb, t, GDN_PRE_TC), lambda i, j: (i, 0, j)),
                  pl.BlockSpec((bb, GDN_CONV_W - 1, GDN_PRE_TC), lambda i, j: (i, 0, j)),
                  pl.BlockSpec((GDN_CONV_W, GDN_PRE_TC), lambda i, j: (0, j))],
        out_specs=[pl.BlockSpec((bb, t, GDN_PRE_TC), lambda i, j: (i, 0, j)),
                   pl.BlockSpec((bb, GDN_CONV_W - 1, GDN_PRE_TC), lambda i, j: (i, 0, j))],
        out_shape=[jax.ShapeDtypeStruct((b, t, GDN_CONV_CH), BF16),
                   jax.ShapeDtypeStruct((b, GDN_CONV_W - 1, GDN_CONV_CH), F32)],
        compiler_params=_params("parallel", "parallel"),
        name="gdn_pre",
    )(proj, buf, cw)


def _gdn_gate_body(x_ref, alog_ref, dtb_ref, g_ref, gcum_ref, beta_ref, *, chunk):
    x = x_ref[...]
    z = x + dtb_ref[...]
    softplus = jnp.maximum(z, 0.0) + jnp.log(1.0 + jnp.exp(-jnp.abs(z)))
    g = -jnp.exp(alog_ref[...]) * softplus
    g_ref[...] = g
    row = lax.broadcasted_iota(jnp.int32, x.shape, 1) % chunk
    acc = g
    s = 1
    while s < chunk:
        acc = acc + jnp.where(row >= s, pltpu.roll(acc, s, 1), 0.0)
        s *= 2
    gcum_ref[...] = acc
    beta_ref[...] = _sigmoid(x)


def _gdn_gate(proj, alog_pad, dtb_pad, lane_block):
    b, t, _ = proj.shape
    chunk = min(GDN_CHUNK, t)
    bb = _batch_block(b, t)
    spec = pl.BlockSpec((bb, t, 128), lambda i: (i, 0, 0))
    return pl.pallas_call(
        functools.partial(_gdn_gate_body, chunk=chunk),
        grid=(b // bb,),
        in_specs=[pl.BlockSpec((bb, t, 128), lambda i: (i, 0, lane_block)),
                  pl.BlockSpec((1, 128), lambda i: (0, 0)),
                  pl.BlockSpec((1, 128), lambda i: (0, 0))],
        out_specs=[spec, spec, spec],
        out_shape=[jax.ShapeDtypeStruct((b, t, 128), F32)] * 3,
        compiler_params=_params("parallel"),
        name="gdn_gate",
    )(proj, alog_pad, dtb_pad)


def _unit_lower_inverses(lows, n):
    eye = (lax.broadcasted_iota(jnp.int32, (n, n), 0) == lax.broadcasted_iota(jnp.int32, (n, n), 1)).astype(F32)
    ps = [eye - low for low in lows]
    ms = [_mm(low, low) for low in lows]
    k = 2
    while True:
        ps = [p + _mm(p, m) for p, m in zip(ps, ms)]
        k *= 2
        if k >= n:
            break
        ms = [_mm(m, m) for m in ms]
    return ps


def _gdn_scan_body(q_ref, k_ref, v_ref, z_ref, gc_ref, bc_ref, gr_ref, br_ref, s0_ref, onw_ref, o_ref, s_ref,
                   u_ref, w_ref, qk_ref, qd_ref, kd_ref):
    c = GDN_CHUNK
    hb = s0_ref.shape[1]
    n_chunks = q_ref.shape[1] // c
    s_ref[...] = s0_ref[...]
    ri = lax.broadcasted_iota(jnp.int32, (c, c), 0)
    ci = lax.broadcasted_iota(jnp.int32, (c, c), 1)
    tri = ri >= ci
    stri = ri > ci
    onw = onw_ref[...]

    def qk_slice(ref, r0, qh):
        return ref[0, pl.ds(r0, c), qh * GDN_DK:(qh + 1) * GDN_DK]

    def v_slice(ref, r0, hh):
        return ref[0, pl.ds(r0, c), hh * GDN_DV:(hh + 1) * GDN_DV]

    cpb = max(c_ for c_ in (GDN_PREP_CHUNKS, 4, 2, 1) if n_chunks % c_ == 0)

    def prep(nb, carry):
        items = []
        kk, qk, ks, qs = {}, {}, {}, {}
        for ch in range(cpb):
            n = nb * cpb + ch
            r0 = pl.multiple_of(n * c, c)
            gcol = gc_ref[0, 0, pl.ds(r0, c), :]
            bcol = bc_ref[0, 0, pl.ds(r0, c), :]
            grow = gr_ref[0, 0, n]
            brow = br_ref[0, 0, n]
            for qh in range(hb // 2):
                ks[ch, qh] = qk_slice(k_ref, r0, qh)
                qs[ch, qh] = qk_slice(q_ref, r0, qh)
            for hh in range(hb):
                items.append(dict(ch=ch, hh=hh, r0=r0, v=v_slice(v_ref, r0, hh),
                                  gcb=jnp.broadcast_to(gcol[:, hh:hh + 1], (c, GDN_DK)),
                                  bcb=jnp.broadcast_to(bcol[:, hh:hh + 1], (c, c)),
                                  gr=grow[hh:hh + 1, :], br=brow[hh:hh + 1, :]))
        for key in ks:
            kk[key] = _mm_nt(ks[key], ks[key])
        for key in ks:
            qk[key] = _mm_nt(qs[key], ks[key])
        lows = []
        for it in items:
            key = (it["ch"], it["hh"] // 2)
            it["decay"] = jnp.where(tri, jnp.exp(jnp.where(tri, it["gcb"][:, :c] - it["gr"], 0.0)), 0.0)
            lows.append(jnp.where(stri, (kk[key] * it["bcb"]) * it["decay"], 0.0))
        tinvs = _unit_lower_inverses(lows, c)
        us = [_mm(tinv * it["br"], it["v"]) for tinv, it in zip(tinvs, items)]
        ws = [_mm(tinv * (it["br"] * jnp.exp(it["gr"])), ks[it["ch"], it["hh"] // 2])
              for tinv, it in zip(tinvs, items)]
        for it, u, w in zip(items, us, ws):
            key = (it["ch"], it["hh"] // 2)
            hh, r0, gcb = it["hh"], it["r0"], it["gcb"]
            u_ref[hh, pl.ds(r0, c), :] = u
            w_ref[hh, pl.ds(r0, c), :] = w.astype(BF16)
            qk_ref[hh, pl.ds(r0, c), :] = jnp.where(tri, qk[key] * it["decay"], 0.0).astype(BF16)
            qd_ref[hh, pl.ds(r0, c), :] = (qs[key] * jnp.exp(gcb)).astype(BF16)
            kd_ref[hh, pl.ds(r0, c), :] = (ks[key] * jnp.exp(gcb[c - 1:c, :] - gcb)).astype(BF16)
        return carry

    lax.fori_loop(0, n_chunks // cpb, prep, 0)

    def scan(n, carry):
        r0 = pl.multiple_of(n * c, c)
        g_last = gc_ref[0, 0, pl.ds(r0 + (c - 1), 1), :]
        loaded = []
        for hh in range(hb):
            loaded.append((u_ref[hh, pl.ds(r0, c), :], w_ref[hh, pl.ds(r0, c), :], qk_ref[hh, pl.ds(r0, c), :],
                           qd_ref[hh, pl.ds(r0, c), :], kd_ref[hh, pl.ds(r0, c), :], v_slice(z_ref, r0, hh),
                           s_ref[0, hh]))
        ws_s = [_mm(w, s) for (u, w, qkm, qd, kd, z, s) in loaded]
        qd_s = [_mm(qd, s) for (u, w, qkm, qd, kd, z, s) in loaded]
        v_news = [ld[0] - ws for ld, ws in zip(loaded, ws_s)]
        qk_v = [_mm(ld[2], vn) for ld, vn in zip(loaded, v_news)]
        kd_v = [_mm_tn(ld[4], vn) for ld, vn in zip(loaded, v_news)]
        results = []
        for hh, ld in enumerate(loaded):
            o = qd_s[hh] + qk_v[hh]
            s_new = ld[6] * jnp.exp(g_last[:, hh:hh + 1]) + kd_v[hh]
            inv = lax.rsqrt(jnp.mean(o * o, axis=-1, keepdims=True) + RMS_EPS)
            results.append((((o * inv) * onw) * _silu(ld[5].astype(F32)), s_new))
        for hh, (og, s_new) in enumerate(results):
            s_ref[0, hh] = s_new
            o_ref[0, pl.ds(r0, c), hh * GDN_DV:(hh + 1) * GDN_DV] = og.astype(o_ref.dtype)
        return carry

    lax.fori_loop(0, n_chunks, scan, 0)


def _gdn_scan(qkv, proj, gcol, bcol, grow, brow, s0, layer, onw):
    b, t, _ = qkv.shape
    hb = gcol.shape[3]
    hg = GDN_V_HEADS // hb
    qw = hb // 2 * GDN_DK
    vw = hb * GDN_DV
    n = t // GDN_CHUNK
    return pl.pallas_call(
        _gdn_scan_body,
        grid=(b, hg),
        in_specs=[pl.BlockSpec((1, t, qw), lambda i, j: (i, 0, j)),
                  pl.BlockSpec((1, t, qw), lambda i, j: (i, 0, GDN_QK_DIM // qw + j)),
                  pl.BlockSpec((1, t, vw), lambda i, j: (i, 0, 2 * GDN_QK_DIM // vw + j)),
                  pl.BlockSpec((1, t, vw), lambda i, j: (i, 0, GDN_CONV_CH // vw + j)),
                  pl.BlockSpec((1, 1, t, hb), lambda i, j: (i, j, 0, 0)),
                  pl.BlockSpec((1, 1, t, hb), lambda i, j: (i, j, 0, 0)),
                  pl.BlockSpec((1, 1, n, hb, GDN_CHUNK), lambda i, j: (i, j, 0, 0, 0)),
                  pl.BlockSpec((1, 1, n, hb, GDN_CHUNK), lambda i, j: (i, j, 0, 0, 0)),
                  pl.BlockSpec((None, 1, hb, GDN_DK, GDN_DV), lambda i, j: (layer, i, j, 0, 0)),
                  pl.BlockSpec((1, GDN_DV), lambda i, j: (0, 0))],
        out_specs=[pl.BlockSpec((1, t, vw), lambda i, j: (i, 0, j)),
                   pl.BlockSpec((1, hb, GDN_DK, GDN_DV), lambda i, j: (i, j, 0, 0))],
        out_shape=[jax.ShapeDtypeStruct((b, t, GDN_V_DIM), BF16),
                   jax.ShapeDtypeStruct((b, GDN_V_HEADS, GDN_DK, GDN_DV), F32)],
        scratch_shapes=[pltpu.VMEM((hb, t, GDN_DV), F32),
                        pltpu.VMEM((hb, t, GDN_DK), BF16),
                        pltpu.VMEM((hb, t, GDN_CHUNK), BF16),
                        pltpu.VMEM((hb, t, GDN_DK), BF16),
                        pltpu.VMEM((hb, t, GDN_DK), BF16)],
        compiler_params=_params("parallel", "parallel"),
        name="gdn_scan",
    )(qkv, qkv, qkv, proj, gcol, bcol, grow, brow, s0, onw.reshape(1, GDN_DV))


def _gdn_layer(x, s0, layer, conv_buf, norm_w, w_in, conv_w, alog_pad, dtb_pad, out_norm_w, w_out):
    b, t, d = x.shape
    xf = x.reshape(b * t, d)
    w_main, w_tail = w_in
    proj = _rms_matmul(xf, norm_w, w_main, 2048, BF16).reshape(b, t, -1)
    tail = _rms_matmul(xf, norm_w, w_tail, 128).reshape(b, t, -1)
    qkv, new_buf = _gdn_pre(proj, conv_buf, conv_w)
    _, gcum, beta = _gdn_gate(tail, alog_pad, dtb_pad, 0)
    gcum = gcum[:, :, :GDN_V_HEADS]
    beta = beta[:, :, GDN_V_HEADS:2 * GDN_V_HEADS]
    tp = -(-t // GDN_CHUNK) * GDN_CHUNK
    if tp != t:
        pad = [(0, 0), (0, tp - t), (0, 0)]
        qkv = jnp.pad(qkv, pad)
        proj_z = jnp.pad(proj, pad)
        gcum = jnp.pad(gcum, pad, mode="edge")
        beta = jnp.pad(beta, pad)
    else:
        proj_z = proj
    hb = GDN_V_HEADS if tp == GDN_CHUNK else GDN_HEADS_PER_STEP
    hg = GDN_V_HEADS // hb
    n = tp // GDN_CHUNK
    gcol = gcum.reshape(b, tp, hg, hb).transpose(0, 2, 1, 3)
    bcol = beta.reshape(b, tp, hg, hb).transpose(0, 2, 1, 3)
    grow = gcum.reshape(b, n, GDN_CHUNK, hg, hb).transpose(0, 3, 1, 4, 2)
    brow = beta.reshape(b, n, GDN_CHUNK, hg, hb).transpose(0, 3, 1, 4, 2)
    o, s_new = _gdn_scan(qkv, proj_z, gcol, bcol, grow, brow, s0, layer, out_norm_w)
    o = o[:, :t].reshape(b * t, GDN_V_DIM)
    y = _matmul_res(o, w_out, xf).reshape(b, t, d)
    return y, s_new, new_buf


def _page_gather_body(pt_ref, *refs, pps):
    cmp_refs, o_ref = refs[:pps], refs[pps]
    sub = cmp_refs[0].shape[3]
    for s in range(pps):
        o_ref[0, :, s * sub:(s + 1) * sub, :] = cmp_refs[s][0, 0]


def _page_gather(cache_cmp, layer, page_table):
    b, n_pages = page_table.shape
    _, _, n_cg, sub, flat = cache_cmp.shape
    pps = 16 if n_pages % 16 == 0 else (8 if n_pages % 8 == 0 else 1)
    n_steps = n_pages // pps

    def page_map(s):
        return lambda i, p, pt: (layer, pt[i, p * pps + s], 0, 0, 0)

    grid_spec = pltpu.PrefetchScalarGridSpec(
        num_scalar_prefetch=1,
        grid=(b, n_steps),
        in_specs=[pl.BlockSpec((1, 1, n_cg, sub, flat), page_map(s)) for s in range(pps)],
        out_specs=pl.BlockSpec((1, n_cg, pps * sub, flat), lambda i, p, pt: (i, 0, p, 0)),
    )
    return pl.pallas_call(
        functools.partial(_page_gather_body, pps=pps),
        grid_spec=grid_spec,
        out_shape=jax.ShapeDtypeStruct((b, n_cg, n_pages * sub, flat), cache_cmp.dtype),
        compiler_params=_params("parallel", "arbitrary"),
        name="page_gather",
    )(page_table, *([cache_cmp] * pps))


def _compress_body(x_ref, w1_ref, pe_ref, w2_ref, o_ref, bias_ref, *, n_sub):
    hd = CMP_HIDDEN

    @pl.when(pl.program_id(1) == 0)
    def _():
        pe = pe_ref[0]
        pe0 = jnp.broadcast_to(pe[0:1], (8, pe.shape[1]))
        pe1 = jnp.broadcast_to(pe[1:2], (8, pe.shape[1]))
        bias_ref[:, :hd] = jnp.dot(pe0.astype(BF16), w1_ref[0, :, :hd], preferred_element_type=F32)
        bias_ref[:, hd:] = jnp.dot(pe1.astype(BF16), w1_ref[0, :, hd:], preferred_element_type=F32)

    acc = jnp.dot(x_ref[0, 0], w1_ref[0], preferred_element_type=F32)
    first = acc[:, :hd] + bias_ref[0:1, :hd]
    second = acc[:, hd:] + bias_ref[0:1, hd:]
    hid = _silu(first + pltpu.roll(second, n_sub - 1, 0))
    o_ref[0, 0] = jnp.dot(hid.astype(BF16), w2_ref[0], preferred_element_type=F32)


def _compress(x, n_sub, w1cat, pe_flat, w2):
    b, _, _, flat = x.shape
    dh = flat // CMP_STRIDE
    return pl.pallas_call(
        functools.partial(_compress_body, n_sub=n_sub),
        grid=(2 * NSA_G, b),
        in_specs=[pl.BlockSpec((1, 1, n_sub, flat), lambda j, i: (i, j, 0, 0)),
                  pl.BlockSpec((1, flat, 2 * CMP_HIDDEN), lambda j, i: (j // NSA_G, 0, 0)),
                  pl.BlockSpec((1, 2, flat), lambda j, i: (j // NSA_G, 0, 0)),
                  pl.BlockSpec((1, CMP_HIDDEN, dh), lambda j, i: (j // NSA_G, 0, 0))],
        out_specs=pl.BlockSpec((1, 1, n_sub, dh), lambda j, i: (i, j, 0, 0)),
        out_shape=jax.ShapeDtypeStruct((b, 2 * NSA_G, n_sub, dh), F32),
        scratch_shapes=[pltpu.VMEM((8, 2 * CMP_HIDDEN), F32)],
        compiler_params=_params("arbitrary", "arbitrary"),
        name="nsa_compress",
    )(x, w1cat, pe_flat, w2)


def _inv_or_zero(l):
    return jnp.where(l > 0.0, 1.0 / jnp.where(l > 0.0, l, 1.0), 0.0)


def _attn_branch(q, slopes, t_pos, k, v, kp, ok_fn, may_be_empty=False):
    r, qb, nk = slopes.shape[0], t_pos.shape[1], k.shape[0]
    dist = t_pos - kp
    ok = ok_fn(dist)
    pen = jnp.where(ok, dist.astype(F32), MASKED_DIST)
    s = _mm_nt(q, k).reshape(r, qb, nk) - slopes * pen
    m = jnp.max(s, axis=-1, keepdims=True)
    p = jnp.exp2(s - m)
    if may_be_empty:
        p = jnp.where(ok, p, 0.0)
    inv = _inv_or_zero(jnp.sum(p, axis=-1, keepdims=True))
    o = _mm(p.reshape(r * qb, nk), v) * inv.reshape(r * qb, 1)
    return o, p, inv


def _pick_blocks(p_sum, t_lane, n_cmp, n_blk, nb_rows):
    nq, nc = p_sum.shape
    nb = -(-n_blk // 8) * 8
    jj = lax.broadcasted_iota(jnp.int32, (nb, nc), 0)
    nn = lax.broadcasted_iota(jnp.int32, (nb, nc), 1)
    overlap_t = ((nn * CMP_STRIDE < (jj + 1) * SEL_BLOCK) & (nn * CMP_STRIDE + CMP_LEN > jj * SEL_BLOCK)
                 & (nn < n_cmp) & (jj < n_blk)).astype(F32)
    imp = lax.dot_general(overlap_t, p_sum, (((1,), (1,)), ((), ())), preferred_element_type=F32,
                          precision=lax.Precision.HIGHEST)
    j = lax.broadcasted_iota(jnp.int32, (nb, nq), 0)
    cur = t_lane // SEL_BLOCK
    imp = jnp.where((j == 0) | (j == cur) | (j == cur - 1), FORCE_SCORE, imp)
    imp = jnp.where(j > cur, -FORCE_SCORE, imp)
    imp = jnp.where(j >= n_blk, NEVER, imp)
    rank = jnp.zeros((nb, nq), jnp.int32)
    for jp in range(n_blk):
        row = imp[jp:jp + 1, :]
        rank = rank + ((row > imp) | ((row == imp) & (j > jp))).astype(jnp.int32)
    picked_t = ((rank < SEL_TOPN) & (j < n_blk)).astype(F32)
    if nb_rows > nb:
        picked_t = jnp.concatenate([picked_t, jnp.zeros((nb_rows - nb, nq), F32)], axis=0)
    return picked_t


def _transpose_01(x_t):
    m = x_t.shape[1]
    eye = (lax.broadcasted_iota(jnp.int32, (m, m), 0) == lax.broadcasted_iota(jnp.int32, (m, m), 1))
    return _mm_nt(eye.astype(BF16), x_t)


def _head_rows(x, n_heads):
    dh = x.shape[1] // n_heads
    return jnp.concatenate([x[:, h * dh:(h + 1) * dh] for h in range(n_heads)], axis=0)


def _nsa_attn_body(q_ref, gate_ref, slope_ref, ck_ref, cv_ref, ks_ref, vs_ref, kw_ref, vw_ref, o_ref, osel_ref,
                   *, nsb, **kw):
    refs = (q_ref, gate_ref, slope_ref, ck_ref, cv_ref, ks_ref, vs_ref, kw_ref, vw_ref, o_ref, osel_ref)
    if nsb == 1:
        _nsa_attn_block(0, pl.program_id(2), *refs, **kw)
    else:
        def one(sub, carry):
            _nsa_attn_block(sub, pl.program_id(2) * nsb + sub, *refs, **kw)
            return carry

        lax.fori_loop(0, nsb, one, 0)


def _nsa_attn_block(sub, i, q_ref, gate_ref, slope_ref, ck_ref, cv_ref, ks_ref, vs_ref, kw_ref, vw_ref, o_ref,
                    osel_ref, *, qb, p_len, n_cmp, n_blk, w0, sel_classes, kw_len):
    r = NSA_R
    rows = r * qb
    rq = sub * qb if isinstance(sub, int) else pl.multiple_of(sub * qb, qb)
    q = _head_rows(q_ref[0, pl.ds(rq, qb), :].astype(F32), r)
    q = (q * (NSA_DH ** -0.5 * LOG2E)).astype(BF16)
    slopes = slope_ref[0] * LOG2E
    q0 = p_len + i * qb
    t_pos = q0 + lax.broadcasted_iota(jnp.int32, (1, qb, 1), 1)
    branch = functools.partial(_attn_branch, q, slopes, t_pos)

    tw = kw_ref.shape[2]
    if tw == kw_len:
        start = 0
    else:
        start = pl.multiple_of(jnp.clip(q0 + qb - w0 - kw_len, 0, tw - kw_len), SEL_BLOCK)
    kp_win = w0 + start + lax.broadcasted_iota(jnp.int32, (1, 1, kw_len), 2)
    o_win, _, _ = branch(kw_ref[0, 0, pl.ds(start, kw_len), :], vw_ref[0, 0, pl.ds(start, kw_len), :], kp_win,
                         lambda dist: (dist >= 0) & (dist < WINDOW) & (kp_win >= 0))

    nc = ck_ref.shape[2]
    n_idx = lax.broadcasted_iota(jnp.int32, (1, 1, nc), 2)
    o_cmp, p_cmp, inv_cmp = branch(ck_ref[0, 0], cv_ref[0, 0], n_idx * CMP_STRIDE + (CMP_LEN - 1),
                                   lambda dist: (dist >= 0) & (n_idx < n_cmp), may_be_empty=True)
    p_sum = jnp.sum(p_cmp * inv_cmp, axis=0)
    nb8 = -(-n_blk // 16) * 16
    picked_t = _pick_blocks(p_sum, q0 + lax.broadcasted_iota(jnp.int32, (1, qb), 1), n_cmp, n_blk, nb8)
    picked = _transpose_01(picked_t)

    def sel_branch(nk):
        blk_of_key = lax.broadcasted_iota(jnp.int32, (nb8, nk), 1) // SEL_BLOCK
        expand = (blk_of_key == lax.broadcasted_iota(jnp.int32, (nb8, nk), 0)).astype(BF16)
        key_picked = (_mm(picked, expand) > 0.5).reshape(1, qb, nk)
        o, _, _ = branch(ks_ref[0, 0, 0:nk, :], vs_ref[0, 0, 0:nk, :],
                         lax.broadcasted_iota(jnp.int32, (1, 1, nk), 2),
                         lambda dist: key_picked & (dist >= 0))
        osel_ref[...] = o

    if len(sel_classes) == 1:
        sel_branch(sel_classes[0])
    else:
        need = q0 + qb
        prev = 0
        for nk in sel_classes:
            pl.when((need > prev) & (need <= nk))(functools.partial(sel_branch, nk))
            prev = nk

    gates = _sigmoid(gate_ref[0, 0, :, pl.ds(rq, qb), :]).reshape(rows, 3)
    o = gates[:, 0:1] * o_cmp + gates[:, 1:2] * osel_ref[...] + gates[:, 2:3] * o_win
    for h in range(r):
        o_ref[0, pl.ds(rq, qb), h * NSA_DH:(h + 1) * NSA_DH] = o[h * qb:(h + 1) * qb].astype(o_ref.dtype)


def _nsa_attn(q, gates_t, slopes, cmp_kv, sel_arr, sel_k0, sel_v0, win_arr, win_k0, win_v0,
              *, p_len, n_cmp, n_blk, w0):
    b, tq, _ = q.shape
    g, r, dh = NSA_G, NSA_R, NSA_DH
    qb = NSA_Q_BLOCK if tq % NSA_Q_BLOCK == 0 else tq
    nc = cmp_kv.shape[2]
    tk = sel_arr.shape[2]
    tw = win_arr.shape[2]
    if tq == qb or tk % NSA_SEL_CLASS != 0:
        sel_classes = (tk,)
    else:
        sel_classes = tuple(range(NSA_SEL_CLASS, tk + 1, NSA_SEL_CLASS))
    kw_len = min(tw, -(-(WINDOW - 1 + qb) // KEY_CHUNK) * KEY_CHUNK)
    nsb = NSA_Q_BLOCKS_PER_STEP if (tq // qb) % NSA_Q_BLOCKS_PER_STEP == 0 else 1
    body = functools.partial(_nsa_attn_body, nsb=nsb, qb=qb, p_len=p_len, n_cmp=n_cmp, n_blk=n_blk, w0=w0,
                             sel_classes=sel_classes, kw_len=kw_len)
    qs = nsb * qb
    return pl.pallas_call(
        body,
        grid=(b, g, tq // qs),
        in_specs=[pl.BlockSpec((1, qs, r * dh), lambda bi, gi, i: (bi, i, gi)),
                  pl.BlockSpec((1, 1, r, qs, 3), lambda bi, gi, i: (bi, gi, 0, i, 0)),
                  pl.BlockSpec((1, r, 1, 1), lambda bi, gi, i: (gi, 0, 0, 0)),
                  pl.BlockSpec((1, 1, nc, dh), lambda bi, gi, i: (bi, gi, 0, 0)),
                  pl.BlockSpec((1, 1, nc, dh), lambda bi, gi, i: (bi, NSA_G + gi, 0, 0)),
                  pl.BlockSpec((1, 1, tk, dh), lambda bi, gi, i: (bi, sel_k0 + gi, 0, 0)),
                  pl.BlockSpec((1, 1, tk, dh), lambda bi, gi, i: (bi, sel_v0 + gi, 0, 0)),
                  pl.BlockSpec((1, 1, tw, dh), lambda bi, gi, i: (bi, win_k0 + gi, 0, 0)),
                  pl.BlockSpec((1, 1, tw, dh), lambda bi, gi, i: (bi, win_v0 + gi, 0, 0))],
        out_specs=pl.BlockSpec((1, qs, r * dh), lambda bi, gi, i: (bi, i, gi)),
        out_shape=jax.ShapeDtypeStruct((b, tq, g * r * dh), BF16),
        scratch_shapes=[pltpu.VMEM((r * qb, dh), F32)],
        compiler_params=_params("parallel", "parallel", "arbitrary"),
        name="nsa_attn",
    )(q, gates_t, slopes, cmp_kv, cmp_kv, sel_arr, sel_arr, win_arr, win_arr)


def _nsa_paged_body(pt_ref, *refs, tq, p_len, n_cmp, n_blk, w0, pps, n_steps):
    k_refs, v_refs = refs[:pps], refs[pps:2 * pps]
    (q_ref, gate_ref, slope_ref, cmp_ref, new_ref, win_ref, o_ref,
     qbd_ref, pen_ref, m_ref, l_ref, acc_ref, ocmp_ref) = refs[2 * pps:]
    p = pl.program_id(1)
    g_n, r = NSA_G, NSA_R
    dh = NSA_DH
    rg = r * tq
    rows = g_n * rg
    page = k_refs[0].shape[5]
    chunk = pps * page
    n_keys = pen_ref.shape[1]
    t_pos = p_len + lax.broadcasted_iota(jnp.int32, (1, tq, 1), 1)

    @pl.when(p == 0)
    def _():
        qbd_ref[...] = jnp.zeros(qbd_ref.shape, qbd_ref.dtype)
        nc = cmp_ref.shape[2]
        n_idx = lax.broadcasted_iota(jnp.int32, (1, 1, nc), 2)
        p_sums = []
        for g in range(g_n):
            q = _head_rows(q_ref[0, :, g * r * dh:(g + 1) * r * dh].astype(F32), r)
            q = (q * (dh ** -0.5 * LOG2E)).astype(BF16)
            qbd_ref[g * rg:(g + 1) * rg, g * dh:(g + 1) * dh] = q
            o_cmp, p_cmp, inv_cmp = _attn_branch(
                q, slope_ref[g] * LOG2E, t_pos, cmp_ref[0, g], cmp_ref[0, g_n + g],
                n_idx * CMP_STRIDE + (CMP_LEN - 1), lambda dist: (dist >= 0) & (n_idx < n_cmp), may_be_empty=True)
            ocmp_ref[g * rg:(g + 1) * rg, :] = o_cmp
            p_sums.append(jnp.sum(p_cmp * inv_cmp, axis=0))
        nbp = -(-n_blk // 128) * 128
        nq = g_n * tq
        t_lane = p_len + lax.broadcasted_iota(jnp.int32, (1, nq), 1) % tq
        picked_t = _pick_blocks(jnp.concatenate(p_sums, axis=0), t_lane, n_cmp, n_blk, nbp)
        picked = _transpose_01(picked_t.astype(BF16))
        picked_rows = jnp.concatenate([picked[g * tq:(g + 1) * tq] for g in range(g_n) for _ in range(r)], axis=0)
        slope_rows = jnp.broadcast_to(slope_ref[...] * LOG2E, (g_n, r, tq, 1)).reshape(rows, 1)
        t_rows = p_len + lax.broadcasted_iota(jnp.int32, (rows, 1), 0) % tq

        def fill(k0, width):
            kp = k0 + lax.broadcasted_iota(jnp.int32, (1, width), 1)
            blk = lax.broadcasted_iota(jnp.int32, (nbp, width), 0)
            expand = ((k0 + lax.broadcasted_iota(jnp.int32, (nbp, width), 1)) // SEL_BLOCK == blk).astype(BF16)
            key_picked = _mm(picked_rows, expand) > 0.5
            dist = t_rows - kp
            return slope_rows * jnp.where(key_picked & (dist >= 0), dist.astype(F32), MASKED_DIST)

        def fill_chunk(c, carry):
            k0 = pl.multiple_of(c * chunk, chunk)
            pen_ref[:, pl.ds(k0, chunk)] = fill(k0, chunk)
            return carry

        lax.fori_loop(0, n_steps, fill_chunk, 0)
        pen_ref[:, n_steps * chunk:n_keys] = fill(n_steps * chunk, n_keys - n_steps * chunk)
        m_ref[...] = jnp.full(m_ref.shape, NEVER, F32)
        l_ref[...] = jnp.zeros(l_ref.shape, F32)
        acc_ref[...] = jnp.zeros(acc_ref.shape, F32)

    def online_update(kts, vts, k0):
        width = len(kts) * page
        qbd = qbd_ref[...]
        s = jnp.concatenate([jnp.dot(qbd, kt.astype(BF16), preferred_element_type=F32) for kt in kts], axis=1)
        s = s - pen_ref[:, pl.ds(k0, width)]
        m_prev = m_ref[...]
        m_new = jnp.maximum(m_prev, jnp.max(s, axis=-1, keepdims=True))
        alpha = jnp.exp2(m_prev - m_new)
        prob = jnp.exp2(s - m_new)
        l_ref[...] = alpha * l_ref[...] + jnp.sum(prob, axis=-1, keepdims=True)
        pv = [_mm_nt(prob[:, i * page:(i + 1) * page], vt) for i, vt in enumerate(vts)]
        acc_ref[...] = alpha * acc_ref[...] + sum(pv[1:], pv[0])
        m_ref[...] = m_new

    @pl.when(p < n_steps)
    def _():
        online_update([ref[0, 0, 0].reshape(g_n * dh, page) for ref in k_refs],
                      [ref[0, 0, 0].reshape(g_n * dh, page) for ref in v_refs], pl.multiple_of(p * chunk, chunk))

    @pl.when(p == n_steps)
    def _():
        online_update([new_ref[0, 0].reshape(g_n * dh, page)], [new_ref[0, 1].reshape(g_n * dh, page)],
                      n_steps * chunk)
        acc = acc_ref[...] * _inv_or_zero(l_ref[...])
        tw = win_ref.shape[2]
        kp_win = w0 + lax.broadcasted_iota(jnp.int32, (1, 1, tw), 2)
        for g in range(g_n):
            q = qbd_ref[g * rg:(g + 1) * rg, g * dh:(g + 1) * dh]
            o_win, _, _ = _attn_branch(q, slope_ref[g] * LOG2E, t_pos, win_ref[0, g], win_ref[0, g_n + g], kp_win,
                                       lambda dist: (dist >= 0) & (dist < WINDOW) & (kp_win >= 0))
            gates = _sigmoid(gate_ref[0, g]).reshape(rg, 3)
            o = (gates[:, 0:1] * ocmp_ref[g * rg:(g + 1) * rg, :]
                 + gates[:, 1:2] * acc[g * rg:(g + 1) * rg, g * dh:(g + 1) * dh] + gates[:, 2:3] * o_win)
            for h in range(r):
                o_ref[0, :, (g * r + h) * dh:(g * r + h + 1) * dh] = o[h * tq:(h + 1) * tq].astype(o_ref.dtype)


def _nsa_paged(q, gates_t, slopes, cmp_kv, cache_nt, layer, page_table, new_t, win_arr,
               *, p_len, n_cmp, n_blk, w0):
    b, tq, _ = q.shape
    g, r, dh = NSA_G, NSA_R, NSA_DH
    n_pages = page_table.shape[1]
    page = cache_nt.shape[5]
    pps = max(c_ for c_ in (16, 8, 4, 1) if n_pages % c_ == 0)
    n_steps = n_pages // pps
    nc = cmp_kv.shape[2]
    tw = win_arr.shape[2]
    rows = g * r * tq
    n_keys = (n_pages + 1) * page

    def page_map(c, s):
        return lambda i, p, pt: (layer, pt[i, jnp.minimum(p, n_steps - 1) * pps + s], c, 0, 0, 0)

    const = lambda i, p, pt: (i, 0, 0, 0, 0)
    grid_spec = pltpu.PrefetchScalarGridSpec(
        num_scalar_prefetch=1,
        grid=(b, n_steps + 1),
        in_specs=[pl.BlockSpec((1, 1, 1, g, dh, page), page_map(2, s)) for s in range(pps)]
        + [pl.BlockSpec((1, 1, 1, g, dh, page), page_map(3, s)) for s in range(pps)]
        + [pl.BlockSpec((1, tq, g * r * dh), lambda i, p, pt: (i, 0, 0)),
           pl.BlockSpec((1, g, r, tq, 3), const),
           pl.BlockSpec((g, r, 1, 1), lambda i, p, pt: (0, 0, 0, 0)),
           pl.BlockSpec((1, 2 * g, nc, dh), lambda i, p, pt: (i, 0, 0, 0)),
           pl.BlockSpec((1, 2, g, dh, page), const),
           pl.BlockSpec((1, 2 * g, tw, dh), lambda i, p, pt: (i, 0, 0, 0))],
        out_specs=pl.BlockSpec((1, tq, g * r * dh), lambda i, p, pt: (i, 0, 0)),
        scratch_shapes=[pltpu.VMEM((rows, g * dh), BF16),
                        pltpu.VMEM((rows, n_keys), F32),
                        pltpu.VMEM((rows, 1), F32),
                        pltpu.VMEM((rows, 1), F32),
                        pltpu.VMEM((rows, g * dh), F32),
                        pltpu.VMEM((rows, dh), F32)],
    )
    body = functools.partial(_nsa_paged_body, tq=tq, p_len=p_len, n_cmp=n_cmp, n_blk=n_blk, w0=w0, pps=pps,
                             n_steps=n_steps)
    return pl.pallas_call(
        body,
        grid_spec=grid_spec,
        out_shape=jax.ShapeDtypeStruct((b, tq, g * r * dh), BF16),
        compiler_params=_params("parallel", "arbitrary"),
        name="nsa_paged",
    )(page_table, *([cache_nt] * (2 * pps)), q, gates_t, slopes, cmp_kv, new_t, win_arr)


def _nsa_layer(x, cache_hm, layer, page_table, win_buf, norm_w, w_in, w1cat, pe_flat, w2, w_out, slopes):
    b, t, d = x.shape
    g, r, dh = NSA_G, NSA_R, NSA_DH
    xf = x.reshape(b * t, d)
    w_q, w_kv = w_in
    q = _rms_matmul(xf, norm_w, w_q, NSA_Q_DIM, BF16).reshape(b, t, NSA_Q_DIM)
    if cache_hm is None and t % HM_ROWS == 0:
        proj, kv_hm = _rms_matmul_hm(xf, norm_w, w_kv, b, t, 2 * NSA_KV_DIM, 4 * g, dh)
        proj = proj.reshape(b, t, -1)
    else:
        proj = _rms_matmul(xf, norm_w, w_kv, w_kv.shape[1]).reshape(b, t, -1)
        kv_hm = None
    kv = proj[..., :6 * NSA_KV_DIM]
    gates_t = proj[..., 6 * NSA_KV_DIM:6 * NSA_KV_DIM + 3 * NSA_HEADS]
    gates_t = gates_t.reshape(b, t, g, r, 3).transpose(0, 2, 3, 1, 4)
    kv6 = kv.reshape(b, t, 6, g, dh)
    new_rows = kv6[:, :, :4]
    if cache_hm is None:
        p_len = 0
        tk = t
        assert t % CMP_STRIDE == 0
        cmp_x = kv6[:, :, :2].astype(BF16).transpose(0, 2, 3, 1, 4).reshape(b, 2 * g, t // CMP_STRIDE,
                                                                           CMP_STRIDE * dh)
        if kv_hm is None:
            kv_hm = kv6[:, :, 2:].astype(BF16).transpose(0, 2, 3, 1, 4).reshape(b, 4 * g, t, dh)
        new_win = kv6[:, t - min(WINDOW, t):, 4:]
    else:
        cache_cmp, cache_nt = cache_hm
        n_pages = page_table.shape[1]
        page = cache_nt.shape[5]
        p_len = n_pages * page
        tk = p_len + t
        assert p_len % CMP_STRIDE == 0 and t < CMP_STRIDE and t <= page
        cmp_x = _page_gather(cache_cmp, layer, page_table)
        new_t = jnp.pad(kv6[:, :, 2:4].transpose(0, 2, 3, 4, 1), [(0, 0)] * 4 + [(0, page - t)])
        win_all = jnp.concatenate([win_buf, kv6[:, :, 4:]], axis=1)
        wl = win_all.shape[1]
        wlp = -(-wl // KEY_CHUNK) * KEY_CHUNK
        win_arr = jnp.pad(win_all.astype(BF16), [(0, 0), (0, wlp - wl), (0, 0), (0, 0), (0, 0)])
        win_arr = win_arr.transpose(0, 2, 3, 1, 4).reshape(b, 2 * g, wlp, dh)
        new_win = win_all[:, wl - min(WINDOW, tk):]
    n_sub = tk // CMP_STRIDE
    n_cmp = n_sub - 1
    n_blk = -(-tk // SEL_BLOCK)
    cmp_kv = _compress(cmp_x, n_sub, w1cat, pe_flat, w2)
    if cache_hm is None:
        o = _nsa_attn(q, gates_t, slopes, cmp_kv, kv_hm, 0, g, kv_hm, 2 * g, 3 * g,
                      p_len=0, n_cmp=n_cmp, n_blk=n_blk, w0=0)
    else:
        o = _nsa_paged(q, gates_t, slopes, cmp_kv, cache_nt, layer, page_table, new_t, win_arr,
                       p_len=p_len, n_cmp=n_cmp, n_blk=n_blk, w0=p_len - win_buf.shape[1])
    y = _matmul_res(o.reshape(b * t, NSA_Q_DIM), w_out, xf).reshape(b, t, d)
    return y, new_rows, new_win


def _ffn_layer(x, buf, norm_w, w_up, conv_w, w_down):
    b, t, d = x.shape
    xf = x.reshape(b * t, d)
    h = _rms_matmul(xf, norm_w, w_up, D_FF, BF16).reshape(b, t, -1)
    act, new_buf = _ffn_act(h, buf, conv_w)
    y = _matmul_res(act.reshape(b * t, D_FF), w_down, xf).reshape(b, t, d)
    return y, new_buf


def _pad_cols(w, n):
    return jnp.pad(w, [(0, 0)] * (w.ndim - 1) + [(0, n - w.shape[-1])])


def _trunk(x, cache_hm, page_table, nsa_win, gdn_state, gdn_conv, ffn_conv, wts):
    depth = wts["ffn_w_up"].shape[0]
    rows_l, win_l, s_l, gconv_l, fconv_l = [], [], [], [], []
    for i in range(depth):
        j = i // 2
        if i % 2 == 0:
            x, s_new, cb = _gdn_layer(x, gdn_state, j, gdn_conv[j], wts["gdn_norm"][j],
                                      (wts["gdn_w_main"][j], wts["gdn_w_tail"][j]),
                                      wts["gdn_conv_w"][j], wts["gdn_a_log"][j], wts["gdn_dt_bias"][j],
                                      wts["gdn_out_norm"][j], wts["gdn_w_out"][j])
            s_l.append(s_new)
            gconv_l.append(cb)
        else:
            x, rows, wb = _nsa_layer(x, cache_hm, j, page_table, None if nsa_win is None else nsa_win[j],
                                     wts["nsa_norm"][j], (wts["nsa_w_q"][j], wts["nsa_w_kv"][j]),
                                     wts["nsa_w1cat"][j],
                                     wts["nsa_pe_flat"][j], wts["nsa_w2"][j], wts["nsa_w_out"][j],
                                     wts["slopes"])
            rows_l.append(rows)
            win_l.append(wb)
        x, fb = _ffn_layer(x, ffn_conv[i], wts["ffn_norm"][i], wts["ffn_w_up"][i], wts["ffn_conv_w"][i],
                           wts["ffn_w_down"][i])
        fconv_l.append(fb)
    b, t, d = x.shape
    y = _rms(x.reshape(b * t, d), wts["final_norm"]).reshape(b, t, d)
    return (y, jnp.stack(rows_l), jnp.stack(win_l), jnp.stack(s_l), jnp.stack(gconv_l), jnp.stack(fconv_l))


def kernel(x_prompt, x_sample, cache_nsa_kv, cache_nsa_win, state_gdn_s, state_gdn_conv, state_ffn_conv,
           page_table, gdn_norm, gdn_w_in, gdn_conv_w, gdn_a_log, gdn_dt_bias, gdn_out_norm, gdn_w_out,
           nsa_norm, nsa_w_in, nsa_cmp_pe, nsa_cmp_w1, nsa_cmp_w2, nsa_w_out,
           ffn_norm, ffn_w_up, ffn_conv_w, ffn_w_down, final_norm):
    n_gdn = gdn_w_in.shape[0]
    n_nsa = nsa_w_in.shape[0]
    depth = ffn_w_up.shape[0]
    bp = x_prompt.shape[0]
    gdn_main = GDN_CONV_CH + GDN_V_DIM
    nsa_cols = -(-(nsa_w_in.shape[2] - NSA_Q_DIM) // 896) * 896
    w1 = nsa_cmp_w1.reshape(n_nsa, 2, 2, CMP_STRIDE * NSA_DH, CMP_HIDDEN)
    w1cat = jnp.concatenate([w1[:, :, 0], w1[:, :, 1]], axis=-1).astype(BF16)
    slopes = 2.0 ** (-8.0 * jnp.arange(1, NSA_HEADS + 1, dtype=F32) / NSA_HEADS)
    wts = {
        "gdn_norm": gdn_norm,
        "gdn_w_main": gdn_w_in[:, :, :gdn_main].astype(BF16),
        "gdn_w_tail": _pad_cols(gdn_w_in[:, :, gdn_main:], 128).astype(BF16),
        "gdn_conv_w": gdn_conv_w,
        "gdn_a_log": _pad_cols(gdn_a_log, 128).reshape(n_gdn, 1, 128),
        "gdn_dt_bias": _pad_cols(gdn_dt_bias, 128).reshape(n_gdn, 1, 128),
        "gdn_out_norm": gdn_out_norm,
        "gdn_w_out": gdn_w_out.astype(BF16),
        "nsa_norm": nsa_norm,
        "nsa_w_q": nsa_w_in[:, :, :NSA_Q_DIM].astype(BF16),
        "nsa_w_kv": _pad_cols(nsa_w_in[:, :, NSA_Q_DIM:], nsa_cols).astype(BF16),
        "nsa_w1cat": w1cat,
        "nsa_pe_flat": nsa_cmp_pe.reshape(n_nsa, 2, 2, CMP_STRIDE * NSA_DH),
        "nsa_w2": nsa_cmp_w2.astype(BF16),
        "nsa_w_out": nsa_w_out.astype(BF16),
        "slopes": slopes.reshape(NSA_G, NSA_R, 1, 1),
        "ffn_norm": ffn_norm,
        "ffn_w_up": ffn_w_up.astype(BF16),
        "ffn_conv_w": ffn_conv_w,
        "ffn_w_down": ffn_w_down.astype(BF16),
        "final_norm": final_norm,
    }
    n_l, pool, page = cache_nsa_kv.shape[:3]
    cache_cmp = cache_nsa_kv[:, :, :, :2].astype(BF16).transpose(0, 1, 3, 4, 2, 5)
    cache_hm = (cache_cmp.reshape(n_l, pool, 2 * NSA_G, page // CMP_STRIDE, CMP_STRIDE * NSA_DH),
                cache_nsa_kv.transpose(0, 1, 3, 4, 5, 2))

    zeros = functools.partial(jnp.zeros, dtype=F32)
    prompt = _trunk(x_prompt, None, None, None,
                    zeros((n_gdn, bp, GDN_V_HEADS, GDN_DK, GDN_DV)),
                    zeros((n_gdn, bp, GDN_CONV_W - 1, GDN_CONV_CH)),
                    zeros((depth, bp, FFN_CONV_W - 1, D_FF)), wts)
    sample = _trunk(x_sample, cache_hm, page_table, cache_nsa_win, state_gdn_s, state_gdn_conv, state_ffn_conv, wts)
    out = []
    for p, s in zip(prompt, sample):
        out.extend([p, s])
    return tuple(out)
```
